```python
import math
import jax, jax.numpy as jnp
from jax import lax
import numpy as np

D_MODEL = 1024
BATCH = 32
SEQ = 256
DEPTH = 2
DEC_BATCH = 4
DEC_SEQ = 2048
PAST_LEN = 512

GRID_W = 64
Q_BLOCK = 128
EPS = 1e-6
ROPE_THETA = 10000.0
N_MLA = (DEPTH + 1) // 2
N_GQA = DEPTH // 2
MLA_HEADS = 16
MLA_NOPE = 64
MLA_ROPE = 32
MLA_V = 64
Q_LORA = 384
KV_LORA = 256
KV_DIM = KV_LORA + MLA_ROPE
MLA_SCALE = 1.0 / math.sqrt(MLA_NOPE + MLA_ROPE)
GQA_HEADS = 16
GQA_KV_HEADS = 4
GQA_GROUP = GQA_HEADS // GQA_KV_HEADS
GQA_HEAD_DIM = 64
GQA_SCALE = 1.0 / math.sqrt(GQA_HEAD_DIM)
N_GROUPS = 4
EXPERTS_PER_GROUP = 4
N_EXPERTS = N_GROUPS * EXPERTS_PER_GROUP
TOP_K = 2
EXPERT_FF = 256

kernel_name = "hybrid_mla_gqa_hmoe_diffusion_step"


def rms_norm(x, g):
    xf = x.astype(jnp.float32)
    y = xf * lax.rsqrt(jnp.mean(xf * xf, axis=-1, keepdims=True) + EPS)
    return (y * g.astype(jnp.float32)).astype(x.dtype)


def modulate(h, shift, scale):
    return h * (1 + scale[:, None, :]) + shift[:, None, :]


def axial_rope_tables(n_tokens, rot_dim, dtype):
    rows = n_tokens // GRID_W
    t = jnp.arange(rows * GRID_W)
    row = (t // GRID_W).astype(jnp.float32)
    col = (t % GRID_W).astype(jnp.float32)
    axis_dim = rot_dim // 2
    inv = jnp.power(ROPE_THETA, -jnp.arange(0, axis_dim, 2, dtype=jnp.float32) / axis_dim)
    ang = jnp.concatenate([row[:, None] * inv, col[:, None] * inv], axis=-1)
    return jnp.cos(ang).astype(dtype), jnp.sin(ang).astype(dtype)


def apply_rope(x, cos, sin):
    xp = x.reshape(x.shape[:-1] + (x.shape[-1] // 2, 2))
    x0, x1 = xp[..., 0], xp[..., 1]
    c = cos[None, :, None, :]
    s = sin[None, :, None, :]
    return jnp.stack([x0 * c - x1 * s, x0 * s + x1 * c], axis=-1).reshape(x.shape)


def block_attention(q, k, v, scale):
    B, Tq = q.shape[:2]
    nb = Tq // Q_BLOCK
    qb = q.reshape((B, nb, Q_BLOCK) + q.shape[2:]).swapaxes(0, 1)

    def one_block(qblk):
        s = jnp.einsum('bqhgd,bkhd->bhgqk', qblk, k).astype(jnp.float32) * scale
        p = jax.nn.softmax(s, axis=-1).astype(v.dtype)
        return jnp.einsum('bhgqk,bkhe->bqhge', p, v)

    out = lax.map(one_block, qb)
    return out.swapaxes(0, 1).reshape((B, Tq) + out.shape[3:])


def mla_project(h, w_dq, g_q, w_uq, w_dkv, g_kv):
    B, T, _ = h.shape
    q = (rms_norm(h @ w_dq, g_q) @ w_uq).reshape(B, T, MLA_HEADS, MLA_NOPE + MLA_ROPE)
    kv = h @ w_dkv
    c_kv = rms_norm(kv[..., :KV_LORA], g_kv)
    return q[..., :MLA_NOPE], q[..., MLA_NOPE:], c_kv, kv[..., KV_LORA:]


def mla_attend(q_nope, q_pe, keys, w_uk, w_uv, w_o):
    B, T = q_nope.shape[:2]
    q_lat = jnp.einsum('bthn,chn->bthc', q_nope, w_uk)
    q = jnp.concatenate([q_lat, q_pe], axis=-1)[:, :, None]
    k = keys[:, :, None, :]
    v = keys[:, :, None, :KV_LORA]
    o = block_attention(q, k, v, MLA_SCALE)[:, :, 0]
    o = jnp.einsum('bthc,chv->bthv', o, w_uv).reshape(B, T, MLA_HEADS * MLA_V)
    return o @ w_o


def gqa_project(h, w_qkv, g_q, g_k):
    B, T, _ = h.shape
    qkv = h @ w_qkv
    nq = GQA_HEADS * GQA_HEAD_DIM
    nk = GQA_KV_HEADS * GQA_HEAD_DIM
    q = rms_norm(qkv[..., :nq].reshape(B, T, GQA_HEADS, GQA_HEAD_DIM), g_q)
    k = rms_norm(qkv[..., nq:nq + nk].reshape(B, T, GQA_KV_HEADS, GQA_HEAD_DIM), g_k)
    v = qkv[..., nq + nk:].reshape(B, T, GQA_KV_HEADS, GQA_HEAD_DIM)
    return q, k, v


def gqa_attend(q, k, v, w_o):
    B, T = q.shape[:2]
    qg = q.reshape(B, T, GQA_KV_HEADS, GQA_GROUP, GQA_HEAD_DIM)
    o = block_attention(qg, k, v, GQA_SCALE).reshape(B, T, GQA_HEADS * GQA_HEAD_DIM)
    return o @ w_o


def hier_moe(h, w_group, b_group, w_exp, b_exp, w_gate, w_up, w_down):
    B, T, D = h.shape
    xt = h.reshape(B * T, D)
    N = xt.shape[0]
    g_prob = jax.nn.softmax((xt @ w_group).astype(jnp.float32) + b_group.astype(jnp.float32), axis=-1)
    g_w, g_idx = lax.top_k(g_prob, 1)
    e_logits = ((xt @ w_exp).astype(jnp.float32) + b_exp.astype(jnp.float32)).reshape(N, N_GROUPS, EXPERTS_PER_GROUP)
    idx = jnp.broadcast_to(g_idx[:, :, None], (N, 1, EXPERTS_PER_GROUP))
    e_in = jnp.take_along_axis(e_logits, idx, axis=1)[:, 0]
    e_val, e_idx = lax.top_k(e_in, TOP_K)
    e_w = jax.nn.softmax(e_val, axis=-1) * g_w
    expert_id = g_idx * EXPERTS_PER_GROUP + e_idx
    combine = jnp.sum(jax.nn.one_hot(expert_id, N_EXPERTS, dtype=jnp.float32) * e_w[..., None], axis=1)
    hid = jax.nn.silu(jnp.einsum('nd,edf->nef', xt, w_gate)) * jnp.einsum('nd,edf->nef', xt, w_up)
    hid = hid * combine[:, :, None].astype(hid.dtype)
    out = jnp.einsum('nef,efd->nd', hid, w_down)
    return out.reshape(B, T, D)


def setup_inputs(seed: int = 0) -> dict:
    key = jax.random.key(seed)
    ks = iter(jax.random.split(key, 40))
    f32 = jnp.float32
    D = D_MODEL

    def nrm(shape, scale):
        return scale * jax.random.normal(next(ks), shape, f32)

    def gain(shape):
        return 1.0 + 0.05 * jax.random.normal(next(ks), shape, f32)

    return {
        "x_prompt": nrm((BATCH, SEQ, D), 1.0),
        "x_sample": nrm((DEC_BATCH, DEC_SEQ, D), 1.0),
        "cache_mla": nrm((DEC_BATCH, N_MLA, PAST_LEN, KV_DIM), 1.0),
        "cache_gqa": nrm((DEC_BATCH, N_GQA, PAST_LEN, 2, GQA_KV_HEADS, GQA_HEAD_DIM), 1.0),
        "c": nrm((DEC_BATCH, D), 1.0),
        "c_ctx": nrm((D,), 1.0),
        "ada_w": nrm((DEPTH, D, 6 * D), 0.5 * D ** -0.5),
        "ada_b": nrm((DEPTH, 6 * D), 0.01),
        "norm_mix": gain((DEPTH, D)),
        "norm_ffn": gain((DEPTH, D)),
        "mla_w_dq": nrm((N_MLA, D, Q_LORA), D ** -0.5),
        "mla_q_norm": gain((N_MLA, Q_LORA)),
        "mla_w_uq": nrm((N_MLA, Q_LORA, MLA_HEADS * (MLA_NOPE + MLA_ROPE)), Q_LORA ** -0.5),
        "mla_w_dkv": nrm((N_MLA, D, KV_DIM), D ** -0.5),
        "mla_kv_norm": gain((N_MLA, KV_LORA)),
        "mla_w_uk": nrm((N_MLA, KV_LORA, MLA_HEADS, MLA_NOPE), KV_LORA ** -0.5),
        "mla_w_uv": nrm((N_MLA, KV_LORA, MLA_HEADS, MLA_V), KV_LORA ** -0.5),
        "mla_w_o": nrm((N_MLA, MLA_HEADS * MLA_V, D), (MLA_HEADS * MLA_V) ** -0.5),
        "gqa_w_qkv": nrm((N_GQA, D, (GQA_HEADS + 2 * GQA_KV_HEADS) * GQA_HEAD_DIM), D ** -0.5),
        "gqa_q_norm": gain((N_GQA, GQA_HEAD_DIM)),
        "gqa_k_norm": gain((N_GQA, GQA_HEAD_DIM)),
        "gqa_w_o": nrm((N_GQA, GQA_HEADS * GQA_HEAD_DIM, D), (GQA_HEADS * GQA_HEAD_DIM) ** -0.5),
        "moe_w_group": nrm((DEPTH, D, N_GROUPS), D ** -0.5),
        "moe_b_group": nrm((DEPTH, N_GROUPS), 0.01),
        "moe_w_expert": nrm((DEPTH, D, N_EXPERTS), D ** -0.5),
        "moe_b_expert": nrm((DEPTH, N_EXPERTS), 0.01),
        "moe_w_gate": nrm((DEPTH, N_EXPERTS, D, EXPERT_FF), D ** -0.5),
        "moe_w_up": nrm((DEPTH, N_EXPERTS, D, EXPERT_FF), D ** -0.5),
        "moe_w_down": nrm((DEPTH, N_EXPERTS, EXPERT_FF, D), EXPERT_FF ** -0.5),
        "final_norm": gain((D,)),
    }


def reference(x_prompt, x_sample, cache_mla, cache_gqa, c, c_ctx,
              ada_w, ada_b, norm_mix, norm_ffn,
              mla_w_dq, mla_q_norm, mla_w_uq, mla_w_dkv, mla_kv_norm, mla_w_uk, mla_w_uv, mla_w_o,
              gqa_w_qkv, gqa_q_norm, gqa_k_norm, gqa_w_o,
              moe_w_group, moe_b_group, moe_w_expert, moe_b_expert, moe_w_gate, moe_w_up, moe_w_down,
              final_norm):
    n_lat = x_sample.shape[1]
    cos_mla, sin_mla = axial_rope_tables(n_lat, MLA_ROPE, x_sample.dtype)
    cos_gqa, sin_gqa = axial_rope_tables(n_lat, GQA_HEAD_DIM, x_sample.dtype)

    xp, xs = x_prompt, x_sample
    mla_states, gqa_states = [], []
    for l in range(DEPTH):
        j = l // 2
        mod_p = (jax.nn.silu(c_ctx) @ ada_w[l] + ada_b[l])[None]
        mod_s = jax.nn.silu(c) @ ada_w[l] + ada_b[l]
        sh_a_p, sc_a_p, gt_a_p, sh_m_p, sc_m_p, gt_m_p = jnp.split(mod_p, 6, axis=-1)
        sh_a_s, sc_a_s, gt_a_s, sh_m_s, sc_m_s, gt_m_s = jnp.split(mod_s, 6, axis=-1)

        hp = modulate(rms_norm(xp, norm_mix[l]), sh_a_p, sc_a_p)
        hs = modulate(rms_norm(xs, norm_mix[l]), sh_a_s, sc_a_s)
        if l % 2 == 0:
            prm = (mla_w_dq[j], mla_q_norm[j], mla_w_uq[j], mla_w_dkv[j], mla_kv_norm[j])
            qn_p, qpe_p, ckv_p, kpe_p = mla_project(hp, *prm)
            keys_p = jnp.concatenate([ckv_p, kpe_p], axis=-1)
            mix_p = mla_attend(qn_p, qpe_p, keys_p, mla_w_uk[j], mla_w_uv[j], mla_w_o[j])
            mla_states.append(keys_p)
            qn_s, qpe_s, ckv_s, kpe_s = mla_project(hs, *prm)
            qpe_s = apply_rope(qpe_s, cos_mla, sin_mla)
            kpe_s = apply_rope(kpe_s[:, :, None], cos_mla, sin_mla)[:, :, 0]
            keys_s = jnp.concatenate([cache_mla[:, j], jnp.concatenate([ckv_s, kpe_s], axis=-1)], axis=1)
            mix_s = mla_attend(qn_s, qpe_s, keys_s, mla_w_uk[j], mla_w_uv[j], mla_w_o[j])
        else:
            q_p, k_p, v_p = gqa_project(hp, gqa_w_qkv[j], gqa_q_norm[j], gqa_k_norm[j])
            mix_p = gqa_attend(q_p, k_p, v_p, gqa_w_o[j])
            gqa_states.append(jnp.stack([k_p, v_p], axis=2))
            q_s, k_s, v_s = gqa_project(hs, gqa_w_qkv[j], gqa_q_norm[j], gqa_k_norm[j])
            q_s = apply_rope(q_s, cos_gqa, sin_gqa)
            k_s = apply_rope(k_s, cos_gqa, sin_gqa)
            k_all = jnp.concatenate([cache_gqa[:, j, :, 0], k_s], axis=1)
            v_all = jnp.concatenate([cache_gqa[:, j, :, 1], v_s], axis=1)
            mix_s = gqa_attend(q_s, k_all, v_all, gqa_w_o[j])
        xp = xp + gt_a_p[:, None, :] * mix_p
        xs = xs + gt_a_s[:, None, :] * mix_s

        moe_prm = (moe_w_group[l], moe_b_group[l], moe_w_expert[l], moe_b_expert[l],
                   moe_w_gate[l], moe_w_up[l], moe_w_down[l])
        hp = modulate(rms_norm(xp, norm_ffn[l]), sh_m_p, sc_m_p)
        hs = modulate(rms_norm(xs, norm_ffn[l]), sh_m_s, sc_m_s)
        xp = xp + gt_m_p[:, None, :] * hier_moe(hp, *moe_prm)
        xs = xs + gt_m_s[:, None, :] * hier_moe(hs, *moe_prm)

    y_prompt = rms_norm(xp, final_norm)
    y_sample = rms_norm(xs, final_norm)
    new_mla = jnp.stack(mla_states, axis=1)
    new_gqa = jnp.stack(gqa_states, axis=1)
    return (y_prompt, y_sample, new_mla, new_gqa)
```

```python
import functools
import math

import jax
import jax.numpy as jnp
from jax import lax
from jax.experimental import pallas as pl
from jax.experimental.pallas import tpu as pltpu

F32 = jnp.float32
BF16 = jnp.bfloat16

D = 1024
EPS = 1e-6
GRID_W = 64
ROPE_THETA = 10000.0
LANES = 128
HALF = LANES // 2
MLA_HEADS = 16
MLA_NOPE = 64
MLA_ROPE = 32
MLA_V = 64
Q_LORA = 384
KV_LORA = 256
KV_DIM = KV_LORA + MLA_ROPE
KV_PAD = 384
MLA_SCALE = 1.0 / math.sqrt(MLA_NOPE + MLA_ROPE)
GQA_HEADS = 16
GQA_KV_HEADS = 4
GQA_HEAD_DIM = 64
GQA_SCALE = 1.0 / math.sqrt(GQA_HEAD_DIM)
N_GROUPS = 4
EXPERTS_PER_GROUP = 4
N_EXPERTS = 16
EXPERT_FF = 256
NEG = -3.0e38

VMEM_LIMIT = 56 * 1024 * 1024
N_MOD_ROWS = 8


def _params(n_axes):
    return pltpu.CompilerParams(dimension_semantics=("arbitrary",) * n_axes,
                                vmem_limit_bytes=VMEM_LIMIT)


def _rms(x, g):
    ms = jnp.mean(x * x, axis=-1, keepdims=True)
    return x * lax.rsqrt(ms + EPS) * g


def _mod_index(layer, which, row):
    return (layer * 6 + which) * N_MOD_ROWS + row


def _mod_spec(layer, which, row_fn):
    return pl.BlockSpec((None, 1, D), lambda *g: (_mod_index(layer, which, row_fn(*g)), 0, 0))


def _full_spec(shape):
    n = len(shape)
    return pl.BlockSpec(shape, lambda *g: (0,) * n)


def _ada_kernel(c_ref, w_ref, b_ref, o_ref):
    c = c_ref[...]
    a = c / (1.0 + jnp.exp(-c))
    o_ref[...] = jnp.dot(a, w_ref[...], precision=lax.Precision.HIGHEST,
                         preferred_element_type=F32) + b_ref[...]


def _ada_table(cc, ada_w, ada_b):
    depth = ada_w.shape[0]
    out = pl.pallas_call(
        _ada_kernel,
        grid=(depth, 6),
        in_specs=[
            pl.BlockSpec((N_MOD_ROWS, D), lambda l, n: (0, 0)),
            pl.BlockSpec((None, D, D), lambda l, n: (l, 0, n)),
            pl.BlockSpec((None, 1, D), lambda l, n: (l, 0, n)),
        ],
        out_specs=pl.BlockSpec((None, None, N_MOD_ROWS, D), lambda l, n: (l, n, 0, 0)),
        out_shape=jax.ShapeDtypeStruct((depth, 6, N_MOD_ROWS, D), F32),
        compiler_params=_params(2),
        name="ada",
    )(cc, ada_w, ada_b.reshape(depth, 1, 6 * D))
    return out.reshape(depth * 6 * N_MOD_ROWS, 1, D)


def _rope(a, cos, sin_signed):
    lane = lax.broadcasted_iota(jnp.int32, a.shape, 1)
    nxt = pltpu.roll(a, LANES - 1, 1)
    prv = pltpu.roll(a, 1, 1)
    return a * cos + jnp.where((lane & 1) == 0, nxt, prv) * sin_signed


def _rope_tables(n_tokens, rot_dim, lane_offset_pattern):
    t = jnp.arange(n_tokens)
    row = (t // GRID_W).astype(F32)
    col = (t % GRID_W).astype(F32)
    axis_dim = rot_dim // 2
    inv = jnp.power(ROPE_THETA, -jnp.arange(0, axis_dim, 2, dtype=F32) / axis_dim)
    ang = jnp.concatenate([row[:, None] * inv, col[:, None] * inv], axis=-1)
    cos = jnp.repeat(jnp.cos(ang), 2, axis=1)
    sin = jnp.repeat(jnp.sin(ang), 2, axis=1)
    sign = jnp.where(jnp.arange(rot_dim) % 2 == 0, -1.0, 1.0).astype(F32)
    sin = sin * sign
    cos_t = jnp.ones((n_tokens, LANES), F32)
    sin_t = jnp.zeros((n_tokens, LANES), F32)
    for off in lane_offset_pattern:
        cos_t = cos_t.at[:, off:off + rot_dim].set(cos)
        sin_t = sin_t.at[:, off:off + rot_dim].set(sin)
    return cos_t, sin_t


def _mla_proj_kernel(*refs, rope, n_cache, emit_keys):
    it = iter(refs)
    x_ref, sh_ref, sc_ref, gn_ref = next(it), next(it), next(it), next(it)
    wdq_ref, gq_ref, wuq_ref, wdkv_ref, gkv_ref, wkexp_ref, wuv_ref = (next(it) for _ in range(7))
    cos_ref = sin_ref = cache_ref = keys_ref = None
    if rope:
        cos_ref, sin_ref = next(it), next(it)
    if n_cache:
        cache_ref = next(it)
    q_ref, k_ref, v_ref = next(it), next(it), next(it)
    if emit_keys:
        keys_ref = next(it)

    def expand(keys_bf16):
        k_ref[...] = jnp.dot(keys_bf16, wkexp_ref[...], preferred_element_type=F32).astype(BF16)
        v_ref[...] = jnp.dot(keys_bf16[:, :KV_LORA], wuv_ref[...], preferred_element_type=F32).astype(BF16)

    def new_tokens():
        x = x_ref[...]
        h = (_rms(x, gn_ref[...]) * (1.0 + sc_ref[...]) + sh_ref[...]).astype(BF16)
        ql = jnp.dot(h, wdq_ref[...], preferred_element_type=F32)
        qn = _rms(ql, gq_ref[...]).astype(BF16)
        q = jnp.dot(qn, wuq_ref[...], preferred_element_type=F32)
        kv = jnp.dot(h, wdkv_ref[...], preferred_element_type=F32)
        ckv = _rms(kv[:, :KV_LORA], gkv_ref[...])
        pe = kv[:, KV_LORA:]
        if rope:
            cos, sin = cos_ref[...], sin_ref[...]
            pe = _rope(pe, cos, sin)
            q = jnp.concatenate(
                [_rope(q[:, h_ * LANES:(h_ + 1) * LANES], cos, sin) for h_ in range(MLA_HEADS)], axis=1)
        q_ref[...] = (q * MLA_SCALE).astype(BF16)
        if emit_keys:
            keys_ref[:, 0:KV_LORA] = ckv
            keys_ref[:, KV_LORA:KV_DIM] = pe[:, 0:MLA_ROPE]
        expand(jnp.concatenate([ckv, pe], axis=1).astype(BF16))

    if n_cache:
        t = pl.program_id(1)
        pl.when(t >= n_cache)(new_tokens)

        @pl.when(t < n_cache)
        def _():
            expand(cache_ref[...].astype(BF16))
    else:
        new_tokens()


def _mla_proj(x2d, mods, gn, w, *, n_batch, t_new, tm, mod_row, rope_tabs=None, cache=None):
    rope = rope_tabs is not None
    t_cache = 0 if cache is None else cache.shape[0] // n_batch
    n_cache = t_cache // tm
    ntn = t_new // tm
    nt = n_cache + ntn
    emit_keys = cache is None

    def new_idx(b, t):
        return b * ntn + jnp.maximum(t - n_cache, 0)

    row_fn = lambda b, t: mod_row(b)
    in_specs = [
        pl.BlockSpec((tm, D), lambda b, t: (new_idx(b, t), 0)),
        _mod_spec(0, 0, row_fn), _mod_spec(0, 1, row_fn),
        _full_spec((1, D)),
        _full_spec((D, Q_LORA)), _full_spec((1, Q_LORA)), _full_spec((Q_LORA, MLA_HEADS * LANES)),
        _full_spec((D, KV_PAD)), _full_spec((1, KV_LORA)),
        _full_spec((KV_PAD, MLA_HEADS * LANES)), _full_spec((KV_LORA, MLA_HEADS * MLA_V)),
    ]
    args = [x2d, mods, mods, gn, w["wdq"], w["gq"], w["wuq"], w["wdkv"], w["gkv"], w["wkexp"], w["wuv"]]
    if rope:
        in_specs += [pl.BlockSpec((tm, LANES), lambda b, t: (jnp.maximum(t - n_cache, 0), 0))] * 2
        args += list(rope_tabs)
    if n_cache:
        in_specs.append(pl.BlockSpec((tm, KV_PAD), lambda b, t: (b * n_cache + jnp.minimum(t, n_cache - 1), 0)))
        args.append(cache)
    n_new = n_batch * t_new
    n_keys = n_batch * (t_cache + t_new)
    out_specs = [
        pl.BlockSpec((tm, MLA_HEADS * LANES), lambda b, t: (new_idx(b, t), 0)),
        pl.BlockSpec((tm, MLA_HEADS * LANES), lambda b, t: (b * nt + t, 0)),
        pl.BlockSpec((tm, MLA_HEADS * MLA_V), lambda b, t: (b * nt + t, 0)),
    ]
    out_shape = [
        jax.ShapeDtypeStruct((n_new, MLA_HEADS * LANES), BF16),
        jax.ShapeDtypeStruct((n_keys, MLA_HEADS * LANES), BF16),
        jax.ShapeDtypeStruct((n_keys, MLA_HEADS * MLA_V), BF16),
    ]
    if emit_keys:
        out_specs.append(pl.BlockSpec((tm, KV_DIM), lambda b, t: (new_idx(b, t), 0)))
        out_shape.append(jax.ShapeDtypeStruct((n_new, KV_DIM), F32))
    return pl.pallas_call(
        functools.partial(_mla_proj_kernel, rope=rope, n_cache=n_cache, emit_keys=emit_keys),
        grid=(n_batch, nt), in_specs=in_specs, out_specs=out_specs, out_shape=out_shape,
        compiler_params=_params(2), name="mla_proj_s" if rope else "mla_proj_p",
    )(*args)


def _dup_halves(a):
    cols = []
    for c in range(a.shape[1] // LANES):
        blk = a[:, c * LANES:(c + 1) * LANES]
        rot = pltpu.roll(blk, HALF, 1)
        low = lax.broadcasted_iota(jnp.int32, blk.shape, 1) < HALF
        cols += [jnp.where(low, blk, rot), jnp.where(low, rot, blk)]
    return jnp.concatenate(cols, axis=1)


def _group_mean_sq(a, bmat):
    sq = a * a
    hi = sq.astype(BF16)
    lo = (sq - hi.astype(F32)).astype(BF16)
    return (jnp.dot(hi, bmat, preferred_element_type=F32) + jnp.dot(lo, bmat, preferred_element_type=F32))


def _gqa_proj_kernel(*refs, rope, n_cache, emit_kv):
    it = iter(refs)
    x_ref, sh_ref, sc_ref, gn_ref, w_ref, gq_ref, gk_ref, bmat_ref = (next(it) for _ in range(8))
    cos_ref = sin_ref = cache_ref = kv_ref = None
    if rope:
        cos_ref, sin_ref = next(it), next(it)
    if n_cache:
        cache_ref = next(it)
    q_ref, k_ref, v_ref = next(it), next(it), next(it)
    if emit_kv:
        kv_ref = next(it)
    nq = GQA_HEADS * GQA_HEAD_DIM
    nk = GQA_KV_HEADS * GQA_HEAD_DIM
    blk = 2 * LANES

    def new_tokens():
        x = x_ref[...]
        h = (_rms(x, gn_ref[...]) * (1.0 + sc_ref[...]) + sh_ref[...]).astype(BF16)
        qkv = jnp.dot(h, w_ref[...], preferred_element_type=F32)
        bmat = bmat_ref[...]

        def head_norm(a, g):
            return a * lax.rsqrt(_group_mean_sq(a, bmat) + EPS) * g

        def maybe_rope(a):
            if not rope:
                return a
            cos, sin = cos_ref[...], sin_ref[...]
            return jnp.concatenate(
                [_rope(a[:, c * LANES:(c + 1) * LANES], cos, sin) for c in range(a.shape[1] // LANES)], axis=1)

        for c in range(nq // blk):
            qb = head_norm(qkv[:, c * blk:(c + 1) * blk], gq_ref[...])
            q_ref[:, c * blk:(c + 1) * blk] = (maybe_rope(qb) * GQA_SCALE).astype(BF16)
        kn = head_norm(qkv[:, nq:nq + nk], gk_ref[...])
        vv = qkv[:, nq + nk:]
        if emit_kv:
            kv_ref[:, 0:nk] = kn
            kv_ref[:, nk:2 * nk] = vv
        k_ref[...] = _dup_halves(maybe_rope(kn)).astype(BF16)
        v_ref[...] = _dup_halves(vv).astype(BF16)

    if n_cache:
        t = pl.program_id(1)
        pl.when(t >= n_cache)(new_tokens)

        @pl.when(t < n_cache)
        def _():
            c = cache_ref[...]
            k_ref[...] = _dup_halves(c[:, 0:nk]).astype(BF16)
            v_ref[...] = _dup_halves(c[:, nk:2 * nk]).astype(BF16)
    else:
        new_tokens()


def _gqa_proj(x2d, mods, gn, w, *, n_batch, t_new, tm, mod_row, rope_tabs=None, cache=None):
    rope = rope_tabs is not None
    t_cache = 0 if cache is None else cache.shape[0] // n_batch
    n_cache = t_cache // tm
    ntn = t_new // tm
    nt = n_cache + ntn
    emit_kv = cache is None
    nq = GQA_HEADS * GQA_HEAD_DIM
    nk = GQA_KV_HEADS * GQA_HEAD_DIM
    nqkv = nq + 2 * nk
    ndup = GQA_KV_HEADS * LANES

    def new_idx(b, t):
        return b * ntn + jnp.maximum(t - n_cache, 0)

    row_fn = lambda b, t: mod_row(b)
    in_specs = [
        pl.BlockSpec((tm, D), lambda b, t: (new_idx(b, t), 0)),
        _mod_spec(1, 0, row_fn), _mod_spec(1, 1, row_fn),
        _full_spec((1, D)), _full_spec((D, nqkv)),
        _full_spec((1, 2 * LANES)), _full_spec((1, 2 * LANES)), _full_spec((2 * LANES, 2 * LANES)),
    ]
    args = [x2d, mods, mods, gn, w["wqkv"], w["gq"], w["gk"], w["bmat"]]
    if rope:
        in_specs += [pl.BlockSpec((tm, LANES), lambda b, t: (jnp.maximum(t - n_cache, 0), 0))] * 2
        args += list(rope_tabs)
    if n_cache:
        in_specs.append(pl.BlockSpec((tm, 2 * nk), lambda b, t: (b * n_cache + jnp.minimum(t, n_cache - 1), 0)))
        args.append(cache)
    n_new = n_batch * t_new
    n_keys = n_batch * (t_cache + t_new)
    out_specs = [
        pl.BlockSpec((tm, nq), lambda b, t: (new_idx(b, t), 0)),
        pl.BlockSpec((tm, ndup), lambda b, t: (b * nt + t, 0)),
        pl.BlockSpec((tm, ndup), lambda b, t: (b * nt + t, 0)),
    ]
    out_shape = [
        jax.ShapeDtypeStruct((n_new, nq), BF16),
        jax.ShapeDtypeStruct((n_keys, ndup), BF16),
        jax.ShapeDtypeStruct((n_keys, ndup), BF16),
    ]
    if emit_kv:
        out_specs.append(pl.BlockSpec((tm, 2 * nk), lambda b, t: (new_idx(b, t), 0)))
        out_shape.append(jax.ShapeDtypeStruct((n_new, 2 * nk), F32))
    return pl.pallas_call(
        functools.partial(_gqa_proj_kernel, rope=rope, n_cache=n_cache, emit_kv=emit_kv),
        grid=(n_batch, nt), in_specs=in_specs, out_specs=out_specs, out_shape=out_shape,
        compiler_params=_params(2), name="gqa_proj_s" if rope else "gqa_proj_p",
    )(*args)


def _attn_kernel(q_ref, k_ref, v_ref, o_ref, *, mla, n_pairs):
    tq = q_ref.shape[0]
    lane = lax.broadcasted_iota(jnp.int32, (tq, LANES), 1)
    low = lane < HALF
    for p in range(n_pairs):
        outs = []
        for hh in range(2):
            if mla:
                hd = 2 * p + hh
                qh = q_ref[:, hd * LANES:(hd + 1) * LANES]
                kh = k_ref[:, hd * LANES:(hd + 1) * LANES]
                vb = v_ref[:, p * LANES:(p + 1) * LANES]
            else:
                g = p // 2 if n_pairs > 1 else 0
                qp = q_ref[:, p * LANES:(p + 1) * LANES]
                qh = jnp.where(low if hh == 0 else jnp.logical_not(low), qp, jnp.zeros_like(qp))
                kh = k_ref[:, g * LANES:(g + 1) * LANES]
                vb = v_ref[:, g * LANES:(g + 1) * LANES]
            s = lax.dot_general(qh, kh, (((1,), (1,)), ((), ())), preferred_element_type=F32)
            m = jnp.max(s, axis=-1, keepdims=True)
            e = jnp.exp(s - m)
            l = jnp.sum(e, axis=-1, keepdims=True)
            o = jnp.dot(e.astype(BF16), vb, preferred_element_type=F32)
            outs.append(o / l)
        o_ref[:, p * LANES:(p + 1) * LANES] = jnp.where(low, outs[0], outs[1]).astype(BF16)


def _attention(q, k, v, *, mla, n_batch, t_q, t_k, tq, n_pairs, name):
    total_pairs = 8
    nj = total_pairs // n_pairs
    nqt = t_q // tq
    if mla:
        qw, kw, vw = n_pairs * 2 * LANES, n_pairs * 2 * LANES, n_pairs * LANES
        kv_col = lambda j: j
    else:
        qw = n_pairs * LANES
        kw = vw = max(n_pairs // 2, 1) * LANES
        kv_col = (lambda j: j // 2) if n_pairs == 1 else (lambda j: j)
    return pl.pallas_call(
        functools.partial(_attn_kernel, mla=mla, n_pairs=n_pairs),
        grid=(n_batch, nj, nqt),
        in_specs=[
            pl.BlockSpec((tq, qw), lambda b, j, i: (b * nqt + i, j)),
            pl.BlockSpec((t_k, kw), lambda b, j, i: (b, kv_col(j))),
            pl.BlockSpec((t_k, vw), lambda b, j, i: (b, kv_col(j))),
        ],
        out_specs=pl.BlockSpec((tq, n_pairs * LANES), lambda b, j, i: (b * nqt + i, j)),
        out_shape=jax.ShapeDtypeStruct((n_batch * t_q, total_pairs * LANES), BF16),
        compiler_params=_params(3), name=name,
    )(q, k, v)


def _route(logits):
    lane = lax.broadcasted_iota(jnp.int32, logits.shape, 1).astype(F32)
    big = jnp.float32(1e9)
    is_grp = (lane >= N_EXPERTS) & (lane < N_EXPERTS + N_GROUPS)
    gl = jnp.where(is_grp, logits, NEG)
    gm = jnp.max(gl, axis=-1, keepdims=True)
    g_w = 1.0 / jnp.sum(jnp.exp(gl - gm), axis=-1, keepdims=True)
    g_idx = jnp.min(jnp.where(gl == gm, lane, big), axis=-1, keepdims=True) - N_EXPERTS
    lo = g_idx * EXPERTS_PER_GROUP
    el = jnp.where((lane >= lo) & (lane < lo + EXPERTS_PER_GROUP), logits, NEG)
    m1 = jnp.max(el, axis=-1, keepdims=True)
    i1 = jnp.min(jnp.where(el == m1, lane, big), axis=-1, keepdims=True)
    el2 = jnp.where(lane == i1, NEG, el)
    m2 = jnp.max(el2, axis=-1, keepdims=True)
    i2 = jnp.min(jnp.where(el2 == m2, lane, big), axis=-1, keepdims=True)
    t = jnp.exp(m2 - m1)
    w1 = g_w / (1.0 + t)
    w2 = g_w * t / (1.0 + t)
    return jnp.where(lane == i1, w1, 0.0) + jnp.where(lane == i2, w2, 0.0)


def _post_kernel(x_ref, o_ref, wo_ref, gta_ref, shm_ref, scm_ref, gtm_ref, gn_ref,
                 wr_hi_ref, wr_lo_ref, br_ref, wg_ref, wu_ref, wd_ref, fin_ref,
                 y_ref, h_ref, comb_ref, acc_ref, *, final):
    e = pl.program_id(1)

    @pl.when(e == 0)
    def _():
        mix = jnp.dot(o_ref[...], wo_ref[...], preferred_element_type=F32)
        xm = x_ref[...] + gta_ref[...] * mix
        y_ref[...] = xm
        h = _rms(xm, gn_ref[...]) * (1.0 + scm_ref[...]) + shm_ref[...]
        h_hi = h.astype(BF16)
        h_lo = (h - h_hi.astype(F32)).astype(BF16)
        h_ref[...] = h_hi
        logits = (jnp.dot(h_hi, wr_hi_ref[...], preferred_element_type=F32)
                  + jnp.dot(h_lo, wr_hi_ref[...], preferred_element_type=F32)
                  + jnp.dot(h_hi, wr_lo_ref[...], preferred_element_type=F32)) + br_ref[...]
        comb_ref[...] = _route(logits)
        acc_ref[...] = jnp.zeros_like(acc_ref)

    h = h_ref[...]
    g = jnp.dot(h, wg_ref[...].astype(BF16), preferred_element_type=F32)
    u = jnp.dot(h, wu_ref[...].astype(BF16), preferred_element_type=F32)
    comb = comb_ref[...]
    lane = lax.broadcasted_iota(jnp.int32, comb.shape, 1)
    c_e = jnp.sum(jnp.where(lane == e, comb, 0.0), axis=-1, keepdims=True)
    hid = (g / (1.0 + jnp.exp(-g))) * u * c_e
    acc_ref[...] += jnp.dot(hid.astype(BF16), wd_ref[...].astype(BF16), preferred_element_type=F32)

    @pl.when(e == N_EXPERTS - 1)
    def _():
        out = y_ref[...] + gtm_ref[...] * acc_ref[...]
        if final:
            out = _rms(out, fin_ref[...])
        y_ref[...] = out


def _post(x2d, o2d, mods, layer, w, *, tm, mod_row, final):
    n = x2d.shape[0]
    row_fn = lambda i, e: mod_row(i)
    tile = pl.BlockSpec((tm, D), lambda i, e: (i, 0))
    return pl.pallas_call(
        functools.partial(_post_kernel, final=final),
        grid=(n // tm, N_EXPERTS),
        in_specs=[
            tile, tile, _full_spec((D, D)),
            _mod_spec(layer, 2, row_fn), _mod_spec(layer, 3, row_fn),
            _mod_spec(layer, 4, row_fn), _mod_spec(layer, 5, row_fn),
            _full_spec((1, D)),
            _full_spec((D, LANES)), _full_spec((D, LANES)), _full_spec((1, LANES)),
            pl.BlockSpec((None, D, EXPERT_FF), lambda i, e: (e, 0, 0)),
            pl.BlockSpec((None, D, EXPERT_FF), lambda i, e: (e, 0, 0)),
            pl.BlockSpec((None, EXPERT_FF, D), lambda i, e: (e, 0, 0)),
            _full_spec((1, D)),
        ],
        out_specs=tile,
        out_shape=jax.ShapeDtypeStruct((n, D), F32),
        scratch_shapes=[pltpu.VMEM((tm, D), BF16), pltpu.VMEM((tm, LANES), F32), pltpu.VMEM((tm, D), F32)],
        compiler_params=_params(2), name=f"post_l{layer}",
    )(x2d, o2d, w["wo"], mods, mods, mods, mods, w["gn"], w["wr_hi"], w["wr_lo"], w["br"],
      w["wg"], w["wu"], w["wd"], w["fin"])


def _mla_weights(w_dq, g_q, w_uq, w_dkv, g_kv, w_uk, w_uv):
    hd = MLA_NOPE + MLA_ROPE
    wuq = w_uq.reshape(Q_LORA, MLA_HEADS, hd)
    wuq = jnp.concatenate([wuq[..., MLA_NOPE:], wuq[..., :MLA_NOPE],
                           jnp.zeros((Q_LORA, MLA_HEADS, LANES - hd), F32)], axis=-1)
    wuk = jnp.concatenate([jnp.zeros((KV_LORA, MLA_HEADS, MLA_ROPE), F32), w_uk,
                           jnp.zeros((KV_LORA, MLA_HEADS, LANES - hd), F32)], axis=-1)
    pe_rows = jnp.broadcast_to(jnp.eye(MLA_ROPE, LANES, dtype=F32)[:, None, :], (MLA_ROPE, MLA_HEADS, LANES))
    wkexp = jnp.concatenate([wuk, pe_rows, jnp.zeros((KV_PAD - KV_DIM, MLA_HEADS, LANES), F32)], axis=0)
    return {
        "wdq": w_dq.astype(BF16), "gq": g_q.reshape(1, Q_LORA),
        "wuq": wuq.reshape(Q_LORA, MLA_HEADS * LANES).astype(BF16),
        "wdkv": jnp.pad(w_dkv, ((0, 0), (0, KV_PAD - KV_DIM))).astype(BF16), "gkv": g_kv.reshape(1, KV_LORA),
        "wkexp": wkexp.reshape(KV_PAD, MLA_HEADS * LANES).astype(BF16),
        "wuv": w_uv.reshape(KV_LORA, MLA_HEADS * MLA_V).astype(BF16),
    }


def _gqa_weights(w_qkv, g_q, g_k):
    grp = jnp.arange(2 * LANES) // GQA_HEAD_DIM
    bmat = jnp.where(grp[:, None] == grp[None, :], 1.0 / GQA_HEAD_DIM, 0.0).astype(BF16)
    reps = 2 * LANES // GQA_HEAD_DIM
    return {"wqkv": w_qkv.astype(BF16), "gq": jnp.tile(g_q, reps).reshape(1, 2 * LANES),
            "gk": jnp.tile(g_k, reps).reshape(1, 2 * LANES), "bmat": bmat}


def _post_weights(l, w_o, norm_ffn, w_group, b_group, w_exp, b_exp, w_gate, w_up, w_down, final_norm):
    wr = jnp.concatenate([w_exp[l], w_group[l], jnp.zeros((D, LANES - N_EXPERTS - N_GROUPS), F32)], axis=1)
    br = jnp.concatenate([b_exp[l], b_group[l], jnp.zeros((LANES - N_EXPERTS - N_GROUPS,), F32)]).reshape(1, LANES)
    wr_hi = wr.astype(BF16)
    wr_lo = (wr - wr_hi.astype(F32)).astype(BF16)
    return {"wo": w_o.astype(BF16), "gn": norm_ffn[l].reshape(1, D), "wr_hi": wr_hi, "wr_lo": wr_lo, "br": br,
            "wg": w_gate[l], "wu": w_up[l], "wd": w_down[l], "fin": final_norm.reshape(1, D)}


def kernel(x_prompt, x_sample, cache_mla, cache_gqa, c, c_ctx, ada_w, ada_b, norm_mix, norm_ffn,
           mla_w_dq, mla_q_norm, mla_w_uq, mla_w_dkv, mla_kv_norm, mla_w_uk, mla_w_uv, mla_w_o,
           gqa_w_qkv, gqa_q_norm, gqa_k_norm, gqa_w_o,
           moe_w_group, moe_b_group, moe_w_expert, moe_b_expert, moe_w_gate, moe_w_up, moe_w_down,
           final_norm):
    bp, tp, _ = x_prompt.shape
    bs, ts, _ = x_sample.shape
    t_past = cache_mla.shape[2]
    assert ada_w.shape[0] == 2 and 1 + bs <= N_MOD_ROWS
    n_p, n_s = bp * tp, bs * ts

    cc = jnp.concatenate([c_ctx[None], c, jnp.zeros((N_MOD_ROWS - 1 - bs, D), F32)], axis=0)
    mods = _ada_table(cc, ada_w, ada_b)

    xp = x_prompt.reshape(n_p, D)
    xs = x_sample.reshape(n_s, D)
    tm_proj = 512
    tm_post = 1024
    prompt_row = lambda *_: 0
    sample_row_proj = lambda b: 1 + b
    sample_row_post = lambda i: 1 + i // (ts // tm_post)
    moe_args = (norm_ffn, moe_w_group, moe_b_group, moe_w_expert, moe_b_expert,
                moe_w_gate, moe_w_up, moe_w_down, final_norm)

    wm = _mla_weights(mla_w_dq[0], mla_q_norm[0], mla_w_uq[0], mla_w_dkv[0], mla_kv_norm[0],
                      mla_w_uk[0], mla_w_uv[0])
    gn0 = norm_mix[0].reshape(1, D)
    q, k, v, keys_p = _mla_proj(xp, mods, gn0, wm, n_batch=1, t_new=n_p, tm=tm_proj, mod_row=prompt_row)
    o = _attention(q, k, v, mla=True, n_batch=bp, t_q=tp, t_k=tp, tq=tp, n_pairs=8, name="attn_mla_p")
    wp0 = _post_weights(0, mla_w_o[0], *moe_args)
    xp = _post(xp, o, mods, 0, wp0, tm=tm_post, mod_row=prompt_row, final=False)

    rope_mla = _rope_tables(ts, MLA_ROPE, (0,))
    cache0 = jnp.pad(cache_mla[:, 0].reshape(bs * t_past, KV_DIM), ((0, 0), (0, KV_PAD - KV_DIM)))
    q, k, v = _mla_proj(xs, mods, gn0, wm, n_batch=bs, t_new=ts, tm=tm_proj, mod_row=sample_row_proj,
                        rope_tabs=rope_mla, cache=cache0)
    o = _attention(q, k, v, mla=True, n_batch=bs, t_q=ts, t_k=t_past + ts, tq=512, n_pairs=1, name="attn_mla_s")
    xs = _post(xs, o, mods, 0, wp0, tm=tm_post, mod_row=sample_row_post, final=False)

    wg = _gqa_weights(gqa_w_qkv[0], gqa_q_norm[0], gqa_k_norm[0])
    gn1 = norm_mix[1].reshape(1, D)
    q, k, v, kv_p = _gqa_proj(xp, mods, gn1, wg, n_batch=1, t_new=n_p, tm=tm_proj, mod_row=prompt_row)
    o = _attention(q, k, v, mla=False, n_batch=bp, t_q=tp, t_k=tp, tq=tp, n_pairs=8, name="attn_gqa_p")
    wp1 = _post_weights(1, gqa_w_o[0], *moe_args)
    y_prompt = _post(xp, o, mods, 1, wp1, tm=tm_post, mod_row=prompt_row, final=True)

    rope_gqa = _rope_tables(ts, GQA_HEAD_DIM, (0, GQA_HEAD_DIM))
    cache1 = cache_gqa[:, 0].reshape(bs * t_past, 2 * GQA_KV_HEADS * GQA_HEAD_DIM)
    q, k, v = _gqa_proj(xs, mods, gn1, wg, n_batch=bs, t_new=ts, tm=tm_proj, mod_row=sample_row_proj,
                        rope_tabs=rope_gqa, cache=cache1)
    o = _attention(q, k, v, mla=False, n_batch=bs, t_q=ts, t_k=t_past + ts, tq=512, n_pairs=1, name="attn_gqa_s")
    y_sample = _post(xs, o, mods, 1, wp1, tm=tm_post, mod_row=sample_row_post, final=True)

    return (y_prompt.reshape(bp, tp, D), y_sample.reshape(bs, ts, D),
            keys_p.reshape(bp, 1, tp, KV_DIM),
            kv_p.reshape(bp, 1, tp, 2, GQA_KV_HEADS, GQA_HEAD_DIM))
```

```python
import functools
import math

import jax
import jax.numpy as jnp
from jax import lax
from jax.experimental import pallas as pl
from jax.experimental.pallas import tpu as pltpu

F32 = jnp.float32
BF16 = jnp.bfloat16

D = 1024
EPS = 1e-6
GRID_W = 64
ROPE_THETA = 10000.0
LANES = 128
HALF = LANES // 2
MLA_HEADS = 16
MLA_NOPE = 64
MLA_ROPE = 32
MLA_V = 64
Q_LORA = 384
KV_LORA = 256
KV_DIM = KV_LORA + MLA_ROPE
KV_PAD = 384
MLA_SCALE = 1.0 / math.sqrt(MLA_NOPE + MLA_ROPE)
GQA_HEADS = 16
GQA_KV_HEADS = 4
GQA_HEAD_DIM = 64
GQA_SCALE = 1.0 / math.sqrt(GQA_HEAD_DIM)
N_GROUPS = 4
EXPERTS_PER_GROUP = 4
N_EXPERTS = 16
EXPERT_FF = 256
NEG = -3.0e38

VMEM_LIMIT = 56 * 1024 * 1024
N_MOD_ROWS = 8


def _params(n_axes):
    return pltpu.CompilerParams(dimension_semantics=("arbitrary",) * n_axes,
                                vmem_limit_bytes=VMEM_LIMIT)


def _rms(x, g):
    ms = jnp.mean(x * x, axis=-1, keepdims=True)
    return x * lax.rsqrt(ms + EPS) * g


def _mod_index(layer, which, row):
    return (layer * 6 + which) * N_MOD_ROWS + row


def _mod_spec(layer, which, row_fn):
    return pl.BlockSpec((None, 1, D), lambda *g: (_mod_index(layer, which, row_fn(*g)), 0, 0))


def _full_spec(shape):
    n = len(shape)
    return pl.BlockSpec(shape, lambda *g: (0,) * n)


def _ada_kernel(c_ref, w_ref, b_ref, o_ref):
    c = c_ref[...]
    a = c / (1.0 + jnp.exp(-c))
    o_ref[...] = jnp.dot(a, w_ref[...], precision=lax.Precision.HIGHEST,
                         preferred_element_type=F32) + b_ref[...]


def _ada_table(cc, ada_w, ada_b):
    depth = ada_w.shape[0]
    out = pl.pallas_call(
        _ada_kernel,
        grid=(depth, 6),
        in_specs=[
            pl.BlockSpec((N_MOD_ROWS, D), lambda l, n: (0, 0)),
            pl.BlockSpec((None, D, D), lambda l, n: (l, 0, n)),
            pl.BlockSpec((None, 1, D), lambda l, n: (l, 0, n)),
        ],
        out_specs=pl.BlockSpec((None, None, N_MOD_ROWS, D), lambda l, n: (l, n, 0, 0)),
        out_shape=jax.ShapeDtypeStruct((depth, 6, N_MOD_ROWS, D), F32),
        compiler_params=_params(2),
        name="ada",
    )(cc, ada_w, ada_b.reshape(depth, 1, 6 * D))
    return out.reshape(depth * 6 * N_MOD_ROWS, 1, D)


def _rope(a, cos, sin_signed):
    lane = lax.broadcasted_iota(jnp.int32, a.shape, 1)
    nxt = pltpu.roll(a, LANES - 1, 1)
    prv = pltpu.roll(a, 1, 1)
    return a * cos + jnp.where((lane & 1) == 0, nxt, prv) * sin_signed


def _rope_tables(n_tokens, rot_dim, lane_offset_pattern):
    t = jnp.arange(n_tokens)
    row = (t // GRID_W).astype(F32)
    col = (t % GRID_W).astype(F32)
    axis_dim = rot_dim // 2
    inv = jnp.power(ROPE_THETA, -jnp.arange(0, axis_dim, 2, dtype=F32) / axis_dim)
    ang = jnp.concatenate([row[:, None] * inv, col[:, None] * inv], axis=-1)
    cos = jnp.repeat(jnp.cos(ang), 2, axis=1)
    sin = jnp.repeat(jnp.sin(ang), 2, axis=1)
    sign = jnp.where(jnp.arange(rot_dim) % 2 == 0, -1.0, 1.0).astype(F32)
    sin = sin * sign
    cos_t = jnp.ones((n_tokens, LANES), F32)
    sin_t = jnp.zeros((n_tokens, LANES), F32)
    for off in lane_offset_pattern:
        cos_t = cos_t.at[:, off:off + rot_dim].set(cos)
        sin_t = sin_t.at[:, off:off + rot_dim].set(sin)
    return cos_t, sin_t


def _mla_proj_kernel(*refs, rope, n_cache, emit_keys):
    it = iter(refs)
    x_ref, sh_ref, sc_ref, gn_ref = next(it), next(it), next(it), next(it)
    wdq_ref, gq_ref, wuq_ref, wdkv_ref, gkv_ref, wkexp_ref, wuv_ref = (next(it) for _ in range(7))
    cos_ref = sin_ref = cache_ref = keys_ref = None
    if rope:
        cos_ref, sin_ref = next(it), next(it)
    if n_cache:
        cache_ref = next(it)
    q_ref, k_ref, v_ref = next(it), next(it), next(it)
    if emit_keys:
        keys_ref = next(it)

    def expand(keys_bf16):
        k_ref[...] = jnp.dot(keys_bf16, wkexp_ref[...], preferred_element_type=F32).astype(BF16)
        v_ref[...] = jnp.dot(keys_bf16[:, :KV_LORA], wuv_ref[...], preferred_element_type=F32).astype(BF16)

    def new_tokens():
        x = x_ref[...]
        h = (_rms(x, gn_ref[...]) * (1.0 + sc_ref[...]) + sh_ref[...]).astype(BF16)
        ql = jnp.dot(h, wdq_ref[...], preferred_element_type=F32)
        qn = _rms(ql, gq_ref[...]).astype(BF16)
        q = jnp.dot(qn, wuq_ref[...], preferred_element_type=F32)
        kv = jnp.dot(h, wdkv_ref[...], preferred_element_type=F32)
        ckv = _rms(kv[:, :KV_LORA], gkv_ref[...])
        pe = kv[:, KV_LORA:]
        if rope:
            cos, sin = cos_ref[...], sin_ref[...]
            pe = _rope(pe, cos, sin)
            q = jnp.concatenate(
                [_rope(q[:, h_ * LANES:(h_ + 1) * LANES], cos, sin) for h_ in range(MLA_HEADS)], axis=1)
        q_ref[...] = (q * MLA_SCALE).astype(BF16)
        if emit_keys:
            keys_ref[:, 0:KV_LORA] = ckv
            keys_ref[:, KV_LORA:KV_DIM] = pe[:, 0:MLA_ROPE]
        expand(jnp.concatenate([ckv, pe], axis=1).astype(BF16))

    if n_cache:
        t = pl.program_id(1)
        pl.when(t >= n_cache)(new_tokens)

        @pl.when(t < n_cache)
        def _():
            expand(cache_ref[...].astype(BF16))
    else:
        new_tokens()


def _mla_proj(x2d, mods, gn, w, *, n_batch, t_new, tm, mod_row, rope_tabs=None, cache=None):
    rope = rope_tabs is not None
    t_cache = 0 if cache is None else cache.shape[0] // n_batch
    n_cache = t_cache // tm
    ntn = t_new // tm
    nt = n_cache + ntn
    emit_keys = cache is None

    def new_idx(b, t):
        return b * ntn + jnp.maximum(t - n_cache, 0)

    row_fn = lambda b, t: mod_row(b)
    in_specs = [
        pl.BlockSpec((tm, D), lambda b, t: (new_idx(b, t), 0)),
        _mod_spec(0, 0, row_fn), _mod_spec(0, 1, row_fn),
        _full_spec((1, D)),
        _full_spec((D, Q_LORA)), _full_spec((1, Q_LORA)), _full_spec((Q_LORA, MLA_HEADS * LANES)),
        _full_spec((D, KV_PAD)), _full_spec((1, KV_LORA)),
        _full_spec((KV_PAD, MLA_HEADS * LANES)), _full_spec((KV_LORA, MLA_HEADS * MLA_V)),
    ]
    args = [x2d, mods, mods, gn, w["wdq"], w["gq"], w["wuq"], w["wdkv"], w["gkv"], w["wkexp"], w["wuv"]]
    if rope:
        in_specs += [pl.BlockSpec((tm, LANES), lambda b, t: (jnp.maximum(t - n_cache, 0), 0))] * 2
        args += list(rope_tabs)
    if n_cache:
        in_specs.append(pl.BlockSpec((tm, KV_PAD), lambda b, t: (b * n_cache + jnp.minimum(t, n_cache - 1), 0)))
        args.append(cache)
    n_new = n_batch * t_new
    n_keys = n_batch * (t_cache + t_new)
    out_specs = [
        pl.BlockSpec((tm, MLA_HEADS * LANES), lambda b, t: (new_idx(b, t), 0)),
        pl.BlockSpec((tm, MLA_HEADS * LANES), lambda b, t: (b * nt + t, 0)),
        pl.BlockSpec((tm, MLA_HEADS * MLA_V), lambda b, t: (b * nt + t, 0)),
    ]
    out_shape = [
        jax.ShapeDtypeStruct((n_new, MLA_HEADS * LANES), BF16),
        jax.ShapeDtypeStruct((n_keys, MLA_HEADS * LANES), BF16),
        jax.ShapeDtypeStruct((n_keys, MLA_HEADS * MLA_V), BF16),
    ]
    if emit_keys:
        out_specs.append(pl.BlockSpec((tm, KV_DIM), lambda b, t: (new_idx(b, t), 0)))
        out_shape.append(jax.ShapeDtypeStruct((n_new, KV_DIM), F32))
    return pl.pallas_call(
        functools.partial(_mla_proj_kernel, rope=rope, n_cache=n_cache, emit_keys=emit_keys),
        grid=(n_batch, nt), in_specs=in_specs, out_specs=out_specs, out_shape=out_shape,
        compiler_params=_params(2), name="mla_proj_s" if rope else "mla_proj_p",
    )(*args)


def _dup_halves(a):
    cols = []
    for c in range(a.shape[1] // LANES):
        blk = a[:, c * LANES:(c + 1) * LANES]
        rot = pltpu.roll(blk, HALF, 1)
        low = lax.broadcasted_iota(jnp.int32, blk.shape, 1) < HALF
        cols += [jnp.where(low, blk, rot), jnp.where(low, rot, blk)]
    return jnp.concatenate(cols, axis=1)


def _group_mean_sq(a, bmat):
    sq = a * a
    hi = sq.astype(BF16)
    lo = (sq - hi.astype(F32)).astype(BF16)
    return (jnp.dot(hi, bmat, preferred_element_type=F32) + jnp.dot(lo, bmat, preferred_element_type=F32))


def _gqa_proj_kernel(*refs, rope, n_cache, emit_kv):
    it = iter(refs)
    x_ref, sh_ref, sc_ref, gn_ref, w_ref, gq_ref, gk_ref, bmat_ref = (next(it) for _ in range(8))
    cos_ref = sin_ref = cache_ref = kv_ref = None
    if rope:
        cos_ref, sin_ref = next(it), next(it)
    if n_cache:
        cache_ref = next(it)
    q_ref, k_ref, v_ref = next(it), next(it), next(it)
    if emit_kv:
        kv_ref = next(it)
    nq = GQA_HEADS * GQA_HEAD_DIM
    nk = GQA_KV_HEADS * GQA_HEAD_DIM
    blk = 2 * LANES

    def new_tokens():
        x = x_ref[...]
        h = (_rms(x, gn_ref[...]) * (1.0 + sc_ref[...]) + sh_ref[...]).astype(BF16)
        qkv = jnp.dot(h, w_ref[...], preferred_element_type=F32)
        bmat = bmat_ref[...]

        def head_norm(a, g):
            return a * lax.rsqrt(_group_mean_sq(a, bmat) + EPS) * g

        def maybe_rope(a):
            if not rope:
                return a
            cos, sin = cos_ref[...], sin_ref[...]
            return jnp.concatenate(
                [_rope(a[:, c * LANES:(c + 1) * LANES], cos, sin) for c in range(a.shape[1] // LANES)], axis=1)

        for c in range(nq // blk):
            qb = head_norm(qkv[:, c * blk:(c + 1) * blk], gq_ref[...])
            q_ref[:, c * blk:(c + 1) * blk] = (maybe_rope(qb) * GQA_SCALE).astype(BF16)
        kn = head_norm(qkv[:, nq:nq + nk], gk_ref[...])
        vv = qkv[:, nq + nk:]
        if emit_kv:
            kv_ref[:, 0:nk] = kn
            kv_ref[:, nk:2 * nk] = vv
        k_ref[...] = _dup_halves(maybe_rope(kn)).astype(BF16)
        v_ref[...] = _dup_halves(vv).astype(BF16)

    if n_cache:
        t = pl.program_id(1)
        pl.when(t >= n_cache)(new_tokens)

        @pl.when(t < n_cache)
        def _():
            c = cache_ref[...]
            k_ref[...] = _dup_halves(c[:, 0:nk]).astype(BF16)
            v_ref[...] = _dup_halves(c[:, nk:2 * nk]).astype(BF16)
    else:
        new_tokens()


def _gqa_proj(x2d, mods, gn, w, *, n_batch, t_new, tm, mod_row, rope_tabs=None, cache=None):
    rope = rope_tabs is not None
    t_cache = 0 if cache is None else cache.shape[0] // n_batch
    n_cache = t_cache // tm
    ntn = t_new // tm
    nt = n_cache + ntn
    emit_kv = cache is None
    nq = GQA_HEADS * GQA_HEAD_DIM
    nk = GQA_KV_HEADS * GQA_HEAD_DIM
    nqkv = nq + 2 * nk
    ndup = GQA_KV_HEADS * LANES

    def new_idx(b, t):
        return b * ntn + jnp.maximum(t - n_cache, 0)

    row_fn = lambda b, t: mod_row(b)
    in_specs = [
        pl.BlockSpec((tm, D), lambda b, t: (new_idx(b, t), 0)),
        _mod_spec(1, 0, row_fn), _mod_spec(1, 1, row_fn),
        _full_spec((1, D)), _full_spec((D, nqkv)),
        _full_spec((1, 2 * LANES)), _full_spec((1, 2 * LANES)), _full_spec((2 * LANES, 2 * LANES)),
    ]
    args = [x2d, mods, mods, gn, w["wqkv"], w["gq"], w["gk"], w["bmat"]]
    if rope:
        in_specs += [pl.BlockSpec((tm, LANES), lambda b, t: (jnp.maximum(t - n_cache, 0), 0))] * 2
        args += list(rope_tabs)
    if n_cache:
        in_specs.append(pl.BlockSpec((tm, 2 * nk), lambda b, t: (b * n_cache + jnp.minimum(t, n_cache - 1), 0)))
        args.append(cache)
    n_new = n_batch * t_new
    n_keys = n_batch * (t_cache + t_new)
    out_specs = [
        pl.BlockSpec((tm, nq), lambda b, t: (new_idx(b, t), 0)),
        pl.BlockSpec((tm, ndup), lambda b, t: (b * nt + t, 0)),
        pl.BlockSpec((tm, ndup), lambda b, t: (b * nt + t, 0)),
    ]
    out_shape = [
        jax.ShapeDtypeStruct((n_new, nq), BF16),
        jax.ShapeDtypeStruct((n_keys, ndup), BF16),
        jax.ShapeDtypeStruct((n_keys, ndup), BF16),
    ]
    if emit_kv:
        out_specs.append(pl.BlockSpec((tm, 2 * nk), lambda b, t: (new_idx(b, t), 0)))
        out_shape.append(jax.ShapeDtypeStruct((n_new, 2 * nk), F32))
    return pl.pallas_call(
        functools.partial(_gqa_proj_kernel, rope=rope, n_cache=n_cache, emit_kv=emit_kv),
        grid=(n_batch, nt), in_specs=in_specs, out_specs=out_specs, out_shape=out_shape,
        compiler_params=_params(2), name="gqa_proj_s" if rope else "gqa_proj_p",
    )(*args)


def _attn_kernel(q_ref, k_ref, v_ref, o_ref, *, mla, n_pairs):
    tq = q_ref.shape[0]
    lane = lax.broadcasted_iota(jnp.int32, (tq, LANES), 1)
    low = lane < HALF
    for p in range(n_pairs):
        outs = []
        for hh in range(2):
            if mla:
                hd = 2 * p + hh
                qh = q_ref[:, hd * LANES:(hd + 1) * LANES]
                kh = k_ref[:, hd * LANES:(hd + 1) * LANES]
                vb = v_ref[:, p * LANES:(p + 1) * LANES]
            else:
                g = p // 2 if n_pairs > 1 else 0
                qp = q_ref[:, p * LANES:(p + 1) * LANES]
                qh = jnp.where(low if hh == 0 else jnp.logical_not(low), qp, jnp.zeros_like(qp))
                kh = k_ref[:, g * LANES:(g + 1) * LANES]
                vb = v_ref[:, g * LANES:(g + 1) * LANES]
            s = lax.dot_general(qh, kh, (((1,), (1,)), ((), ())), preferred_element_type=F32)
            m = jnp.max(s, axis=-1, keepdims=True)
            e = jnp.exp(s - m)
            l = jnp.sum(e, axis=-1, keepdims=True)
            o = jnp.dot(e.astype(BF16), vb, preferred_element_type=F32)
            outs.append(o / l)
        o_ref[:, p * LANES:(p + 1) * LANES] = jnp.where(low, outs[0], outs[1]).astype(BF16)


def _attention(q, k, v, *, mla, n_batch, t_q, t_k, tq, n_pairs, name):
    total_pairs = 8
    nj = total_pairs // n_pairs
    nqt = t_q // tq
    if mla:
        qw, kw, vw = n_pairs * 2 * LANES, n_pairs * 2 * LANES, n_pairs * LANES
        kv_col = lambda j: j
    else:
        qw = n_pairs * LANES
        kw = vw = max(n_pairs // 2, 1) * LANES
        kv_col = (lambda j: j // 2) if n_pairs == 1 else (lambda j: j)
    return pl.pallas_call(
        functools.partial(_attn_kernel, mla=mla, n_pairs=n_pairs),
        grid=(n_batch, nj, nqt),
        in_specs=[
            pl.BlockSpec((tq, qw), lambda b, j, i: (b * nqt + i, j)),
            pl.BlockSpec((t_k, kw), lambda b, j, i: (b, kv_col(j))),
            pl.BlockSpec((t_k, vw), lambda b, j, i: (b, kv_col(j))),
        ],
        out_specs=pl.BlockSpec((tq, n_pairs * LANES), lambda b, j, i: (b * nqt + i, j)),
        out_shape=jax.ShapeDtypeStruct((n_batch * t_q, total_pairs * LANES), BF16),
        compiler_params=_params(3), name=name,
    )(q, k, v)


def _route(logits):
    lane = lax.broadcasted_iota(jnp.int32, logits.shape, 1).astype(F32)
    big = jnp.float32(1e9)
    is_grp = (lane >= N_EXPERTS) & (lane < N_EXPERTS + N_GROUPS)
    gl = jnp.where(is_grp, logits, NEG)
    gm = jnp.max(gl, axis=-1, keepdims=True)
    g_w = 1.0 / jnp.sum(jnp.exp(gl - gm), axis=-1, keepdims=True)
    g_idx = jnp.min(jnp.where(gl == gm, lane, big), axis=-1, keepdims=True) - N_EXPERTS
    lo = g_idx * EXPERTS_PER_GROUP
    el = jnp.where((lane >= lo) & (lane < lo + EXPERTS_PER_GROUP), logits, NEG)
    m1 = jnp.max(el, axis=-1, keepdims=True)
    i1 = jnp.min(jnp.where(el == m1, lane, big), axis=-1, keepdims=True)
    el2 = jnp.where(lane == i1, NEG, el)
    m2 = jnp.max(el2, axis=-1, keepdims=True)
    i2 = jnp.min(jnp.where(el2 == m2, lane, big), axis=-1, keepdims=True)
    t = jnp.exp(m2 - m1)
    w1 = g_w / (1.0 + t)
    w2 = g_w * t / (1.0 + t)
    return jnp.where(lane == i1, w1, 0.0) + jnp.where(lane == i2, w2, 0.0), g_idx


ROW_W = D + LANES
TM_EXPERT = 256
DISPATCH_CHUNK = 256


def _route_kernel(x_ref, o_ref, wo_ref, gta_ref, shm_ref, scm_ref, gn_ref,
                  wr_hi_ref, wr_lo_ref, br_ref, sel_ref, xm_ref, row_ref, gidx_ref):
    mix = jnp.dot(o_ref[...], wo_ref[...], preferred_element_type=F32)
    xm = x_ref[...] + gta_ref[...] * mix
    xm_ref[...] = xm
    h = _rms(xm, gn_ref[...]) * (1.0 + scm_ref[...]) + shm_ref[...]
    h_hi = h.astype(BF16)
    h_lo = (h - h_hi.astype(F32)).astype(BF16)
    logits = (jnp.dot(h_hi, wr_hi_ref[...], preferred_element_type=F32)
              + jnp.dot(h_lo, wr_hi_ref[...], preferred_element_type=F32)
              + jnp.dot(h_hi, wr_lo_ref[...], preferred_element_type=F32)) + br_ref[...]
    comb, g_idx = _route(logits)
    row_ref[:, 0:D] = h
    row_ref[:, D:ROW_W] = comb
    lane = lax.broadcasted_iota(jnp.int32, comb.shape, 1)
    g_mat = jnp.where(lane == 0, g_idx, 0.0).astype(BF16)
    g_row = lax.dot_general(sel_ref[...], g_mat, (((1,), (1,)), ((), ())), preferred_element_type=F32)
    gidx_ref[...] = g_row[0:1, :].astype(jnp.int32)


def _dispatch_kernel(gidx_ref, row_hbm, hs_hbm, dest_ref, cnt_ref, sem, *, n, cap):
    n_chunks = n // DISPATCH_CHUNK

    def wait_chunk(slot):
        pltpu.make_async_copy(row_hbm.at[pl.ds(0, DISPATCH_CHUNK)], hs_hbm.at[pl.ds(0, DISPATCH_CHUNK)],
                              sem.at[slot]).wait()

    def chunk_body(ci, counts):
        slot = ci % 2

        def tok_body(t, counts):
            tok = ci * DISPATCH_CHUNK + t
            g = gidx_ref[tok]
            c = counts[0]
            for k in range(1, N_GROUPS):
                c = jnp.where(g == k, counts[k], c)
            d = g * cap + c
            dest_ref[tok] = d
            pltpu.make_async_copy(row_hbm.at[pl.ds(tok, 1)], hs_hbm.at[pl.ds(d, 1)], sem.at[slot]).start()
            return tuple(counts[k] + (g == k).astype(jnp.int32) for k in range(N_GROUPS))

        counts = lax.fori_loop(0, DISPATCH_CHUNK, tok_body, counts)

        @pl.when(ci > 0)
        def _():
            wait_chunk(1 - slot)

        return counts

    zero = jnp.int32(0)
    counts = lax.fori_loop(0, n_chunks, chunk_body, (zero,) * N_GROUPS)
    wait_chunk((n_chunks - 1) % 2)
    for k in range(N_GROUPS):
        cnt_ref[k] = counts[k]


def _experts_kernel(blk_ref, grp_ref, first_ref, nt_ref, hs_ref, wg_ref, wu_ref, wd_ref, ys_ref,
                    wgb_ref, wub_ref, wdb_ref):
    i = pl.program_id(0)

    @pl.when(i < nt_ref[0])
    def _():
        @pl.when(first_ref[i] == 1)
        def _():
            for j in range(EXPERTS_PER_GROUP):
                wgb_ref[j] = wg_ref[j].astype(BF16)
                wub_ref[j] = wu_ref[j].astype(BF16)
                wdb_ref[j * EXPERT_FF:(j + 1) * EXPERT_FF, :] = wd_ref[j].astype(BF16)

        x = hs_ref[:, 0:D].astype(BF16)
        comb = hs_ref[:, D:ROW_W]
        lane = lax.broadcasted_iota(jnp.int32, comb.shape, 1)
        e0 = grp_ref[i] * EXPERTS_PER_GROUP
        hids = []
        for j in range(EXPERTS_PER_GROUP):
            g = jnp.dot(x, wgb_ref[j], preferred_element_type=F32)
            u = jnp.dot(x, wub_ref[j], preferred_element_type=F32)
            c = jnp.sum(jnp.where(lane == e0 + j, comb, 0.0), axis=-1, keepdims=True)
            hids.append(((g / (1.0 + jnp.exp(-g))) * u * c).astype(BF16))
        hid = jnp.concatenate(hids, axis=1)
        ys_ref[...] = jnp.dot(hid, wdb_ref[...], preferred_element_type=F32)


def _combine_kernel(dest_ref, xm_ref, gtm_ref, fin_ref, ys_hbm, y_ref, ybuf, sem, *, tm, final):
    i = pl.program_id(0)
    slot = i % 2

    def issue(tile, s):
        def body(t, carry):
            d = dest_ref[tile * tm + t]
            pltpu.make_async_copy(ys_hbm.at[pl.ds(d, 1)], ybuf.at[s, pl.ds(t, 1)], sem.at[s]).start()
            return carry
        lax.fori_loop(0, tm, body, 0)

    @pl.when(i == 0)
    def _():
        issue(0, 0)

    @pl.when(i + 1 < pl.num_programs(0))
    def _():
        issue(i + 1, 1 - slot)

    pltpu.make_async_copy(ys_hbm.at[pl.ds(0, tm)], ybuf.at[slot], sem.at[slot]).wait()
    out = xm_ref[...] + gtm_ref[...] * ybuf[slot]
    if final:
        out = _rms(out, fin_ref[...])
    y_ref[...] = out


def _tile_table(counts, cap, max_tiles):
    nt_g = (counts + TM_EXPERT - 1) // TM_EXPERT
    ends = jnp.cumsum(nt_g)
    starts = ends - nt_g
    total = ends[-1]
    i = jnp.arange(max_tiles, dtype=jnp.int32)
    iv = jnp.minimum(i, total - 1)
    grp = jnp.sum((iv[:, None] >= ends[None, :]).astype(jnp.int32), axis=1)
    blk = grp * (cap // TM_EXPERT) + iv - starts[grp]
    first = ((iv == starts[grp]) & (i < total)).astype(jnp.int32)
    return blk.astype(jnp.int32), grp.astype(jnp.int32), first, total.reshape(1).astype(jnp.int32)


def _post(x2d, o2d, mods, layer, w, *, tm, mod_row, final):
    n = x2d.shape[0]
    nt = n // tm
    cap = n
    row_fn = lambda i, *_: mod_row(i)
    tile = pl.BlockSpec((tm, D), lambda i: (i, 0))
    xm, rows, gidx = pl.pallas_call(
        _route_kernel,
        grid=(nt,),
        in_specs=[
            tile, tile, _full_spec((D, D)),
            _mod_spec(layer, 2, row_fn), _mod_spec(layer, 3, row_fn), _mod_spec(layer, 4, row_fn),
            _full_spec((1, D)),
            _full_spec((D, LANES)), _full_spec((D, LANES)), _full_spec((1, LANES)), _full_spec((8, LANES)),
        ],
        out_specs=[tile, pl.BlockSpec((tm, ROW_W), lambda i: (i, 0)), pl.BlockSpec((None, 1, tm), lambda i: (i, 0, 0))],
        out_shape=[jax.ShapeDtypeStruct((n, D), F32), jax.ShapeDtypeStruct((n, ROW_W), F32),
                   jax.ShapeDtypeStruct((nt, 1, tm), jnp.int32)],
        compiler_params=_params(1), name=f"route_l{layer}",
    )(x2d, o2d, w["wo"], mods, mods, mods, w["gn"], w["wr_hi"], w["wr_lo"], w["br"], w["sel"])

    hs, dest, counts = pl.pallas_call(
        functools.partial(_dispatch_kernel, n=n, cap=cap),
        grid_spec=pltpu.PrefetchScalarGridSpec(
            num_scalar_prefetch=1, grid=(1,),
            in_specs=[pl.BlockSpec(memory_space=pl.ANY)],
            out_specs=[pl.BlockSpec(memory_space=pl.ANY), pl.BlockSpec(memory_space=pltpu.SMEM),
                       pl.BlockSpec(memory_space=pltpu.SMEM)],
            scratch_shapes=[pltpu.SemaphoreType.DMA((2,))]),
        out_shape=[jax.ShapeDtypeStruct((N_GROUPS * cap, ROW_W), F32), jax.ShapeDtypeStruct((n,), jnp.int32),
                   jax.ShapeDtypeStruct((N_GROUPS,), jnp.int32)],
        compiler_params=_params(1), name=f"dispatch_l{layer}",
    )(gidx.reshape(n), rows)

    max_tiles = n // TM_EXPERT + N_GROUPS
    blk, grp, first, total = _tile_table(counts, cap, max_tiles)
    wspec = lambda shape: pl.BlockSpec(shape, lambda i, blk, grp, first, nt_: (grp[i], 0, 0))
    ys = pl.pallas_call(
        _experts_kernel,
        grid_spec=pltpu.PrefetchScalarGridSpec(
            num_scalar_prefetch=4, grid=(max_tiles,),
            in_specs=[
                pl.BlockSpec((TM_EXPERT, ROW_W), lambda i, blk, grp, first, nt_: (blk[i], 0)),
                wspec((EXPERTS_PER_GROUP, D, EXPERT_FF)), wspec((EXPERTS_PER_GROUP, D, EXPERT_FF)),
                wspec((EXPERTS_PER_GROUP, EXPERT_FF, D)),
            ],
            out_specs=pl.BlockSpec((TM_EXPERT, D), lambda i, blk, grp, first, nt_: (blk[i], 0)),
            scratch_shapes=[pltpu.VMEM((EXPERTS_PER_GROUP, D, EXPERT_FF), BF16),
                            pltpu.VMEM((EXPERTS_PER_GROUP, D, EXPERT_FF), BF16),
                            pltpu.VMEM((EXPERTS_PER_GROUP * EXPERT_FF, D), BF16)]),
        out_shape=jax.ShapeDtypeStruct((N_GROUPS * cap, D), F32),
        compiler_params=_params(1), name=f"experts_l{layer}",
    )(blk, grp, first, total, hs, w["wg"], w["wu"], w["wd"])

    return pl.pallas_call(
        functools.partial(_combine_kernel, tm=tm, final=final),
        grid_spec=pltpu.PrefetchScalarGridSpec(
            num_scalar_prefetch=1, grid=(nt,),
            in_specs=[pl.BlockSpec((tm, D), lambda i, dest_: (i, 0)), _mod_spec(layer, 5, row_fn),
                      _full_spec((1, D)), pl.BlockSpec(memory_space=pl.ANY)],
            out_specs=pl.BlockSpec((tm, D), lambda i, dest_: (i, 0)),
            scratch_shapes=[pltpu.VMEM((2, tm, D), F32), pltpu.SemaphoreType.DMA((2,))]),
        out_shape=jax.ShapeDtypeStruct((n, D), F32),
        compiler_params=_params(1), name=f"combine_l{layer}",
    )(dest, xm, mods, w["fin"], ys)


def _mla_weights(w_dq, g_q, w_uq, w_dkv, g_kv, w_uk, w_uv):
    hd = MLA_NOPE + MLA_ROPE
    wuq = w_uq.reshape(Q_LORA, MLA_HEADS, hd)
    wuq = jnp.concatenate([wuq[..., MLA_NOPE:], wuq[..., :MLA_NOPE],
                           jnp.zeros((Q_LORA, MLA_HEADS, LANES - hd), F32)], axis=-1)
    wuk = jnp.concatenate([jnp.zeros((KV_LORA, MLA_HEADS, MLA_ROPE), F32), w_uk,
                           jnp.zeros((KV_LORA, MLA_HEADS, LANES - hd), F32)], axis=-1)
    pe_rows = jnp.broadcast_to(jnp.eye(MLA_ROPE, LANES, dtype=F32)[:, None, :], (MLA_ROPE, MLA_HEADS, LANES))
    wkexp = jnp.concatenate([wuk, pe_rows, jnp.zeros((KV_PAD - KV_DIM, MLA_HEADS, LANES), F32)], axis=0)
    return {
        "wdq": w_dq.astype(BF16), "gq": g_q.reshape(1, Q_LORA),
        "wuq": wuq.reshape(Q_LORA, MLA_HEADS * LANES).astype(BF16),
        "wdkv": jnp.pad(w_dkv, ((0, 0), (0, KV_PAD - KV_DIM))).astype(BF16), "gkv": g_kv.reshape(1, KV_LORA),
        "wkexp": wkexp.reshape(KV_PAD, MLA_HEADS * LANES).astype(BF16),
        "wuv": w_uv.reshape(KV_LORA, MLA_HEADS * MLA_V).astype(BF16),
    }


def _gqa_weights(w_qkv, g_q, g_k):
    grp = jnp.arange(2 * LANES) // GQA_HEAD_DIM
    bmat = jnp.where(grp[:, None] == grp[None, :], 1.0 / GQA_HEAD_DIM, 0.0).astype(BF16)
    reps = 2 * LANES // GQA_HEAD_DIM
    return {"wqkv": w_qkv.astype(BF16), "gq": jnp.tile(g_q, reps).reshape(1, 2 * LANES),
            "gk": jnp.tile(g_k, reps).reshape(1, 2 * LANES), "bmat": bmat}


def _post_weights(l, w_o, norm_ffn, w_group, b_group, w_exp, b_exp, w_gate, w_up, w_down, final_norm):
    wr = jnp.concatenate([w_exp[l], w_group[l], jnp.zeros((D, LANES - N_EXPERTS - N_GROUPS), F32)], axis=1)
    br = jnp.concatenate([b_exp[l], b_group[l], jnp.zeros((LANES - N_EXPERTS - N_GROUPS,), F32)]).reshape(1, LANES)
    wr_hi = wr.astype(BF16)
    wr_lo = (wr - wr_hi.astype(F32)).astype(BF16)
    return {"wo": w_o.astype(BF16), "gn": norm_ffn[l].reshape(1, D), "wr_hi": wr_hi, "wr_lo": wr_lo, "br": br,
            "wg": w_gate[l], "wu": w_up[l], "wd": w_down[l], "fin": final_norm.reshape(1, D),
            "sel": jnp.zeros((8, LANES), BF16).at[0, 0].set(1.0)}


def kernel(x_prompt, x_sample, cache_mla, cache_gqa, c, c_ctx, ada_w, ada_b, norm_mix, norm_ffn,
           mla_w_dq, mla_q_norm, mla_w_uq, mla_w_dkv, mla_kv_norm, mla_w_uk, mla_w_uv, mla_w_o,
           gqa_w_qkv, gqa_q_norm, gqa_k_norm, gqa_w_o,
           moe_w_group, moe_b_group, moe_w_expert, moe_b_expert, moe_w_gate, moe_w_up, moe_w_down,
           final_norm):
    bp, tp, _ = x_prompt.shape
    bs, ts, _ = x_sample.shape
    t_past = cache_mla.shape[2]
    assert ada_w.shape[0] == 2 and 1 + bs <= N_MOD_ROWS
    n_p, n_s = bp * tp, bs * ts

    cc = jnp.concatenate([c_ctx[None], c, jnp.zeros((N_MOD_ROWS - 1 - bs, D), F32)], axis=0)
    mods = _ada_table(cc, ada_w, ada_b)

    xp = x_prompt.reshape(n_p, D)
    xs = x_sample.reshape(n_s, D)
    tm_proj = 512
    tm_post = 512
    prompt_row = lambda *_: 0
    sample_row_proj = lambda b: 1 + b
    sample_row_post = lambda i: 1 + i // (ts // tm_post)
    moe_args = (norm_ffn, moe_w_group, moe_b_group, moe_w_expert, moe_b_expert,
                moe_w_gate, moe_w_up, moe_w_down, final_norm)

    wm = _mla_weights(mla_w_dq[0], mla_q_norm[0], mla_w_uq[0], mla_w_dkv[0], mla_kv_norm[0],
                      mla_w_uk[0], mla_w_uv[0])
    gn0 = norm_mix[0].reshape(1, D)
    q, k, v, keys_p = _mla_proj(xp, mods, gn0, wm, n_batch=1, t_new=n_p, tm=tm_proj, mod_row=prompt_row)
    o = _attention(q, k, v, mla=True, n_batch=bp, t_q=tp, t_k=tp, tq=tp, n_pairs=8, name="attn_mla_p")
    wp0 = _post_weights(0, mla_w_o[0], *moe_args)
    xp = _post(xp, o, mods, 0, wp0, tm=tm_post, mod_row=prompt_row, final=False)

    rope_mla = _rope_tables(ts, MLA_ROPE, (0,))
    cache0 = jnp.pad(cache_mla[:, 0].reshape(bs * t_past, KV_DIM), ((0, 0), (0, KV_PAD - KV_DIM)))
    q, k, v = _mla_proj(xs, mods, gn0, wm, n_batch=bs, t_new=ts, tm=tm_proj, mod_row=sample_row_proj,
                        rope_tabs=rope_mla, cache=cache0)
    o = _attention(q, k, v, mla=True, n_batch=bs, t_q=ts, t_k=t_past + ts, tq=512, n_pairs=1, name="attn_mla_s")
    xs = _post(xs, o, mods, 0, wp0, tm=tm_post, mod_row=sample_row_post, final=False)

    wg = _gqa_weights(gqa_w_qkv[0], gqa_q_norm[0], gqa_k_norm[0])
    gn1 = norm_mix[1].reshape(1, D)
    q, k, v, kv_p = _gqa_proj(xp, mods, gn1, wg, n_batch=1, t_new=n_p, tm=tm_proj, mod_row=prompt_row)
    o = _attention(q, k, v, mla=False, n_batch=bp, t_q=tp, t_k=tp, tq=tp, n_pairs=8, name="attn_gqa_p")
    wp1 = _post_weights(1, gqa_w_o[0], *moe_args)
    y_prompt = _post(xp, o, mods, 1, wp1, tm=tm_post, mod_row=prompt_row, final=True)

    rope_gqa = _rope_tables(ts, GQA_HEAD_DIM, (0, GQA_HEAD_DIM))
    cache1 = cache_gqa[:, 0].reshape(bs * t_past, 2 * GQA_KV_HEADS * GQA_HEAD_DIM)
    q, k, v = _gqa_proj(xs, mods, gn1, wg, n_batch=bs, t_new=ts, tm=tm_proj, mod_row=sample_row_proj,
                        rope_tabs=rope_gqa, cache=cache1)
    o = _attention(q, k, v, mla=False, n_batch=bs, t_q=ts, t_k=t_past + ts, tq=512, n_pairs=1, name="attn_gqa_s")
    y_sample = _post(xs, o, mods, 1, wp1, tm=tm_post, mod_row=sample_row_post, final=True)

    return (y_prompt.reshape(bp, tp, D), y_sample.reshape(bs, ts, D),
            keys_p.reshape(bp, 1, tp, KV_DIM),
            kv_p.reshape(bp, 1, tp, 2, GQA_KV_HEADS, GQA_HEAD_DIM))
```

```python
import functools
import math

import jax
import jax.numpy as jnp
from jax import lax
from jax.experimental import pallas as pl
from jax.experimental.pallas import tpu as pltpu

F32 = jnp.float32
BF16 = jnp.bfloat16

D = 1024
EPS = 1e-6
GRID_W = 64
ROPE_THETA = 10000.0
LANES = 128
HALF = LANES // 2
MLA_HEADS = 16
MLA_NOPE = 64
MLA_ROPE = 32
MLA_V = 64
Q_LORA = 384
KV_LORA = 256
KV_DIM = KV_LORA + MLA_ROPE
KV_PAD = 384
MLA_SCALE = 1.0 / math.sqrt(MLA_NOPE + MLA_ROPE)
GQA_HEADS = 16
GQA_KV_HEADS = 4
GQA_HEAD_DIM = 64
GQA_SCALE = 1.0 / math.sqrt(GQA_HEAD_DIM)
N_GROUPS = 4
EXPERTS_PER_GROUP = 4
N_EXPERTS = 16
EXPERT_FF = 256
NEG = -3.0e38
LOG2E = 1.4426950408889634
ATTN_KEY_CHUNK = 512

VMEM_LIMIT = 56 * 1024 * 1024
N_MOD_ROWS = 8


def _params(n_axes):
    return pltpu.CompilerParams(dimension_semantics=("arbitrary",) * n_axes,
                                vmem_limit_bytes=VMEM_LIMIT)


def _rms(x, g):
    ms = jnp.mean(x * x, axis=-1, keepdims=True)
    return x * lax.rsqrt(ms + EPS) * g


def _mod_index(layer, which, row):
    return (layer * 6 + which) * N_MOD_ROWS + row


def _mod_spec(layer, which, row_fn):
    return pl.BlockSpec((None, 1, D), lambda *g: (_mod_index(layer, which, row_fn(*g)), 0, 0))


def _full_spec(shape):
    n = len(shape)
    return pl.BlockSpec(shape, lambda *g: (0,) * n)


def _ada_kernel(c_ref, w_ref, b_ref, o_ref):
    c = c_ref[...]
    a = c / (1.0 + jnp.exp(-c))
    o_ref[...] = jnp.dot(a, w_ref[...], precision=lax.Precision.HIGHEST,
                         preferred_element_type=F32) + b_ref[...]


def _ada_table(cc, ada_w, ada_b):
    depth = ada_w.shape[0]
    out = pl.pallas_call(
        _ada_kernel,
        grid=(depth, 6),
        in_specs=[
            pl.BlockSpec((N_MOD_ROWS, D), lambda l, n: (0, 0)),
            pl.BlockSpec((None, D, D), lambda l, n: (l, 0, n)),
            pl.BlockSpec((None, 1, D), lambda l, n: (l, 0, n)),
        ],
        out_specs=pl.BlockSpec((None, None, N_MOD_ROWS, D), lambda l, n: (l, n, 0, 0)),
        out_shape=jax.ShapeDtypeStruct((depth, 6, N_MOD_ROWS, D), F32),
        compiler_params=_params(2),
        name="ada",
    )(cc, ada_w, ada_b.reshape(depth, 1, 6 * D))
    return out.reshape(depth * 6 * N_MOD_ROWS, 1, D)


def _rope(a, cos, sin_signed):
    lane = lax.broadcasted_iota(jnp.int32, a.shape, 1)
    nxt = pltpu.roll(a, LANES - 1, 1)
    prv = pltpu.roll(a, 1, 1)
    return a * cos + jnp.where((lane & 1) == 0, nxt, prv) * sin_signed


def _rope_tables(n_tokens, rot_dim, lane_offset_pattern):
    t = jnp.arange(n_tokens)
    row = (t // GRID_W).astype(F32)
    col = (t % GRID_W).astype(F32)
    axis_dim = rot_dim // 2
    inv = jnp.power(ROPE_THETA, -jnp.arange(0, axis_dim, 2, dtype=F32) / axis_dim)
    ang = jnp.concatenate([row[:, None] * inv, col[:, None] * inv], axis=-1)
    cos = jnp.repeat(jnp.cos(ang), 2, axis=1)
    sin = jnp.repeat(jnp.sin(ang), 2, axis=1)
    sign = jnp.where(jnp.arange(rot_dim) % 2 == 0, -1.0, 1.0).astype(F32)
    sin = sin * sign
    cos_t = jnp.ones((n_tokens, LANES), F32)
    sin_t = jnp.zeros((n_tokens, LANES), F32)
    for off in lane_offset_pattern:
        cos_t = cos_t.at[:, off:off + rot_dim].set(cos)
        sin_t = sin_t.at[:, off:off + rot_dim].set(sin)
    return cos_t, sin_t


def _mla_proj_kernel(*refs, rope, n_cache, emit_keys):
    it = iter(refs)
    x_ref, sh_ref, sc_ref, gn_ref = next(it), next(it), next(it), next(it)
    wdq_ref, gq_ref, wuq_ref, wdkv_ref, gkv_ref, wkexp_ref, wuv_ref = (next(it) for _ in range(7))
    cos_ref = sin_ref = cache_ref = keys_ref = None
    if rope:
        cos_ref, sin_ref = next(it), next(it)
    if n_cache:
        cache_ref = next(it)
    q_ref, k_ref, v_ref = next(it), next(it), next(it)
    if emit_keys:
        keys_ref = next(it)

    def expand(keys_bf16):
        k_ref[...] = jnp.dot(keys_bf16, wkexp_ref[...], preferred_element_type=F32).astype(BF16)
        v_ref[...] = jnp.dot(keys_bf16[:, :KV_LORA], wuv_ref[...], preferred_element_type=F32).astype(BF16)

    def new_tokens():
        x = x_ref[...]
        h = (_rms(x, gn_ref[...]) * (1.0 + sc_ref[...]) + sh_ref[...]).astype(BF16)
        ql = jnp.dot(h, wdq_ref[...], preferred_element_type=F32)
        qn = _rms(ql, gq_ref[...]).astype(BF16)
        q = jnp.dot(qn, wuq_ref[...], preferred_element_type=F32)
        kv = jnp.dot(h, wdkv_ref[...], preferred_element_type=F32)
        ckv = _rms(kv[:, :KV_LORA], gkv_ref[...])
        pe = kv[:, KV_LORA:]
        if rope:
            cos, sin = cos_ref[...], sin_ref[...]
            pe = _rope(pe, cos, sin)
            q = jnp.concatenate(
                [_rope(q[:, h_ * LANES:(h_ + 1) * LANES], cos, sin) for h_ in range(MLA_HEADS)], axis=1)
        q_ref[...] = (q * (MLA_SCALE * LOG2E)).astype(BF16)
        if emit_keys:
            keys_ref[:, 0:KV_LORA] = ckv
            keys_ref[:, KV_LORA:KV_DIM] = pe[:, 0:MLA_ROPE]
        expand(jnp.concatenate([ckv, pe], axis=1).astype(BF16))

    if n_cache:
        t = pl.program_id(1)
        pl.when(t >= n_cache)(new_tokens)

        @pl.when(t < n_cache)
        def _():
            expand(cache_ref[...].astype(BF16))
    else:
        new_tokens()


def _mla_proj(x2d, mods, gn, w, *, n_batch, t_new, tm, mod_row, rope_tabs=None, cache=None):
    rope = rope_tabs is not None
    t_cache = 0 if cache is None else cache.shape[0] // n_batch
    n_cache = t_cache // tm
    ntn = t_new // tm
    nt = n_cache + ntn
    emit_keys = cache is None

    def new_idx(b, t):
        return b * ntn + jnp.maximum(t - n_cache, 0)

    row_fn = lambda b, t: mod_row(b)
    in_specs = [
        pl.BlockSpec((tm, D), lambda b, t: (new_idx(b, t), 0)),
        _mod_spec(0, 0, row_fn), _mod_spec(0, 1, row_fn),
        _full_spec((1, D)),
        _full_spec((D, Q_LORA)), _full_spec((1, Q_LORA)), _full_spec((Q_LORA, MLA_HEADS * LANES)),
        _full_spec((D, KV_PAD)), _full_spec((1, KV_LORA)),
        _full_spec((KV_PAD, MLA_HEADS * LANES)), _full_spec((KV_LORA, MLA_HEADS * MLA_V)),
    ]
    args = [x2d, mods, mods, gn, w["wdq"], w["gq"], w["wuq"], w["wdkv"], w["gkv"], w["wkexp"], w["wuv"]]
    if rope:
        in_specs += [pl.BlockSpec((tm, LANES), lambda b, t: (jnp.maximum(t - n_cache, 0), 0))] * 2
        args += list(rope_tabs)
    if n_cache:
        in_specs.append(pl.BlockSpec((tm, KV_PAD), lambda b, t: (b * n_cache + jnp.minimum(t, n_cache - 1), 0)))
        args.append(cache)
    n_new = n_batch * t_new
    n_keys = n_batch * (t_cache + t_new)
    out_specs = [
        pl.BlockSpec((tm, MLA_HEADS * LANES), lambda b, t: (new_idx(b, t), 0)),
        pl.BlockSpec((tm, MLA_HEADS * LANES), lambda b, t: (b * nt + t, 0)),
        pl.BlockSpec((tm, MLA_HEADS * MLA_V), lambda b, t: (b * nt + t, 0)),
    ]
    out_shape = [
        jax.ShapeDtypeStruct((n_new, MLA_HEADS * LANES), BF16),
        jax.ShapeDtypeStruct((n_keys, MLA_HEADS * LANES), BF16),
        jax.ShapeDtypeStruct((n_keys, MLA_HEADS * MLA_V), BF16),
    ]
    if emit_keys:
        out_specs.append(pl.BlockSpec((tm, KV_DIM), lambda b, t: (new_idx(b, t), 0)))
        out_shape.append(jax.ShapeDtypeStruct((n_new, KV_DIM), F32))
    return pl.pallas_call(
        functools.partial(_mla_proj_kernel, rope=rope, n_cache=n_cache, emit_keys=emit_keys),
        grid=(n_batch, nt), in_specs=in_specs, out_specs=out_specs, out_shape=out_shape,
        compiler_params=_params(2), name="mla_proj_s" if rope else "mla_proj_p",
    )(*args)


def _dup_halves(a):
    cols = []
    for c in range(a.shape[1] // LANES):
        blk = a[:, c * LANES:(c + 1) * LANES]
        rot = pltpu.roll(blk, HALF, 1)
        low = lax.broadcasted_iota(jnp.int32, blk.shape, 1) < HALF
        cols += [jnp.where(low, blk, rot), jnp.where(low, rot, blk)]
    return jnp.concatenate(cols, axis=1)


def _group_mean_sq(a, bmat):
    sq = a * a
    hi = sq.astype(BF16)
    lo = (sq - hi.astype(F32)).astype(BF16)
    return (jnp.dot(hi, bmat, preferred_element_type=F32) + jnp.dot(lo, bmat, preferred_element_type=F32))


def _gqa_proj_kernel(*refs, rope, n_cache, emit_kv):
    it = iter(refs)
    x_ref, sh_ref, sc_ref, gn_ref, w_ref, gq_ref, gk_ref, bmat_ref = (next(it) for _ in range(8))
    cos_ref = sin_ref = cache_ref = kv_ref = None
    if rope:
        cos_ref, sin_ref = next(it), next(it)
    if n_cache:
        cache_ref = next(it)
    q_ref, k_ref, v_ref = next(it), next(it), next(it)
    if emit_kv:
        kv_ref = next(it)
    nq = GQA_HEADS * GQA_HEAD_DIM
    nk = GQA_KV_HEADS * GQA_HEAD_DIM
    blk = 2 * LANES

    def new_tokens():
        x = x_ref[...]
        h = (_rms(x, gn_ref[...]) * (1.0 + sc_ref[...]) + sh_ref[...]).astype(BF16)
        qkv = jnp.dot(h, w_ref[...], preferred_element_type=F32)
        bmat = bmat_ref[...]

        def head_norm(a, g):
            return a * lax.rsqrt(_group_mean_sq(a, bmat) + EPS) * g

        def maybe_rope(a):
            if not rope:
                return a
            cos, sin = cos_ref[...], sin_ref[...]
            return jnp.concatenate(
                [_rope(a[:, c * LANES:(c + 1) * LANES], cos, sin) for c in range(a.shape[1] // LANES)], axis=1)

        for c in range(nq // blk):
            qb = head_norm(qkv[:, c * blk:(c + 1) * blk], gq_ref[...])
            q_ref[:, c * blk:(c + 1) * blk] = (maybe_rope(qb) * (GQA_SCALE * LOG2E)).astype(BF16)
        kn = head_norm(qkv[:, nq:nq + nk], gk_ref[...])
        vv = qkv[:, nq + nk:]
        if emit_kv:
            kv_ref[:, 0:nk] = kn
            kv_ref[:, nk:2 * nk] = vv
        k_ref[...] = _dup_halves(maybe_rope(kn)).astype(BF16)
        v_ref[...] = _dup_halves(vv).astype(BF16)

    if n_cache:
        t = pl.program_id(1)
        pl.when(t >= n_cache)(new_tokens)

        @pl.when(t < n_cache)
        def _():
            c = cache_ref[...]
            k_ref[...] = _dup_halves(c[:, 0:nk]).astype(BF16)
            v_ref[...] = _dup_halves(c[:, nk:2 * nk]).astype(BF16)
    else:
        new_tokens()


def _gqa_proj(x2d, mods, gn, w, *, n_batch, t_new, tm, mod_row, rope_tabs=None, cache=None):
    rope = rope_tabs is not None
    t_cache = 0 if cache is None else cache.shape[0] // n_batch
    n_cache = t_cache // tm
    ntn = t_new // tm
    nt = n_cache + ntn
    emit_kv = cache is None
    nq = GQA_HEADS * GQA_HEAD_DIM
    nk = GQA_KV_HEADS * GQA_HEAD_DIM
    nqkv = nq + 2 * nk
    ndup = GQA_KV_HEADS * LANES

    def new_idx(b, t):
        return b * ntn + jnp.maximum(t - n_cache, 0)

    row_fn = lambda b, t: mod_row(b)
    in_specs = [
        pl.BlockSpec((tm, D), lambda b, t: (new_idx(b, t), 0)),
        _mod_spec(1, 0, row_fn), _mod_spec(1, 1, row_fn),
        _full_spec((1, D)), _full_spec((D, nqkv)),
        _full_spec((1, 2 * LANES)), _full_spec((1, 2 * LANES)), _full_spec((2 * LANES, 2 * LANES)),
    ]
    args = [x2d, mods, mods, gn, w["wqkv"], w["gq"], w["gk"], w["bmat"]]
    if rope:
        in_specs += [pl.BlockSpec((tm, LANES), lambda b, t: (jnp.maximum(t - n_cache, 0), 0))] * 2
        args += list(rope_tabs)
    if n_cache:
        in_specs.append(pl.BlockSpec((tm, 2 * nk), lambda b, t: (b * n_cache + jnp.minimum(t, n_cache - 1), 0)))
        args.append(cache)
    n_new = n_batch * t_new
    n_keys = n_batch * (t_cache + t_new)
    out_specs = [
        pl.BlockSpec((tm, nq), lambda b, t: (new_idx(b, t), 0)),
        pl.BlockSpec((tm, ndup), lambda b, t: (b * nt + t, 0)),
        pl.BlockSpec((tm, ndup), lambda b, t: (b * nt + t, 0)),
    ]
    out_shape = [
        jax.ShapeDtypeStruct((n_new, nq), BF16),
        jax.ShapeDtypeStruct((n_keys, ndup), BF16),
        jax.ShapeDtypeStruct((n_keys, ndup), BF16),
    ]
    if emit_kv:
        out_specs.append(pl.BlockSpec((tm, 2 * nk), lambda b, t: (new_idx(b, t), 0)))
        out_shape.append(jax.ShapeDtypeStruct((n_new, 2 * nk), F32))
    return pl.pallas_call(
        functools.partial(_gqa_proj_kernel, rope=rope, n_cache=n_cache, emit_kv=emit_kv),
        grid=(n_batch, nt), in_specs=in_specs, out_specs=out_specs, out_shape=out_shape,
        compiler_params=_params(2), name="gqa_proj_s" if rope else "gqa_proj_p",
    )(*args)


def _attn_kernel(q_ref, k_ref, v_ref, o_ref, *, mla, n_pairs, ck):
    tq = q_ref.shape[0]
    nc = k_ref.shape[0] // ck
    lane = lax.broadcasted_iota(jnp.int32, (tq, LANES), 1)
    low = lane < HALF
    items = [(p, c, hh) for p in range(n_pairs) for c in range(nc) for hh in range(2)]
    heads = {}
    state = {}

    def head_operands(p, hh):
        if (p, hh) not in heads:
            if mla:
                hd = 2 * p + hh
                heads[(p, hh)] = (q_ref[:, hd * LANES:(hd + 1) * LANES], hd, p)
            else:
                g = p // 2 if n_pairs > 1 else 0
                qp = q_ref[:, p * LANES:(p + 1) * LANES]
                qh = jnp.where(low if hh == 0 else jnp.logical_not(low), qp, jnp.zeros_like(qp))
                heads[(p, hh)] = (qh, g, g)
        return heads[(p, hh)]

    def scores(item):
        p, c, hh = item
        qh, kc, _ = head_operands(p, hh)
        kh = k_ref[c * ck:(c + 1) * ck, kc * LANES:(kc + 1) * LANES]
        return lax.dot_general(qh, kh, (((1,), (1,)), ((), ())), preferred_element_type=F32)

    s_cur = scores(items[0])
    for idx, (p, c, hh) in enumerate(items):
        s_next = scores(items[idx + 1]) if idx + 1 < len(items) else None
        vc = head_operands(p, hh)[2]
        vb = v_ref[c * ck:(c + 1) * ck, vc * LANES:(vc + 1) * LANES]
        m_c = jnp.max(s_cur, axis=-1, keepdims=True)
        if c == 0:
            m = m_c
            e = jnp.exp2(s_cur - m)
            l = jnp.sum(e, axis=-1, keepdims=True)
            acc = jnp.dot(e.astype(BF16), vb, preferred_element_type=F32)
        else:
            m_old, l_old, acc_old = state[(p, hh)]
            m = jnp.maximum(m_old, m_c)
            alpha = jnp.exp2(m_old - m)
            e = jnp.exp2(s_cur - m)
            l = alpha * l_old + jnp.sum(e, axis=-1, keepdims=True)
            acc = alpha * acc_old + jnp.dot(e.astype(BF16), vb, preferred_element_type=F32)
        state[(p, hh)] = (m, l, acc)
        if c == nc - 1 and hh == 1:
            o0 = state[(p, 0)][2] / state[(p, 0)][1]
            o1 = state[(p, 1)][2] / state[(p, 1)][1]
            o_ref[:, p * LANES:(p + 1) * LANES] = jnp.where(low, o0, o1).astype(BF16)
        s_cur = s_next


def _attention(q, k, v, *, mla, n_batch, t_q, t_k, tq, n_pairs, name):
    total_pairs = 8
    nj = total_pairs // n_pairs
    nqt = t_q // tq
    if mla:
        qw, kw, vw = n_pairs * 2 * LANES, n_pairs * 2 * LANES, n_pairs * LANES
        kv_col = lambda j: j
    else:
        qw = n_pairs * LANES
        kw = vw = max(n_pairs // 2, 1) * LANES
        kv_col = (lambda j: j // 2) if n_pairs == 1 else (lambda j: j)
    return pl.pallas_call(
        functools.partial(_attn_kernel, mla=mla, n_pairs=n_pairs, ck=min(t_k, ATTN_KEY_CHUNK)),
        grid=(n_batch, nj, nqt),
        in_specs=[
            pl.BlockSpec((tq, qw), lambda b, j, i: (b * nqt + i, j)),
            pl.BlockSpec((t_k, kw), lambda b, j, i: (b, kv_col(j))),
            pl.BlockSpec((t_k, vw), lambda b, j, i: (b, kv_col(j))),
        ],
        out_specs=pl.BlockSpec((tq, n_pairs * LANES), lambda b, j, i: (b * nqt + i, j)),
        out_shape=jax.ShapeDtypeStruct((n_batch * t_q, total_pairs * LANES), BF16),
        compiler_params=_params(3), name=name,
    )(q, k, v)


def _route(logits):
    lane = lax.broadcasted_iota(jnp.int32, logits.shape, 1).astype(F32)
    big = jnp.float32(1e9)
    is_grp = (lane >= N_EXPERTS) & (lane < N_EXPERTS + N_GROUPS)
    gl = jnp.where(is_grp, logits, NEG)
    gm = jnp.max(gl, axis=-1, keepdims=True)
    g_w = 1.0 / jnp.sum(jnp.exp(gl - gm), axis=-1, keepdims=True)
    g_idx = jnp.min(jnp.where(gl == gm, lane, big), axis=-1, keepdims=True) - N_EXPERTS
    lo = g_idx * EXPERTS_PER_GROUP
    el = jnp.where((lane >= lo) & (lane < lo + EXPERTS_PER_GROUP), logits, NEG)
    m1 = jnp.max(el, axis=-1, keepdims=True)
    i1 = jnp.min(jnp.where(el == m1, lane, big), axis=-1, keepdims=True)
    el2 = jnp.where(lane == i1, NEG, el)
    m2 = jnp.max(el2, axis=-1, keepdims=True)
    i2 = jnp.min(jnp.where(el2 == m2, lane, big), axis=-1, keepdims=True)
    t = jnp.exp(m2 - m1)
    w1 = g_w / (1.0 + t)
    w2 = g_w * t / (1.0 + t)
    return jnp.where(lane == i1, w1, 0.0) + jnp.where(lane == i2, w2, 0.0), g_idx


ROW_W = D + LANES
TM_EXPERT = 256
ROW_DMA_UNROLL = 8


def _route_kernel(x_ref, o_ref, wo_ref, gta_ref, shm_ref, scm_ref, gn_ref,
                  wr_hi_ref, wr_lo_ref, br_ref, sel_ref, xm_ref, row_ref, gidx_ref):
    mix = jnp.dot(o_ref[...], wo_ref[...], preferred_element_type=F32)
    xm = x_ref[...] + gta_ref[...] * mix
    xm_ref[...] = xm
    h = _rms(xm, gn_ref[...]) * (1.0 + scm_ref[...]) + shm_ref[...]
    h_hi = h.astype(BF16)
    h_lo = (h - h_hi.astype(F32)).astype(BF16)
    logits = (jnp.dot(h_hi, wr_hi_ref[...], preferred_element_type=F32)
              + jnp.dot(h_lo, wr_hi_ref[...], preferred_element_type=F32)
              + jnp.dot(h_hi, wr_lo_ref[...], preferred_element_type=F32)) + br_ref[...]
    comb, g_idx = _route(logits)
    row_ref[:, 0:D] = h
    row_ref[:, D:ROW_W] = comb
    lane = lax.broadcasted_iota(jnp.int32, comb.shape, 1)
    g_mat = jnp.where(lane == 0, g_idx, 0.0).astype(BF16)
    g_row = lax.dot_general(sel_ref[...], g_mat, (((1,), (1,)), ((), ())), preferred_element_type=F32)
    gidx_ref[...] = g_row[0:1, :].astype(jnp.int32)


def _dispatch_kernel(gidx_ref, rows_ref, hs_hbm, dest_ref, cnt_ref, sem, *, tm, cap):
    i = pl.program_id(0)

    @pl.when(i == 0)
    def _():
        for k in range(N_GROUPS):
            cnt_ref[k] = 0

    def body(t, carry):
        tok = i * tm + t
        g = gidx_ref[tok]
        c = cnt_ref[g]
        cnt_ref[g] = c + 1
        d = g * cap + c
        dest_ref[tok] = d
        pltpu.make_async_copy(rows_ref.at[pl.ds(t, 1)], hs_hbm.at[pl.ds(d, 1)], sem.at[0]).start()
        return carry

    lax.fori_loop(0, tm, body, 0, unroll=ROW_DMA_UNROLL)
    pltpu.make_async_copy(rows_ref, hs_hbm.at[pl.ds(0, tm)], sem.at[0]).wait()


def _experts_kernel(blk_ref, grp_ref, first_ref, nt_ref, hs_ref, wg_ref, wu_ref, wd_ref, ys_ref,
                    wgb_ref, wub_ref, wdb_ref):
    i = pl.program_id(0)

    @pl.when(i < nt_ref[0])
    def _():
        @pl.when(first_ref[i] == 1)
        def _():
            for j in range(EXPERTS_PER_GROUP):
                wgb_ref[j] = wg_ref[j].astype(BF16)
                wub_ref[j] = wu_ref[j].astype(BF16)
                wdb_ref[j * EXPERT_FF:(j + 1) * EXPERT_FF, :] = wd_ref[j].astype(BF16)

        x = hs_ref[:, 0:D].astype(BF16)
        comb = hs_ref[:, D:ROW_W]
        lane = lax.broadcasted_iota(jnp.int32, comb.shape, 1)
        e0 = grp_ref[i] * EXPERTS_PER_GROUP
        hids = []
        for j in range(EXPERTS_PER_GROUP):
            g = jnp.dot(x, wgb_ref[j], preferred_element_type=F32)
            u = jnp.dot(x, wub_ref[j], preferred_element_type=F32)
            c = jnp.sum(jnp.where(lane == e0 + j, comb, 0.0), axis=-1, keepdims=True)
            hids.append(((g / (1.0 + jnp.exp(-g))) * u * c).astype(BF16))
        hid = jnp.concatenate(hids, axis=1)
        ys_ref[...] = jnp.dot(hid, wdb_ref[...], preferred_element_type=F32)


def _combine_kernel(dest_ref, xm_ref, gtm_ref, fin_ref, ys_hbm, y_ref, ybuf, sem, *, tm, final):
    i = pl.program_id(0)
    slot = i % 2

    def issue(tile, s):
        def body(t, carry):
            d = dest_ref[tile * tm + t]
            pltpu.make_async_copy(ys_hbm.at[pl.ds(d, 1)], ybuf.at[s, pl.ds(t, 1)], sem.at[s]).start()
            return carry
        lax.fori_loop(0, tm, body, 0, unroll=ROW_DMA_UNROLL)

    @pl.when(i == 0)
    def _():
        issue(0, 0)

    @pl.when(i + 1 < pl.num_programs(0))
    def _():
        issue(i + 1, 1 - slot)

    pltpu.make_async_copy(ys_hbm.at[pl.ds(0, tm)], ybuf.at[slot], sem.at[slot]).wait()
    out = xm_ref[...] + gtm_ref[...] * ybuf[slot]
    if final:
        out = _rms(out, fin_ref[...])
    y_ref[...] = out


def _tile_table(counts, cap, max_tiles):
    nt_g = (counts + TM_EXPERT - 1) // TM_EXPERT
    ends = jnp.cumsum(nt_g)
    starts = ends - nt_g
    total = ends[-1]
    i = jnp.arange(max_tiles, dtype=jnp.int32)
    iv = jnp.minimum(i, total - 1)
    grp = jnp.sum((iv[:, None] >= ends[None, :]).astype(jnp.int32), axis=1)
    blk = grp * (cap // TM_EXPERT) + iv - starts[grp]
    first = ((iv == starts[grp]) & (i < total)).astype(jnp.int32)
    return blk.astype(jnp.int32), grp.astype(jnp.int32), first, total.reshape(1).astype(jnp.int32)


def _post(x2d, o2d, mods, layer, w, *, tm, mod_row, final):
    n = x2d.shape[0]
    nt = n // tm
    cap = n
    row_fn = lambda i, *_: mod_row(i)
    tile = pl.BlockSpec((tm, D), lambda i: (i, 0))
    xm, rows, gidx = pl.pallas_call(
        _route_kernel,
        grid=(nt,),
        in_specs=[
            tile, tile, _full_spec((D, D)),
            _mod_spec(layer, 2, row_fn), _mod_spec(layer, 3, row_fn), _mod_spec(layer, 4, row_fn),
            _full_spec((1, D)),
            _full_spec((D, LANES)), _full_spec((D, LANES)), _full_spec((1, LANES)), _full_spec((8, LANES)),
        ],
        out_specs=[tile, pl.BlockSpec((tm, ROW_W), lambda i: (i, 0)), pl.BlockSpec((None, 1, tm), lambda i: (i, 0, 0))],
        out_shape=[jax.ShapeDtypeStruct((n, D), F32), jax.ShapeDtypeStruct((n, ROW_W), F32),
                   jax.ShapeDtypeStruct((nt, 1, tm), jnp.int32)],
        compiler_params=_params(1), name=f"route_l{layer}",
    )(x2d, o2d, w["wo"], mods, mods, mods, w["gn"], w["wr_hi"], w["wr_lo"], w["br"], w["sel"])

    hs, dest, counts = pl.pallas_call(
        functools.partial(_dispatch_kernel, tm=tm, cap=cap),
        grid_spec=pltpu.PrefetchScalarGridSpec(
            num_scalar_prefetch=1, grid=(nt,),
            in_specs=[pl.BlockSpec((tm, ROW_W), lambda i, gidx_: (i, 0))],
            out_specs=[pl.BlockSpec(memory_space=pl.ANY), pl.BlockSpec(memory_space=pltpu.SMEM),
                       pl.BlockSpec(memory_space=pltpu.SMEM)],
            scratch_shapes=[pltpu.SemaphoreType.DMA((1,))]),
        out_shape=[jax.ShapeDtypeStruct((N_GROUPS * cap, ROW_W), F32), jax.ShapeDtypeStruct((n,), jnp.int32),
                   jax.ShapeDtypeStruct((N_GROUPS,), jnp.int32)],
        compiler_params=_params(1), name=f"dispatch_l{layer}",
    )(gidx.reshape(n), rows)

    max_tiles = n // TM_EXPERT + N_GROUPS
    blk, grp, first, total = _tile_table(counts, cap, max_tiles)
    wspec = lambda shape: pl.BlockSpec(shape, lambda i, blk, grp, first, nt_: (grp[i], 0, 0))
    ys = pl.pallas_call(
        _experts_kernel,
        grid_spec=pltpu.PrefetchScalarGridSpec(
            num_scalar_prefetch=4, grid=(max_tiles,),
            in_specs=[
                pl.BlockSpec((TM_EXPERT, ROW_W), lambda i, blk, grp, first, nt_: (blk[i], 0)),
                wspec((EXPERTS_PER_GROUP, D, EXPERT_FF)), wspec((EXPERTS_PER_GROUP, D, EXPERT_FF)),
                wspec((EXPERTS_PER_GROUP, EXPERT_FF, D)),
            ],
            out_specs=pl.BlockSpec((TM_EXPERT, D), lambda i, blk, grp, first, nt_: (blk[i], 0)),
            scratch_shapes=[pltpu.VMEM((EXPERTS_PER_GROUP, D, EXPERT_FF), BF16),
                            pltpu.VMEM((EXPERTS_PER_GROUP, D, EXPERT_FF), BF16),
                            pltpu.VMEM((EXPERTS_PER_GROUP * EXPERT_FF, D), BF16)]),
        out_shape=jax.ShapeDtypeStruct((N_GROUPS * cap, D), F32),
        compiler_params=_params(1), name=f"experts_l{layer}",
    )(blk, grp, first, total, hs, w["wg"], w["wu"], w["wd"])

    return pl.pallas_call(
        functools.partial(_combine_kernel, tm=tm, final=final),
        grid_spec=pltpu.PrefetchScalarGridSpec(
            num_scalar_prefetch=1, grid=(nt,),
            in_specs=[pl.BlockSpec((tm, D), lambda i, dest_: (i, 0)), _mod_spec(layer, 5, row_fn),
                      _full_spec((1, D)), pl.BlockSpec(memory_space=pl.ANY)],
            out_specs=pl.BlockSpec((tm, D), lambda i, dest_: (i, 0)),
            scratch_shapes=[pltpu.VMEM((2, tm, D), F32), pltpu.SemaphoreType.DMA((2,))]),
        out_shape=jax.ShapeDtypeStruct((n, D), F32),
        compiler_params=_params(1), name=f"combine_l{layer}",
    )(dest, xm, mods, w["fin"], ys)


def _mla_weights(w_dq, g_q, w_uq, w_dkv, g_kv, w_uk, w_uv):
    hd = MLA_NOPE + MLA_ROPE
    wuq = w_uq.reshape(Q_LORA, MLA_HEADS, hd)
    wuq = jnp.concatenate([wuq[..., MLA_NOPE:], wuq[..., :MLA_NOPE],
                           jnp.zeros((Q_LORA, MLA_HEADS, LANES - hd), F32)], axis=-1)
    wuk = jnp.concatenate([jnp.zeros((KV_LORA, MLA_HEADS, MLA_ROPE), F32), w_uk,
                           jnp.zeros((KV_LORA, MLA_HEADS, LANES - hd), F32)], axis=-1)
    pe_rows = jnp.broadcast_to(jnp.eye(MLA_ROPE, LANES, dtype=F32)[:, None, :], (MLA_ROPE, MLA_HEADS, LANES))
    wkexp = jnp.concatenate([wuk, pe_rows, jnp.zeros((KV_PAD - KV_DIM, MLA_HEADS, LANES), F32)], axis=0)
    return {
        "wdq": w_dq.astype(BF16), "gq": g_q.reshape(1, Q_LORA),
        "wuq": wuq.reshape(Q_LORA, MLA_HEADS * LANES).astype(BF16),
        "wdkv": jnp.pad(w_dkv, ((0, 0), (0, KV_PAD - KV_DIM))).astype(BF16), "gkv": g_kv.reshape(1, KV_LORA),
        "wkexp": wkexp.reshape(KV_PAD, MLA_HEADS * LANES).astype(BF16),
        "wuv": w_uv.reshape(KV_LORA, MLA_HEADS * MLA_V).astype(BF16),
    }


def _gqa_weights(w_qkv, g_q, g_k):
    grp = jnp.arange(2 * LANES) // GQA_HEAD_DIM
    bmat = jnp.where(grp[:, None] == grp[None, :], 1.0 / GQA_HEAD_DIM, 0.0).astype(BF16)
    reps = 2 * LANES // GQA_HEAD_DIM
    return {"wqkv": w_qkv.astype(BF16), "gq": jnp.tile(g_q, reps).reshape(1, 2 * LANES),
            "gk": jnp.tile(g_k, reps).reshape(1, 2 * LANES), "bmat": bmat}


def _post_weights(l, w_o, norm_ffn, w_group, b_group, w_exp, b_exp, w_gate, w_up, w_down, final_norm):
    wr = jnp.concatenate([w_exp[l], w_group[l], jnp.zeros((D, LANES - N_EXPERTS - N_GROUPS), F32)], axis=1)
    br = jnp.concatenate([b_exp[l], b_group[l], jnp.zeros((LANES - N_EXPERTS - N_GROUPS,), F32)]).reshape(1, LANES)
    wr_hi = wr.astype(BF16)
    wr_lo = (wr - wr_hi.astype(F32)).astype(BF16)
    return {"wo": w_o.astype(BF16), "gn": norm_ffn[l].reshape(1, D), "wr_hi": wr_hi, "wr_lo": wr_lo, "br": br,
            "wg": w_gate[l], "wu": w_up[l], "wd": w_down[l], "fin": final_norm.reshape(1, D),
            "sel": jnp.zeros((8, LANES), BF16).at[0, 0].set(1.0)}


def kernel(x_prompt, x_sample, cache_mla, cache_gqa, c, c_ctx, ada_w, ada_b, norm_mix, norm_ffn,
           mla_w_dq, mla_q_norm, mla_w_uq, mla_w_dkv, mla_kv_norm, mla_w_uk, mla_w_uv, mla_w_o,
           gqa_w_qkv, gqa_q_norm, gqa_k_norm, gqa_w_o,
           moe_w_group, moe_b_group, moe_w_expert, moe_b_expert, moe_w_gate, moe_w_up, moe_w_down,
           final_norm):
    bp, tp, _ = x_prompt.shape
    bs, ts, _ = x_sample.shape
    t_past = cache_mla.shape[2]
    assert ada_w.shape[0] == 2 and 1 + bs <= N_MOD_ROWS
    n_p, n_s = bp * tp, bs * ts

    cc = jnp.concatenate([c_ctx[None], c, jnp.zeros((N_MOD_ROWS - 1 - bs, D), F32)], axis=0)
    mods = _ada_table(cc, ada_w, ada_b)

    xp = x_prompt.reshape(n_p, D)
    xs = x_sample.reshape(n_s, D)
    tm_proj = 512
    tm_post = 512
    prompt_row = lambda *_: 0
    sample_row_proj = lambda b: 1 + b
    sample_row_post = lambda i: 1 + i // (ts // tm_post)
    moe_args = (norm_ffn, moe_w_group, moe_b_group, moe_w_expert, moe_b_expert,
                moe_w_gate, moe_w_up, moe_w_down, final_norm)

    wm = _mla_weights(mla_w_dq[0], mla_q_norm[0], mla_w_uq[0], mla_w_dkv[0], mla_kv_norm[0],
                      mla_w_uk[0], mla_w_uv[0])
    gn0 = norm_mix[0].reshape(1, D)
    q, k, v, keys_p = _mla_proj(xp, mods, gn0, wm, n_batch=1, t_new=n_p, tm=tm_proj, mod_row=prompt_row)
    o = _attention(q, k, v, mla=True, n_batch=bp, t_q=tp, t_k=tp, tq=tp, n_pairs=8, name="attn_mla_p")
    wp0 = _post_weights(0, mla_w_o[0], *moe_args)
    xp = _post(xp, o, mods, 0, wp0, tm=tm_post, mod_row=prompt_row, final=False)

    rope_mla = _rope_tables(ts, MLA_ROPE, (0,))
    cache0 = jnp.pad(cache_mla[:, 0].reshape(bs * t_past, KV_DIM), ((0, 0), (0, KV_PAD - KV_DIM)))
    q, k, v = _mla_proj(xs, mods, gn0, wm, n_batch=bs, t_new=ts, tm=tm_proj, mod_row=sample_row_proj,
                        rope_tabs=rope_mla, cache=cache0)
    o = _attention(q, k, v, mla=True, n_batch=bs, t_q=ts, t_k=t_past + ts, tq=512, n_pairs=1, name="attn_mla_s")
    xs = _post(xs, o, mods, 0, wp0, tm=tm_post, mod_row=sample_row_post, final=False)

    wg = _gqa_weights(gqa_w_qkv[0], gqa_q_norm[0], gqa_k_norm[0])
    gn1 = norm_mix[1].reshape(1, D)
    q, k, v, kv_p = _gqa_proj(xp, mods, gn1, wg, n_batch=1, t_new=n_p, tm=tm_proj, mod_row=prompt_row)
    o = _attention(q, k, v, mla=False, n_batch=bp, t_q=tp, t_k=tp, tq=tp, n_pairs=8, name="attn_gqa_p")
    wp1 = _post_weights(1, gqa_w_o[0], *moe_args)
    y_prompt = _post(xp, o, mods, 1, wp1, tm=tm_post, mod_row=prompt_row, final=True)

    rope_gqa = _rope_tables(ts, GQA_HEAD_DIM, (0, GQA_HEAD_DIM))
    cache1 = cache_gqa[:, 0].reshape(bs * t_past, 2 * GQA_KV_HEADS * GQA_HEAD_DIM)
    q, k, v = _gqa_proj(xs, mods, gn1, wg, n_batch=bs, t_new=ts, tm=tm_proj, mod_row=sample_row_proj,
                        rope_tabs=rope_gqa, cache=cache1)
    o = _attention(q, k, v, mla=False, n_batch=bs, t_q=ts, t_k=t_past + ts, tq=512, n_pairs=1, name="attn_gqa_s")
    y_sample = _post(xs, o, mods, 1, wp1, tm=tm_post, mod_row=sample_row_post, final=True)

    return (y_prompt.reshape(bp, tp, D), y_sample.reshape(bs, ts, D),
            keys_p.reshape(bp, 1, tp, KV_DIM),
            kv_p.reshape(bp, 1, tp, 2, GQA_KV_HEADS, GQA_HEAD_DIM))
```

```python
import functools
import math

import jax
import jax.numpy as jnp
from jax import lax
from jax.experimental import pallas as pl
from jax.experimental.pallas import tpu as pltpu

F32 = jnp.float32
BF16 = jnp.bfloat16

D = 1024
EPS = 1e-6
GRID_W = 64
ROPE_THETA = 10000.0
LANES = 128
HALF = LANES // 2
MLA_HEADS = 16
MLA_NOPE = 64
MLA_ROPE = 32
MLA_V = 64
Q_LORA = 384
KV_LORA = 256
KV_DIM = KV_LORA + MLA_ROPE
KV_PAD = 384
MLA_SCALE = 1.0 / math.sqrt(MLA_NOPE + MLA_ROPE)
GQA_HEADS = 16
GQA_KV_HEADS = 4
GQA_HEAD_DIM = 64
GQA_SCALE = 1.0 / math.sqrt(GQA_HEAD_DIM)
N_GROUPS = 4
EXPERTS_PER_GROUP = 4
N_EXPERTS = 16
EXPERT_FF = 256
NEG = -3.0e38
LOG2E = 1.4426950408889634
ATTN_KEY_CHUNK = 512

VMEM_LIMIT = 56 * 1024 * 1024
N_MOD_ROWS = 8


def _params(n_axes):
    return pltpu.CompilerParams(dimension_semantics=("arbitrary",) * n_axes,
                                vmem_limit_bytes=VMEM_LIMIT)


def _rms(x, g):
    ms = jnp.mean(x * x, axis=-1, keepdims=True)
    return x * lax.rsqrt(ms + EPS) * g


def _mod_index(layer, which, row):
    return (layer * 6 + which) * N_MOD_ROWS + row


def _mod_spec(layer, which, row_fn):
    return pl.BlockSpec((None, 1, D), lambda *g: (_mod_index(layer, which, row_fn(*g)), 0, 0))


def _full_spec(shape):
    n = len(shape)
    return pl.BlockSpec(shape, lambda *g: (0,) * n)


def _ada_kernel(c_ref, w_ref, b_ref, o_ref):
    c = c_ref[...]
    a = c / (1.0 + jnp.exp(-c))
    o_ref[...] = jnp.dot(a, w_ref[...], precision=lax.Precision.HIGHEST,
                         preferred_element_type=F32) + b_ref[...]


def _ada_table(cc, ada_w, ada_b):
    depth = ada_w.shape[0]
    out = pl.pallas_call(
        _ada_kernel,
        grid=(depth, 6),
        in_specs=[
            pl.BlockSpec((N_MOD_ROWS, D), lambda l, n: (0, 0)),
            pl.BlockSpec((None, D, D), lambda l, n: (l, 0, n)),
            pl.BlockSpec((None, 1, D), lambda l, n: (l, 0, n)),
        ],
        out_specs=pl.BlockSpec((None, None, N_MOD_ROWS, D), lambda l, n: (l, n, 0, 0)),
        out_shape=jax.ShapeDtypeStruct((depth, 6, N_MOD_ROWS, D), F32),
        compiler_params=_params(2),
        name="ada",
    )(cc, ada_w, ada_b.reshape(depth, 1, 6 * D))
    return out.reshape(depth * 6 * N_MOD_ROWS, 1, D)


def _rope(a, cos, sin_signed):
    lane = lax.broadcasted_iota(jnp.int32, a.shape, 1)
    nxt = pltpu.roll(a, LANES - 1, 1)
    prv = pltpu.roll(a, 1, 1)
    return a * cos + jnp.where((lane & 1) == 0, nxt, prv) * sin_signed


def _rope_tables(n_tokens, rot_dim, lane_offset_pattern):
    t = jnp.arange(n_tokens)
    row = (t // GRID_W).astype(F32)
    col = (t % GRID_W).astype(F32)
    axis_dim = rot_dim // 2
    inv = jnp.power(ROPE_THETA, -jnp.arange(0, axis_dim, 2, dtype=F32) / axis_dim)
    ang = jnp.concatenate([row[:, None] * inv, col[:, None] * inv], axis=-1)
    cos = jnp.repeat(jnp.cos(ang), 2, axis=1)
    sin = jnp.repeat(jnp.sin(ang), 2, axis=1)
    sign = jnp.where(jnp.arange(rot_dim) % 2 == 0, -1.0, 1.0).astype(F32)
    sin = sin * sign
    cos_t = jnp.ones((n_tokens, LANES), F32)
    sin_t = jnp.zeros((n_tokens, LANES), F32)
    for off in lane_offset_pattern:
        cos_t = cos_t.at[:, off:off + rot_dim].set(cos)
        sin_t = sin_t.at[:, off:off + rot_dim].set(sin)
    return cos_t, sin_t


def _mla_proj_kernel(*refs, rope, n_cache, emit_keys):
    it = iter(refs)
    x_ref, sh_ref, sc_ref, gn_ref = next(it), next(it), next(it), next(it)
    wdq_ref, gq_ref, wuq_ref, wdkv_ref, gkv_ref, wkexp_ref, wuv_ref = (next(it) for _ in range(7))
    cos_ref = sin_ref = cache_ref = keys_ref = None
    if rope:
        cos_ref, sin_ref = next(it), next(it)
    if n_cache:
        cache_ref = next(it)
    q_ref, k_ref, v_ref = next(it), next(it), next(it)
    if emit_keys:
        keys_ref = next(it)

    def expand(keys_bf16):
        k_ref[...] = jnp.dot(keys_bf16, wkexp_ref[...], preferred_element_type=F32).astype(BF16)
        v_ref[...] = jnp.dot(keys_bf16[:, :KV_LORA], wuv_ref[...], preferred_element_type=F32).astype(BF16)

    def new_tokens():
        x = x_ref[...]
        h = (_rms(x, gn_ref[...]) * (1.0 + sc_ref[...]) + sh_ref[...]).astype(BF16)
        ql = jnp.dot(h, wdq_ref[...], preferred_element_type=F32)
        qn = _rms(ql, gq_ref[...]).astype(BF16)
        q = jnp.dot(qn, wuq_ref[...], preferred_element_type=F32)
        kv = jnp.dot(h, wdkv_ref[...], preferred_element_type=F32)
        ckv = _rms(kv[:, :KV_LORA], gkv_ref[...])
        pe = kv[:, KV_LORA:]
        if rope:
            cos, sin = cos_ref[...], sin_ref[...]
            pe = _rope(pe, cos, sin)
            q = jnp.concatenate(
                [_rope(q[:, h_ * LANES:(h_ + 1) * LANES], cos, sin) for h_ in range(MLA_HEADS)], axis=1)
        q_ref[...] = (q * (MLA_SCALE * LOG2E)).astype(BF16)
        if emit_keys:
            keys_ref[:, 0:KV_LORA] = ckv
            keys_ref[:, KV_LORA:KV_DIM] = pe[:, 0:MLA_ROPE]
        expand(jnp.concatenate([ckv, pe], axis=1).astype(BF16))

    if n_cache:
        t = pl.program_id(1)
        pl.when(t >= n_cache)(new_tokens)

        @pl.when(t < n_cache)
        def _():
            expand(cache_ref[...].astype(BF16))
    else:
        new_tokens()


def _mla_proj(x2d, mods, gn, w, *, n_batch, t_new, tm, mod_row, rope_tabs=None, cache=None):
    rope = rope_tabs is not None
    t_cache = 0 if cache is None else cache.shape[0] // n_batch
    n_cache = t_cache // tm
    ntn = t_new // tm
    nt = n_cache + ntn
    emit_keys = cache is None

    def new_idx(b, t):
        return b * ntn + jnp.maximum(t - n_cache, 0)

    row_fn = lambda b, t: mod_row(b)
    in_specs = [
        pl.BlockSpec((tm, D), lambda b, t: (new_idx(b, t), 0)),
        _mod_spec(0, 0, row_fn), _mod_spec(0, 1, row_fn),
        _full_spec((1, D)),
        _full_spec((D, Q_LORA)), _full_spec((1, Q_LORA)), _full_spec((Q_LORA, MLA_HEADS * LANES)),
        _full_spec((D, KV_PAD)), _full_spec((1, KV_LORA)),
        _full_spec((KV_PAD, MLA_HEADS * LANES)), _full_spec((KV_LORA, MLA_HEADS * MLA_V)),
    ]
    args = [x2d, mods, mods, gn, w["wdq"], w["gq"], w["wuq"], w["wdkv"], w["gkv"], w["wkexp"], w["wuv"]]
    if rope:
        in_specs += [pl.BlockSpec((tm, LANES), lambda b, t: (jnp.maximum(t - n_cache, 0), 0))] * 2
        args += list(rope_tabs)
    if n_cache:
        in_specs.append(pl.BlockSpec((tm, KV_PAD), lambda b, t: (b * n_cache + jnp.minimum(t, n_cache - 1), 0)))
        args.append(cache)
    n_new = n_batch * t_new
    n_keys = n_batch * (t_cache + t_new)
    out_specs = [
        pl.BlockSpec((tm, MLA_HEADS * LANES), lambda b, t: (new_idx(b, t), 0)),
        pl.BlockSpec((tm, MLA_HEADS * LANES), lambda b, t: (b * nt + t, 0)),
        pl.BlockSpec((tm, MLA_HEADS * MLA_V), lambda b, t: (b * nt + t, 0)),
    ]
    out_shape = [
        jax.ShapeDtypeStruct((n_new, MLA_HEADS * LANES), BF16),
        jax.ShapeDtypeStruct((n_keys, MLA_HEADS * LANES), BF16),
        jax.ShapeDtypeStruct((n_keys, MLA_HEADS * MLA_V), BF16),
    ]
    if emit_keys:
        out_specs.append(pl.BlockSpec((tm, KV_DIM), lambda b, t: (new_idx(b, t), 0)))
        out_shape.append(jax.ShapeDtypeStruct((n_new, KV_DIM), F32))
    return pl.pallas_call(
        functools.partial(_mla_proj_kernel, rope=rope, n_cache=n_cache, emit_keys=emit_keys),
        grid=(n_batch, nt), in_specs=in_specs, out_specs=out_specs, out_shape=out_shape,
        compiler_params=_params(2), name="mla_proj_s" if rope else "mla_proj_p",
    )(*args)


def _dup_halves(a):
    cols = []
    for c in range(a.shape[1] // LANES):
        blk = a[:, c * LANES:(c + 1) * LANES]
        rot = pltpu.roll(blk, HALF, 1)
        low = lax.broadcasted_iota(jnp.int32, blk.shape, 1) < HALF
        cols += [jnp.where(low, blk, rot), jnp.where(low, rot, blk)]
    return jnp.concatenate(cols, axis=1)


def _group_mean_sq(a, bmat):
    sq = a * a
    hi = sq.astype(BF16)
    lo = (sq - hi.astype(F32)).astype(BF16)
    return (jnp.dot(hi, bmat, preferred_element_type=F32) + jnp.dot(lo, bmat, preferred_element_type=F32))


def _gqa_proj_kernel(*refs, rope, n_cache, emit_kv):
    it = iter(refs)
    x_ref, sh_ref, sc_ref, gn_ref, w_ref, gq_ref, gk_ref, bmat_ref = (next(it) for _ in range(8))
    cos_ref = sin_ref = cache_ref = kv_ref = None
    if rope:
        cos_ref, sin_ref = next(it), next(it)
    if n_cache:
        cache_ref = next(it)
    q_ref, k_ref, v_ref = next(it), next(it), next(it)
    if emit_kv:
        kv_ref = next(it)
    nq = GQA_HEADS * GQA_HEAD_DIM
    nk = GQA_KV_HEADS * GQA_HEAD_DIM
    blk = 2 * LANES

    def new_tokens():
        x = x_ref[...]
        h = (_rms(x, gn_ref[...]) * (1.0 + sc_ref[...]) + sh_ref[...]).astype(BF16)
        qkv = jnp.dot(h, w_ref[...], preferred_element_type=F32)
        bmat = bmat_ref[...]

        def head_norm(a, g):
            return a * lax.rsqrt(_group_mean_sq(a, bmat) + EPS) * g

        def maybe_rope(a):
            if not rope:
                return a
            cos, sin = cos_ref[...], sin_ref[...]
            return jnp.concatenate(
                [_rope(a[:, c * LANES:(c + 1) * LANES], cos, sin) for c in range(a.shape[1] // LANES)], axis=1)

        for c in range(nq // blk):
            qb = head_norm(qkv[:, c * blk:(c + 1) * blk], gq_ref[...])
            q_ref[:, c * blk:(c + 1) * blk] = (maybe_rope(qb) * (GQA_SCALE * LOG2E)).astype(BF16)
        kn = head_norm(qkv[:, nq:nq + nk], gk_ref[...])
        vv = qkv[:, nq + nk:]
        if emit_kv:
            kv_ref[:, 0:nk] = kn
            kv_ref[:, nk:2 * nk] = vv
        k_ref[...] = _dup_halves(maybe_rope(kn)).astype(BF16)
        v_ref[...] = _dup_halves(vv).astype(BF16)

    if n_cache:
        t = pl.program_id(1)
        pl.when(t >= n_cache)(new_tokens)

        @pl.when(t < n_cache)
        def _():
            c = cache_ref[...]
            k_ref[...] = _dup_halves(c[:, 0:nk]).astype(BF16)
            v_ref[...] = _dup_halves(c[:, nk:2 * nk]).astype(BF16)
    else:
        new_tokens()


def _gqa_proj(x2d, mods, gn, w, *, n_batch, t_new, tm, mod_row, rope_tabs=None, cache=None):
    rope = rope_tabs is not None
    t_cache = 0 if cache is None else cache.shape[0] // n_batch
    n_cache = t_cache // tm
    ntn = t_new // tm
    nt = n_cache + ntn
    emit_kv = cache is None
    nq = GQA_HEADS * GQA_HEAD_DIM
    nk = GQA_KV_HEADS * GQA_HEAD_DIM
    nqkv = nq + 2 * nk
    ndup = GQA_KV_HEADS * LANES

    def new_idx(b, t):
        return b * ntn + jnp.maximum(t - n_cache, 0)

    row_fn = lambda b, t: mod_row(b)
    in_specs = [
        pl.BlockSpec((tm, D), lambda b, t: (new_idx(b, t), 0)),
        _mod_spec(1, 0, row_fn), _mod_spec(1, 1, row_fn),
        _full_spec((1, D)), _full_spec((D, nqkv)),
        _full_spec((1, 2 * LANES)), _full_spec((1, 2 * LANES)), _full_spec((2 * LANES, 2 * LANES)),
    ]
    args = [x2d, mods, mods, gn, w["wqkv"], w["gq"], w["gk"], w["bmat"]]
    if rope:
        in_specs += [pl.BlockSpec((tm, LANES), lambda b, t: (jnp.maximum(t - n_cache, 0), 0))] * 2
        args += list(rope_tabs)
    if n_cache:
        in_specs.append(pl.BlockSpec((tm, 2 * nk), lambda b, t: (b * n_cache + jnp.minimum(t, n_cache - 1), 0)))
        args.append(cache)
    n_new = n_batch * t_new
    n_keys = n_batch * (t_cache + t_new)
    out_specs = [
        pl.BlockSpec((tm, nq), lambda b, t: (new_idx(b, t), 0)),
        pl.BlockSpec((tm, ndup), lambda b, t: (b * nt + t, 0)),
        pl.BlockSpec((tm, ndup), lambda b, t: (b * nt + t, 0)),
    ]
    out_shape = [
        jax.ShapeDtypeStruct((n_new, nq), BF16),
        jax.ShapeDtypeStruct((n_keys, ndup), BF16),
        jax.ShapeDtypeStruct((n_keys, ndup), BF16),
    ]
    if emit_kv:
        out_specs.append(pl.BlockSpec((tm, 2 * nk), lambda b, t: (new_idx(b, t), 0)))
        out_shape.append(jax.ShapeDtypeStruct((n_new, 2 * nk), F32))
    return pl.pallas_call(
        functools.partial(_gqa_proj_kernel, rope=rope, n_cache=n_cache, emit_kv=emit_kv),
        grid=(n_batch, nt), in_specs=in_specs, out_specs=out_specs, out_shape=out_shape,
        compiler_params=_params(2), name="gqa_proj_s" if rope else "gqa_proj_p",
    )(*args)


def _attn_kernel(q_ref, k_ref, v_ref, o_ref, *, mla, n_pairs, ck):
    tq = q_ref.shape[0]
    nc = k_ref.shape[0] // ck
    lane = lax.broadcasted_iota(jnp.int32, (tq, LANES), 1)
    low = lane < HALF
    items = [(p, c, hh) for c in range(nc) for p in range(n_pairs) for hh in range(2)]
    heads = {}
    state = {}

    def head_operands(p, hh):
        if (p, hh) not in heads:
            if mla:
                hd = 2 * p + hh
                heads[(p, hh)] = (q_ref[:, hd * LANES:(hd + 1) * LANES], hd, p)
            else:
                g = p // 2 if n_pairs > 1 else 0
                qp = q_ref[:, p * LANES:(p + 1) * LANES]
                qh = jnp.where(low if hh == 0 else jnp.logical_not(low), qp, jnp.zeros_like(qp))
                heads[(p, hh)] = (qh, g, g)
        return heads[(p, hh)]

    def scores(item):
        p, c, hh = item
        qh, kc, _ = head_operands(p, hh)
        kh = k_ref[c * ck:(c + 1) * ck, kc * LANES:(kc + 1) * LANES]
        return lax.dot_general(qh, kh, (((1,), (1,)), ((), ())), preferred_element_type=F32)

    s_cur = scores(items[0])
    for idx, (p, c, hh) in enumerate(items):
        s_next = scores(items[idx + 1]) if idx + 1 < len(items) else None
        vc = head_operands(p, hh)[2]
        vb = v_ref[c * ck:(c + 1) * ck, vc * LANES:(vc + 1) * LANES]
        m_c = jnp.max(s_cur, axis=-1, keepdims=True)
        if c == 0:
            m = m_c
            e = jnp.exp2(s_cur - m)
            l = jnp.sum(e, axis=-1, keepdims=True)
            acc = jnp.dot(e.astype(BF16), vb, preferred_element_type=F32)
        else:
            m_old, l_old, acc_old = state[(p, hh)]
            m = jnp.maximum(m_old, m_c)
            alpha = jnp.exp2(m_old - m)
            e = jnp.exp2(s_cur - m)
            l = alpha * l_old + jnp.sum(e, axis=-1, keepdims=True)
            acc = alpha * acc_old + jnp.dot(e.astype(BF16), vb, preferred_element_type=F32)
        state[(p, hh)] = (m, l, acc)
        if c == nc - 1 and hh == 1:
            o0 = state[(p, 0)][2] / state[(p, 0)][1]
            o1 = state[(p, 1)][2] / state[(p, 1)][1]
            o_ref[:, p * LANES:(p + 1) * LANES] = jnp.where(low, o0, o1).astype(BF16)
        s_cur = s_next


def _attention(q, k, v, *, mla, n_batch, t_q, t_k, tq, n_pairs, name):
    total_pairs = 8
    nj = total_pairs // n_pairs
    nqt = t_q // tq
    if mla:
        qw, kw, vw = n_pairs * 2 * LANES, n_pairs * 2 * LANES, n_pairs * LANES
        kv_col = lambda j: j
    else:
        qw = n_pairs * LANES
        kw = vw = max(n_pairs // 2, 1) * LANES
        kv_col = (lambda j: j // 2) if n_pairs == 1 else (lambda j: j)
    return pl.pallas_call(
        functools.partial(_attn_kernel, mla=mla, n_pairs=n_pairs, ck=min(t_k, ATTN_KEY_CHUNK)),
        grid=(n_batch, nj, nqt),
        in_specs=[
            pl.BlockSpec((tq, qw), lambda b, j, i: (b * nqt + i, j)),
            pl.BlockSpec((t_k, kw), lambda b, j, i: (b, kv_col(j))),
            pl.BlockSpec((t_k, vw), lambda b, j, i: (b, kv_col(j))),
        ],
        out_specs=pl.BlockSpec((tq, n_pairs * LANES), lambda b, j, i: (b * nqt + i, j)),
        out_shape=jax.ShapeDtypeStruct((n_batch * t_q, total_pairs * LANES), BF16),
        compiler_params=_params(3), name=name,
    )(q, k, v)


def _route(logits):
    lane = lax.broadcasted_iota(jnp.int32, logits.shape, 1).astype(F32)
    big = jnp.float32(1e9)
    is_grp = (lane >= N_EXPERTS) & (lane < N_EXPERTS + N_GROUPS)
    gl = jnp.where(is_grp, logits, NEG)
    gm = jnp.max(gl, axis=-1, keepdims=True)
    g_w = 1.0 / jnp.sum(jnp.exp(gl - gm), axis=-1, keepdims=True)
    g_idx = jnp.min(jnp.where(gl == gm, lane, big), axis=-1, keepdims=True) - N_EXPERTS
    lo = g_idx * EXPERTS_PER_GROUP
    el = jnp.where((lane >= lo) & (lane < lo + EXPERTS_PER_GROUP), logits, NEG)
    m1 = jnp.max(el, axis=-1, keepdims=True)
    i1 = jnp.min(jnp.where(el == m1, lane, big), axis=-1, keepdims=True)
    el2 = jnp.where(lane == i1, NEG, el)
    m2 = jnp.max(el2, axis=-1, keepdims=True)
    i2 = jnp.min(jnp.where(el2 == m2, lane, big), axis=-1, keepdims=True)
    t = jnp.exp(m2 - m1)
    w1 = g_w / (1.0 + t)
    w2 = g_w * t / (1.0 + t)
    return jnp.where(lane == i1, w1, 0.0) + jnp.where(lane == i2, w2, 0.0), g_idx


SUB = 8
Y_ROWS = D // LANES
H_ROWS = 2 * SUB


def _to_token_tiles(ref, rows_per_token, n_tokens, pieces):
    for j, piece in enumerate(pieces):
        ref[pl.ds(j, n_tokens, stride=rows_per_token), :] = piece


def _from_token_tiles(ref, rows_per_token, n_tokens, j, lead=()):
    return ref[lead + (pl.ds(j, n_tokens, stride=rows_per_token), slice(None))]


TM_EXPERT = 256
ROW_DMA_UNROLL = 8


def _route_kernel(x_ref, o_ref, wo_ref, gta_ref, shm_ref, scm_ref, gn_ref,
                  wr_hi_ref, wr_lo_ref, br_ref, sel_ref, xm_ref, row_ref, gidx_ref):
    mix = jnp.dot(o_ref[...], wo_ref[...], preferred_element_type=F32)
    xm = x_ref[...] + gta_ref[...] * mix
    xm_ref[...] = xm
    h = _rms(xm, gn_ref[...]) * (1.0 + scm_ref[...]) + shm_ref[...]
    h_hi = h.astype(BF16)
    h_lo = (h - h_hi.astype(F32)).astype(BF16)
    logits = (jnp.dot(h_hi, wr_hi_ref[...], preferred_element_type=F32)
              + jnp.dot(h_lo, wr_hi_ref[...], preferred_element_type=F32)
              + jnp.dot(h_hi, wr_lo_ref[...], preferred_element_type=F32)) + br_ref[...]
    comb, g_idx = _route(logits)
    tm = h.shape[0]
    zero = jnp.zeros_like(comb)
    _to_token_tiles(row_ref, H_ROWS, tm,
                    [h[:, j * LANES:(j + 1) * LANES] for j in range(Y_ROWS)] + [comb]
                    + [zero] * (H_ROWS - Y_ROWS - 1))
    lane = lax.broadcasted_iota(jnp.int32, comb.shape, 1)
    g_mat = jnp.where(lane == 0, g_idx, 0.0).astype(BF16)
    g_row = lax.dot_general(sel_ref[...], g_mat, (((1,), (1,)), ((), ())), preferred_element_type=F32)
    gidx_ref[...] = g_row[0:1, :].astype(jnp.int32)


def _dispatch_kernel(gidx_ref, rows_ref, hs_hbm, dest_ref, cnt_ref, sem, *, tm, cap):
    i = pl.program_id(0)

    @pl.when(i == 0)
    def _():
        for k in range(N_GROUPS):
            cnt_ref[k] = 0

    def body(t, carry):
        tok = i * tm + t
        g = gidx_ref[tok]
        c = cnt_ref[g]
        cnt_ref[g] = c + 1
        d = g * cap + c
        dest_ref[tok] = d
        pltpu.make_async_copy(rows_ref.at[pl.ds(pl.multiple_of(t * H_ROWS, H_ROWS), H_ROWS)],
                              hs_hbm.at[pl.ds(pl.multiple_of(d * H_ROWS, H_ROWS), H_ROWS)], sem.at[0]).start()
        return carry

    lax.fori_loop(0, tm, body, 0, unroll=ROW_DMA_UNROLL)
    pltpu.make_async_copy(rows_ref, hs_hbm.at[pl.ds(0, tm * H_ROWS)], sem.at[0]).wait()


def _experts_kernel(blk_ref, grp_ref, first_ref, nt_ref, hs_ref, wg_ref, wu_ref, wd_ref, ys_ref,
                    wgb_ref, wub_ref, wdb_ref):
    i = pl.program_id(0)

    @pl.when(i < nt_ref[0])
    def _():
        @pl.when(first_ref[i] == 1)
        def _():
            for j in range(EXPERTS_PER_GROUP):
                wgb_ref[j] = wg_ref[j].astype(BF16)
                wub_ref[j] = wu_ref[j].astype(BF16)
                wdb_ref[j * EXPERT_FF:(j + 1) * EXPERT_FF, :] = wd_ref[j].astype(BF16)

        x = jnp.concatenate([_from_token_tiles(hs_ref, H_ROWS, TM_EXPERT, j) for j in range(Y_ROWS)],
                            axis=1).astype(BF16)
        comb = _from_token_tiles(hs_ref, H_ROWS, TM_EXPERT, Y_ROWS)
        lane = lax.broadcasted_iota(jnp.int32, comb.shape, 1)
        e0 = grp_ref[i] * EXPERTS_PER_GROUP
        hids = []
        for j in range(EXPERTS_PER_GROUP):
            g = jnp.dot(x, wgb_ref[j], preferred_element_type=F32)
            u = jnp.dot(x, wub_ref[j], preferred_element_type=F32)
            c = jnp.sum(jnp.where(lane == e0 + j, comb, 0.0), axis=-1, keepdims=True)
            hids.append(((g / (1.0 + jnp.exp(-g))) * u * c).astype(BF16))
        hid = jnp.concatenate(hids, axis=1)
        y = jnp.dot(hid, wdb_ref[...], preferred_element_type=F32)
        _to_token_tiles(ys_ref, Y_ROWS, TM_EXPERT, [y[:, j * LANES:(j + 1) * LANES] for j in range(Y_ROWS)])


def _combine_kernel(dest_ref, xm_ref, gtm_ref, fin_ref, ys_hbm, y_ref, ybuf, sem, *, tm, final):
    i = pl.program_id(0)
    slot = i % 2

    def issue(tile, s):
        def body(t, carry):
            d = dest_ref[tile * tm + t]
            pltpu.make_async_copy(ys_hbm.at[pl.ds(pl.multiple_of(d * Y_ROWS, Y_ROWS), Y_ROWS)],
                                  ybuf.at[s, pl.ds(pl.multiple_of(t * Y_ROWS, Y_ROWS), Y_ROWS)], sem.at[s]).start()
            return carry
        lax.fori_loop(0, tm, body, 0, unroll=ROW_DMA_UNROLL)

    @pl.when(i == 0)
    def _():
        issue(0, 0)

    @pl.when(i + 1 < pl.num_programs(0))
    def _():
        issue(i + 1, 1 - slot)

    pltpu.make_async_copy(ys_hbm.at[pl.ds(0, tm * Y_ROWS)], ybuf.at[slot], sem.at[slot]).wait()
    y = jnp.concatenate([_from_token_tiles(ybuf, Y_ROWS, tm, j, lead=(slot,)) for j in range(Y_ROWS)], axis=1)
    out = xm_ref[...] + gtm_ref[...] * y
    if final:
        out = _rms(out, fin_ref[...])
    y_ref[...] = out


def _tile_table(counts, cap, max_tiles):
    nt_g = (counts + TM_EXPERT - 1) // TM_EXPERT
    ends = jnp.cumsum(nt_g)
    starts = ends - nt_g
    total = ends[-1]
    i = jnp.arange(max_tiles, dtype=jnp.int32)
    iv = jnp.minimum(i, total - 1)
    grp = jnp.sum((iv[:, None] >= ends[None, :]).astype(jnp.int32), axis=1)
    blk = grp * (cap // TM_EXPERT) + iv - starts[grp]
    first = ((iv == starts[grp]) & (i < total)).astype(jnp.int32)
    return blk.astype(jnp.int32), grp.astype(jnp.int32), first, total.reshape(1).astype(jnp.int32)


def _post(x2d, o2d, mods, layer, w, *, tm, mod_row, final):
    n = x2d.shape[0]
    nt = n // tm
    cap = n
    row_fn = lambda i, *_: mod_row(i)
    tile = pl.BlockSpec((tm, D), lambda i: (i, 0))
    xm, rows, gidx = pl.pallas_call(
        _route_kernel,
        grid=(nt,),
        in_specs=[
            tile, tile, _full_spec((D, D)),
            _mod_spec(layer, 2, row_fn), _mod_spec(layer, 3, row_fn), _mod_spec(layer, 4, row_fn),
            _full_spec((1, D)),
            _full_spec((D, LANES)), _full_spec((D, LANES)), _full_spec((1, LANES)), _full_spec((8, LANES)),
        ],
        out_specs=[tile, pl.BlockSpec((tm * H_ROWS, LANES), lambda i: (i, 0)),
                   pl.BlockSpec((None, 1, tm), lambda i: (i, 0, 0))],
        out_shape=[jax.ShapeDtypeStruct((n, D), F32), jax.ShapeDtypeStruct((n * H_ROWS, LANES), F32),
                   jax.ShapeDtypeStruct((nt, 1, tm), jnp.int32)],
        compiler_params=_params(1), name=f"route_l{layer}",
    )(x2d, o2d, w["wo"], mods, mods, mods, w["gn"], w["wr_hi"], w["wr_lo"], w["br"], w["sel"])

    hs, dest, counts = pl.pallas_call(
        functools.partial(_dispatch_kernel, tm=tm, cap=cap),
        grid_spec=pltpu.PrefetchScalarGridSpec(
            num_scalar_prefetch=1, grid=(nt,),
            in_specs=[pl.BlockSpec((tm * H_ROWS, LANES), lambda i, gidx_: (i, 0))],
            out_specs=[pl.BlockSpec(memory_space=pl.ANY), pl.BlockSpec(memory_space=pltpu.SMEM),
                       pl.BlockSpec(memory_space=pltpu.SMEM)],
            scratch_shapes=[pltpu.SemaphoreType.DMA((1,))]),
        out_shape=[jax.ShapeDtypeStruct((N_GROUPS * cap * H_ROWS, LANES), F32), jax.ShapeDtypeStruct((n,), jnp.int32),
                   jax.ShapeDtypeStruct((N_GROUPS,), jnp.int32)],
        compiler_params=_params(1), name=f"dispatch_l{layer}",
    )(gidx.reshape(n), rows)

    max_tiles = n // TM_EXPERT + N_GROUPS
    blk, grp, first, total = _tile_table(counts, cap, max_tiles)
    wspec = lambda shape: pl.BlockSpec((None,) + shape, lambda i, blk, grp, first, nt_: (layer, grp[i], 0, 0))
    ys = pl.pallas_call(
        _experts_kernel,
        grid_spec=pltpu.PrefetchScalarGridSpec(
            num_scalar_prefetch=4, grid=(max_tiles,),
            in_specs=[
                pl.BlockSpec((TM_EXPERT * H_ROWS, LANES), lambda i, blk, grp, first, nt_: (blk[i], 0)),
                wspec((EXPERTS_PER_GROUP, D, EXPERT_FF)), wspec((EXPERTS_PER_GROUP, D, EXPERT_FF)),
                wspec((EXPERTS_PER_GROUP, EXPERT_FF, D)),
            ],
            out_specs=pl.BlockSpec((TM_EXPERT * Y_ROWS, LANES), lambda i, blk, grp, first, nt_: (blk[i], 0)),
            scratch_shapes=[pltpu.VMEM((EXPERTS_PER_GROUP, D, EXPERT_FF), BF16),
                            pltpu.VMEM((EXPERTS_PER_GROUP, D, EXPERT_FF), BF16),
                            pltpu.VMEM((EXPERTS_PER_GROUP * EXPERT_FF, D), BF16)]),
        out_shape=jax.ShapeDtypeStruct((N_GROUPS * cap * Y_ROWS, LANES), F32),
        compiler_params=_params(1), name=f"experts_l{layer}",
    )(blk, grp, first, total, hs, w["wg"], w["wu"], w["wd"])

    return pl.pallas_call(
        functools.partial(_combine_kernel, tm=tm, final=final),
        grid_spec=pltpu.PrefetchScalarGridSpec(
            num_scalar_prefetch=1, grid=(nt,),
            in_specs=[pl.BlockSpec((tm, D), lambda i, dest_: (i, 0)), _mod_spec(layer, 5, row_fn),
                      _full_spec((1, D)), pl.BlockSpec(memory_space=pl.ANY)],
            out_specs=pl.BlockSpec((tm, D), lambda i, dest_: (i, 0)),
            scratch_shapes=[pltpu.VMEM((2, tm * Y_ROWS, LANES), F32), pltpu.SemaphoreType.DMA((2,))]),
        out_shape=jax.ShapeDtypeStruct((n, D), F32),
        compiler_params=_params(1), name=f"combine_l{layer}",
    )(dest, xm, mods, w["fin"], ys)


def _mla_weights(w_dq, g_q, w_uq, w_dkv, g_kv, w_uk, w_uv):
    hd = MLA_NOPE + MLA_ROPE
    wuq = w_uq.reshape(Q_LORA, MLA_HEADS, hd)
    wuq = jnp.concatenate([wuq[..., MLA_NOPE:], wuq[..., :MLA_NOPE],
                           jnp.zeros((Q_LORA, MLA_HEADS, LANES - hd), F32)], axis=-1)
    wuk = jnp.concatenate([jnp.zeros((KV_LORA, MLA_HEADS, MLA_ROPE), F32), w_uk,
                           jnp.zeros((KV_LORA, MLA_HEADS, LANES - hd), F32)], axis=-1)
    pe_rows = jnp.broadcast_to(jnp.eye(MLA_ROPE, LANES, dtype=F32)[:, None, :], (MLA_ROPE, MLA_HEADS, LANES))
    wkexp = jnp.concatenate([wuk, pe_rows, jnp.zeros((KV_PAD - KV_DIM, MLA_HEADS, LANES), F32)], axis=0)
    return {
        "wdq": w_dq.astype(BF16), "gq": g_q.reshape(1, Q_LORA),
        "wuq": wuq.reshape(Q_LORA, MLA_HEADS * LANES).astype(BF16),
        "wdkv": jnp.pad(w_dkv, ((0, 0), (0, KV_PAD - KV_DIM))).astype(BF16), "gkv": g_kv.reshape(1, KV_LORA),
        "wkexp": wkexp.reshape(KV_PAD, MLA_HEADS * LANES).astype(BF16),
        "wuv": w_uv.reshape(KV_LORA, MLA_HEADS * MLA_V).astype(BF16),
    }


def _gqa_weights(w_qkv, g_q, g_k):
    grp = jnp.arange(2 * LANES) // GQA_HEAD_DIM
    bmat = jnp.where(grp[:, None] == grp[None, :], 1.0 / GQA_HEAD_DIM, 0.0).astype(BF16)
    reps = 2 * LANES // GQA_HEAD_DIM
    return {"wqkv": w_qkv.astype(BF16), "gq": jnp.tile(g_q, reps).reshape(1, 2 * LANES),
            "gk": jnp.tile(g_k, reps).reshape(1, 2 * LANES), "bmat": bmat}


def _post_weights(l, w_o, norm_ffn, w_group, b_group, w_exp, b_exp, w_gate, w_up, w_down, final_norm):
    wr = jnp.concatenate([w_exp[l], w_group[l], jnp.zeros((D, LANES - N_EXPERTS - N_GROUPS), F32)], axis=1)
    br = jnp.concatenate([b_exp[l], b_group[l], jnp.zeros((LANES - N_EXPERTS - N_GROUPS,), F32)]).reshape(1, LANES)
    wr_hi = wr.astype(BF16)
    wr_lo = (wr - wr_hi.astype(F32)).astype(BF16)
    return {"wo": w_o.astype(BF16), "gn": norm_ffn[l].reshape(1, D), "wr_hi": wr_hi, "wr_lo": wr_lo, "br": br,
            "wg": w_gate, "wu": w_up, "wd": w_down, "fin": final_norm.reshape(1, D),
            "sel": jnp.zeros((8, LANES), BF16).at[0, 0].set(1.0)}


def kernel(x_prompt, x_sample, cache_mla, cache_gqa, c, c_ctx, ada_w, ada_b, norm_mix, norm_ffn,
           mla_w_dq, mla_q_norm, mla_w_uq, mla_w_dkv, mla_kv_norm, mla_w_uk, mla_w_uv, mla_w_o,
           gqa_w_qkv, gqa_q_norm, gqa_k_norm, gqa_w_o,
           moe_w_group, moe_b_group, moe_w_expert, moe_b_expert, moe_w_gate, moe_w_up, moe_w_down,
           final_norm):
    bp, tp, _ = x_prompt.shape
    bs, ts, _ = x_sample.shape
    t_past = cache_mla.shape[2]
    assert ada_w.shape[0] == 2 and 1 + bs <= N_MOD_ROWS
    n_p, n_s = bp * tp, bs * ts

    cc = jnp.concatenate([c_ctx[None], c, jnp.zeros((N_MOD_ROWS - 1 - bs, D), F32)], axis=0)
    mods = _ada_table(cc, ada_w, ada_b)

    xp = x_prompt.reshape(n_p, D)
    xs = x_sample.reshape(n_s, D)
    tm_proj = 512
    tm_post = 512
    prompt_row = lambda *_: 0
    sample_row_proj = lambda b: 1 + b
    sample_row_post = lambda i: 1 + i // (ts // tm_post)
    moe_args = (norm_ffn, moe_w_group, moe_b_group, moe_w_expert, moe_b_expert,
                moe_w_gate, moe_w_up, moe_w_down, final_norm)

    wm = _mla_weights(mla_w_dq[0], mla_q_norm[0], mla_w_uq[0], mla_w_dkv[0], mla_kv_norm[0],
                      mla_w_uk[0], mla_w_uv[0])
    gn0 = norm_mix[0].reshape(1, D)
    q, k, v, keys_p = _mla_proj(xp, mods, gn0, wm, n_batch=1, t_new=n_p, tm=tm_proj, mod_row=prompt_row)
    o = _attention(q, k, v, mla=True, n_batch=bp, t_q=tp, t_k=tp, tq=tp, n_pairs=8, name="attn_mla_p")
    wp0 = _post_weights(0, mla_w_o[0], *moe_args)
    xp = _post(xp, o, mods, 0, wp0, tm=tm_post, mod_row=prompt_row, final=False)

    rope_mla = _rope_tables(ts, MLA_ROPE, (0,))
    cache0 = jnp.pad(cache_mla[:, 0].reshape(bs * t_past, KV_DIM), ((0, 0), (0, KV_PAD - KV_DIM)))
    q, k, v = _mla_proj(xs, mods, gn0, wm, n_batch=bs, t_new=ts, tm=tm_proj, mod_row=sample_row_proj,
                        rope_tabs=rope_mla, cache=cache0)
    o = _attention(q, k, v, mla=True, n_batch=bs, t_q=ts, t_k=t_past + ts, tq=512, n_pairs=1, name="attn_mla_s")
    xs = _post(xs, o, mods, 0, wp0, tm=tm_post, mod_row=sample_row_post, final=False)

    wg = _gqa_weights(gqa_w_qkv[0], gqa_q_norm[0], gqa_k_norm[0])
    gn1 = norm_mix[1].reshape(1, D)
    q, k, v, kv_p = _gqa_proj(xp, mods, gn1, wg, n_batch=1, t_new=n_p, tm=tm_proj, mod_row=prompt_row)
    o = _attention(q, k, v, mla=False, n_batch=bp, t_q=tp, t_k=tp, tq=tp, n_pairs=8, name="attn_gqa_p")
    wp1 = _post_weights(1, gqa_w_o[0], *moe_args)
    y_prompt = _post(xp, o, mods, 1, wp1, tm=tm_post, mod_row=prompt_row, final=True)

    rope_gqa = _rope_tables(ts, GQA_HEAD_DIM, (0, GQA_HEAD_DIM))
    cache1 = cache_gqa[:, 0].reshape(bs * t_past, 2 * GQA_KV_HEADS * GQA_HEAD_DIM)
    q, k, v = _gqa_proj(xs, mods, gn1, wg, n_batch=bs, t_new=ts, tm=tm_proj, mod_row=sample_row_proj,
                        rope_tabs=rope_gqa, cache=cache1)
    o = _attention(q, k, v, mla=False, n_batch=bs, t_q=ts, t_k=t_past + ts, tq=512, n_pairs=1, name="attn_gqa_s")
    y_sample = _post(xs, o, mods, 1, wp1, tm=tm_post, mod_row=sample_row_post, final=True)

    return (y_prompt.reshape(bp, tp, D), y_sample.reshape(bs, ts, D),
            keys_p.reshape(bp, 1, tp, KV_DIM),
            kv_p.reshape(bp, 1, tp, 2, GQA_KV_HEADS, GQA_HEAD_DIM))
```

```python
import functools
import math

import jax
import jax.numpy as jnp
from jax import lax
from jax.experimental import pallas as pl
from jax.experimental.pallas import tpu as pltpu

F32 = jnp.float32
BF16 = jnp.bfloat16

D = 1024
EPS = 1e-6
GRID_W = 64
ROPE_THETA = 10000.0
LANES = 128
HALF = LANES // 2
MLA_HEADS = 16
MLA_NOPE = 64
MLA_ROPE = 32
MLA_V = 64
Q_LORA = 384
KV_LORA = 256
KV_DIM = KV_LORA + MLA_ROPE
KV_PAD = 384
MLA_SCALE = 1.0 / math.sqrt(MLA_NOPE + MLA_ROPE)
GQA_HEADS = 16
GQA_KV_HEADS = 4
GQA_HEAD_DIM = 64
GQA_SCALE = 1.0 / math.sqrt(GQA_HEAD_DIM)
N_GROUPS = 4
EXPERTS_PER_GROUP = 4
N_EXPERTS = 16
EXPERT_FF = 256
NEG = -3.0e38
LOG2E = 1.4426950408889634
ATTN_KEY_CHUNK = 512

VMEM_LIMIT = 56 * 1024 * 1024
N_MOD_ROWS = 8


def _params(n_axes):
    return pltpu.CompilerParams(dimension_semantics=("arbitrary",) * n_axes,
                                vmem_limit_bytes=VMEM_LIMIT)


def _rms(x, g):
    ms = jnp.mean(x * x, axis=-1, keepdims=True)
    return x * lax.rsqrt(ms + EPS) * g


def _mod_index(layer, which, row):
    return (layer * 6 + which) * N_MOD_ROWS + row


def _mod_spec(layer, which, row_fn):
    return pl.BlockSpec((None, 1, D), lambda *g: (_mod_index(layer, which, row_fn(*g)), 0, 0))


def _full_spec(shape):
    n = len(shape)
    return pl.BlockSpec(shape, lambda *g: (0,) * n)


def _ada_kernel(c_ref, w_ref, b_ref, o_ref):
    c = c_ref[...]
    a = c / (1.0 + jnp.exp(-c))
    o_ref[...] = jnp.dot(a, w_ref[...], precision=lax.Precision.HIGHEST,
                         preferred_element_type=F32) + b_ref[...]


def _ada_table(cc, ada_w, ada_b):
    depth = ada_w.shape[0]
    out = pl.pallas_call(
        _ada_kernel,
        grid=(depth, 6),
        in_specs=[
            pl.BlockSpec((N_MOD_ROWS, D), lambda l, n: (0, 0)),
            pl.BlockSpec((None, D, D), lambda l, n: (l, 0, n)),
            pl.BlockSpec((None, 1, D), lambda l, n: (l, 0, n)),
        ],
        out_specs=pl.BlockSpec((None, None, N_MOD_ROWS, D), lambda l, n: (l, n, 0, 0)),
        out_shape=jax.ShapeDtypeStruct((depth, 6, N_MOD_ROWS, D), F32),
        compiler_params=_params(2),
        name="ada",
    )(cc, ada_w, ada_b.reshape(depth, 1, 6 * D))
    return out.reshape(depth * 6 * N_MOD_ROWS, 1, D)


def _rope(a, cos, sin_signed):
    lane = lax.broadcasted_iota(jnp.int32, a.shape, 1)
    nxt = pltpu.roll(a, LANES - 1, 1)
    prv = pltpu.roll(a, 1, 1)
    return a * cos + jnp.where((lane & 1) == 0, nxt, prv) * sin_signed


def _rope_tables(n_tokens, rot_dim, lane_offset_pattern):
    t = jnp.arange(n_tokens)
    row = (t // GRID_W).astype(F32)
    col = (t % GRID_W).astype(F32)
    axis_dim = rot_dim // 2
    inv = jnp.power(ROPE_THETA, -jnp.arange(0, axis_dim, 2, dtype=F32) / axis_dim)
    ang = jnp.concatenate([row[:, None] * inv, col[:, None] * inv], axis=-1)
    cos = jnp.repeat(jnp.cos(ang), 2, axis=1)
    sin = jnp.repeat(jnp.sin(ang), 2, axis=1)
    sign = jnp.where(jnp.arange(rot_dim) % 2 == 0, -1.0, 1.0).astype(F32)
    sin = sin * sign
    cos_t = jnp.ones((n_tokens, LANES), F32)
    sin_t = jnp.zeros((n_tokens, LANES), F32)
    for off in lane_offset_pattern:
        cos_t = cos_t.at[:, off:off + rot_dim].set(cos)
        sin_t = sin_t.at[:, off:off + rot_dim].set(sin)
    return cos_t, sin_t


def _mla_proj_kernel(*refs, rope, n_cache, emit_keys):
    it = iter(refs)
    x_ref, sh_ref, sc_ref, gn_ref = next(it), next(it), next(it), next(it)
    wdq_ref, gq_ref, wuq_ref, wdkv_ref, gkv_ref, wkexp_ref, wuv_ref = (next(it) for _ in range(7))
    cos_ref = sin_ref = cache_ref = keys_ref = None
    if rope:
        cos_ref, sin_ref = next(it), next(it)
    if n_cache:
        cache_ref = next(it)
    q_ref, k_ref, v_ref = next(it), next(it), next(it)
    if emit_keys:
        keys_ref = next(it)

    def expand(keys_bf16):
        k_ref[...] = jnp.dot(keys_bf16, wkexp_ref[...], preferred_element_type=F32).astype(BF16)
        v_ref[...] = jnp.dot(keys_bf16[:, :KV_LORA], wuv_ref[...], preferred_element_type=F32).astype(BF16)

    def new_tokens():
        x = x_ref[...]
        h = (_rms(x, gn_ref[...]) * (1.0 + sc_ref[...]) + sh_ref[...]).astype(BF16)
        ql = jnp.dot(h, wdq_ref[...], preferred_element_type=F32)
        qn = _rms(ql, gq_ref[...]).astype(BF16)
        q = jnp.dot(qn, wuq_ref[...], preferred_element_type=F32)
        kv = jnp.dot(h, wdkv_ref[...], preferred_element_type=F32)
        ckv = _rms(kv[:, :KV_LORA], gkv_ref[...])
        pe = kv[:, KV_LORA:]
        if rope:
            cos, sin = cos_ref[...], sin_ref[...]
            pe = _rope(pe, cos, sin)
            q = jnp.concatenate(
                [_rope(q[:, h_ * LANES:(h_ + 1) * LANES], cos, sin) for h_ in range(MLA_HEADS)], axis=1)
        q_ref[...] = (q * (MLA_SCALE * LOG2E)).astype(BF16)
        if emit_keys:
            keys_ref[:, 0:KV_LORA] = ckv
            keys_ref[:, KV_LORA:KV_DIM] = pe[:, 0:MLA_ROPE]
        expand(jnp.concatenate([ckv, pe], axis=1).astype(BF16))

    if n_cache:
        t = pl.program_id(1)
        pl.when(t >= n_cache)(new_tokens)

        @pl.when(t < n_cache)
        def _():
            expand(cache_ref[...].astype(BF16))
    else:
        new_tokens()


def _mla_proj(x2d, mods, gn, w, *, n_batch, t_new, tm, mod_row, rope_tabs=None, cache=None):
    rope = rope_tabs is not None
    t_cache = 0 if cache is None else cache.shape[0] // n_batch
    n_cache = t_cache // tm
    ntn = t_new // tm
    nt = n_cache + ntn
    emit_keys = cache is None

    def new_idx(b, t):
        return b * ntn + jnp.maximum(t - n_cache, 0)

    row_fn = lambda b, t: mod_row(b)
    in_specs = [
        pl.BlockSpec((tm, D), lambda b, t: (new_idx(b, t), 0)),
        _mod_spec(0, 0, row_fn), _mod_spec(0, 1, row_fn),
        _full_spec((1, D)),
        _full_spec((D, Q_LORA)), _full_spec((1, Q_LORA)), _full_spec((Q_LORA, MLA_HEADS * LANES)),
        _full_spec((D, KV_PAD)), _full_spec((1, KV_LORA)),
        _full_spec((KV_PAD, MLA_HEADS * LANES)), _full_spec((KV_LORA, MLA_HEADS * MLA_V)),
    ]
    args = [x2d, mods, mods, gn, w["wdq"], w["gq"], w["wuq"], w["wdkv"], w["gkv"], w["wkexp"], w["wuv"]]
    if rope:
        in_specs += [pl.BlockSpec((tm, LANES), lambda b, t: (jnp.maximum(t - n_cache, 0), 0))] * 2
        args += list(rope_tabs)
    if n_cache:
        in_specs.append(pl.BlockSpec((tm, KV_PAD), lambda b, t: (b * n_cache + jnp.minimum(t, n_cache - 1), 0)))
        args.append(cache)
    n_new = n_batch * t_new
    n_keys = n_batch * (t_cache + t_new)
    out_specs = [
        pl.BlockSpec((tm, MLA_HEADS * LANES), lambda b, t: (new_idx(b, t), 0)),
        pl.BlockSpec((tm, MLA_HEADS * LANES), lambda b, t: (b * nt + t, 0)),
        pl.BlockSpec((tm, MLA_HEADS * MLA_V), lambda b, t: (b * nt + t, 0)),
    ]
    out_shape = [
        jax.ShapeDtypeStruct((n_new, MLA_HEADS * LANES), BF16),
        jax.ShapeDtypeStruct((n_keys, MLA_HEADS * LANES), BF16),
        jax.ShapeDtypeStruct((n_keys, MLA_HEADS * MLA_V), BF16),
    ]
    if emit_keys:
        out_specs.append(pl.BlockSpec((tm, KV_DIM), lambda b, t: (new_idx(b, t), 0)))
        out_shape.append(jax.ShapeDtypeStruct((n_new, KV_DIM), F32))
    return pl.pallas_call(
        functools.partial(_mla_proj_kernel, rope=rope, n_cache=n_cache, emit_keys=emit_keys),
        grid=(n_batch, nt), in_specs=in_specs, out_specs=out_specs, out_shape=out_shape,
        compiler_params=_params(2), name="mla_proj_s" if rope else "mla_proj_p",
    )(*args)


def _dup_halves(a):
    cols = []
    for c in range(a.shape[1] // LANES):
        blk = a[:, c * LANES:(c + 1) * LANES]
        rot = pltpu.roll(blk, HALF, 1)
        low = lax.broadcasted_iota(jnp.int32, blk.shape, 1) < HALF
        cols += [jnp.where(low, blk, rot), jnp.where(low, rot, blk)]
    return jnp.concatenate(cols, axis=1)


def _group_mean_sq(a, bmat):
    sq = a * a
    hi = sq.astype(BF16)
    lo = (sq - hi.astype(F32)).astype(BF16)
    return (jnp.dot(hi, bmat, preferred_element_type=F32) + jnp.dot(lo, bmat, preferred_element_type=F32))


def _gqa_proj_kernel(*refs, rope, n_cache, emit_kv):
    it = iter(refs)
    x_ref, sh_ref, sc_ref, gn_ref, w_ref, gq_ref, gk_ref, bmat_ref = (next(it) for _ in range(8))
    cos_ref = sin_ref = cache_ref = kv_ref = None
    if rope:
        cos_ref, sin_ref = next(it), next(it)
    if n_cache:
        cache_ref = next(it)
    q_ref, k_ref, v_ref = next(it), next(it), next(it)
    if emit_kv:
        kv_ref = next(it)
    nq = GQA_HEADS * GQA_HEAD_DIM
    nk = GQA_KV_HEADS * GQA_HEAD_DIM
    blk = 2 * LANES

    def new_tokens():
        x = x_ref[...]
        h = (_rms(x, gn_ref[...]) * (1.0 + sc_ref[...]) + sh_ref[...]).astype(BF16)
        qkv = jnp.dot(h, w_ref[...], preferred_element_type=F32)
        bmat = bmat_ref[...]

        def head_norm(a, g):
            return a * lax.rsqrt(_group_mean_sq(a, bmat) + EPS) * g

        def maybe_rope(a):
            if not rope:
                return a
            cos, sin = cos_ref[...], sin_ref[...]
            return jnp.concatenate(
                [_rope(a[:, c * LANES:(c + 1) * LANES], cos, sin) for c in range(a.shape[1] // LANES)], axis=1)

        for c in range(nq // blk):
            qb = head_norm(qkv[:, c * blk:(c + 1) * blk], gq_ref[...])
            q_ref[:, c * blk:(c + 1) * blk] = (maybe_rope(qb) * (GQA_SCALE * LOG2E)).astype(BF16)
        kn = head_norm(qkv[:, nq:nq + nk], gk_ref[...])
        vv = qkv[:, nq + nk:]
        if emit_kv:
            kv_ref[:, 0:nk] = kn
            kv_ref[:, nk:2 * nk] = vv
        k_ref[...] = _dup_halves(maybe_rope(kn)).astype(BF16)
        v_ref[...] = _dup_halves(vv).astype(BF16)

    if n_cache:
        t = pl.program_id(1)
        pl.when(t >= n_cache)(new_tokens)

        @pl.when(t < n_cache)
        def _():
            c = cache_ref[...]
            k_ref[...] = _dup_halves(c[:, 0:nk]).astype(BF16)
            v_ref[...] = _dup_halves(c[:, nk:2 * nk]).astype(BF16)
    else:
        new_tokens()


def _gqa_proj(x2d, mods, gn, w, *, n_batch, t_new, tm, mod_row, rope_tabs=None, cache=None):
    rope = rope_tabs is not None
    t_cache = 0 if cache is None else cache.shape[0] // n_batch
    n_cache = t_cache // tm
    ntn = t_new // tm
    nt = n_cache + ntn
    emit_kv = cache is None
    nq = GQA_HEADS * GQA_HEAD_DIM
    nk = GQA_KV_HEADS * GQA_HEAD_DIM
    nqkv = nq + 2 * nk
    ndup = GQA_KV_HEADS * LANES

    def new_idx(b, t):
        return b * ntn + jnp.maximum(t - n_cache, 0)

    row_fn = lambda b, t: mod_row(b)
    in_specs = [
        pl.BlockSpec((tm, D), lambda b, t: (new_idx(b, t), 0)),
        _mod_spec(1, 0, row_fn), _mod_spec(1, 1, row_fn),
        _full_spec((1, D)), _full_spec((D, nqkv)),
        _full_spec((1, 2 * LANES)), _full_spec((1, 2 * LANES)), _full_spec((2 * LANES, 2 * LANES)),
    ]
    args = [x2d, mods, mods, gn, w["wqkv"], w["gq"], w["gk"], w["bmat"]]
    if rope:
        in_specs += [pl.BlockSpec((tm, LANES), lambda b, t: (jnp.maximum(t - n_cache, 0), 0))] * 2
        args += list(rope_tabs)
    if n_cache:
        in_specs.append(pl.BlockSpec((tm, 2 * nk), lambda b, t: (b * n_cache + jnp.minimum(t, n_cache - 1), 0)))
        args.append(cache)
    n_new = n_batch * t_new
    n_keys = n_batch * (t_cache + t_new)
    out_specs = [
        pl.BlockSpec((tm, nq), lambda b, t: (new_idx(b, t), 0)),
        pl.BlockSpec((tm, ndup), lambda b, t: (b * nt + t, 0)),
        pl.BlockSpec((tm, ndup), lambda b, t: (b * nt + t, 0)),
    ]
    out_shape = [
        jax.ShapeDtypeStruct((n_new, nq), BF16),
        jax.ShapeDtypeStruct((n_keys, ndup), BF16),
        jax.ShapeDtypeStruct((n_keys, ndup), BF16),
    ]
    if emit_kv:
        out_specs.append(pl.BlockSpec((tm, 2 * nk), lambda b, t: (new_idx(b, t), 0)))
        out_shape.append(jax.ShapeDtypeStruct((n_new, 2 * nk), F32))
    return pl.pallas_call(
        functools.partial(_gqa_proj_kernel, rope=rope, n_cache=n_cache, emit_kv=emit_kv),
        grid=(n_batch, nt), in_specs=in_specs, out_specs=out_specs, out_shape=out_shape,
        compiler_params=_params(2), name="gqa_proj_s" if rope else "gqa_proj_p",
    )(*args)


def _attn_kernel(q_ref, k_ref, v_ref, o_ref, *, mla, n_pairs, ck):
    tq = q_ref.shape[0]
    nc = k_ref.shape[0] // ck
    lane = lax.broadcasted_iota(jnp.int32, (tq, LANES), 1)
    low = lane < HALF
    items = [(p, c, hh) for c in range(nc) for p in range(n_pairs) for hh in range(2)]
    heads = {}
    state = {}

    def head_operands(p, hh):
        if (p, hh) not in heads:
            if mla:
                hd = 2 * p + hh
                heads[(p, hh)] = (q_ref[:, hd * LANES:(hd + 1) * LANES], hd, p)
            else:
                g = p // 2 if n_pairs > 1 else 0
                qp = q_ref[:, p * LANES:(p + 1) * LANES]
                qh = jnp.where(low if hh == 0 else jnp.logical_not(low), qp, jnp.zeros_like(qp))
                heads[(p, hh)] = (qh, g, g)
        return heads[(p, hh)]

    def scores(item):
        p, c, hh = item
        qh, kc, _ = head_operands(p, hh)
        kh = k_ref[c * ck:(c + 1) * ck, kc * LANES:(kc + 1) * LANES]
        return lax.dot_general(qh, kh, (((1,), (1,)), ((), ())), preferred_element_type=F32)

    s_cur = scores(items[0])
    for idx, (p, c, hh) in enumerate(items):
        s_next = scores(items[idx + 1]) if idx + 1 < len(items) else None
        vc = head_operands(p, hh)[2]
        vb = v_ref[c * ck:(c + 1) * ck, vc * LANES:(vc + 1) * LANES]
        m_c = jnp.max(s_cur, axis=-1, keepdims=True)
        if c == 0:
            m = m_c
            e = jnp.exp2(s_cur - m)
            l = jnp.sum(e, axis=-1, keepdims=True)
            acc = jnp.dot(e.astype(BF16), vb, preferred_element_type=F32)
        else:
            m_old, l_old, acc_old = state[(p, hh)]
            m = jnp.maximum(m_old, m_c)
            alpha = jnp.exp2(m_old - m)
            e = jnp.exp2(s_cur - m)
            l = alpha * l_old + jnp.sum(e, axis=-1, keepdims=True)
            acc = alpha * acc_old + jnp.dot(e.astype(BF16), vb, preferred_element_type=F32)
        state[(p, hh)] = (m, l, acc)
        if c == nc - 1 and hh == 1:
            o0 = state[(p, 0)][2] / state[(p, 0)][1]
            o1 = state[(p, 1)][2] / state[(p, 1)][1]
            o_ref[:, p * LANES:(p + 1) * LANES] = jnp.where(low, o0, o1).astype(BF16)
        s_cur = s_next


def _attention(q, k, v, *, mla, n_batch, t_q, t_k, tq, n_pairs, name):
    total_pairs = 8
    nj = total_pairs // n_pairs
    nqt = t_q // tq
    if mla:
        qw, kw, vw = n_pairs * 2 * LANES, n_pairs * 2 * LANES, n_pairs * LANES
        kv_col = lambda j: j
    else:
        qw = n_pairs * LANES
        kw = vw = max(n_pairs // 2, 1) * LANES
        kv_col = (lambda j: j // 2) if n_pairs == 1 else (lambda j: j)
    return pl.pallas_call(
        functools.partial(_attn_kernel, mla=mla, n_pairs=n_pairs, ck=min(t_k, ATTN_KEY_CHUNK)),
        grid=(n_batch, nj, nqt),
        in_specs=[
            pl.BlockSpec((tq, qw), lambda b, j, i: (b * nqt + i, j)),
            pl.BlockSpec((t_k, kw), lambda b, j, i: (b, kv_col(j))),
            pl.BlockSpec((t_k, vw), lambda b, j, i: (b, kv_col(j))),
        ],
        out_specs=pl.BlockSpec((tq, n_pairs * LANES), lambda b, j, i: (b * nqt + i, j)),
        out_shape=jax.ShapeDtypeStruct((n_batch * t_q, total_pairs * LANES), BF16),
        compiler_params=_params(3), name=name,
    )(q, k, v)


def _route(logits):
    lane = lax.broadcasted_iota(jnp.int32, logits.shape, 1).astype(F32)
    big = jnp.float32(1e9)
    is_grp = (lane >= N_EXPERTS) & (lane < N_EXPERTS + N_GROUPS)
    gl = jnp.where(is_grp, logits, NEG)
    gm = jnp.max(gl, axis=-1, keepdims=True)
    g_w = 1.0 / jnp.sum(jnp.exp(gl - gm), axis=-1, keepdims=True)
    g_idx = jnp.min(jnp.where(gl == gm, lane, big), axis=-1, keepdims=True) - N_EXPERTS
    lo = g_idx * EXPERTS_PER_GROUP
    el = jnp.where((lane >= lo) & (lane < lo + EXPERTS_PER_GROUP), logits, NEG)
    m1 = jnp.max(el, axis=-1, keepdims=True)
    i1 = jnp.min(jnp.where(el == m1, lane, big), axis=-1, keepdims=True)
    el2 = jnp.where(lane == i1, NEG, el)
    m2 = jnp.max(el2, axis=-1, keepdims=True)
    i2 = jnp.min(jnp.where(el2 == m2, lane, big), axis=-1, keepdims=True)
    t = jnp.exp(m2 - m1)
    w1 = g_w / (1.0 + t)
    w2 = g_w * t / (1.0 + t)
    return jnp.where(lane == i1, w1, 0.0) + jnp.where(lane == i2, w2, 0.0), g_idx


SUB = 8
Y_ROWS = D // LANES
H_ROWS = 2 * SUB


def _to_token_tiles(ref, rows_per_token, n_tokens, pieces):
    for j, piece in enumerate(pieces):
        ref[pl.ds(j, n_tokens, stride=rows_per_token), :] = piece


def _from_token_tiles(ref, rows_per_token, n_tokens, j, lead=()):
    return ref[lead + (pl.ds(j, n_tokens, stride=rows_per_token), slice(None))]


TM_EXPERT = 256
ROW_DMA_UNROLL = 8


def _route_kernel(x_ref, o_ref, wo_ref, gta_ref, shm_ref, scm_ref, gn_ref,
                  wr_hi_ref, wr_lo_ref, br_ref, sel_ref, xm_ref, row_ref, gidx_ref):
    mix = jnp.dot(o_ref[...], wo_ref[...], preferred_element_type=F32)
    xm = x_ref[...] + gta_ref[...] * mix
    xm_ref[...] = xm
    h = _rms(xm, gn_ref[...]) * (1.0 + scm_ref[...]) + shm_ref[...]
    h_hi = h.astype(BF16)
    h_lo = (h - h_hi.astype(F32)).astype(BF16)
    logits = (jnp.dot(h_hi, wr_hi_ref[...], preferred_element_type=F32)
              + jnp.dot(h_lo, wr_hi_ref[...], preferred_element_type=F32)
              + jnp.dot(h_hi, wr_lo_ref[...], preferred_element_type=F32)) + br_ref[...]
    comb, g_idx = _route(logits)
    tm = h.shape[0]
    zero = jnp.zeros_like(comb)
    _to_token_tiles(row_ref, H_ROWS, tm,
                    [h[:, j * LANES:(j + 1) * LANES] for j in range(Y_ROWS)] + [comb]
                    + [zero] * (H_ROWS - Y_ROWS - 1))
    lane = lax.broadcasted_iota(jnp.int32, comb.shape, 1)
    g_mat = jnp.where(lane == 0, g_idx, 0.0).astype(BF16)
    g_row = lax.dot_general(sel_ref[...], g_mat, (((1,), (1,)), ((), ())), preferred_element_type=F32)
    gidx_ref[...] = g_row[0:1, :].astype(jnp.int32)


def _dispatch_kernel(gidx_ref, rows_ref, hs_hbm, dest_ref, cnt_ref, sem, *, tm, cap):
    i = pl.program_id(0)

    @pl.when(i == 0)
    def _():
        for k in range(N_GROUPS):
            cnt_ref[k] = 0

    def body(t, carry):
        tok = i * tm + t
        g = gidx_ref[tok]
        c = cnt_ref[g]
        cnt_ref[g] = c + 1
        d = g * cap + c
        dest_ref[tok] = d
        pltpu.make_async_copy(rows_ref.at[pl.ds(pl.multiple_of(t * H_ROWS, H_ROWS), H_ROWS)],
                              hs_hbm.at[pl.ds(pl.multiple_of(d * H_ROWS, H_ROWS), H_ROWS)], sem.at[0]).start()
        return carry

    lax.fori_loop(0, tm, body, 0, unroll=ROW_DMA_UNROLL)
    pltpu.make_async_copy(rows_ref, hs_hbm.at[pl.ds(0, tm * H_ROWS)], sem.at[0]).wait()


def _experts_kernel(blk_ref, grp_ref, first_ref, nt_ref, hs_ref, wg_ref, wu_ref, wd_ref, ys_ref,
                    wgb_ref, wub_ref, wdb_ref):
    i = pl.program_id(0)

    @pl.when(i < nt_ref[0])
    def _():
        @pl.when(first_ref[i] == 1)
        def _():
            for j in range(EXPERTS_PER_GROUP):
                wgb_ref[j] = wg_ref[j].astype(BF16)
                wub_ref[j] = wu_ref[j].astype(BF16)
                wdb_ref[j * EXPERT_FF:(j + 1) * EXPERT_FF, :] = wd_ref[j].astype(BF16)

        x = jnp.concatenate([_from_token_tiles(hs_ref, H_ROWS, TM_EXPERT, j) for j in range(Y_ROWS)],
                            axis=1).astype(BF16)
        comb = _from_token_tiles(hs_ref, H_ROWS, TM_EXPERT, Y_ROWS)
        lane = lax.broadcasted_iota(jnp.int32, comb.shape, 1)
        e0 = grp_ref[i] * EXPERTS_PER_GROUP
        hids = []
        for j in range(EXPERTS_PER_GROUP):
            g = jnp.dot(x, wgb_ref[j], preferred_element_type=F32)
            u = jnp.dot(x, wub_ref[j], preferred_element_type=F32)
            c = jnp.sum(jnp.where(lane == e0 + j, comb, 0.0), axis=-1, keepdims=True)
            hids.append(((g / (1.0 + jnp.exp(-g))) * u * c).astype(BF16))
        hid = jnp.concatenate(hids, axis=1)
        y = jnp.dot(hid, wdb_ref[...], preferred_element_type=F32)
        _to_token_tiles(ys_ref, Y_ROWS, TM_EXPERT, [y[:, j * LANES:(j + 1) * LANES] for j in range(Y_ROWS)])


def _combine_kernel(dest_ref, xm_ref, gtm_ref, fin_ref, ys_hbm, y_ref, ybuf, sem, *, tm, final):
    i = pl.program_id(0)
    slot = i % 2

    def issue(tile, s):
        def body(t, carry):
            d = dest_ref[tile * tm + t]
            pltpu.make_async_copy(ys_hbm.at[pl.ds(pl.multiple_of(d * Y_ROWS, Y_ROWS), Y_ROWS)],
                                  ybuf.at[s, pl.ds(pl.multiple_of(t * Y_ROWS, Y_ROWS), Y_ROWS)], sem.at[s]).start()
            return carry
        lax.fori_loop(0, tm, body, 0, unroll=ROW_DMA_UNROLL)

    @pl.when(i == 0)
    def _():
        issue(0, 0)

    @pl.when(i + 1 < pl.num_programs(0))
    def _():
        issue(i + 1, 1 - slot)

    pltpu.make_async_copy(ys_hbm.at[pl.ds(0, tm * Y_ROWS)], ybuf.at[slot], sem.at[slot]).wait()
    y = jnp.concatenate([_from_token_tiles(ybuf, Y_ROWS, tm, j, lead=(slot,)) for j in range(Y_ROWS)], axis=1)
    out = xm_ref[...] + gtm_ref[...] * y
    if final:
        out = _rms(out, fin_ref[...])
    y_ref[...] = out


def _tile_table(counts, cap, max_tiles):
    nt_g = (counts + TM_EXPERT - 1) // TM_EXPERT
    ends = jnp.cumsum(nt_g)
    starts = ends - nt_g
    total = ends[-1]
    i = jnp.arange(max_tiles, dtype=jnp.int32)
    iv = jnp.minimum(i, total - 1)
    grp = jnp.sum((iv[:, None] >= ends[None, :]).astype(jnp.int32), axis=1)
    blk = grp * (cap // TM_EXPERT) + iv - starts[grp]
    first = ((iv == starts[grp]) & (i < total)).astype(jnp.int32)
    return blk.astype(jnp.int32), grp.astype(jnp.int32), first, total.reshape(1).astype(jnp.int32)


def _post(x2d, o2d, mods, layer, w, *, tm, mod_row, final):
    n = x2d.shape[0]
    nt = n // tm
    cap = n
    row_fn = lambda i, *_: mod_row(i)
    tile = pl.BlockSpec((tm, D), lambda i: (i, 0))
    xm, rows, gidx = pl.pallas_call(
        _route_kernel,
        grid=(nt,),
        in_specs=[
            tile, tile, _full_spec((D, D)),
            _mod_spec(layer, 2, row_fn), _mod_spec(layer, 3, row_fn), _mod_spec(layer, 4, row_fn),
            _full_spec((1, D)),
            _full_spec((D, LANES)), _full_spec((D, LANES)), _full_spec((1, LANES)), _full_spec((8, LANES)),
        ],
        out_specs=[tile, pl.BlockSpec((tm * H_ROWS, LANES), lambda i: (i, 0)),
                   pl.BlockSpec((None, 1, tm), lambda i: (i, 0, 0))],
        out_shape=[jax.ShapeDtypeStruct((n, D), F32), jax.ShapeDtypeStruct((n * H_ROWS, LANES), F32),
                   jax.ShapeDtypeStruct((nt, 1, tm), jnp.int32)],
        compiler_params=_params(1), name=f"route_l{layer}",
    )(x2d, o2d, w["wo"], mods, mods, mods, w["gn"], w["wr_hi"], w["wr_lo"], w["br"], w["sel"])

    hs, dest, counts = pl.pallas_call(
        functools.partial(_dispatch_kernel, tm=tm, cap=cap),
        grid_spec=pltpu.PrefetchScalarGridSpec(
            num_scalar_prefetch=1, grid=(nt,),
            in_specs=[pl.BlockSpec((tm * H_ROWS, LANES), lambda i, gidx_: (i, 0))],
            out_specs=[pl.BlockSpec(memory_space=pl.ANY), pl.BlockSpec(memory_space=pltpu.SMEM),
                       pl.BlockSpec(memory_space=pltpu.SMEM)],
            scratch_shapes=[pltpu.SemaphoreType.DMA((1,))]),
        out_shape=[jax.ShapeDtypeStruct((N_GROUPS * cap * H_ROWS, LANES), F32), jax.ShapeDtypeStruct((n,), jnp.int32),
                   jax.ShapeDtypeStruct((N_GROUPS,), jnp.int32)],
        compiler_params=_params(1), name=f"dispatch_l{layer}",
    )(gidx.reshape(n), rows)

    max_tiles = n // TM_EXPERT + N_GROUPS
    blk, grp, first, total = _tile_table(counts, cap, max_tiles)
    wspec = lambda shape: pl.BlockSpec((None,) + shape, lambda i, blk, grp, first, nt_: (layer, grp[i], 0, 0))
    ys = pl.pallas_call(
        _experts_kernel,
        grid_spec=pltpu.PrefetchScalarGridSpec(
            num_scalar_prefetch=4, grid=(max_tiles,),
            in_specs=[
                pl.BlockSpec((TM_EXPERT * H_ROWS, LANES), lambda i, blk, grp, first, nt_: (blk[i], 0)),
                wspec((EXPERTS_PER_GROUP, D, EXPERT_FF)), wspec((EXPERTS_PER_GROUP, D, EXPERT_FF)),
                wspec((EXPERTS_PER_GROUP, EXPERT_FF, D)),
            ],
            out_specs=pl.BlockSpec((TM_EXPERT * Y_ROWS, LANES), lambda i, blk, grp, first, nt_: (blk[i], 0)),
            scratch_shapes=[pltpu.VMEM((EXPERTS_PER_GROUP, D, EXPERT_FF), BF16),
                            pltpu.VMEM((EXPERTS_PER_GROUP, D, EXPERT_FF), BF16),
                            pltpu.VMEM((EXPERTS_PER_GROUP * EXPERT_FF, D), BF16)]),
        out_shape=jax.ShapeDtypeStruct((N_GROUPS * cap * Y_ROWS, LANES), F32),
        compiler_params=_params(1), name=f"experts_l{layer}",
    )(blk, grp, first, total, hs, w["wg"], w["wu"], w["wd"])

    return pl.pallas_call(
        functools.partial(_combine_kernel, tm=tm, final=final),
        grid_spec=pltpu.PrefetchScalarGridSpec(
            num_scalar_prefetch=1, grid=(nt,),
            in_specs=[pl.BlockSpec((tm, D), lambda i, dest_: (i, 0)), _mod_spec(layer, 5, row_fn),
                      _full_spec((1, D)), pl.BlockSpec(memory_space=pl.ANY)],
            out_specs=pl.BlockSpec((tm, D), lambda i, dest_: (i, 0)),
            scratch_shapes=[pltpu.VMEM((2, tm * Y_ROWS, LANES), F32), pltpu.SemaphoreType.DMA((2,))]),
        out_shape=jax.ShapeDtypeStruct((n, D), F32),
        compiler_params=_params(1), name=f"combine_l{layer}",
    )(dest, xm, mods, w["fin"], ys)


MOE_TM = 512
MOE_CHUNK = 128
MOE_SLOTS = 1024
SLOT_RADIX = 32


def _moe_kernel(x_ref, o_ref, wo_ref, gta_ref, shm_ref, scm_ref, gtm_ref, gn_ref,
                wr_hi_ref, wr_lo_ref, br_ref, sel_ref, ltri_ref, wg_ref, wu_ref, wd_ref, fin_ref,
                y_ref, hs_ref, cs_ref, ys_ref, pt_ref, *, final):
    tm = x_ref.shape[0]
    ffg = EXPERTS_PER_GROUP * EXPERT_FF
    mix = jnp.dot(o_ref[...], wo_ref[...], preferred_element_type=F32)
    xm = x_ref[...] + gta_ref[...] * mix
    y_ref[...] = xm
    h = _rms(xm, gn_ref[...]) * (1.0 + scm_ref[...]) + shm_ref[...]
    h_hi = h.astype(BF16)
    h_lo = (h - h_hi.astype(F32)).astype(BF16)
    logits = (jnp.dot(h_hi, wr_hi_ref[...], preferred_element_type=F32)
              + jnp.dot(h_lo, wr_hi_ref[...], preferred_element_type=F32)
              + jnp.dot(h_hi, wr_lo_ref[...], preferred_element_type=F32)) + br_ref[...]
    comb, g_idx = _route(logits)

    lane_i = lax.broadcasted_iota(jnp.int32, (tm, LANES), 1)
    lane = lane_i.astype(F32)
    onehot = jnp.where(lane == g_idx, 1.0, 0.0)
    rank = jnp.dot(ltri_ref[...], onehot.astype(BF16), preferred_element_type=F32)
    cnt = jnp.sum(onehot, axis=0, keepdims=True)
    ends = []
    end = jnp.int32(0)
    start_vec = jnp.zeros_like(lane)
    for g in range(N_GROUPS):
        start_vec = jnp.where(lane_i == g, (end * MOE_CHUNK).astype(F32), start_vec)
        n_g = cnt[0, g].astype(jnp.int32)
        end = end + lax.shift_right_logical(n_g + (MOE_CHUNK - 1), MOE_CHUNK.bit_length() - 1)
        ends.append(end)
    total = ends[-1]
    slot = jnp.sum(onehot * (start_vec + rank), axis=-1, keepdims=True)
    slot_i = slot.astype(jnp.int32)
    pt = jnp.where(lax.broadcasted_iota(jnp.int32, (tm, MOE_SLOTS), 1) == slot_i, 1.0, 0.0).astype(BF16)
    pt_ref[...] = pt
    a = jnp.floor(slot * (1.0 / SLOT_RADIX))
    b = slot - SLOT_RADIX * a
    digits = jnp.where(lane_i == 0, a, jnp.where(lane_i == 1, b, 0.0)).astype(BF16)
    rows = lax.dot_general(sel_ref[...], digits, (((1,), (1,)), ((), ())), preferred_element_type=F32)
    slot_row = (rows[0:1, :] * SLOT_RADIX + rows[1:2, :]).astype(jnp.int32)
    p = jnp.where(lax.broadcasted_iota(jnp.int32, (MOE_SLOTS, tm), 0) == slot_row, 1.0, 0.0).astype(BF16)

    hs_ref[...] = jnp.dot(p, h_hi, preferred_element_type=F32).astype(BF16)
    c_hi = comb.astype(BF16)
    r1 = comb - c_hi.astype(F32)
    c_mid = r1.astype(BF16)
    c_lo = (r1 - c_mid.astype(F32)).astype(BF16)
    packed = (c_hi.astype(F32) + pltpu.roll(c_mid.astype(F32), N_EXPERTS, 1)
              + pltpu.roll(c_lo.astype(F32), 2 * N_EXPERTS, 1)).astype(BF16)
    cs = jnp.dot(p, packed, preferred_element_type=F32)
    cs_ref[...] = cs + pltpu.roll(cs, LANES - N_EXPERTS, 1) + pltpu.roll(cs, LANES - 2 * N_EXPERTS, 1)

    lane_c = lax.broadcasted_iota(jnp.int32, (MOE_CHUNK, LANES), 1)
    for c in range(MOE_SLOTS // MOE_CHUNK):
        rows_c = slice(c * MOE_CHUNK, (c + 1) * MOE_CHUNK)

        @pl.when(c < total)
        def _():
            g_c = sum((ends[g] <= c).astype(jnp.int32) for g in range(N_GROUPS - 1))
            xc = hs_ref[rows_c, :]
            cc = cs_ref[rows_c, :]
            hids = []
            for j in range(EXPERTS_PER_GROUP):
                e = g_c * EXPERTS_PER_GROUP + j
                gt = jnp.dot(xc, wg_ref[e], preferred_element_type=F32)
                up = jnp.dot(xc, wu_ref[e], preferred_element_type=F32)
                cj = jnp.sum(jnp.where(lane_c == e, cc, 0.0), axis=-1, keepdims=True)
                hids.append(((gt / (1.0 + jnp.exp(-gt))) * up * cj).astype(BF16))
            hid = jnp.concatenate(hids, axis=1)
            wd_g = wd_ref[pl.ds(pl.multiple_of(g_c * ffg, ffg), ffg), :]
            ys_ref[rows_c, :] = jnp.dot(hid, wd_g, preferred_element_type=F32)

        @pl.when(c >= total)
        def _():
            ys_ref[rows_c, :] = jnp.zeros((MOE_CHUNK, D), F32)

    ys = ys_ref[...]
    y_hi = ys.astype(BF16)
    y_lo = (ys - y_hi.astype(F32)).astype(BF16)
    pt = pt_ref[...]
    y_tok = jnp.dot(pt, y_hi, preferred_element_type=F32) + jnp.dot(pt, y_lo, preferred_element_type=F32)
    out = y_ref[...] + gtm_ref[...] * y_tok
    if final:
        out = _rms(out, fin_ref[...])
    y_ref[...] = out


def _moe(x2d, o2d, mods, layer, w, *, mod_row, final):
    n = x2d.shape[0]
    tm = MOE_TM
    row_fn = lambda i: mod_row(i)
    tile = pl.BlockSpec((tm, D), lambda i: (i, 0))
    once = lambda shape: pl.BlockSpec(shape, lambda i: (0,) * len(shape), pipeline_mode=pl.Buffered(1))
    return pl.pallas_call(
        functools.partial(_moe_kernel, final=final),
        grid=(n // tm,),
        in_specs=[
            tile, tile, once((D, D)),
            _mod_spec(layer, 2, row_fn), _mod_spec(layer, 3, row_fn), _mod_spec(layer, 4, row_fn),
            _mod_spec(layer, 5, row_fn), _full_spec((1, D)),
            _full_spec((D, LANES)), _full_spec((D, LANES)), _full_spec((1, LANES)), _full_spec((8, LANES)),
            once((tm, tm)),
            once((N_EXPERTS, D, EXPERT_FF)), once((N_EXPERTS, D, EXPERT_FF)), once((N_EXPERTS * EXPERT_FF, D)),
            _full_spec((1, D)),
        ],
        out_specs=tile,
        out_shape=jax.ShapeDtypeStruct((n, D), F32),
        scratch_shapes=[pltpu.VMEM((MOE_SLOTS, D), BF16), pltpu.VMEM((MOE_SLOTS, LANES), F32),
                        pltpu.VMEM((MOE_SLOTS, D), F32), pltpu.VMEM((tm, MOE_SLOTS), BF16)],
        compiler_params=_params(1), name=f"moe_l{layer}",
    )(x2d, o2d, w["wo"], mods, mods, mods, mods, w["gn"], w["wr_hi"], w["wr_lo"], w["br"], w["sel2"],
      w["ltri"], w["wg_bf"], w["wu_bf"], w["wd_bf"], w["fin"])


def _mla_weights(w_dq, g_q, w_uq, w_dkv, g_kv, w_uk, w_uv):
    hd = MLA_NOPE + MLA_ROPE
    wuq = w_uq.reshape(Q_LORA, MLA_HEADS, hd)
    wuq = jnp.concatenate([wuq[..., MLA_NOPE:], wuq[..., :MLA_NOPE],
                           jnp.zeros((Q_LORA, MLA_HEADS, LANES - hd), F32)], axis=-1)
    wuk = jnp.concatenate([jnp.zeros((KV_LORA, MLA_HEADS, MLA_ROPE), F32), w_uk,
                           jnp.zeros((KV_LORA, MLA_HEADS, LANES - hd), F32)], axis=-1)
    pe_rows = jnp.broadcast_to(jnp.eye(MLA_ROPE, LANES, dtype=F32)[:, None, :], (MLA_ROPE, MLA_HEADS, LANES))
    wkexp = jnp.concatenate([wuk, pe_rows, jnp.zeros((KV_PAD - KV_DIM, MLA_HEADS, LANES), F32)], axis=0)
    return {
        "wdq": w_dq.astype(BF16), "gq": g_q.reshape(1, Q_LORA),
        "wuq": wuq.reshape(Q_LORA, MLA_HEADS * LANES).astype(BF16),
        "wdkv": jnp.pad(w_dkv, ((0, 0), (0, KV_PAD - KV_DIM))).astype(BF16), "gkv": g_kv.reshape(1, KV_LORA),
        "wkexp": wkexp.reshape(KV_PAD, MLA_HEADS * LANES).astype(BF16),
        "wuv": w_uv.reshape(KV_LORA, MLA_HEADS * MLA_V).astype(BF16),
    }


def _gqa_weights(w_qkv, g_q, g_k):
    grp = jnp.arange(2 * LANES) // GQA_HEAD_DIM
    bmat = jnp.where(grp[:, None] == grp[None, :], 1.0 / GQA_HEAD_DIM, 0.0).astype(BF16)
    reps = 2 * LANES // GQA_HEAD_DIM
    return {"wqkv": w_qkv.astype(BF16), "gq": jnp.tile(g_q, reps).reshape(1, 2 * LANES),
            "gk": jnp.tile(g_k, reps).reshape(1, 2 * LANES), "bmat": bmat}


def _post_weights(l, w_o, norm_ffn, w_group, b_group, w_exp, b_exp, w_gate, w_up, w_down, final_norm):
    wr = jnp.concatenate([w_exp[l], w_group[l], jnp.zeros((D, LANES - N_EXPERTS - N_GROUPS), F32)], axis=1)
    br = jnp.concatenate([b_exp[l], b_group[l], jnp.zeros((LANES - N_EXPERTS - N_GROUPS,), F32)]).reshape(1, LANES)
    wr_hi = wr.astype(BF16)
    wr_lo = (wr - wr_hi.astype(F32)).astype(BF16)
    return {"wo": w_o.astype(BF16), "gn": norm_ffn[l].reshape(1, D), "wr_hi": wr_hi, "wr_lo": wr_lo, "br": br,
            "wg_bf": w_gate[l].astype(BF16), "wu_bf": w_up[l].astype(BF16),
            "wd_bf": w_down[l].reshape(N_EXPERTS * EXPERT_FF, D).astype(BF16),
            "fin": final_norm.reshape(1, D),
            "sel2": jnp.zeros((8, LANES), BF16).at[0, 0].set(1.0).at[1, 1].set(1.0),
            "ltri": jnp.tri(MOE_TM, MOE_TM, -1, dtype=BF16)}


def kernel(x_prompt, x_sample, cache_mla, cache_gqa, c, c_ctx, ada_w, ada_b, norm_mix, norm_ffn,
           mla_w_dq, mla_q_norm, mla_w_uq, mla_w_dkv, mla_kv_norm, mla_w_uk, mla_w_uv, mla_w_o,
           gqa_w_qkv, gqa_q_norm, gqa_k_norm, gqa_w_o,
           moe_w_group, moe_b_group, moe_w_expert, moe_b_expert, moe_w_gate, moe_w_up, moe_w_down,
           final_norm):
    bp, tp, _ = x_prompt.shape
    bs, ts, _ = x_sample.shape
    t_past = cache_mla.shape[2]
    assert ada_w.shape[0] == 2 and 1 + bs <= N_MOD_ROWS
    n_p, n_s = bp * tp, bs * ts

    cc = jnp.concatenate([c_ctx[None], c, jnp.zeros((N_MOD_ROWS - 1 - bs, D), F32)], axis=0)
    mods = _ada_table(cc, ada_w, ada_b)

    xp = x_prompt.reshape(n_p, D)
    xs = x_sample.reshape(n_s, D)
    tm_proj = 512
    tm_post = MOE_TM
    prompt_row = lambda *_: 0
    sample_row_proj = lambda b: 1 + b
    sample_row_post = lambda i: 1 + i // (ts // tm_post)
    moe_args = (norm_ffn, moe_w_group, moe_b_group, moe_w_expert, moe_b_expert,
                moe_w_gate, moe_w_up, moe_w_down, final_norm)

    wm = _mla_weights(mla_w_dq[0], mla_q_norm[0], mla_w_uq[0], mla_w_dkv[0], mla_kv_norm[0],
                      mla_w_uk[0], mla_w_uv[0])
    gn0 = norm_mix[0].reshape(1, D)
    q, k, v, keys_p = _mla_proj(xp, mods, gn0, wm, n_batch=1, t_new=n_p, tm=tm_proj, mod_row=prompt_row)
    o = _attention(q, k, v, mla=True, n_batch=bp, t_q=tp, t_k=tp, tq=tp, n_pairs=8, name="attn_mla_p")
    wp0 = _post_weights(0, mla_w_o[0], *moe_args)
    xp = _moe(xp, o, mods, 0, wp0, mod_row=prompt_row, final=False)

    rope_mla = _rope_tables(ts, MLA_ROPE, (0,))
    cache0 = jnp.pad(cache_mla[:, 0].reshape(bs * t_past, KV_DIM), ((0, 0), (0, KV_PAD - KV_DIM)))
    q, k, v = _mla_proj(xs, mods, gn0, wm, n_batch=bs, t_new=ts, tm=tm_proj, mod_row=sample_row_proj,
                        rope_tabs=rope_mla, cache=cache0)
    o = _attention(q, k, v, mla=True, n_batch=bs, t_q=ts, t_k=t_past + ts, tq=512, n_pairs=1, name="attn_mla_s")
    xs = _moe(xs, o, mods, 0, wp0, mod_row=sample_row_post, final=False)

    wg = _gqa_weights(gqa_w_qkv[0], gqa_q_norm[0], gqa_k_norm[0])
    gn1 = norm_mix[1].reshape(1, D)
    q, k, v, kv_p = _gqa_proj(xp, mods, gn1, wg, n_batch=1, t_new=n_p, tm=tm_proj, mod_row=prompt_row)
    o = _attention(q, k, v, mla=False, n_batch=bp, t_q=tp, t_k=tp, tq=tp, n_pairs=8, name="attn_gqa_p")
    wp1 = _post_weights(1, gqa_w_o[0], *moe_args)
    y_prompt = _moe(xp, o, mods, 1, wp1, mod_row=prompt_row, final=True)

    rope_gqa = _rope_tables(ts, GQA_HEAD_DIM, (0, GQA_HEAD_DIM))
    cache1 = cache_gqa[:, 0].reshape(bs * t_past, 2 * GQA_KV_HEADS * GQA_HEAD_DIM)
    q, k, v = _gqa_proj(xs, mods, gn1, wg, n_batch=bs, t_new=ts, tm=tm_proj, mod_row=sample_row_proj,
                        rope_tabs=rope_gqa, cache=cache1)
    o = _attention(q, k, v, mla=False, n_batch=bs, t_q=ts, t_k=t_past + ts, tq=512, n_pairs=1, name="attn_gqa_s")
    y_sample = _moe(xs, o, mods, 1, wp1, mod_row=sample_row_post, final=True)

    return (y_prompt.reshape(bp, tp, D), y_sample.reshape(bs, ts, D),
            keys_p.reshape(bp, 1, tp, KV_DIM),
            kv_p.reshape(bp, 1, tp, 2, GQA_KV_HEADS, GQA_HEAD_DIM))
```

```python
import functools
import math

import jax
import jax.numpy as jnp
from jax import lax
from jax.experimental import pallas as pl
from jax.experimental.pallas import tpu as pltpu

F32 = jnp.float32
BF16 = jnp.bfloat16

D = 1024
EPS = 1e-6
GRID_W = 64
ROPE_THETA = 10000.0
LANES = 128
HALF = LANES // 2
MLA_HEADS = 16
MLA_NOPE = 64
MLA_ROPE = 32
MLA_V = 64
Q_LORA = 384
KV_LORA = 256
KV_DIM = KV_LORA + MLA_ROPE
KV_PAD = 384
MLA_SCALE = 1.0 / math.sqrt(MLA_NOPE + MLA_ROPE)
GQA_HEADS = 16
GQA_KV_HEADS = 4
GQA_HEAD_DIM = 64
GQA_SCALE = 1.0 / math.sqrt(GQA_HEAD_DIM)
N_GROUPS = 4
EXPERTS_PER_GROUP = 4
N_EXPERTS = 16
EXPERT_FF = 256
NEG = -3.0e38
LOG2E = 1.4426950408889634
ATTN_KEY_CHUNK = 512

VMEM_LIMIT = 56 * 1024 * 1024
N_MOD_ROWS = 8


def _params(n_axes):
    return pltpu.CompilerParams(dimension_semantics=("arbitrary",) * n_axes,
                                vmem_limit_bytes=VMEM_LIMIT)


def _rms(x, g):
    ms = jnp.mean(x * x, axis=-1, keepdims=True)
    return x * lax.rsqrt(ms + EPS) * g


def _mod_index(layer, which, row):
    return (layer * 6 + which) * N_MOD_ROWS + row


def _mod_spec(layer, which, row_fn):
    return pl.BlockSpec((None, 1, D), lambda *g: (_mod_index(layer, which, row_fn(*g)), 0, 0))


def _full_spec(shape):
    n = len(shape)
    return pl.BlockSpec(shape, lambda *g: (0,) * n)


def _ada_kernel(c_ref, w_ref, b_ref, o_ref):
    c = c_ref[...]
    a = c / (1.0 + jnp.exp(-c))
    o_ref[...] = jnp.dot(a, w_ref[...], precision=lax.Precision.HIGHEST,
                         preferred_element_type=F32) + b_ref[...]


def _ada_table(cc, ada_w, ada_b):
    depth = ada_w.shape[0]
    out = pl.pallas_call(
        _ada_kernel,
        grid=(depth, 6),
        in_specs=[
            pl.BlockSpec((N_MOD_ROWS, D), lambda l, n: (0, 0)),
            pl.BlockSpec((None, D, D), lambda l, n: (l, 0, n)),
            pl.BlockSpec((None, 1, D), lambda l, n: (l, 0, n)),
        ],
        out_specs=pl.BlockSpec((None, None, N_MOD_ROWS, D), lambda l, n: (l, n, 0, 0)),
        out_shape=jax.ShapeDtypeStruct((depth, 6, N_MOD_ROWS, D), F32),
        compiler_params=_params(2),
        name="ada",
    )(cc, ada_w, ada_b.reshape(depth, 1, 6 * D))
    return out.reshape(depth * 6 * N_MOD_ROWS, 1, D)


def _rope(a, cos, sin_signed):
    lane = lax.broadcasted_iota(jnp.int32, a.shape, 1)
    nxt = pltpu.roll(a, LANES - 1, 1)
    prv = pltpu.roll(a, 1, 1)
    return a * cos + jnp.where((lane & 1) == 0, nxt, prv) * sin_signed


def _rope_tables(n_tokens, rot_dim, lane_offset_pattern):
    t = jnp.arange(n_tokens)
    row = (t // GRID_W).astype(F32)
    col = (t % GRID_W).astype(F32)
    axis_dim = rot_dim // 2
    inv = jnp.power(ROPE_THETA, -jnp.arange(0, axis_dim, 2, dtype=F32) / axis_dim)
    ang = jnp.concatenate([row[:, None] * inv, col[:, None] * inv], axis=-1)
    cos = jnp.repeat(jnp.cos(ang), 2, axis=1)
    sin = jnp.repeat(jnp.sin(ang), 2, axis=1)
    sign = jnp.where(jnp.arange(rot_dim) % 2 == 0, -1.0, 1.0).astype(F32)
    sin = sin * sign
    cos_t = jnp.ones((n_tokens, LANES), F32)
    sin_t = jnp.zeros((n_tokens, LANES), F32)
    for off in lane_offset_pattern:
        cos_t = cos_t.at[:, off:off + rot_dim].set(cos)
        sin_t = sin_t.at[:, off:off + rot_dim].set(sin)
    return cos_t, sin_t


def _mla_proj_kernel(*refs, rope, n_cache, emit_keys):
    it = iter(refs)
    x_ref, sh_ref, sc_ref, gn_ref = next(it), next(it), next(it), next(it)
    wdq_ref, gq_ref, wuq_ref, wdkv_ref, gkv_ref, wkexp_ref, wuv_ref = (next(it) for _ in range(7))
    cos_ref = sin_ref = cache_ref = keys_ref = None
    if rope:
        cos_ref, sin_ref = next(it), next(it)
    if n_cache:
        cache_ref = next(it)
    q_ref, k_ref, v_ref = next(it), next(it), next(it)
    if emit_keys:
        keys_ref = next(it)

    def expand(ckv, pe):
        ckv = ckv.astype(BF16)
        kn = jnp.dot(ckv, wkexp_ref[...], preferred_element_type=F32)
        k_ref[...] = jnp.concatenate(
            [kn[:, h_ * LANES:(h_ + 1) * LANES] + pe for h_ in range(MLA_HEADS)], axis=1).astype(BF16)
        v_ref[...] = jnp.dot(ckv, wuv_ref[...], preferred_element_type=F32).astype(BF16)

    def new_tokens():
        x = x_ref[...]
        h = (_rms(x, gn_ref[...]) * (1.0 + sc_ref[...]) + sh_ref[...]).astype(BF16)
        ql = jnp.dot(h, wdq_ref[...], preferred_element_type=F32)
        qn = _rms(ql, gq_ref[...]).astype(BF16)
        q = jnp.dot(qn, wuq_ref[...], preferred_element_type=F32)
        kv = jnp.dot(h, wdkv_ref[...], preferred_element_type=F32)
        ckv = _rms(kv[:, :KV_LORA], gkv_ref[...])
        pe = kv[:, KV_LORA:]
        if rope:
            cos, sin = cos_ref[...], sin_ref[...]
            pe = _rope(pe, cos, sin)
            q = jnp.concatenate(
                [_rope(q[:, h_ * LANES:(h_ + 1) * LANES], cos, sin) for h_ in range(MLA_HEADS)], axis=1)
        q_ref[...] = (q * (MLA_SCALE * LOG2E)).astype(BF16)
        if emit_keys:
            keys_ref[:, 0:KV_LORA] = ckv
            keys_ref[:, KV_LORA:KV_DIM] = pe[:, 0:MLA_ROPE]
        expand(ckv, pe)

    if n_cache:
        t = pl.program_id(1)
        pl.when(t >= n_cache)(new_tokens)

        @pl.when(t < n_cache)
        def _():
            expand(cache_ref[:, 0:KV_LORA], cache_ref[:, KV_LORA:KV_PAD])
    else:
        new_tokens()


def _mla_proj(x2d, mods, gn, w, *, n_batch, t_new, tm, mod_row, rope_tabs=None, cache=None):
    rope = rope_tabs is not None
    t_cache = 0 if cache is None else cache.shape[0] // n_batch
    n_cache = t_cache // tm
    ntn = t_new // tm
    nt = n_cache + ntn
    emit_keys = cache is None

    def new_idx(b, t):
        return b * ntn + jnp.maximum(t - n_cache, 0)

    row_fn = lambda b, t: mod_row(b)
    in_specs = [
        pl.BlockSpec((tm, D), lambda b, t: (new_idx(b, t), 0)),
        _mod_spec(0, 0, row_fn), _mod_spec(0, 1, row_fn),
        _full_spec((1, D)),
        _full_spec((D, Q_LORA)), _full_spec((1, Q_LORA)), _full_spec((Q_LORA, MLA_HEADS * LANES)),
        _full_spec((D, KV_PAD)), _full_spec((1, KV_LORA)),
        _full_spec((KV_LORA, MLA_HEADS * LANES)), _full_spec((KV_LORA, MLA_HEADS * MLA_V)),
    ]
    args = [x2d, mods, mods, gn, w["wdq"], w["gq"], w["wuq"], w["wdkv"], w["gkv"], w["wkexp"], w["wuv"]]
    if rope:
        in_specs += [pl.BlockSpec((tm, LANES), lambda b, t: (jnp.maximum(t - n_cache, 0), 0))] * 2
        args += list(rope_tabs)
    if n_cache:
        in_specs.append(pl.BlockSpec((tm, KV_PAD), lambda b, t: (b * n_cache + jnp.minimum(t, n_cache - 1), 0)))
        args.append(cache)
    n_new = n_batch * t_new
    n_keys = n_batch * (t_cache + t_new)
    out_specs = [
        pl.BlockSpec((tm, MLA_HEADS * LANES), lambda b, t: (new_idx(b, t), 0)),
        pl.BlockSpec((tm, MLA_HEADS * LANES), lambda b, t: (b * nt + t, 0)),
        pl.BlockSpec((tm, MLA_HEADS * MLA_V), lambda b, t: (b * nt + t, 0)),
    ]
    out_shape = [
        jax.ShapeDtypeStruct((n_new, MLA_HEADS * LANES), BF16),
        jax.ShapeDtypeStruct((n_keys, MLA_HEADS * LANES), BF16),
        jax.ShapeDtypeStruct((n_keys, MLA_HEADS * MLA_V), BF16),
    ]
    if emit_keys:
        out_specs.append(pl.BlockSpec((tm, KV_DIM), lambda b, t: (new_idx(b, t), 0)))
        out_shape.append(jax.ShapeDtypeStruct((n_new, KV_DIM), F32))
    return pl.pallas_call(
        functools.partial(_mla_proj_kernel, rope=rope, n_cache=n_cache, emit_keys=emit_keys),
        grid=(n_batch, nt), in_specs=in_specs, out_specs=out_specs, out_shape=out_shape,
        compiler_params=_params(2), name="mla_proj_s" if rope else "mla_proj_p",
    )(*args)


def _dup_halves(a):
    cols = []
    for c in range(a.shape[1] // LANES):
        blk = a[:, c * LANES:(c + 1) * LANES]
        rot = pltpu.roll(blk, HALF, 1)
        low = lax.broadcasted_iota(jnp.int32, blk.shape, 1) < HALF
        cols += [jnp.where(low, blk, rot), jnp.where(low, rot, blk)]
    return jnp.concatenate(cols, axis=1)


def _group_mean_sq(a, bmat):
    sq = a * a
    hi = sq.astype(BF16)
    lo = (sq - hi.astype(F32)).astype(BF16)
    return (jnp.dot(hi, bmat, preferred_element_type=F32) + jnp.dot(lo, bmat, preferred_element_type=F32))


def _gqa_proj_kernel(*refs, rope, n_cache, emit_kv):
    it = iter(refs)
    x_ref, sh_ref, sc_ref, gn_ref, w_ref, gq_ref, gk_ref, bmat_ref = (next(it) for _ in range(8))
    cos_ref = sin_ref = cache_ref = kv_ref = None
    if rope:
        cos_ref, sin_ref = next(it), next(it)
    if n_cache:
        cache_ref = next(it)
    q_ref, k_ref, v_ref = next(it), next(it), next(it)
    if emit_kv:
        kv_ref = next(it)
    nq = GQA_HEADS * GQA_HEAD_DIM
    nk = GQA_KV_HEADS * GQA_HEAD_DIM
    blk = 2 * LANES

    def new_tokens():
        x = x_ref[...]
        h = (_rms(x, gn_ref[...]) * (1.0 + sc_ref[...]) + sh_ref[...]).astype(BF16)
        qkv = jnp.dot(h, w_ref[...], preferred_element_type=F32)
        bmat = bmat_ref[...]

        def head_norm(a, g):
            return a * lax.rsqrt(_group_mean_sq(a, bmat) + EPS) * g

        def maybe_rope(a):
            if not rope:
                return a
            cos, sin = cos_ref[...], sin_ref[...]
            return jnp.concatenate(
                [_rope(a[:, c * LANES:(c + 1) * LANES], cos, sin) for c in range(a.shape[1] // LANES)], axis=1)

        for c in range(nq // blk):
            qb = head_norm(qkv[:, c * blk:(c + 1) * blk], gq_ref[...])
            q_ref[:, c * blk:(c + 1) * blk] = (maybe_rope(qb) * (GQA_SCALE * LOG2E)).astype(BF16)
        kn = head_norm(qkv[:, nq:nq + nk], gk_ref[...])
        vv = qkv[:, nq + nk:]
        if emit_kv:
            kv_ref[:, 0:nk] = kn
            kv_ref[:, nk:2 * nk] = vv
        k_ref[...] = _dup_halves(maybe_rope(kn)).astype(BF16)
        v_ref[...] = _dup_halves(vv).astype(BF16)

    if n_cache:
        t = pl.program_id(1)
        pl.when(t >= n_cache)(new_tokens)

        @pl.when(t < n_cache)
        def _():
            c = cache_ref[...]
            k_ref[...] = _dup_halves(c[:, 0:nk]).astype(BF16)
            v_ref[...] = _dup_halves(c[:, nk:2 * nk]).astype(BF16)
    else:
        new_tokens()


def _gqa_proj(x2d, mods, gn, w, *, n_batch, t_new, tm, mod_row, rope_tabs=None, cache=None):
    rope = rope_tabs is not None
    t_cache = 0 if cache is None else cache.shape[0] // n_batch
    n_cache = t_cache // tm
    ntn = t_new // tm
    nt = n_cache + ntn
    emit_kv = cache is None
    nq = GQA_HEADS * GQA_HEAD_DIM
    nk = GQA_KV_HEADS * GQA_HEAD_DIM
    nqkv = nq + 2 * nk
    ndup = GQA_KV_HEADS * LANES

    def new_idx(b, t):
        return b * ntn + jnp.maximum(t - n_cache, 0)

    row_fn = lambda b, t: mod_row(b)
    in_specs = [
        pl.BlockSpec((tm, D), lambda b, t: (new_idx(b, t), 0)),
        _mod_spec(1, 0, row_fn), _mod_spec(1, 1, row_fn),
        _full_spec((1, D)), _full_spec((D, nqkv)),
        _full_spec((1, 2 * LANES)), _full_spec((1, 2 * LANES)), _full_spec((2 * LANES, 2 * LANES)),
    ]
    args = [x2d, mods, mods, gn, w["wqkv"], w["gq"], w["gk"], w["bmat"]]
    if rope:
        in_specs += [pl.BlockSpec((tm, LANES), lambda b, t: (jnp.maximum(t - n_cache, 0), 0))] * 2
        args += list(rope_tabs)
    if n_cache:
        in_specs.append(pl.BlockSpec((tm, 2 * nk), lambda b, t: (b * n_cache + jnp.minimum(t, n_cache - 1), 0)))
        args.append(cache)
    n_new = n_batch * t_new
    n_keys = n_batch * (t_cache + t_new)
    out_specs = [
        pl.BlockSpec((tm, nq), lambda b, t: (new_idx(b, t), 0)),
        pl.BlockSpec((tm, ndup), lambda b, t: (b * nt + t, 0)),
        pl.BlockSpec((tm, ndup), lambda b, t: (b * nt + t, 0)),
    ]
    out_shape = [
        jax.ShapeDtypeStruct((n_new, nq), BF16),
        jax.ShapeDtypeStruct((n_keys, ndup), BF16),
        jax.ShapeDtypeStruct((n_keys, ndup), BF16),
    ]
    if emit_kv:
        out_specs.append(pl.BlockSpec((tm, 2 * nk), lambda b, t: (new_idx(b, t), 0)))
        out_shape.append(jax.ShapeDtypeStruct((n_new, 2 * nk), F32))
    return pl.pallas_call(
        functools.partial(_gqa_proj_kernel, rope=rope, n_cache=n_cache, emit_kv=emit_kv),
        grid=(n_batch, nt), in_specs=in_specs, out_specs=out_specs, out_shape=out_shape,
        compiler_params=_params(2), name="gqa_proj_s" if rope else "gqa_proj_p",
    )(*args)


def _attn_kernel(q_ref, k_ref, v_ref, o_ref, *, mla, n_pairs, ck):
    tq = q_ref.shape[0]
    nc = k_ref.shape[0] // ck
    lane = lax.broadcasted_iota(jnp.int32, (tq, LANES), 1)
    low = lane < HALF
    items = [(p, c, hh) for c in range(nc) for p in range(n_pairs) for hh in range(2)]
    heads = {}
    state = {}

    def head_operands(p, hh):
        if (p, hh) not in heads:
            if mla:
                hd = 2 * p + hh
                heads[(p, hh)] = (q_ref[:, hd * LANES:(hd + 1) * LANES], hd, p)
            else:
                g = p // 2 if n_pairs > 1 else 0
                qp = q_ref[:, p * LANES:(p + 1) * LANES]
                qh = jnp.where(low if hh == 0 else jnp.logical_not(low), qp, jnp.zeros_like(qp))
                heads[(p, hh)] = (qh, g, g)
        return heads[(p, hh)]

    def scores(item):
        p, c, hh = item
        qh, kc, _ = head_operands(p, hh)
        kh = k_ref[c * ck:(c + 1) * ck, kc * LANES:(kc + 1) * LANES]
        return lax.dot_general(qh, kh, (((1,), (1,)), ((), ())), preferred_element_type=F32)

    s_cur = scores(items[0])
    for idx, (p, c, hh) in enumerate(items):
        s_next = scores(items[idx + 1]) if idx + 1 < len(items) else None
        vc = head_operands(p, hh)[2]
        vb = v_ref[c * ck:(c + 1) * ck, vc * LANES:(vc + 1) * LANES]
        m_c = jnp.max(s_cur, axis=-1, keepdims=True)
        if c == 0:
            m = m_c
            e = jnp.exp2(s_cur - m)
            l = jnp.sum(e, axis=-1, keepdims=True)
            acc = jnp.dot(e.astype(BF16), vb, preferred_element_type=F32)
        else:
            m_old, l_old, acc_old = state[(p, hh)]
            m = jnp.maximum(m_old, m_c)
            alpha = jnp.exp2(m_old - m)
            e = jnp.exp2(s_cur - m)
            l = alpha * l_old + jnp.sum(e, axis=-1, keepdims=True)
            acc = alpha * acc_old + jnp.dot(e.astype(BF16), vb, preferred_element_type=F32)
        state[(p, hh)] = (m, l, acc)
        if c == nc - 1 and hh == 1:
            o0 = state[(p, 0)][2] / state[(p, 0)][1]
            o1 = state[(p, 1)][2] / state[(p, 1)][1]
            o_ref[:, p * LANES:(p + 1) * LANES] = jnp.where(low, o0, o1).astype(BF16)
        s_cur = s_next


def _attention(q, k, v, *, mla, n_batch, t_q, t_k, tq, n_pairs, name):
    total_pairs = 8
    nj = total_pairs // n_pairs
    nqt = t_q // tq
    if mla:
        qw, kw, vw = n_pairs * 2 * LANES, n_pairs * 2 * LANES, n_pairs * LANES
        kv_col = lambda j: j
    else:
        qw = n_pairs * LANES
        kw = vw = max(n_pairs // 2, 1) * LANES
        kv_col = (lambda j: j // 2) if n_pairs == 1 else (lambda j: j)
    return pl.pallas_call(
        functools.partial(_attn_kernel, mla=mla, n_pairs=n_pairs, ck=min(t_k, ATTN_KEY_CHUNK)),
        grid=(n_batch, nj, nqt),
        in_specs=[
            pl.BlockSpec((tq, qw), lambda b, j, i: (b * nqt + i, j)),
            pl.BlockSpec((t_k, kw), lambda b, j, i: (b, kv_col(j))),
            pl.BlockSpec((t_k, vw), lambda b, j, i: (b, kv_col(j))),
        ],
        out_specs=pl.BlockSpec((tq, n_pairs * LANES), lambda b, j, i: (b * nqt + i, j)),
        out_shape=jax.ShapeDtypeStruct((n_batch * t_q, total_pairs * LANES), BF16),
        compiler_params=_params(3), name=name,
    )(q, k, v)


def _route(logits):
    lane = lax.broadcasted_iota(jnp.int32, logits.shape, 1).astype(F32)
    big = jnp.float32(1e9)
    is_grp = (lane >= N_EXPERTS) & (lane < N_EXPERTS + N_GROUPS)
    gl = jnp.where(is_grp, logits, NEG)
    gm = jnp.max(gl, axis=-1, keepdims=True)
    g_w = 1.0 / jnp.sum(jnp.exp(gl - gm), axis=-1, keepdims=True)
    g_idx = jnp.min(jnp.where(gl == gm, lane, big), axis=-1, keepdims=True) - N_EXPERTS
    lo = g_idx * EXPERTS_PER_GROUP
    el = jnp.where((lane >= lo) & (lane < lo + EXPERTS_PER_GROUP), logits, NEG)
    m1 = jnp.max(el, axis=-1, keepdims=True)
    i1 = jnp.min(jnp.where(el == m1, lane, big), axis=-1, keepdims=True)
    el2 = jnp.where(lane == i1, NEG, el)
    m2 = jnp.max(el2, axis=-1, keepdims=True)
    i2 = jnp.min(jnp.where(el2 == m2, lane, big), axis=-1, keepdims=True)
    t = jnp.exp(m2 - m1)
    w1 = g_w / (1.0 + t)
    w2 = g_w * t / (1.0 + t)
    return jnp.where(lane == i1, w1, 0.0) + jnp.where(lane == i2, w2, 0.0), g_idx


MOE_TM = 512
MOE_CHUNK = 144
MOE_SLOTS = 1152
SLOT_RADIX = 32


def _moe_kernel(x_ref, o_ref, wo_ref, gta_ref, shm_ref, scm_ref, gtm_ref, gn_ref,
                wr_hi_ref, wr_lo_ref, br_ref, sel_ref, ltri_ref, wg_ref, wu_ref, wd_ref, fin_ref,
                y_ref, hs_ref, cs_ref, ys_ref, pt_ref, *, final):
    tm = x_ref.shape[0]
    ffg = EXPERTS_PER_GROUP * EXPERT_FF
    mix = jnp.dot(o_ref[...], wo_ref[...], preferred_element_type=F32)
    xm = x_ref[...] + gta_ref[...] * mix
    y_ref[...] = xm
    h = _rms(xm, gn_ref[...]) * (1.0 + scm_ref[...]) + shm_ref[...]
    h_hi = h.astype(BF16)
    h_lo = (h - h_hi.astype(F32)).astype(BF16)
    logits = (jnp.dot(h_hi, wr_hi_ref[...], preferred_element_type=F32)
              + jnp.dot(h_lo, wr_hi_ref[...], preferred_element_type=F32)
              + jnp.dot(h_hi, wr_lo_ref[...], preferred_element_type=F32)) + br_ref[...]
    comb, g_idx = _route(logits)

    lane_i = lax.broadcasted_iota(jnp.int32, (tm, LANES), 1)
    lane = lane_i.astype(F32)
    onehot = jnp.where(lane == g_idx, 1.0, 0.0)
    rank = jnp.dot(ltri_ref[...], onehot.astype(BF16), preferred_element_type=F32)
    cnt = jnp.sum(onehot, axis=0, keepdims=True)
    ends = []
    end = jnp.int32(0)
    start_vec = jnp.zeros_like(lane)
    for g in range(N_GROUPS):
        start_vec = jnp.where(lane_i == g, (end * MOE_CHUNK).astype(F32), start_vec)
        n_g = cnt[0, g].astype(jnp.int32)
        end = end + sum((n_g > k * MOE_CHUNK).astype(jnp.int32) for k in range(-(-tm // MOE_CHUNK)))
        ends.append(end)
    total = ends[-1]
    slot = jnp.sum(onehot * (start_vec + rank), axis=-1, keepdims=True)
    slot_i = slot.astype(jnp.int32)
    pt = jnp.where(lax.broadcasted_iota(jnp.int32, (tm, MOE_SLOTS), 1) == slot_i, 1.0, 0.0).astype(BF16)
    pt_ref[...] = pt
    a = jnp.floor(slot * (1.0 / SLOT_RADIX))
    b = slot - SLOT_RADIX * a
    digits = jnp.where(lane_i == 0, a, jnp.where(lane_i == 1, b, 0.0)).astype(BF16)
    rows = lax.dot_general(sel_ref[...], digits, (((1,), (1,)), ((), ())), preferred_element_type=F32)
    slot_row = (rows[0:1, :] * SLOT_RADIX + rows[1:2, :]).astype(jnp.int32)
    p = jnp.where(lax.broadcasted_iota(jnp.int32, (MOE_SLOTS, tm), 0) == slot_row, 1.0, 0.0).astype(BF16)

    hs_ref[...] = jnp.dot(p, h_hi, preferred_element_type=F32).astype(BF16)
    c_hi = comb.astype(BF16)
    r1 = comb - c_hi.astype(F32)
    c_mid = r1.astype(BF16)
    c_lo = (r1 - c_mid.astype(F32)).astype(BF16)
    packed = (c_hi.astype(F32) + pltpu.roll(c_mid.astype(F32), N_EXPERTS, 1)
              + pltpu.roll(c_lo.astype(F32), 2 * N_EXPERTS, 1)).astype(BF16)
    cs = jnp.dot(p, packed, preferred_element_type=F32)
    cs_ref[...] = cs + pltpu.roll(cs, LANES - N_EXPERTS, 1) + pltpu.roll(cs, LANES - 2 * N_EXPERTS, 1)

    lane_c = lax.broadcasted_iota(jnp.int32, (MOE_CHUNK, LANES), 1)
    for c in range(MOE_SLOTS // MOE_CHUNK):
        rows_c = slice(c * MOE_CHUNK, (c + 1) * MOE_CHUNK)

        @pl.when(c < total)
        def _():
            g_c = sum((ends[g] <= c).astype(jnp.int32) for g in range(N_GROUPS - 1))
            xc = hs_ref[rows_c, :]
            cc = cs_ref[rows_c, :]
            hids = []
            for j in range(EXPERTS_PER_GROUP):
                e = g_c * EXPERTS_PER_GROUP + j
                gt = jnp.dot(xc, wg_ref[e], preferred_element_type=F32)
                up = jnp.dot(xc, wu_ref[e], preferred_element_type=F32)
                cj = jnp.sum(jnp.where(lane_c == e, cc, 0.0), axis=-1, keepdims=True)
                hids.append(((gt / (1.0 + jnp.exp(-gt))) * up * cj).astype(BF16))
            hid = jnp.concatenate(hids, axis=1)
            wd_g = wd_ref[pl.ds(pl.multiple_of(g_c * ffg, ffg), ffg), :]
            ys_ref[rows_c, :] = jnp.dot(hid, wd_g, preferred_element_type=F32).astype(BF16)

        @pl.when(c >= total)
        def _():
            ys_ref[rows_c, :] = jnp.zeros((MOE_CHUNK, D), BF16)

    y_tok = jnp.dot(pt_ref[...], ys_ref[...], preferred_element_type=F32)
    out = y_ref[...] + gtm_ref[...] * y_tok
    if final:
        out = _rms(out, fin_ref[...])
    y_ref[...] = out


def _moe(x2d, o2d, mods, layer, w, *, mod_row, final):
    n = x2d.shape[0]
    tm = MOE_TM
    row_fn = lambda i: mod_row(i)
    tile = pl.BlockSpec((tm, D), lambda i: (i, 0))
    once = lambda shape: pl.BlockSpec(shape, lambda i: (0,) * len(shape), pipeline_mode=pl.Buffered(1))
    return pl.pallas_call(
        functools.partial(_moe_kernel, final=final),
        grid=(n // tm,),
        in_specs=[
            tile, tile, once((D, D)),
            _mod_spec(layer, 2, row_fn), _mod_spec(layer, 3, row_fn), _mod_spec(layer, 4, row_fn),
            _mod_spec(layer, 5, row_fn), _full_spec((1, D)),
            _full_spec((D, LANES)), _full_spec((D, LANES)), _full_spec((1, LANES)), _full_spec((8, LANES)),
            once((tm, tm)),
            once((N_EXPERTS, D, EXPERT_FF)), once((N_EXPERTS, D, EXPERT_FF)), once((N_EXPERTS * EXPERT_FF, D)),
            _full_spec((1, D)),
        ],
        out_specs=tile,
        out_shape=jax.ShapeDtypeStruct((n, D), F32),
        scratch_shapes=[pltpu.VMEM((MOE_SLOTS, D), BF16), pltpu.VMEM((MOE_SLOTS, LANES), F32),
                        pltpu.VMEM((MOE_SLOTS, D), BF16), pltpu.VMEM((tm, MOE_SLOTS), BF16)],
        compiler_params=_params(1), name=f"moe_l{layer}",
    )(x2d, o2d, w["wo"], mods, mods, mods, mods, w["gn"], w["wr_hi"], w["wr_lo"], w["br"], w["sel2"],
      w["ltri"], w["wg_bf"], w["wu_bf"], w["wd_bf"], w["fin"])


def _mla_weights(w_dq, g_q, w_uq, w_dkv, g_kv, w_uk, w_uv):
    hd = MLA_NOPE + MLA_ROPE
    wuq = w_uq.reshape(Q_LORA, MLA_HEADS, hd)
    wuq = jnp.concatenate([wuq[..., MLA_NOPE:], wuq[..., :MLA_NOPE],
                           jnp.zeros((Q_LORA, MLA_HEADS, LANES - hd), F32)], axis=-1)
    wuk = jnp.concatenate([jnp.zeros((KV_LORA, MLA_HEADS, MLA_ROPE), F32), w_uk,
                           jnp.zeros((KV_LORA, MLA_HEADS, LANES - hd), F32)], axis=-1)
    return {
        "wdq": w_dq.astype(BF16), "gq": g_q.reshape(1, Q_LORA),
        "wuq": wuq.reshape(Q_LORA, MLA_HEADS * LANES).astype(BF16),
        "wdkv": jnp.pad(w_dkv, ((0, 0), (0, KV_PAD - KV_DIM))).astype(BF16), "gkv": g_kv.reshape(1, KV_LORA),
        "wkexp": wuk.reshape(KV_LORA, MLA_HEADS * LANES).astype(BF16),
        "wuv": w_uv.reshape(KV_LORA, MLA_HEADS * MLA_V).astype(BF16),
    }


def _gqa_weights(w_qkv, g_q, g_k):
    grp = jnp.arange(2 * LANES) // GQA_HEAD_DIM
    bmat = jnp.where(grp[:, None] == grp[None, :], 1.0 / GQA_HEAD_DIM, 0.0).astype(BF16)
    reps = 2 * LANES // GQA_HEAD_DIM
    return {"wqkv": w_qkv.astype(BF16), "gq": jnp.tile(g_q, reps).reshape(1, 2 * LANES),
            "gk": jnp.tile(g_k, reps).reshape(1, 2 * LANES), "bmat": bmat}


def _post_weights(l, w_o, norm_ffn, w_group, b_group, w_exp, b_exp, w_gate, w_up, w_down, final_norm):
    wr = jnp.concatenate([w_exp[l], w_group[l], jnp.zeros((D, LANES - N_EXPERTS - N_GROUPS), F32)], axis=1)
    br = jnp.concatenate([b_exp[l], b_group[l], jnp.zeros((LANES - N_EXPERTS - N_GROUPS,), F32)]).reshape(1, LANES)
    wr_hi = wr.astype(BF16)
    wr_lo = (wr - wr_hi.astype(F32)).astype(BF16)
    return {"wo": w_o.astype(BF16), "gn": norm_ffn[l].reshape(1, D), "wr_hi": wr_hi, "wr_lo": wr_lo, "br": br,
            "wg_bf": w_gate[l].astype(BF16), "wu_bf": w_up[l].astype(BF16),
            "wd_bf": w_down[l].reshape(N_EXPERTS * EXPERT_FF, D).astype(BF16),
            "fin": final_norm.reshape(1, D),
            "sel2": jnp.zeros((8, LANES), BF16).at[0, 0].set(1.0).at[1, 1].set(1.0),
            "ltri": jnp.tri(MOE_TM, MOE_TM, -1, dtype=BF16)}


def kernel(x_prompt, x_sample, cache_mla, cache_gqa, c, c_ctx, ada_w, ada_b, norm_mix, norm_ffn,
           mla_w_dq, mla_q_norm, mla_w_uq, mla_w_dkv, mla_kv_norm, mla_w_uk, mla_w_uv, mla_w_o,
           gqa_w_qkv, gqa_q_norm, gqa_k_norm, gqa_w_o,
           moe_w_group, moe_b_group, moe_w_expert, moe_b_expert, moe_w_gate, moe_w_up, moe_w_down,
           final_norm):
    bp, tp, _ = x_prompt.shape
    bs, ts, _ = x_sample.shape
    t_past = cache_mla.shape[2]
    assert ada_w.shape[0] == 2 and 1 + bs <= N_MOD_ROWS
    n_p, n_s = bp * tp, bs * ts

    cc = jnp.concatenate([c_ctx[None], c, jnp.zeros((N_MOD_ROWS - 1 - bs, D), F32)], axis=0)
    mods = _ada_table(cc, ada_w, ada_b)

    xp = x_prompt.reshape(n_p, D)
    xs = x_sample.reshape(n_s, D)
    tm_proj = 512
    tm_post = MOE_TM
    prompt_row = lambda *_: 0
    sample_row_proj = lambda b: 1 + b
    sample_row_post = lambda i: 1 + i // (ts // tm_post)
    moe_args = (norm_ffn, moe_w_group, moe_b_group, moe_w_expert, moe_b_expert,
                moe_w_gate, moe_w_up, moe_w_down, final_norm)

    wm = _mla_weights(mla_w_dq[0], mla_q_norm[0], mla_w_uq[0], mla_w_dkv[0], mla_kv_norm[0],
                      mla_w_uk[0], mla_w_uv[0])
    gn0 = norm_mix[0].reshape(1, D)
    q, k, v, keys_p = _mla_proj(xp, mods, gn0, wm, n_batch=1, t_new=n_p, tm=tm_proj, mod_row=prompt_row)
    o = _attention(q, k, v, mla=True, n_batch=bp, t_q=tp, t_k=tp, tq=tp, n_pairs=8, name="attn_mla_p")
    wp0 = _post_weights(0, mla_w_o[0], *moe_args)
    xp = _moe(xp, o, mods, 0, wp0, mod_row=prompt_row, final=False)

    rope_mla = _rope_tables(ts, MLA_ROPE, (0,))
    cache0 = jnp.pad(cache_mla[:, 0].reshape(bs * t_past, KV_DIM), ((0, 0), (0, KV_PAD - KV_DIM)))
    q, k, v = _mla_proj(xs, mods, gn0, wm, n_batch=bs, t_new=ts, tm=tm_proj, mod_row=sample_row_proj,
                        rope_tabs=rope_mla, cache=cache0)
    o = _attention(q, k, v, mla=True, n_batch=bs, t_q=ts, t_k=t_past + ts, tq=512, n_pairs=1, name="attn_mla_s")
    xs = _moe(xs, o, mods, 0, wp0, mod_row=sample_row_post, final=False)

    wg = _gqa_weights(gqa_w_qkv[0], gqa_q_norm[0], gqa_k_norm[0])
    gn1 = norm_mix[1].reshape(1, D)
    q, k, v, kv_p = _gqa_proj(xp, mods, gn1, wg, n_batch=1, t_new=n_p, tm=tm_proj, mod_row=prompt_row)
    o = _attention(q, k, v, mla=False, n_batch=bp, t_q=tp, t_k=tp, tq=tp, n_pairs=8, name="attn_gqa_p")
    wp1 = _post_weights(1, gqa_w_o[0], *moe_args)
    y_prompt = _moe(xp, o, mods, 1, wp1, mod_row=prompt_row, final=True)

    rope_gqa = _rope_tables(ts, GQA_HEAD_DIM, (0, GQA_HEAD_DIM))
    cache1 = cache_gqa[:, 0].reshape(bs * t_past, 2 * GQA_KV_HEADS * GQA_HEAD_DIM)
    q, k, v = _gqa_proj(xs, mods, gn1, wg, n_batch=bs, t_new=ts, tm=tm_proj, mod_row=sample_row_proj,
                        rope_tabs=rope_gqa, cache=cache1)
    o = _attention(q, k, v, mla=False, n_batch=bs, t_q=ts, t_k=t_past + ts, tq=512, n_pairs=1, name="attn_gqa_s")
    y_sample = _moe(xs, o, mods, 1, wp1, mod_row=sample_row_post, final=True)

    return (y_prompt.reshape(bp, tp, D), y_sample.reshape(bs, ts, D),
            keys_p.reshape(bp, 1, tp, KV_DIM),
            kv_p.reshape(bp, 1, tp, 2, GQA_KV_HEADS, GQA_HEAD_DIM))
```

```python
import functools
import math

import jax
import jax.numpy as jnp
from jax import lax
from jax.experimental import pallas as pl
from jax.experimental.pallas import tpu as pltpu

F32 = jnp.float32
BF16 = jnp.bfloat16

D = 1024
EPS = 1e-6
GRID_W = 64
ROPE_THETA = 10000.0
LANES = 128
HALF = LANES // 2
MLA_HEADS = 16
MLA_NOPE = 64
MLA_ROPE = 32
MLA_V = 64
Q_LORA = 384
KV_LORA = 256
KV_DIM = KV_LORA + MLA_ROPE
KV_PAD = 384
MLA_SCALE = 1.0 / math.sqrt(MLA_NOPE + MLA_ROPE)
GQA_HEADS = 16
GQA_KV_HEADS = 4
GQA_HEAD_DIM = 64
GQA_SCALE = 1.0 / math.sqrt(GQA_HEAD_DIM)
N_GROUPS = 4
EXPERTS_PER_GROUP = 4
N_EXPERTS = 16
EXPERT_FF = 256
NEG = -3.0e38
LOG2E = 1.4426950408889634
ATTN_KEY_CHUNK = 512

VMEM_LIMIT = 56 * 1024 * 1024
N_MOD_ROWS = 8


def _params(n_axes):
    return pltpu.CompilerParams(dimension_semantics=("arbitrary",) * n_axes,
                                vmem_limit_bytes=VMEM_LIMIT)


def _rms(x, g):
    ms = jnp.mean(x * x, axis=-1, keepdims=True)
    return x * lax.rsqrt(ms + EPS) * g


def _mod_index(layer, which, row):
    return (layer * 6 + which) * N_MOD_ROWS + row


def _mod_spec(layer, which, row_fn):
    return pl.BlockSpec((None, 1, D), lambda *g: (_mod_index(layer, which, row_fn(*g)), 0, 0))


def _full_spec(shape):
    n = len(shape)
    return pl.BlockSpec(shape, lambda *g: (0,) * n)


def _ada_kernel(c_ref, w_ref, b_ref, o_ref):
    c = c_ref[...]
    a = c / (1.0 + jnp.exp(-c))
    o_ref[...] = jnp.dot(a, w_ref[...], precision=lax.Precision.HIGHEST,
                         preferred_element_type=F32) + b_ref[...]


def _ada_table(cc, ada_w, ada_b):
    depth = ada_w.shape[0]
    out = pl.pallas_call(
        _ada_kernel,
        grid=(depth, 6),
        in_specs=[
            pl.BlockSpec((N_MOD_ROWS, D), lambda l, n: (0, 0)),
            pl.BlockSpec((None, D, D), lambda l, n: (l, 0, n)),
            pl.BlockSpec((None, 1, D), lambda l, n: (l, 0, n)),
        ],
        out_specs=pl.BlockSpec((None, None, N_MOD_ROWS, D), lambda l, n: (l, n, 0, 0)),
        out_shape=jax.ShapeDtypeStruct((depth, 6, N_MOD_ROWS, D), F32),
        compiler_params=_params(2),
        name="ada",
    )(cc, ada_w, ada_b.reshape(depth, 1, 6 * D))
    return out.reshape(depth * 6 * N_MOD_ROWS, 1, D)


def _rope(a, cos, sin_signed):
    lane = lax.broadcasted_iota(jnp.int32, a.shape, 1)
    nxt = pltpu.roll(a, LANES - 1, 1)
    prv = pltpu.roll(a, 1, 1)
    return a * cos + jnp.where((lane & 1) == 0, nxt, prv) * sin_signed


def _rope_tables(n_tokens, rot_dim, n_rep):
    t = jnp.arange(n_tokens)
    row = (t // GRID_W).astype(F32)
    col = (t % GRID_W).astype(F32)
    axis_dim = rot_dim // 2
    inv = jnp.power(ROPE_THETA, -jnp.arange(0, axis_dim, 2, dtype=F32) / axis_dim)
    ang = jnp.concatenate([row[:, None] * inv, col[:, None] * inv], axis=-1)
    cos = jnp.repeat(jnp.cos(ang), 2, axis=1)
    sin = jnp.repeat(jnp.sin(ang), 2, axis=1)
    sign = jnp.where(jnp.arange(rot_dim) % 2 == 0, -1.0, 1.0).astype(F32)
    sin = sin * sign
    rest = LANES - n_rep * rot_dim
    cos_t = jnp.concatenate([cos] * n_rep + [jnp.ones((n_tokens, rest), F32)], axis=1)
    sin_t = jnp.concatenate([sin] * n_rep + [jnp.zeros((n_tokens, rest), F32)], axis=1)
    return cos_t, sin_t


def _mla_proj_kernel(*refs, rope, n_cache, emit_keys):
    it = iter(refs)
    x_ref, sh_ref, sc_ref, gn_ref = next(it), next(it), next(it), next(it)
    wdq_ref, gq_ref, wuq_ref, wdkv_ref, gkv_ref, wkexp_ref, wuv_ref = (next(it) for _ in range(7))
    cos_ref = sin_ref = cache_ref = keys_ref = None
    if rope:
        cos_ref, sin_ref = next(it), next(it)
    if n_cache:
        cache_ref = next(it)
    q_ref, k_ref, v_ref = next(it), next(it), next(it)
    if emit_keys:
        keys_ref = next(it)

    def expand(ckv, pe):
        ckv = ckv.astype(BF16)
        kn = jnp.dot(ckv, wkexp_ref[...], preferred_element_type=F32)
        k_ref[...] = jnp.concatenate(
            [kn[:, h_ * LANES:(h_ + 1) * LANES] + pe for h_ in range(MLA_HEADS)], axis=1).astype(BF16)
        v_ref[...] = jnp.dot(ckv, wuv_ref[...], preferred_element_type=F32).astype(BF16)

    def new_tokens():
        x = x_ref[...]
        h = (_rms(x, gn_ref[...]) * (1.0 + sc_ref[...]) + sh_ref[...]).astype(BF16)
        ql = jnp.dot(h, wdq_ref[...], preferred_element_type=F32)
        qn = _rms(ql, gq_ref[...]).astype(BF16)
        q = jnp.dot(qn, wuq_ref[...], preferred_element_type=F32)
        kv = jnp.dot(h, wdkv_ref[...], preferred_element_type=F32)
        ckv = _rms(kv[:, :KV_LORA], gkv_ref[...])
        pe = kv[:, KV_LORA:]
        if rope:
            cos, sin = cos_ref[...], sin_ref[...]
            pe = _rope(pe, cos, sin)
            q = jnp.concatenate(
                [_rope(q[:, h_ * LANES:(h_ + 1) * LANES], cos, sin) for h_ in range(MLA_HEADS)], axis=1)
        q_ref[...] = (q * (MLA_SCALE * LOG2E)).astype(BF16)
        if emit_keys:
            keys_ref[:, 0:KV_LORA] = ckv
            keys_ref[:, KV_LORA:KV_DIM] = pe[:, 0:MLA_ROPE]
        expand(ckv, pe)

    if n_cache:
        t = pl.program_id(1)
        pl.when(t >= n_cache)(new_tokens)

        @pl.when(t < n_cache)
        def _():
            expand(cache_ref[:, 0:KV_LORA], cache_ref[:, KV_LORA:KV_PAD])
    else:
        new_tokens()


def _mla_proj(x2d, mods, gn, w, *, n_batch, t_new, tm, mod_row, rope_tabs=None, cache=None):
    rope = rope_tabs is not None
    t_cache = 0 if cache is None else cache.shape[0] // n_batch
    n_cache = t_cache // tm
    ntn = t_new // tm
    nt = n_cache + ntn
    emit_keys = cache is None

    def new_idx(b, t):
        return b * ntn + jnp.maximum(t - n_cache, 0)

    row_fn = lambda b, t: mod_row(b)
    in_specs = [
        pl.BlockSpec((tm, D), lambda b, t: (new_idx(b, t), 0)),
        _mod_spec(0, 0, row_fn), _mod_spec(0, 1, row_fn),
        _full_spec((1, D)),
        _full_spec((D, Q_LORA)), _full_spec((1, Q_LORA)), _full_spec((Q_LORA, MLA_HEADS * LANES)),
        _full_spec((D, KV_PAD)), _full_spec((1, KV_LORA)),
        _full_spec((KV_LORA, MLA_HEADS * LANES)), _full_spec((KV_LORA, MLA_HEADS * MLA_V)),
    ]
    args = [x2d, mods, mods, gn, w["wdq"], w["gq"], w["wuq"], w["wdkv"], w["gkv"], w["wkexp"], w["wuv"]]
    if rope:
        in_specs += [pl.BlockSpec((tm, LANES), lambda b, t: (jnp.maximum(t - n_cache, 0), 0))] * 2
        args += list(rope_tabs)
    if n_cache:
        in_specs.append(pl.BlockSpec((tm, KV_PAD), lambda b, t: (b * n_cache + jnp.minimum(t, n_cache - 1), 0)))
        args.append(cache)
    n_new = n_batch * t_new
    n_keys = n_batch * (t_cache + t_new)
    out_specs = [
        pl.BlockSpec((tm, MLA_HEADS * LANES), lambda b, t: (new_idx(b, t), 0)),
        pl.BlockSpec((tm, MLA_HEADS * LANES), lambda b, t: (b * nt + t, 0)),
        pl.BlockSpec((tm, MLA_HEADS * MLA_V), lambda b, t: (b * nt + t, 0)),
    ]
    out_shape = [
        jax.ShapeDtypeStruct((n_new, MLA_HEADS * LANES), BF16),
        jax.ShapeDtypeStruct((n_keys, MLA_HEADS * LANES), BF16),
        jax.ShapeDtypeStruct((n_keys, MLA_HEADS * MLA_V), BF16),
    ]
    if emit_keys:
        out_specs.append(pl.BlockSpec((tm, KV_DIM), lambda b, t: (new_idx(b, t), 0)))
        out_shape.append(jax.ShapeDtypeStruct((n_new, KV_DIM), F32))
    return pl.pallas_call(
        functools.partial(_mla_proj_kernel, rope=rope, n_cache=n_cache, emit_keys=emit_keys),
        grid=(n_batch, nt), in_specs=in_specs, out_specs=out_specs, out_shape=out_shape,
        compiler_params=_params(2), name="mla_proj_s" if rope else "mla_proj_p",
    )(*args)


def _dup_halves(a):
    cols = []
    for c in range(a.shape[1] // LANES):
        blk = a[:, c * LANES:(c + 1) * LANES]
        rot = pltpu.roll(blk, HALF, 1)
        low = lax.broadcasted_iota(jnp.int32, blk.shape, 1) < HALF
        cols += [jnp.where(low, blk, rot), jnp.where(low, rot, blk)]
    return jnp.concatenate(cols, axis=1)


def _group_mean_sq(a, bmat):
    sq = a * a
    hi = sq.astype(BF16)
    lo = (sq - hi.astype(F32)).astype(BF16)
    return (jnp.dot(hi, bmat, preferred_element_type=F32) + jnp.dot(lo, bmat, preferred_element_type=F32))


def _gqa_proj_kernel(*refs, rope, n_cache, emit_kv):
    it = iter(refs)
    x_ref, sh_ref, sc_ref, gn_ref, w_ref, gq_ref, gk_ref, bmat_ref = (next(it) for _ in range(8))
    cos_ref = sin_ref = cache_ref = kv_ref = None
    if rope:
        cos_ref, sin_ref = next(it), next(it)
    if n_cache:
        cache_ref = next(it)
    q_ref, k_ref, v_ref = next(it), next(it), next(it)
    if emit_kv:
        kv_ref = next(it)
    nq = GQA_HEADS * GQA_HEAD_DIM
    nk = GQA_KV_HEADS * GQA_HEAD_DIM
    blk = 2 * LANES

    def new_tokens():
        x = x_ref[...]
        h = (_rms(x, gn_ref[...]) * (1.0 + sc_ref[...]) + sh_ref[...]).astype(BF16)
        qkv = jnp.dot(h, w_ref[...], preferred_element_type=F32)
        bmat = bmat_ref[...]

        def head_norm(a, g):
            return a * lax.rsqrt(_group_mean_sq(a, bmat) + EPS) * g

        def maybe_rope(a):
            if not rope:
                return a
            cos, sin = cos_ref[...], sin_ref[...]
            return jnp.concatenate(
                [_rope(a[:, c * LANES:(c + 1) * LANES], cos, sin) for c in range(a.shape[1] // LANES)], axis=1)

        for c in range(nq // blk):
            qb = head_norm(qkv[:, c * blk:(c + 1) * blk], gq_ref[...])
            q_ref[:, c * blk:(c + 1) * blk] = (maybe_rope(qb) * (GQA_SCALE * LOG2E)).astype(BF16)
        kn = head_norm(qkv[:, nq:nq + nk], gk_ref[...])
        vv = qkv[:, nq + nk:]
        if emit_kv:
            kv_ref[:, 0:nk] = kn
            kv_ref[:, nk:2 * nk] = vv
        k_ref[...] = _dup_halves(maybe_rope(kn)).astype(BF16)
        v_ref[...] = _dup_halves(vv).astype(BF16)

    if n_cache:
        t = pl.program_id(1)
        pl.when(t >= n_cache)(new_tokens)

        @pl.when(t < n_cache)
        def _():
            c = cache_ref[...]
            k_ref[...] = _dup_halves(c[:, 0:nk]).astype(BF16)
            v_ref[...] = _dup_halves(c[:, nk:2 * nk]).astype(BF16)
    else:
        new_tokens()


def _gqa_proj(x2d, mods, gn, w, *, n_batch, t_new, tm, mod_row, rope_tabs=None, cache=None):
    rope = rope_tabs is not None
    t_cache = 0 if cache is None else cache.shape[0] // n_batch
    n_cache = t_cache // tm
    ntn = t_new // tm
    nt = n_cache + ntn
    emit_kv = cache is None
    nq = GQA_HEADS * GQA_HEAD_DIM
    nk = GQA_KV_HEADS * GQA_HEAD_DIM
    nqkv = nq + 2 * nk
    ndup = GQA_KV_HEADS * LANES

    def new_idx(b, t):
        return b * ntn + jnp.maximum(t - n_cache, 0)

    row_fn = lambda b, t: mod_row(b)
    in_specs = [
        pl.BlockSpec((tm, D), lambda b, t: (new_idx(b, t), 0)),
        _mod_spec(1, 0, row_fn), _mod_spec(1, 1, row_fn),
        _full_spec((1, D)), _full_spec((D, nqkv)),
        _full_spec((1, 2 * LANES)), _full_spec((1, 2 * LANES)), _full_spec((2 * LANES, 2 * LANES)),
    ]
    args = [x2d, mods, mods, gn, w["wqkv"], w["gq"], w["gk"], w["bmat"]]
    if rope:
        in_specs += [pl.BlockSpec((tm, LANES), lambda b, t: (jnp.maximum(t - n_cache, 0), 0))] * 2
        args += list(rope_tabs)
    if n_cache:
        in_specs.append(pl.BlockSpec((tm, 2 * nk), lambda b, t: (b * n_cache + jnp.minimum(t, n_cache - 1), 0)))
        args.append(cache)
    n_new = n_batch * t_new
    n_keys = n_batch * (t_cache + t_new)
    out_specs = [
        pl.BlockSpec((tm, nq), lambda b, t: (new_idx(b, t), 0)),
        pl.BlockSpec((tm, ndup), lambda b, t: (b * nt + t, 0)),
        pl.BlockSpec((tm, ndup), lambda b, t: (b * nt + t, 0)),
    ]
    out_shape = [
        jax.ShapeDtypeStruct((n_new, nq), BF16),
        jax.ShapeDtypeStruct((n_keys, ndup), BF16),
        jax.ShapeDtypeStruct((n_keys, ndup), BF16),
    ]
    if emit_kv:
        out_specs.append(pl.BlockSpec((tm, 2 * nk), lambda b, t: (new_idx(b, t), 0)))
        out_shape.append(jax.ShapeDtypeStruct((n_new, 2 * nk), F32))
    return pl.pallas_call(
        functools.partial(_gqa_proj_kernel, rope=rope, n_cache=n_cache, emit_kv=emit_kv),
        grid=(n_batch, nt), in_specs=in_specs, out_specs=out_specs, out_shape=out_shape,
        compiler_params=_params(2), name="gqa_proj_s" if rope else "gqa_proj_p",
    )(*args)


def _attn_kernel(q_ref, k_ref, v_ref, o_ref, *, mla, n_pairs, ck):
    tq = q_ref.shape[0]
    nc = k_ref.shape[0] // ck
    lane = lax.broadcasted_iota(jnp.int32, (tq, LANES), 1)
    low = lane < HALF
    items = [(p, c, hh) for c in range(nc) for p in range(n_pairs) for hh in range(2)]
    heads = {}
    state = {}

    def head_operands(p, hh):
        if (p, hh) not in heads:
            if mla:
                hd = 2 * p + hh
                heads[(p, hh)] = (q_ref[:, hd * LANES:(hd + 1) * LANES], hd, p)
            else:
                g = p // 2 if n_pairs > 1 else 0
                qp = q_ref[:, p * LANES:(p + 1) * LANES]
                qh = jnp.where(low if hh == 0 else jnp.logical_not(low), qp, jnp.zeros_like(qp))
                heads[(p, hh)] = (qh, g, g)
        return heads[(p, hh)]

    def scores(item):
        p, c, hh = item
        qh, kc, _ = head_operands(p, hh)
        kh = k_ref[c * ck:(c + 1) * ck, kc * LANES:(kc + 1) * LANES]
        return lax.dot_general(qh, kh, (((1,), (1,)), ((), ())), preferred_element_type=F32)

    s_cur = scores(items[0])
    for idx, (p, c, hh) in enumerate(items):
        s_next = scores(items[idx + 1]) if idx + 1 < len(items) else None
        vc = head_operands(p, hh)[2]
        vb = v_ref[c * ck:(c + 1) * ck, vc * LANES:(vc + 1) * LANES]
        m_c = jnp.max(s_cur, axis=-1, keepdims=True)
        if c == 0:
            m = m_c
            e = jnp.exp2(s_cur - m)
            l = jnp.sum(e, axis=-1, keepdims=True)
            acc = jnp.dot(e.astype(BF16), vb, preferred_element_type=F32)
        else:
            m_old, l_old, acc_old = state[(p, hh)]
            m = jnp.maximum(m_old, m_c)
            alpha = jnp.exp2(m_old - m)
            e = jnp.exp2(s_cur - m)
            l = alpha * l_old + jnp.sum(e, axis=-1, keepdims=True)
            acc = alpha * acc_old + jnp.dot(e.astype(BF16), vb, preferred_element_type=F32)
        state[(p, hh)] = (m, l, acc)
        if c == nc - 1 and hh == 1:
            o0 = state[(p, 0)][2] / state[(p, 0)][1]
            o1 = state[(p, 1)][2] / state[(p, 1)][1]
            o_ref[:, p * LANES:(p + 1) * LANES] = jnp.where(low, o0, o1).astype(BF16)
        s_cur = s_next


def _attention(q, k, v, *, mla, n_batch, t_q, t_k, tq, n_pairs, name):
    total_pairs = 8
    nj = total_pairs // n_pairs
    nqt = t_q // tq
    if mla:
        qw, kw, vw = n_pairs * 2 * LANES, n_pairs * 2 * LANES, n_pairs * LANES
        kv_col = lambda j: j
    else:
        qw = n_pairs * LANES
        kw = vw = max(n_pairs // 2, 1) * LANES
        kv_col = (lambda j: j // 2) if n_pairs == 1 else (lambda j: j)
    return pl.pallas_call(
        functools.partial(_attn_kernel, mla=mla, n_pairs=n_pairs, ck=min(t_k, ATTN_KEY_CHUNK)),
        grid=(n_batch, nj, nqt),
        in_specs=[
            pl.BlockSpec((tq, qw), lambda b, j, i: (b * nqt + i, j)),
            pl.BlockSpec((t_k, kw), lambda b, j, i: (b, kv_col(j))),
            pl.BlockSpec((t_k, vw), lambda b, j, i: (b, kv_col(j))),
        ],
        out_specs=pl.BlockSpec((tq, n_pairs * LANES), lambda b, j, i: (b * nqt + i, j)),
        out_shape=jax.ShapeDtypeStruct((n_batch * t_q, total_pairs * LANES), BF16),
        compiler_params=_params(3), name=name,
    )(q, k, v)


def _route(logits):
    lane = lax.broadcasted_iota(jnp.int32, logits.shape, 1).astype(F32)
    big = jnp.float32(1e9)
    is_grp = (lane >= N_EXPERTS) & (lane < N_EXPERTS + N_GROUPS)
    gl = jnp.where(is_grp, logits, NEG)
    gm = jnp.max(gl, axis=-1, keepdims=True)
    g_w = 1.0 / jnp.sum(jnp.exp(gl - gm), axis=-1, keepdims=True)
    g_idx = jnp.min(jnp.where(gl == gm, lane, big), axis=-1, keepdims=True) - N_EXPERTS
    lo = g_idx * EXPERTS_PER_GROUP
    el = jnp.where((lane >= lo) & (lane < lo + EXPERTS_PER_GROUP), logits, NEG)
    m1 = jnp.max(el, axis=-1, keepdims=True)
    i1 = jnp.min(jnp.where(el == m1, lane, big), axis=-1, keepdims=True)
    el2 = jnp.where(lane == i1, NEG, el)
    m2 = jnp.max(el2, axis=-1, keepdims=True)
    i2 = jnp.min(jnp.where(el2 == m2, lane, big), axis=-1, keepdims=True)
    t = jnp.exp(m2 - m1)
    w1 = g_w / (1.0 + t)
    w2 = g_w * t / (1.0 + t)
    return jnp.where(lane == i1, w1, 0.0) + jnp.where(lane == i2, w2, 0.0), g_idx


MOE_TM = 512
MOE_CHUNK = 192
MOE_SLOTS = 1344
SLOT_RADIX = 32


def _moe_kernel(x_ref, o_ref, wo_ref, gta_ref, shm_ref, scm_ref, gtm_ref, gn_ref,
                wr_hi_ref, wr_lo_ref, br_ref, sel_ref, ltri_ref, wg_ref, wu_ref, wd_ref, fin_ref,
                y_ref, hs_ref, cs_ref, ys_ref, pt_ref, *, final):
    tm = x_ref.shape[0]
    ffg = EXPERTS_PER_GROUP * EXPERT_FF
    mix = jnp.dot(o_ref[...], wo_ref[...], preferred_element_type=F32)
    xm = x_ref[...] + gta_ref[...] * mix
    y_ref[...] = xm
    h = _rms(xm, gn_ref[...]) * (1.0 + scm_ref[...]) + shm_ref[...]
    h_hi = h.astype(BF16)
    h_lo = (h - h_hi.astype(F32)).astype(BF16)
    logits = (jnp.dot(h_hi, wr_hi_ref[...], preferred_element_type=F32)
              + jnp.dot(h_lo, wr_hi_ref[...], preferred_element_type=F32)
              + jnp.dot(h_hi, wr_lo_ref[...], preferred_element_type=F32)) + br_ref[...]
    comb, g_idx = _route(logits)

    lane_i = lax.broadcasted_iota(jnp.int32, (tm, LANES), 1)
    lane = lane_i.astype(F32)
    onehot = jnp.where(lane == g_idx, 1.0, 0.0)
    rank = jnp.dot(ltri_ref[...], onehot.astype(BF16), preferred_element_type=F32)
    cnt = jnp.sum(onehot, axis=0, keepdims=True)
    ends = []
    end = jnp.int32(0)
    start_vec = jnp.zeros_like(lane)
    for g in range(N_GROUPS):
        start_vec = jnp.where(lane_i == g, (end * MOE_CHUNK).astype(F32), start_vec)
        n_g = cnt[0, g].astype(jnp.int32)
        end = end + sum((n_g > k * MOE_CHUNK).astype(jnp.int32) for k in range(-(-tm // MOE_CHUNK)))
        ends.append(end)
    total = ends[-1]
    slot = jnp.sum(onehot * (start_vec + rank), axis=-1, keepdims=True)
    slot_i = slot.astype(jnp.int32)
    pt = jnp.where(lax.broadcasted_iota(jnp.int32, (tm, MOE_SLOTS), 1) == slot_i, 1.0, 0.0).astype(BF16)
    pt_ref[...] = pt
    a = jnp.floor(slot * (1.0 / SLOT_RADIX))
    b = slot - SLOT_RADIX * a
    digits = jnp.where(lane_i == 0, a, jnp.where(lane_i == 1, b, 0.0)).astype(BF16)
    rows = lax.dot_general(sel_ref[...], digits, (((1,), (1,)), ((), ())), preferred_element_type=F32)
    slot_row = (rows[0:1, :] * SLOT_RADIX + rows[1:2, :]).astype(jnp.int32)
    p = jnp.where(lax.broadcasted_iota(jnp.int32, (MOE_SLOTS, tm), 0) == slot_row, 1.0, 0.0).astype(BF16)

    hs_ref[...] = jnp.dot(p, h_hi, preferred_element_type=F32).astype(BF16)
    c_hi = comb.astype(BF16)
    r1 = comb - c_hi.astype(F32)
    c_mid = r1.astype(BF16)
    c_lo = (r1 - c_mid.astype(F32)).astype(BF16)
    packed = (c_hi.astype(F32) + pltpu.roll(c_mid.astype(F32), N_EXPERTS, 1)
              + pltpu.roll(c_lo.astype(F32), 2 * N_EXPERTS, 1)).astype(BF16)
    cs = jnp.dot(p, packed, preferred_element_type=F32)
    cs_ref[...] = cs + pltpu.roll(cs, LANES - N_EXPERTS, 1) + pltpu.roll(cs, LANES - 2 * N_EXPERTS, 1)

    lane_c = lax.broadcasted_iota(jnp.int32, (MOE_CHUNK, LANES), 1)
    for c in range(MOE_SLOTS // MOE_CHUNK):
        rows_c = slice(c * MOE_CHUNK, (c + 1) * MOE_CHUNK)

        @pl.when(c < total)
        def _():
            g_c = sum((ends[g] <= c).astype(jnp.int32) for g in range(N_GROUPS - 1))
            xc = hs_ref[rows_c, :]
            cc = cs_ref[rows_c, :]
            hids = []
            for j in range(EXPERTS_PER_GROUP):
                e = g_c * EXPERTS_PER_GROUP + j
                gt = jnp.dot(xc, wg_ref[e], preferred_element_type=F32)
                up = jnp.dot(xc, wu_ref[e], preferred_element_type=F32)
                cj = jnp.sum(jnp.where(lane_c == e, cc, 0.0), axis=-1, keepdims=True)
                hids.append(((gt / (1.0 + jnp.exp(-gt))) * up * cj).astype(BF16))
            hid = jnp.concatenate(hids, axis=1)
            wd_g = wd_ref[pl.ds(pl.multiple_of(g_c * ffg, ffg), ffg), :]
            ys_ref[rows_c, :] = jnp.dot(hid, wd_g, preferred_element_type=F32).astype(BF16)

        @pl.when(c >= total)
        def _():
            ys_ref[rows_c, :] = jnp.zeros((MOE_CHUNK, D), BF16)

    y_tok = jnp.dot(pt_ref[...], ys_ref[...], preferred_element_type=F32)
    out = y_ref[...] + gtm_ref[...] * y_tok
    if final:
        out = _rms(out, fin_ref[...])
    y_ref[...] = out


def _moe(x2d, o2d, mods, layer, w, *, mod_row, final):
    n = x2d.shape[0]
    tm = MOE_TM
    row_fn = lambda i: mod_row(i)
    tile = pl.BlockSpec((tm, D), lambda i: (i, 0))
    once = lambda shape: pl.BlockSpec(shape, lambda i: (0,) * len(shape), pipeline_mode=pl.Buffered(1))
    per_layer = lambda shape: pl.BlockSpec((None,) + shape, lambda i: (layer,) + (0,) * len(shape),
                                           pipeline_mode=pl.Buffered(1))
    return pl.pallas_call(
        functools.partial(_moe_kernel, final=final),
        grid=(n // tm,),
        in_specs=[
            tile, tile, once((D, D)),
            _mod_spec(layer, 2, row_fn), _mod_spec(layer, 3, row_fn), _mod_spec(layer, 4, row_fn),
            _mod_spec(layer, 5, row_fn), _full_spec((1, D)),
            _full_spec((D, LANES)), _full_spec((D, LANES)), _full_spec((1, LANES)), _full_spec((8, LANES)),
            once((tm, tm)),
            per_layer((N_EXPERTS, D, EXPERT_FF)), per_layer((N_EXPERTS, D, EXPERT_FF)),
            per_layer((N_EXPERTS * EXPERT_FF, D)),
            _full_spec((1, D)),
        ],
        out_specs=tile,
        out_shape=jax.ShapeDtypeStruct((n, D), F32),
        scratch_shapes=[pltpu.VMEM((MOE_SLOTS, D), BF16), pltpu.VMEM((MOE_SLOTS, LANES), F32),
                        pltpu.VMEM((MOE_SLOTS, D), BF16), pltpu.VMEM((tm, MOE_SLOTS), BF16)],
        compiler_params=_params(1), name=f"moe_l{layer}",
    )(x2d, o2d, w["wo"], mods, mods, mods, mods, w["gn"], w["wr_hi"], w["wr_lo"], w["br"], w["sel2"],
      w["ltri"], w["wg_bf"], w["wu_bf"], w["wd_bf"], w["fin"])


def _mla_weights(w_dq, g_q, w_uq, w_dkv, g_kv, w_uk, w_uv):
    hd = MLA_NOPE + MLA_ROPE
    wuq = w_uq.reshape(Q_LORA, MLA_HEADS, hd)
    wuq = jnp.concatenate([wuq[..., MLA_NOPE:], wuq[..., :MLA_NOPE],
                           jnp.zeros((Q_LORA, MLA_HEADS, LANES - hd), F32)], axis=-1)
    wuk = jnp.concatenate([jnp.zeros((KV_LORA, MLA_HEADS, MLA_ROPE), F32), w_uk,
                           jnp.zeros((KV_LORA, MLA_HEADS, LANES - hd), F32)], axis=-1)
    return {
        "wdq": w_dq.astype(BF16), "gq": g_q.reshape(1, Q_LORA),
        "wuq": wuq.reshape(Q_LORA, MLA_HEADS * LANES).astype(BF16),
        "wdkv": jnp.pad(w_dkv, ((0, 0), (0, KV_PAD - KV_DIM))).astype(BF16), "gkv": g_kv.reshape(1, KV_LORA),
        "wkexp": wuk.reshape(KV_LORA, MLA_HEADS * LANES).astype(BF16),
        "wuv": w_uv.reshape(KV_LORA, MLA_HEADS * MLA_V).astype(BF16),
    }


def _gqa_weights(w_qkv, g_q, g_k):
    grp = jnp.arange(2 * LANES) // GQA_HEAD_DIM
    bmat = jnp.where(grp[:, None] == grp[None, :], 1.0 / GQA_HEAD_DIM, 0.0).astype(BF16)
    reps = 2 * LANES // GQA_HEAD_DIM
    return {"wqkv": w_qkv.astype(BF16), "gq": jnp.tile(g_q, reps).reshape(1, 2 * LANES),
            "gk": jnp.tile(g_k, reps).reshape(1, 2 * LANES), "bmat": bmat}


def _post_weights(l, w_o, norm_ffn, w_group, b_group, w_exp, b_exp, w_gate, w_up, w_down, final_norm):
    wr = jnp.concatenate([w_exp[l], w_group[l], jnp.zeros((D, LANES - N_EXPERTS - N_GROUPS), F32)], axis=1)
    br = jnp.concatenate([b_exp[l], b_group[l], jnp.zeros((LANES - N_EXPERTS - N_GROUPS,), F32)]).reshape(1, LANES)
    wr_hi = wr.astype(BF16)
    wr_lo = (wr - wr_hi.astype(F32)).astype(BF16)
    return {"wo": w_o.astype(BF16), "gn": norm_ffn[l].reshape(1, D), "wr_hi": wr_hi, "wr_lo": wr_lo, "br": br,
            "wg_bf": w_gate, "wu_bf": w_up, "wd_bf": w_down,
            "fin": final_norm.reshape(1, D),
            "sel2": jnp.zeros((8, LANES), BF16).at[0, 0].set(1.0).at[1, 1].set(1.0),
            "ltri": jnp.tri(MOE_TM, MOE_TM, -1, dtype=BF16)}


def kernel(x_prompt, x_sample, cache_mla, cache_gqa, c, c_ctx, ada_w, ada_b, norm_mix, norm_ffn,
           mla_w_dq, mla_q_norm, mla_w_uq, mla_w_dkv, mla_kv_norm, mla_w_uk, mla_w_uv, mla_w_o,
           gqa_w_qkv, gqa_q_norm, gqa_k_norm, gqa_w_o,
           moe_w_group, moe_b_group, moe_w_expert, moe_b_expert, moe_w_gate, moe_w_up, moe_w_down,
           final_norm):
    bp, tp, _ = x_prompt.shape
    bs, ts, _ = x_sample.shape
    t_past = cache_mla.shape[2]
    assert ada_w.shape[0] == 2 and 1 + bs <= N_MOD_ROWS
    n_p, n_s = bp * tp, bs * ts

    cc = jnp.concatenate([c_ctx[None], c, jnp.zeros((N_MOD_ROWS - 1 - bs, D), F32)], axis=0)
    mods = _ada_table(cc, ada_w, ada_b)

    xp = x_prompt.reshape(n_p, D)
    xs = x_sample.reshape(n_s, D)
    tm_proj = 512
    tm_post = MOE_TM
    prompt_row = lambda *_: 0
    sample_row_proj = lambda b: 1 + b
    sample_row_post = lambda i: 1 + i // (ts // tm_post)
    depth = moe_w_gate.shape[0]
    moe_args = (norm_ffn, moe_w_group, moe_b_group, moe_w_expert, moe_b_expert,
                moe_w_gate.astype(BF16), moe_w_up.astype(BF16),
                moe_w_down.reshape(depth, N_EXPERTS * EXPERT_FF, D).astype(BF16), final_norm)

    wm = _mla_weights(mla_w_dq[0], mla_q_norm[0], mla_w_uq[0], mla_w_dkv[0], mla_kv_norm[0],
                      mla_w_uk[0], mla_w_uv[0])
    gn0 = norm_mix[0].reshape(1, D)
    q, k, v, keys_p = _mla_proj(xp, mods, gn0, wm, n_batch=1, t_new=n_p, tm=tm_proj, mod_row=prompt_row)
    o = _attention(q, k, v, mla=True, n_batch=bp, t_q=tp, t_k=tp, tq=tp, n_pairs=8, name="attn_mla_p")
    wp0 = _post_weights(0, mla_w_o[0], *moe_args)
    xp = _moe(xp, o, mods, 0, wp0, mod_row=prompt_row, final=False)

    rope_mla = _rope_tables(ts, MLA_ROPE, 1)
    cache0 = jnp.pad(cache_mla[:, 0].reshape(bs * t_past, KV_DIM), ((0, 0), (0, KV_PAD - KV_DIM)))
    q, k, v = _mla_proj(xs, mods, gn0, wm, n_batch=bs, t_new=ts, tm=tm_proj, mod_row=sample_row_proj,
                        rope_tabs=rope_mla, cache=cache0)
    o = _attention(q, k, v, mla=True, n_batch=bs, t_q=ts, t_k=t_past + ts, tq=512, n_pairs=1, name="attn_mla_s")
    xs = _moe(xs, o, mods, 0, wp0, mod_row=sample_row_post, final=False)

    wg = _gqa_weights(gqa_w_qkv[0], gqa_q_norm[0], gqa_k_norm[0])
    gn1 = norm_mix[1].reshape(1, D)
    q, k, v, kv_p = _gqa_proj(xp, mods, gn1, wg, n_batch=1, t_new=n_p, tm=tm_proj, mod_row=prompt_row)
    o = _attention(q, k, v, mla=False, n_batch=bp, t_q=tp, t_k=tp, tq=tp, n_pairs=8, name="attn_gqa_p")
    wp1 = _post_weights(1, gqa_w_o[0], *moe_args)
    y_prompt = _moe(xp, o, mods, 1, wp1, mod_row=prompt_row, final=True)

    rope_gqa = _rope_tables(ts, GQA_HEAD_DIM, LANES // GQA_HEAD_DIM)
    cache1 = cache_gqa[:, 0].reshape(bs * t_past, 2 * GQA_KV_HEADS * GQA_HEAD_DIM)
    q, k, v = _gqa_proj(xs, mods, gn1, wg, n_batch=bs, t_new=ts, tm=tm_proj, mod_row=sample_row_proj,
                        rope_tabs=rope_gqa, cache=cache1)
    o = _attention(q, k, v, mla=False, n_batch=bs, t_q=ts, t_k=t_past + ts, tq=512, n_pairs=1, name="attn_gqa_s")
    y_sample = _moe(xs, o, mods, 1, wp1, mod_row=sample_row_post, final=True)

    return (y_prompt.reshape(bp, tp, D), y_sample.reshape(bs, ts, D),
            keys_p.reshape(bp, 1, tp, KV_DIM),
            kv_p.reshape(bp, 1, tp, 2, GQA_KV_HEADS, GQA_HEAD_DIM))
```

```python
import functools
import math

import jax
import jax.numpy as jnp
from jax import lax
from jax.experimental import pallas as pl
from jax.experimental.pallas import tpu as pltpu

F32 = jnp.float32
BF16 = jnp.bfloat16

D = 1024
EPS = 1e-6
GRID_W = 64
ROPE_THETA = 10000.0
LANES = 128
HALF = LANES // 2
MLA_HEADS = 16
MLA_NOPE = 64
MLA_ROPE = 32
MLA_V = 64
Q_LORA = 384
KV_LORA = 256
KV_DIM = KV_LORA + MLA_ROPE
KV_PAD = 384
MLA_SCALE = 1.0 / math.sqrt(MLA_NOPE + MLA_ROPE)
GQA_HEADS = 16
GQA_KV_HEADS = 4
GQA_HEAD_DIM = 64
GQA_SCALE = 1.0 / math.sqrt(GQA_HEAD_DIM)
N_GROUPS = 4
EXPERTS_PER_GROUP = 4
N_EXPERTS = 16
EXPERT_FF = 256
NEG = -3.0e38
LOG2E = 1.4426950408889634
ATTN_KEY_CHUNK = 512

VMEM_LIMIT = 56 * 1024 * 1024
N_MOD_ROWS = 8


def _params(n_axes):
    return pltpu.CompilerParams(dimension_semantics=("arbitrary",) * n_axes,
                                vmem_limit_bytes=VMEM_LIMIT)


def _rms(x, g):
    ms = jnp.mean(x * x, axis=-1, keepdims=True)
    return x * lax.rsqrt(ms + EPS) * g


def _mod_index(layer, which, row):
    return (layer * 6 + which) * N_MOD_ROWS + row


def _mod_spec(layer, which, row_fn):
    return pl.BlockSpec((None, 1, D), lambda *g: (_mod_index(layer, which, row_fn(*g)), 0, 0))


def _full_spec(shape):
    n = len(shape)
    return pl.BlockSpec(shape, lambda *g: (0,) * n)


def _ada_kernel(c_ref, w_ref, b_ref, o_ref):
    c = c_ref[...]
    a = c / (1.0 + jnp.exp(-c))
    o_ref[...] = jnp.dot(a, w_ref[...], precision=lax.Precision.HIGHEST,
                         preferred_element_type=F32) + b_ref[...]


def _ada_table(cc, ada_w, ada_b):
    depth = ada_w.shape[0]
    out = pl.pallas_call(
        _ada_kernel,
        grid=(depth, 6),
        in_specs=[
            pl.BlockSpec((N_MOD_ROWS, D), lambda l, n: (0, 0)),
            pl.BlockSpec((None, D, D), lambda l, n: (l, 0, n)),
            pl.BlockSpec((None, 1, D), lambda l, n: (l, 0, n)),
        ],
        out_specs=pl.BlockSpec((None, None, N_MOD_ROWS, D), lambda l, n: (l, n, 0, 0)),
        out_shape=jax.ShapeDtypeStruct((depth, 6, N_MOD_ROWS, D), F32),
        compiler_params=_params(2),
        name="ada",
    )(cc, ada_w, ada_b.reshape(depth, 1, 6 * D))
    return out.reshape(depth * 6 * N_MOD_ROWS, 1, D)


def _rope(a, cos, sin_signed):
    lane = lax.broadcasted_iota(jnp.int32, a.shape, 1)
    nxt = pltpu.roll(a, LANES - 1, 1)
    prv = pltpu.roll(a, 1, 1)
    return a * cos + jnp.where((lane & 1) == 0, nxt, prv) * sin_signed


def _rope_tables(n_tokens, rot_dim, n_rep):
    t = jnp.arange(n_tokens)
    row = (t // GRID_W).astype(F32)
    col = (t % GRID_W).astype(F32)
    axis_dim = rot_dim // 2
    inv = jnp.power(ROPE_THETA, -jnp.arange(0, axis_dim, 2, dtype=F32) / axis_dim)
    ang = jnp.concatenate([row[:, None] * inv, col[:, None] * inv], axis=-1)
    cos = jnp.repeat(jnp.cos(ang), 2, axis=1)
    sin = jnp.repeat(jnp.sin(ang), 2, axis=1)
    sign = jnp.where(jnp.arange(rot_dim) % 2 == 0, -1.0, 1.0).astype(F32)
    sin = sin * sign
    rest = LANES - n_rep * rot_dim
    cos_t = jnp.concatenate([cos] * n_rep + [jnp.ones((n_tokens, rest), F32)], axis=1)
    sin_t = jnp.concatenate([sin] * n_rep + [jnp.zeros((n_tokens, rest), F32)], axis=1)
    return cos_t, sin_t


def _mla_proj_kernel(*refs, rope, n_cache, emit_keys):
    it = iter(refs)
    x_ref, sh_ref, sc_ref, gn_ref = next(it), next(it), next(it), next(it)
    wdq_ref, gq_ref, wuq_ref, wdkv_ref, gkv_ref, wkexp_ref, wuv_ref = (next(it) for _ in range(7))
    cos_ref = sin_ref = cache_ref = keys_ref = None
    if rope:
        cos_ref, sin_ref = next(it), next(it)
    if n_cache:
        cache_ref = next(it)
    q_ref, k_ref, v_ref = next(it), next(it), next(it)
    if emit_keys:
        keys_ref = next(it)

    def expand(ckv, pe):
        ckv = ckv.astype(BF16)
        kn = jnp.dot(ckv, wkexp_ref[...], preferred_element_type=F32)
        k_ref[...] = jnp.concatenate(
            [kn[:, h_ * LANES:(h_ + 1) * LANES] + pe for h_ in range(MLA_HEADS)], axis=1).astype(BF16)
        v_ref[...] = jnp.dot(ckv, wuv_ref[...], preferred_element_type=F32).astype(BF16)

    def new_tokens():
        x = x_ref[...]
        h = (_rms(x, gn_ref[...]) * (1.0 + sc_ref[...]) + sh_ref[...]).astype(BF16)
        ql = jnp.dot(h, wdq_ref[...], preferred_element_type=F32)
        qn = _rms(ql, gq_ref[...]).astype(BF16)
        q = jnp.dot(qn, wuq_ref[...], preferred_element_type=F32)
        kv = jnp.dot(h, wdkv_ref[...], preferred_element_type=F32)
        ckv = _rms(kv[:, :KV_LORA], gkv_ref[...])
        pe = kv[:, KV_LORA:]
        if rope:
            cos, sin = cos_ref[...], sin_ref[...]
            pe = _rope(pe, cos, sin)
            q = jnp.concatenate(
                [_rope(q[:, h_ * LANES:(h_ + 1) * LANES], cos, sin) for h_ in range(MLA_HEADS)], axis=1)
        q_ref[...] = (q * (MLA_SCALE * LOG2E)).astype(BF16)
        if emit_keys:
            t_req = keys_ref.shape[-1]
            for r in range(keys_ref.shape[0]):
                rows = slice(r * t_req, (r + 1) * t_req)
                keys_ref[r, 0:KV_LORA, :] = ckv[rows, :].T
                keys_ref[r, KV_LORA:KV_DIM, :] = pe[rows, :].T[0:MLA_ROPE, :]
        expand(ckv, pe)

    if n_cache:
        t = pl.program_id(1)
        pl.when(t >= n_cache)(new_tokens)

        @pl.when(t < n_cache)
        def _():
            pe_t = jnp.concatenate([cache_ref[KV_LORA:KV_DIM, :],
                                    jnp.zeros((LANES - MLA_ROPE, cache_ref.shape[-1]), F32)], axis=0)
            expand(cache_ref[0:KV_LORA, :].T, pe_t.T)
    else:
        new_tokens()


def _mla_proj(x2d, mods, gn, w, *, n_batch, t_new, tm, mod_row, rope_tabs=None, cache=None, t_state=None):
    rope = rope_tabs is not None
    t_cache = 0 if cache is None else cache.shape[-1]
    n_cache = t_cache // tm
    ntn = t_new // tm
    nt = n_cache + ntn
    emit_keys = cache is None

    def new_idx(b, t):
        return b * ntn + jnp.maximum(t - n_cache, 0)

    row_fn = lambda b, t: mod_row(b)
    in_specs = [
        pl.BlockSpec((tm, D), lambda b, t: (new_idx(b, t), 0)),
        _mod_spec(0, 0, row_fn), _mod_spec(0, 1, row_fn),
        _full_spec((1, D)),
        _full_spec((D, Q_LORA)), _full_spec((1, Q_LORA)), _full_spec((Q_LORA, MLA_HEADS * LANES)),
        _full_spec((D, KV_PAD)), _full_spec((1, KV_LORA)),
        _full_spec((KV_LORA, MLA_HEADS * LANES)), _full_spec((KV_LORA, MLA_HEADS * MLA_V)),
    ]
    args = [x2d, mods, mods, gn, w["wdq"], w["gq"], w["wuq"], w["wdkv"], w["gkv"], w["wkexp"], w["wuv"]]
    if rope:
        in_specs += [pl.BlockSpec((tm, LANES), lambda b, t: (jnp.maximum(t - n_cache, 0), 0))] * 2
        args += list(rope_tabs)
    if n_cache:
        in_specs.append(pl.BlockSpec((None, KV_DIM, tm), lambda b, t: (b, 0, jnp.minimum(t, n_cache - 1))))
        args.append(cache)
    n_new = n_batch * t_new
    n_keys = n_batch * (t_cache + t_new)
    out_specs = [
        pl.BlockSpec((tm, MLA_HEADS * LANES), lambda b, t: (new_idx(b, t), 0)),
        pl.BlockSpec((tm, MLA_HEADS * LANES), lambda b, t: (b * nt + t, 0)),
        pl.BlockSpec((tm, MLA_HEADS * MLA_V), lambda b, t: (b * nt + t, 0)),
    ]
    out_shape = [
        jax.ShapeDtypeStruct((n_new, MLA_HEADS * LANES), BF16),
        jax.ShapeDtypeStruct((n_keys, MLA_HEADS * LANES), BF16),
        jax.ShapeDtypeStruct((n_keys, MLA_HEADS * MLA_V), BF16),
    ]
    if emit_keys:
        assert tm % t_state == 0
        out_specs.append(pl.BlockSpec((tm // t_state, KV_DIM, t_state), lambda b, t: (new_idx(b, t), 0, 0)))
        out_shape.append(jax.ShapeDtypeStruct((n_new // t_state, KV_DIM, t_state), F32))
    return pl.pallas_call(
        functools.partial(_mla_proj_kernel, rope=rope, n_cache=n_cache, emit_keys=emit_keys),
        grid=(n_batch, nt), in_specs=in_specs, out_specs=out_specs, out_shape=out_shape,
        compiler_params=_params(2), name="mla_proj_s" if rope else "mla_proj_p",
    )(*args)


def _dup_halves(a):
    cols = []
    for c in range(a.shape[1] // LANES):
        blk = a[:, c * LANES:(c + 1) * LANES]
        rot = pltpu.roll(blk, HALF, 1)
        low = lax.broadcasted_iota(jnp.int32, blk.shape, 1) < HALF
        cols += [jnp.where(low, blk, rot), jnp.where(low, rot, blk)]
    return jnp.concatenate(cols, axis=1)


def _group_mean_sq(a, bmat):
    sq = a * a
    hi = sq.astype(BF16)
    lo = (sq - hi.astype(F32)).astype(BF16)
    return (jnp.dot(hi, bmat, preferred_element_type=F32) + jnp.dot(lo, bmat, preferred_element_type=F32))


def _gqa_proj_kernel(*refs, rope, n_cache, emit_kv):
    it = iter(refs)
    x_ref, sh_ref, sc_ref, gn_ref, w_ref, gq_ref, gk_ref, bmat_ref = (next(it) for _ in range(8))
    cos_ref = sin_ref = cache_ref = kv_ref = None
    if rope:
        cos_ref, sin_ref = next(it), next(it)
    if n_cache:
        cache_ref = next(it)
    q_ref, k_ref, v_ref = next(it), next(it), next(it)
    if emit_kv:
        kv_ref = next(it)
    nq = GQA_HEADS * GQA_HEAD_DIM
    nk = GQA_KV_HEADS * GQA_HEAD_DIM
    blk = 2 * LANES

    def new_tokens():
        x = x_ref[...]
        h = (_rms(x, gn_ref[...]) * (1.0 + sc_ref[...]) + sh_ref[...]).astype(BF16)
        qkv = jnp.dot(h, w_ref[...], preferred_element_type=F32)
        bmat = bmat_ref[...]

        def head_norm(a, g):
            return a * lax.rsqrt(_group_mean_sq(a, bmat) + EPS) * g

        def maybe_rope(a):
            if not rope:
                return a
            cos, sin = cos_ref[...], sin_ref[...]
            return jnp.concatenate(
                [_rope(a[:, c * LANES:(c + 1) * LANES], cos, sin) for c in range(a.shape[1] // LANES)], axis=1)

        for c in range(nq // blk):
            qb = head_norm(qkv[:, c * blk:(c + 1) * blk], gq_ref[...])
            q_ref[:, c * blk:(c + 1) * blk] = (maybe_rope(qb) * (GQA_SCALE * LOG2E)).astype(BF16)
        kn = head_norm(qkv[:, nq:nq + nk], gk_ref[...])
        vv = qkv[:, nq + nk:]
        if emit_kv:
            t_req = kv_ref.shape[-1]
            for r in range(kv_ref.shape[0]):
                kv_ref[r, 0] = kn[r * t_req:(r + 1) * t_req, :].T
                kv_ref[r, 1] = vv[r * t_req:(r + 1) * t_req, :].T
        k_ref[...] = _dup_halves(maybe_rope(kn)).astype(BF16)
        v_ref[...] = _dup_halves(vv).astype(BF16)

    if n_cache:
        t = pl.program_id(1)
        pl.when(t >= n_cache)(new_tokens)

        @pl.when(t < n_cache)
        def _():
            k_ref[...] = _dup_halves(cache_ref[0].T).astype(BF16)
            v_ref[...] = _dup_halves(cache_ref[1].T).astype(BF16)
    else:
        new_tokens()


def _gqa_proj(x2d, mods, gn, w, *, n_batch, t_new, tm, mod_row, rope_tabs=None, cache=None, t_state=None):
    rope = rope_tabs is not None
    t_cache = 0 if cache is None else cache.shape[-1]
    n_cache = t_cache // tm
    ntn = t_new // tm
    nt = n_cache + ntn
    emit_kv = cache is None
    nq = GQA_HEADS * GQA_HEAD_DIM
    nk = GQA_KV_HEADS * GQA_HEAD_DIM
    nqkv = nq + 2 * nk
    ndup = GQA_KV_HEADS * LANES

    def new_idx(b, t):
        return b * ntn + jnp.maximum(t - n_cache, 0)

    row_fn = lambda b, t: mod_row(b)
    in_specs = [
        pl.BlockSpec((tm, D), lambda b, t: (new_idx(b, t), 0)),
        _mod_spec(1, 0, row_fn), _mod_spec(1, 1, row_fn),
        _full_spec((1, D)), _full_spec((D, nqkv)),
        _full_spec((1, 2 * LANES)), _full_spec((1, 2 * LANES)), _full_spec((2 * LANES, 2 * LANES)),
    ]
    args = [x2d, mods, mods, gn, w["wqkv"], w["gq"], w["gk"], w["bmat"]]
    if rope:
        in_specs += [pl.BlockSpec((tm, LANES), lambda b, t: (jnp.maximum(t - n_cache, 0), 0))] * 2
        args += list(rope_tabs)
    if n_cache:
        in_specs.append(pl.BlockSpec((None, 2, nk, tm), lambda b, t: (b, 0, 0, jnp.minimum(t, n_cache - 1))))
        args.append(cache)
    n_new = n_batch * t_new
    n_keys = n_batch * (t_cache + t_new)
    out_specs = [
        pl.BlockSpec((tm, nq), lambda b, t: (new_idx(b, t), 0)),
        pl.BlockSpec((tm, ndup), lambda b, t: (b * nt + t, 0)),
        pl.BlockSpec((tm, ndup), lambda b, t: (b * nt + t, 0)),
    ]
    out_shape = [
        jax.ShapeDtypeStruct((n_new, nq), BF16),
        jax.ShapeDtypeStruct((n_keys, ndup), BF16),
        jax.ShapeDtypeStruct((n_keys, ndup), BF16),
    ]
    if emit_kv:
        assert tm % t_state == 0
        out_specs.append(pl.BlockSpec((tm // t_state, 2, nk, t_state), lambda b, t: (new_idx(b, t), 0, 0, 0)))
        out_shape.append(jax.ShapeDtypeStruct((n_new // t_state, 2, nk, t_state), F32))
    return pl.pallas_call(
        functools.partial(_gqa_proj_kernel, rope=rope, n_cache=n_cache, emit_kv=emit_kv),
        grid=(n_batch, nt), in_specs=in_specs, out_specs=out_specs, out_shape=out_shape,
        compiler_params=_params(2), name="gqa_proj_s" if rope else "gqa_proj_p",
    )(*args)


def _attn_kernel(q_ref, k_ref, v_ref, o_ref, *, mla, n_pairs, ck):
    tq = q_ref.shape[0]
    nc = k_ref.shape[0] // ck
    lane = lax.broadcasted_iota(jnp.int32, (tq, LANES), 1)
    low = lane < HALF
    items = [(p, c, hh) for c in range(nc) for p in range(n_pairs) for hh in range(2)]
    heads = {}
    state = {}

    def head_operands(p, hh):
        if (p, hh) not in heads:
            if mla:
                hd = 2 * p + hh
                heads[(p, hh)] = (q_ref[:, hd * LANES:(hd + 1) * LANES], hd, p)
            else:
                g = p // 2 if n_pairs > 1 else 0
                qp = q_ref[:, p * LANES:(p + 1) * LANES]
                qh = jnp.where(low if hh == 0 else jnp.logical_not(low), qp, jnp.zeros_like(qp))
                heads[(p, hh)] = (qh, g, g)
        return heads[(p, hh)]

    def scores(item):
        p, c, hh = item
        qh, kc, _ = head_operands(p, hh)
        kh = k_ref[c * ck:(c + 1) * ck, kc * LANES:(kc + 1) * LANES]
        return lax.dot_general(qh, kh, (((1,), (1,)), ((), ())), preferred_element_type=F32)

    s_cur = scores(items[0])
    for idx, (p, c, hh) in enumerate(items):
        s_next = scores(items[idx + 1]) if idx + 1 < len(items) else None
        vc = head_operands(p, hh)[2]
        vb = v_ref[c * ck:(c + 1) * ck, vc * LANES:(vc + 1) * LANES]
        m_c = jnp.max(s_cur, axis=-1, keepdims=True)
        if c == 0:
            m = m_c
            e = jnp.exp2(s_cur - m)
            l = jnp.sum(e, axis=-1, keepdims=True)
            acc = jnp.dot(e.astype(BF16), vb, preferred_element_type=F32)
        else:
            m_old, l_old, acc_old = state[(p, hh)]
            m = jnp.maximum(m_old, m_c)
            alpha = jnp.exp2(m_old - m)
            e = jnp.exp2(s_cur - m)
            l = alpha * l_old + jnp.sum(e, axis=-1, keepdims=True)
            acc = alpha * acc_old + jnp.dot(e.astype(BF16), vb, preferred_element_type=F32)
        state[(p, hh)] = (m, l, acc)
        if c == nc - 1 and hh == 1:
            o0 = state[(p, 0)][2] / state[(p, 0)][1]
            o1 = state[(p, 1)][2] / state[(p, 1)][1]
            o_ref[:, p * LANES:(p + 1) * LANES] = jnp.where(low, o0, o1).astype(BF16)
        s_cur = s_next


def _attention(q, k, v, *, mla, n_batch, t_q, t_k, tq, n_pairs, name):
    total_pairs = 8
    nj = total_pairs // n_pairs
    nqt = t_q // tq
    if mla:
        qw, kw, vw = n_pairs * 2 * LANES, n_pairs * 2 * LANES, n_pairs * LANES
        kv_col = lambda j: j
    else:
        qw = n_pairs * LANES
        kw = vw = max(n_pairs // 2, 1) * LANES
        kv_col = (lambda j: j // 2) if n_pairs == 1 else (lambda j: j)
    return pl.pallas_call(
        functools.partial(_attn_kernel, mla=mla, n_pairs=n_pairs, ck=min(t_k, ATTN_KEY_CHUNK)),
        grid=(n_batch, nj, nqt),
        in_specs=[
            pl.BlockSpec((tq, qw), lambda b, j, i: (b * nqt + i, j)),
            pl.BlockSpec((t_k, kw), lambda b, j, i: (b, kv_col(j))),
            pl.BlockSpec((t_k, vw), lambda b, j, i: (b, kv_col(j))),
        ],
        out_specs=pl.BlockSpec((tq, n_pairs * LANES), lambda b, j, i: (b * nqt + i, j)),
        out_shape=jax.ShapeDtypeStruct((n_batch * t_q, total_pairs * LANES), BF16),
        compiler_params=_params(3), name=name,
    )(q, k, v)


def _route(logits):
    lane = lax.broadcasted_iota(jnp.int32, logits.shape, 1).astype(F32)
    big = jnp.float32(1e9)
    is_grp = (lane >= N_EXPERTS) & (lane < N_EXPERTS + N_GROUPS)
    gl = jnp.where(is_grp, logits, NEG)
    gm = jnp.max(gl, axis=-1, keepdims=True)
    g_w = 1.0 / jnp.sum(jnp.exp(gl - gm), axis=-1, keepdims=True)
    g_idx = jnp.min(jnp.where(gl == gm, lane, big), axis=-1, keepdims=True) - N_EXPERTS
    lo = g_idx * EXPERTS_PER_GROUP
    el = jnp.where((lane >= lo) & (lane < lo + EXPERTS_PER_GROUP), logits, NEG)
    m1 = jnp.max(el, axis=-1, keepdims=True)
    i1 = jnp.min(jnp.where(el == m1, lane, big), axis=-1, keepdims=True)
    el2 = jnp.where(lane == i1, NEG, el)
    m2 = jnp.max(el2, axis=-1, keepdims=True)
    i2 = jnp.min(jnp.where(el2 == m2, lane, big), axis=-1, keepdims=True)
    t = jnp.exp(m2 - m1)
    w1 = g_w / (1.0 + t)
    w2 = g_w * t / (1.0 + t)
    return jnp.where(lane == i1, w1, 0.0) + jnp.where(lane == i2, w2, 0.0), g_idx


MOE_TM = 512
MOE_CHUNK = 144
MOE_SLOTS = 1152
SLOT_RADIX = 32


def _moe_kernel(x_ref, o_ref, wo_ref, gta_ref, shm_ref, scm_ref, gtm_ref, gn_ref,
                wr_hi_ref, wr_lo_ref, br_ref, sel_ref, ltri_ref, wg_ref, wu_ref, wd_ref, fin_ref,
                y_ref, hs_ref, cs_ref, ys_ref, pt_ref, *, final):
    tm = x_ref.shape[0]
    ffg = EXPERTS_PER_GROUP * EXPERT_FF
    mix = jnp.dot(o_ref[...], wo_ref[...], preferred_element_type=F32)
    xm = x_ref[...] + gta_ref[...] * mix
    y_ref[...] = xm
    h = _rms(xm, gn_ref[...]) * (1.0 + scm_ref[...]) + shm_ref[...]
    h_hi = h.astype(BF16)
    h_lo = (h - h_hi.astype(F32)).astype(BF16)
    logits = (jnp.dot(h_hi, wr_hi_ref[...], preferred_element_type=F32)
              + jnp.dot(h_lo, wr_hi_ref[...], preferred_element_type=F32)
              + jnp.dot(h_hi, wr_lo_ref[...], preferred_element_type=F32)) + br_ref[...]
    comb, g_idx = _route(logits)

    lane_i = lax.broadcasted_iota(jnp.int32, (tm, LANES), 1)
    lane = lane_i.astype(F32)
    onehot = jnp.where(lane == g_idx, 1.0, 0.0)
    rank = jnp.dot(ltri_ref[...], onehot.astype(BF16), preferred_element_type=F32)
    cnt = jnp.sum(onehot, axis=0, keepdims=True)
    ends = []
    end = jnp.int32(0)
    start_vec = jnp.zeros_like(lane)
    for g in range(N_GROUPS):
        start_vec = jnp.where(lane_i == g, (end * MOE_CHUNK).astype(F32), start_vec)
        n_g = cnt[0, g].astype(jnp.int32)
        end = end + sum((n_g > k * MOE_CHUNK).astype(jnp.int32) for k in range(-(-tm // MOE_CHUNK)))
        ends.append(end)
    total = ends[-1]
    slot = jnp.sum(onehot * (start_vec + rank), axis=-1, keepdims=True)
    slot_i = slot.astype(jnp.int32)
    pt = jnp.where(lax.broadcasted_iota(jnp.int32, (tm, MOE_SLOTS), 1) == slot_i, 1.0, 0.0).astype(BF16)
    pt_ref[...] = pt
    a = jnp.floor(slot * (1.0 / SLOT_RADIX))
    b = slot - SLOT_RADIX * a
    digits = jnp.where(lane_i == 0, a, jnp.where(lane_i == 1, b, 0.0)).astype(BF16)
    rows = lax.dot_general(sel_ref[...], digits, (((1,), (1,)), ((), ())), preferred_element_type=F32)
    slot_row = (rows[0:1, :] * SLOT_RADIX + rows[1:2, :]).astype(jnp.int32)
    p = jnp.where(lax.broadcasted_iota(jnp.int32, (MOE_SLOTS, tm), 0) == slot_row, 1.0, 0.0).astype(BF16)

    hs_ref[...] = jnp.dot(p, h_hi, preferred_element_type=F32).astype(BF16)
    c_hi = comb.astype(BF16)
    r1 = comb - c_hi.astype(F32)
    c_mid = r1.astype(BF16)
    c_lo = (r1 - c_mid.astype(F32)).astype(BF16)
    packed = (c_hi.astype(F32) + pltpu.roll(c_mid.astype(F32), N_EXPERTS, 1)
              + pltpu.roll(c_lo.astype(F32), 2 * N_EXPERTS, 1)).astype(BF16)
    cs = jnp.dot(p, packed, preferred_element_type=F32)
    cs_ref[...] = cs + pltpu.roll(cs, LANES - N_EXPERTS, 1) + pltpu.roll(cs, LANES - 2 * N_EXPERTS, 1)

    lane_c = lax.broadcasted_iota(jnp.int32, (MOE_CHUNK, LANES), 1)
    for c in range(MOE_SLOTS // MOE_CHUNK):
        rows_c = slice(c * MOE_CHUNK, (c + 1) * MOE_CHUNK)

        @pl.when(c < total)
        def _():
            g_c = sum((ends[g] <= c).astype(jnp.int32) for g in range(N_GROUPS - 1))
            xc = hs_ref[rows_c, :]
            cc = cs_ref[rows_c, :]
            hids = []
            for j in range(EXPERTS_PER_GROUP):
                e = g_c * EXPERTS_PER_GROUP + j
                gt = jnp.dot(xc, wg_ref[e], preferred_element_type=F32)
                up = jnp.dot(xc, wu_ref[e], preferred_element_type=F32)
                cj = jnp.sum(jnp.where(lane_c == e, cc, 0.0), axis=-1, keepdims=True)
                hids.append(((gt / (1.0 + jnp.exp(-gt))) * up * cj).astype(BF16))
            hid = jnp.concatenate(hids, axis=1)
            wd_g = wd_ref[pl.ds(pl.multiple_of(g_c * ffg, ffg), ffg), :]
            ys_ref[rows_c, :] = jnp.dot(hid, wd_g, preferred_element_type=F32).astype(BF16)

        @pl.when(c >= total)
        def _():
            ys_ref[rows_c, :] = jnp.zeros((MOE_CHUNK, D), BF16)

    y_tok = jnp.dot(pt_ref[...], ys_ref[...], preferred_element_type=F32)
    out = y_ref[...] + gtm_ref[...] * y_tok
    if final:
        out = _rms(out, fin_ref[...])
    y_ref[...] = out


def _moe(x2d, o2d, mods, layer, w, *, mod_row, final):
    n = x2d.shape[0]
    tm = MOE_TM
    row_fn = lambda i: mod_row(i)
    tile = pl.BlockSpec((tm, D), lambda i: (i, 0))
    once = lambda shape: pl.BlockSpec(shape, lambda i: (0,) * len(shape), pipeline_mode=pl.Buffered(1))
    per_layer = lambda shape: pl.BlockSpec((None,) + shape, lambda i: (layer,) + (0,) * len(shape),
                                           pipeline_mode=pl.Buffered(1))
    return pl.pallas_call(
        functools.partial(_moe_kernel, final=final),
        grid=(n // tm,),
        in_specs=[
            tile, tile, once((D, D)),
            _mod_spec(layer, 2, row_fn), _mod_spec(layer, 3, row_fn), _mod_spec(layer, 4, row_fn),
            _mod_spec(layer, 5, row_fn), _full_spec((1, D)),
            _full_spec((D, LANES)), _full_spec((D, LANES)), _full_spec((1, LANES)), _full_spec((8, LANES)),
            once((tm, tm)),
            per_layer((N_EXPERTS, D, EXPERT_FF)), per_layer((N_EXPERTS, D, EXPERT_FF)),
            per_layer((N_EXPERTS * EXPERT_FF, D)),
            _full_spec((1, D)),
        ],
        out_specs=tile,
        out_shape=jax.ShapeDtypeStruct((n, D), F32),
        scratch_shapes=[pltpu.VMEM((MOE_SLOTS, D), BF16), pltpu.VMEM((MOE_SLOTS, LANES), F32),
                        pltpu.VMEM((MOE_SLOTS, D), BF16), pltpu.VMEM((tm, MOE_SLOTS), BF16)],
        compiler_params=_params(1), name=f"moe_l{layer}",
    )(x2d, o2d, w["wo"], mods, mods, mods, mods, w["gn"], w["wr_hi"], w["wr_lo"], w["br"], w["sel2"],
      w["ltri"], w["wg_bf"], w["wu_bf"], w["wd_bf"], w["fin"])


def _mla_weights(w_dq, g_q, w_uq, w_dkv, g_kv, w_uk, w_uv):
    hd = MLA_NOPE + MLA_ROPE
    wuq = w_uq.reshape(Q_LORA, MLA_HEADS, hd)
    wuq = jnp.concatenate([wuq[..., MLA_NOPE:], wuq[..., :MLA_NOPE],
                           jnp.zeros((Q_LORA, MLA_HEADS, LANES - hd), F32)], axis=-1)
    wuk = jnp.concatenate([jnp.zeros((KV_LORA, MLA_HEADS, MLA_ROPE), F32), w_uk,
                           jnp.zeros((KV_LORA, MLA_HEADS, LANES - hd), F32)], axis=-1)
    return {
        "wdq": w_dq.astype(BF16), "gq": g_q.reshape(1, Q_LORA),
        "wuq": wuq.reshape(Q_LORA, MLA_HEADS * LANES).astype(BF16),
        "wdkv": jnp.pad(w_dkv, ((0, 0), (0, KV_PAD - KV_DIM))).astype(BF16), "gkv": g_kv.reshape(1, KV_LORA),
        "wkexp": wuk.reshape(KV_LORA, MLA_HEADS * LANES).astype(BF16),
        "wuv": w_uv.reshape(KV_LORA, MLA_HEADS * MLA_V).astype(BF16),
    }


def _gqa_weights(w_qkv, g_q, g_k):
    grp = jnp.arange(2 * LANES) // GQA_HEAD_DIM
    bmat = jnp.where(grp[:, None] == grp[None, :], 1.0 / GQA_HEAD_DIM, 0.0).astype(BF16)
    reps = 2 * LANES // GQA_HEAD_DIM
    return {"wqkv": w_qkv.astype(BF16), "gq": jnp.tile(g_q, reps).reshape(1, 2 * LANES),
            "gk": jnp.tile(g_k, reps).reshape(1, 2 * LANES), "bmat": bmat}


def _post_weights(l, w_o, norm_ffn, w_group, b_group, w_exp, b_exp, w_gate, w_up, w_down, final_norm):
    wr = jnp.concatenate([w_exp[l], w_group[l], jnp.zeros((D, LANES - N_EXPERTS - N_GROUPS), F32)], axis=1)
    br = jnp.concatenate([b_exp[l], b_group[l], jnp.zeros((LANES - N_EXPERTS - N_GROUPS,), F32)]).reshape(1, LANES)
    wr_hi = wr.astype(BF16)
    wr_lo = (wr - wr_hi.astype(F32)).astype(BF16)
    return {"wo": w_o.astype(BF16), "gn": norm_ffn[l].reshape(1, D), "wr_hi": wr_hi, "wr_lo": wr_lo, "br": br,
            "wg_bf": w_gate, "wu_bf": w_up, "wd_bf": w_down,
            "fin": final_norm.reshape(1, D),
            "sel2": jnp.zeros((8, LANES), BF16).at[0, 0].set(1.0).at[1, 1].set(1.0),
            "ltri": jnp.tri(MOE_TM, MOE_TM, -1, dtype=BF16)}


def kernel(x_prompt, x_sample, cache_mla, cache_gqa, c, c_ctx, ada_w, ada_b, norm_mix, norm_ffn,
           mla_w_dq, mla_q_norm, mla_w_uq, mla_w_dkv, mla_kv_norm, mla_w_uk, mla_w_uv, mla_w_o,
           gqa_w_qkv, gqa_q_norm, gqa_k_norm, gqa_w_o,
           moe_w_group, moe_b_group, moe_w_expert, moe_b_expert, moe_w_gate, moe_w_up, moe_w_down,
           final_norm):
    bp, tp, _ = x_prompt.shape
    bs, ts, _ = x_sample.shape
    t_past = cache_mla.shape[2]
    assert ada_w.shape[0] == 2 and 1 + bs <= N_MOD_ROWS
    n_p, n_s = bp * tp, bs * ts

    cc = jnp.concatenate([c_ctx[None], c, jnp.zeros((N_MOD_ROWS - 1 - bs, D), F32)], axis=0)
    mods = _ada_table(cc, ada_w, ada_b)

    xp = x_prompt.reshape(n_p, D)
    xs = x_sample.reshape(n_s, D)
    tm_proj = 512
    tm_post = MOE_TM
    prompt_row = lambda *_: 0
    sample_row_proj = lambda b: 1 + b
    sample_row_post = lambda i: 1 + i // (ts // tm_post)
    depth = moe_w_gate.shape[0]
    moe_args = (norm_ffn, moe_w_group, moe_b_group, moe_w_expert, moe_b_expert,
                moe_w_gate.astype(BF16), moe_w_up.astype(BF16),
                moe_w_down.reshape(depth, N_EXPERTS * EXPERT_FF, D).astype(BF16), final_norm)

    wm = _mla_weights(mla_w_dq[0], mla_q_norm[0], mla_w_uq[0], mla_w_dkv[0], mla_kv_norm[0],
                      mla_w_uk[0], mla_w_uv[0])
    gn0 = norm_mix[0].reshape(1, D)
    q, k, v, keys_p = _mla_proj(xp, mods, gn0, wm, n_batch=1, t_new=n_p, tm=tm_proj, mod_row=prompt_row, t_state=tp)
    o = _attention(q, k, v, mla=True, n_batch=bp, t_q=tp, t_k=tp, tq=tp, n_pairs=8, name="attn_mla_p")
    wp0 = _post_weights(0, mla_w_o[0], *moe_args)
    xp = _moe(xp, o, mods, 0, wp0, mod_row=prompt_row, final=False)

    rope_mla = _rope_tables(ts, MLA_ROPE, 1)
    cache0 = jnp.transpose(cache_mla[:, 0], (0, 2, 1))
    q, k, v = _mla_proj(xs, mods, gn0, wm, n_batch=bs, t_new=ts, tm=tm_proj, mod_row=sample_row_proj,
                        rope_tabs=rope_mla, cache=cache0)
    o = _attention(q, k, v, mla=True, n_batch=bs, t_q=ts, t_k=t_past + ts, tq=512, n_pairs=1, name="attn_mla_s")
    xs = _moe(xs, o, mods, 0, wp0, mod_row=sample_row_post, final=False)

    wg = _gqa_weights(gqa_w_qkv[0], gqa_q_norm[0], gqa_k_norm[0])
    gn1 = norm_mix[1].reshape(1, D)
    q, k, v, kv_p = _gqa_proj(xp, mods, gn1, wg, n_batch=1, t_new=n_p, tm=tm_proj, mod_row=prompt_row, t_state=tp)
    o = _attention(q, k, v, mla=False, n_batch=bp, t_q=tp, t_k=tp, tq=tp, n_pairs=8, name="attn_gqa_p")
    wp1 = _post_weights(1, gqa_w_o[0], *moe_args)
    y_prompt = _moe(xp, o, mods, 1, wp1, mod_row=prompt_row, final=True)

    rope_gqa = _rope_tables(ts, GQA_HEAD_DIM, LANES // GQA_HEAD_DIM)
    cache1 = jnp.transpose(cache_gqa[:, 0], (0, 2, 3, 4, 1)).reshape(bs, 2, GQA_KV_HEADS * GQA_HEAD_DIM, t_past)
    q, k, v = _gqa_proj(xs, mods, gn1, wg, n_batch=bs, t_new=ts, tm=tm_proj, mod_row=sample_row_proj,
                        rope_tabs=rope_gqa, cache=cache1)
    o = _attention(q, k, v, mla=False, n_batch=bs, t_q=ts, t_k=t_past + ts, tq=512, n_pairs=1, name="attn_gqa_s")
    y_sample = _moe(xs, o, mods, 1, wp1, mod_row=sample_row_post, final=True)

    return (y_prompt.reshape(bp, tp, D), y_sample.reshape(bs, ts, D),
            jnp.transpose(keys_p, (0, 2, 1))[:, None],
            jnp.transpose(kv_p.reshape(bp, 2, GQA_KV_HEADS, GQA_HEAD_DIM, tp), (0, 4, 1, 2, 3))[:, None])
```

```python
import functools
import math

import jax
import jax.numpy as jnp
from jax import lax
from jax.experimental import pallas as pl
from jax.experimental.pallas import tpu as pltpu

F32 = jnp.float32
BF16 = jnp.bfloat16

D = 1024
EPS = 1e-6
GRID_W = 64
ROPE_THETA = 10000.0
LANES = 128
HALF = LANES // 2
MLA_HEADS = 16
MLA_NOPE = 64
MLA_ROPE = 32
MLA_V = 64
Q_LORA = 384
KV_LORA = 256
KV_DIM = KV_LORA + MLA_ROPE
KV_PAD = 384
MLA_SCALE = 1.0 / math.sqrt(MLA_NOPE + MLA_ROPE)
GQA_HEADS = 16
GQA_KV_HEADS = 4
GQA_HEAD_DIM = 64
GQA_SCALE = 1.0 / math.sqrt(GQA_HEAD_DIM)
N_GROUPS = 4
EXPERTS_PER_GROUP = 4
N_EXPERTS = 16
EXPERT_FF = 256
NEG = -3.0e38
LOG2E = 1.4426950408889634
ATTN_KEY_CHUNK = 512
ATTN_LOOKAHEAD = 1
ATTN_TQ = 512
PROJ_TM = 512

VMEM_LIMIT = 56 * 1024 * 1024
N_MOD_ROWS = 8


def _params(n_axes):
    return pltpu.CompilerParams(dimension_semantics=("arbitrary",) * n_axes,
                                vmem_limit_bytes=VMEM_LIMIT)


def _rms(x, g):
    ms = jnp.mean(x * x, axis=-1, keepdims=True)
    return x * lax.rsqrt(ms + EPS) * g


def _mod_index(layer, which, row):
    return (layer * 6 + which) * N_MOD_ROWS + row


def _mod_spec(layer, which, row_fn):
    return pl.BlockSpec((None, 1, D), lambda *g: (_mod_index(layer, which, row_fn(*g)), 0, 0))


def _full_spec(shape):
    n = len(shape)
    return pl.BlockSpec(shape, lambda *g: (0,) * n)


def _ada_kernel(c_ref, w_ref, b_ref, o_ref):
    c = c_ref[...]
    a = c / (1.0 + jnp.exp(-c))
    o_ref[...] = jnp.dot(a, w_ref[...], precision=lax.Precision.HIGHEST,
                         preferred_element_type=F32) + b_ref[...]


def _ada_table(cc, ada_w, ada_b):
    depth = ada_w.shape[0]
    out = pl.pallas_call(
        _ada_kernel,
        grid=(depth, 6),
        in_specs=[
            pl.BlockSpec((N_MOD_ROWS, D), lambda l, n: (0, 0)),
            pl.BlockSpec((None, D, D), lambda l, n: (l, 0, n)),
            pl.BlockSpec((None, 1, D), lambda l, n: (l, 0, n)),
        ],
        out_specs=pl.BlockSpec((None, None, N_MOD_ROWS, D), lambda l, n: (l, n, 0, 0)),
        out_shape=jax.ShapeDtypeStruct((depth, 6, N_MOD_ROWS, D), F32),
        compiler_params=_params(2),
        name="ada",
    )(cc, ada_w, ada_b.reshape(depth, 1, 6 * D))
    return out.reshape(depth * 6 * N_MOD_ROWS, 1, D)


def _rope(a, cos, sin_signed):
    lane = lax.broadcasted_iota(jnp.int32, a.shape, 1)
    nxt = pltpu.roll(a, LANES - 1, 1)
    prv = pltpu.roll(a, 1, 1)
    return a * cos + jnp.where((lane & 1) == 0, nxt, prv) * sin_signed


def _rope_tables(n_tokens, rot_dim, n_rep):
    t = jnp.arange(n_tokens)
    row = (t // GRID_W).astype(F32)
    col = (t % GRID_W).astype(F32)
    axis_dim = rot_dim // 2
    inv = jnp.power(ROPE_THETA, -jnp.arange(0, axis_dim, 2, dtype=F32) / axis_dim)
    ang = jnp.concatenate([row[:, None] * inv, col[:, None] * inv], axis=-1)
    cos = jnp.repeat(jnp.cos(ang), 2, axis=1)
    sin = jnp.repeat(jnp.sin(ang), 2, axis=1)
    sign = jnp.where(jnp.arange(rot_dim) % 2 == 0, -1.0, 1.0).astype(F32)
    sin = sin * sign
    rest = LANES - n_rep * rot_dim
    cos_t = jnp.concatenate([cos] * n_rep + [jnp.ones((n_tokens, rest), F32)], axis=1)
    sin_t = jnp.concatenate([sin] * n_rep + [jnp.zeros((n_tokens, rest), F32)], axis=1)
    return cos_t, sin_t


def _mla_proj_kernel(*refs, rope, n_cache, emit_keys):
    it = iter(refs)
    x_ref, sh_ref, sc_ref, gn_ref = next(it), next(it), next(it), next(it)
    wdq_ref, gq_ref, wuq_ref, wdkv_ref, gkv_ref, wkexp_ref, wuv_ref = (next(it) for _ in range(7))
    cos_ref = sin_ref = cache_ref = keys_ref = None
    if rope:
        cos_ref, sin_ref = next(it), next(it)
    if n_cache:
        cache_ref = next(it)
    q_ref, k_ref, v_ref = next(it), next(it), next(it)
    if emit_keys:
        keys_ref = next(it)

    def expand(ckv, pe):
        ckv = ckv.astype(BF16)
        kn = jnp.dot(ckv, wkexp_ref[...], preferred_element_type=F32)
        k_ref[...] = jnp.concatenate(
            [kn[:, h_ * LANES:(h_ + 1) * LANES] + pe for h_ in range(MLA_HEADS)], axis=1).astype(BF16)
        v_ref[...] = jnp.dot(ckv, wuv_ref[...], preferred_element_type=F32).astype(BF16)

    def new_tokens():
        x = x_ref[...]
        h = (_rms(x, gn_ref[...]) * (1.0 + sc_ref[...]) + sh_ref[...]).astype(BF16)
        ql = jnp.dot(h, wdq_ref[...], preferred_element_type=F32)
        qn = _rms(ql, gq_ref[...]).astype(BF16)
        q = jnp.dot(qn, wuq_ref[...], preferred_element_type=F32)
        kv = jnp.dot(h, wdkv_ref[...], preferred_element_type=F32)
        ckv = _rms(kv[:, :KV_LORA], gkv_ref[...])
        pe = kv[:, KV_LORA:]
        if rope:
            cos, sin = cos_ref[...], sin_ref[...]
            pe = _rope(pe, cos, sin)
            q = jnp.concatenate(
                [_rope(q[:, h_ * LANES:(h_ + 1) * LANES], cos, sin) for h_ in range(MLA_HEADS)], axis=1)
        q_ref[...] = (q * (MLA_SCALE * LOG2E)).astype(BF16)
        if emit_keys:
            t_req = keys_ref.shape[-1]
            for r in range(keys_ref.shape[0]):
                rows = slice(r * t_req, (r + 1) * t_req)
                keys_ref[r, 0:KV_LORA, :] = ckv[rows, :].T
                keys_ref[r, KV_LORA:KV_DIM, :] = pe[rows, :].T[0:MLA_ROPE, :]
        expand(ckv, pe)

    if n_cache:
        t = pl.program_id(1)
        pl.when(t >= n_cache)(new_tokens)

        @pl.when(t < n_cache)
        def _():
            pe_t = jnp.concatenate([cache_ref[KV_LORA:KV_DIM, :],
                                    jnp.zeros((LANES - MLA_ROPE, cache_ref.shape[-1]), F32)], axis=0)
            expand(cache_ref[0:KV_LORA, :].T, pe_t.T)
    else:
        new_tokens()


def _mla_proj(x2d, mods, gn, w, *, n_batch, t_new, tm, mod_row, rope_tabs=None, cache=None, t_state=None):
    rope = rope_tabs is not None
    t_cache = 0 if cache is None else cache.shape[-1]
    n_cache = t_cache // tm
    ntn = t_new // tm
    nt = n_cache + ntn
    emit_keys = cache is None

    def new_idx(b, t):
        return b * ntn + jnp.maximum(t - n_cache, 0)

    row_fn = lambda b, t: mod_row(b)
    in_specs = [
        pl.BlockSpec((tm, D), lambda b, t: (new_idx(b, t), 0)),
        _mod_spec(0, 0, row_fn), _mod_spec(0, 1, row_fn),
        _full_spec((1, D)),
        _full_spec((D, Q_LORA)), _full_spec((1, Q_LORA)), _full_spec((Q_LORA, MLA_HEADS * LANES)),
        _full_spec((D, KV_PAD)), _full_spec((1, KV_LORA)),
        _full_spec((KV_LORA, MLA_HEADS * LANES)), _full_spec((KV_LORA, MLA_HEADS * MLA_V)),
    ]
    args = [x2d, mods, mods, gn, w["wdq"], w["gq"], w["wuq"], w["wdkv"], w["gkv"], w["wkexp"], w["wuv"]]
    if rope:
        in_specs += [pl.BlockSpec((tm, LANES), lambda b, t: (jnp.maximum(t - n_cache, 0), 0))] * 2
        args += list(rope_tabs)
    if n_cache:
        in_specs.append(pl.BlockSpec((None, KV_DIM, tm), lambda b, t: (b, 0, jnp.minimum(t, n_cache - 1))))
        args.append(cache)
    n_new = n_batch * t_new
    n_keys = n_batch * (t_cache + t_new)
    out_specs = [
        pl.BlockSpec((tm, MLA_HEADS * LANES), lambda b, t: (new_idx(b, t), 0)),
        pl.BlockSpec((tm, MLA_HEADS * LANES), lambda b, t: (b * nt + t, 0)),
        pl.BlockSpec((tm, MLA_HEADS * MLA_V), lambda b, t: (b * nt + t, 0)),
    ]
    out_shape = [
        jax.ShapeDtypeStruct((n_new, MLA_HEADS * LANES), BF16),
        jax.ShapeDtypeStruct((n_keys, MLA_HEADS * LANES), BF16),
        jax.ShapeDtypeStruct((n_keys, MLA_HEADS * MLA_V), BF16),
    ]
    if emit_keys:
        assert tm % t_state == 0
        out_specs.append(pl.BlockSpec((tm // t_state, KV_DIM, t_state), lambda b, t: (new_idx(b, t), 0, 0)))
        out_shape.append(jax.ShapeDtypeStruct((n_new // t_state, KV_DIM, t_state), F32))
    return pl.pallas_call(
        functools.partial(_mla_proj_kernel, rope=rope, n_cache=n_cache, emit_keys=emit_keys),
        grid=(n_batch, nt), in_specs=in_specs, out_specs=out_specs, out_shape=out_shape,
        compiler_params=_params(2), name="mla_proj_s" if rope else "mla_proj_p",
    )(*args)


def _dup_halves(a):
    cols = []
    for c in range(a.shape[1] // LANES):
        blk = a[:, c * LANES:(c + 1) * LANES]
        rot = pltpu.roll(blk, HALF, 1)
        low = lax.broadcasted_iota(jnp.int32, blk.shape, 1) < HALF
        cols += [jnp.where(low, blk, rot), jnp.where(low, rot, blk)]
    return jnp.concatenate(cols, axis=1)


def _group_mean_sq(a, bmat):
    sq = a * a
    hi = sq.astype(BF16)
    lo = (sq - hi.astype(F32)).astype(BF16)
    return (jnp.dot(hi, bmat, preferred_element_type=F32) + jnp.dot(lo, bmat, preferred_element_type=F32))


def _gqa_proj_kernel(*refs, rope, n_cache, emit_kv):
    it = iter(refs)
    x_ref, sh_ref, sc_ref, gn_ref, w_ref, gq_ref, gk_ref, bmat_ref = (next(it) for _ in range(8))
    cos_ref = sin_ref = cache_ref = kv_ref = None
    if rope:
        cos_ref, sin_ref = next(it), next(it)
    if n_cache:
        cache_ref = next(it)
    q_ref, k_ref, v_ref = next(it), next(it), next(it)
    if emit_kv:
        kv_ref = next(it)
    nq = GQA_HEADS * GQA_HEAD_DIM
    nk = GQA_KV_HEADS * GQA_HEAD_DIM
    blk = 2 * LANES

    def new_tokens():
        x = x_ref[...]
        h = (_rms(x, gn_ref[...]) * (1.0 + sc_ref[...]) + sh_ref[...]).astype(BF16)
        qkv = jnp.dot(h, w_ref[...], preferred_element_type=F32)
        bmat = bmat_ref[...]

        def head_norm(a, g):
            return a * lax.rsqrt(_group_mean_sq(a, bmat) + EPS) * g

        def maybe_rope(a):
            if not rope:
                return a
            cos, sin = cos_ref[...], sin_ref[...]
            return jnp.concatenate(
                [_rope(a[:, c * LANES:(c + 1) * LANES], cos, sin) for c in range(a.shape[1] // LANES)], axis=1)

        for c in range(nq // blk):
            qb = head_norm(qkv[:, c * blk:(c + 1) * blk], gq_ref[...])
            q_ref[:, c * blk:(c + 1) * blk] = (maybe_rope(qb) * (GQA_SCALE * LOG2E)).astype(BF16)
        kn = head_norm(qkv[:, nq:nq + nk], gk_ref[...])
        vv = qkv[:, nq + nk:]
        if emit_kv:
            t_req = kv_ref.shape[-1]
            for r in range(kv_ref.shape[0]):
                kv_ref[r, 0] = kn[r * t_req:(r + 1) * t_req, :].T
                kv_ref[r, 1] = vv[r * t_req:(r + 1) * t_req, :].T
        k_ref[...] = _dup_halves(maybe_rope(kn)).astype(BF16)
        v_ref[...] = _dup_halves(vv).astype(BF16)

    if n_cache:
        t = pl.program_id(1)
        pl.when(t >= n_cache)(new_tokens)

        @pl.when(t < n_cache)
        def _():
            k_ref[...] = _dup_halves(cache_ref[0].T).astype(BF16)
            v_ref[...] = _dup_halves(cache_ref[1].T).astype(BF16)
    else:
        new_tokens()


def _gqa_proj(x2d, mods, gn, w, *, n_batch, t_new, tm, mod_row, rope_tabs=None, cache=None, t_state=None):
    rope = rope_tabs is not None
    t_cache = 0 if cache is None else cache.shape[-1]
    n_cache = t_cache // tm
    ntn = t_new // tm
    nt = n_cache + ntn
    emit_kv = cache is None
    nq = GQA_HEADS * GQA_HEAD_DIM
    nk = GQA_KV_HEADS * GQA_HEAD_DIM
    nqkv = nq + 2 * nk
    ndup = GQA_KV_HEADS * LANES

    def new_idx(b, t):
        return b * ntn + jnp.maximum(t - n_cache, 0)

    row_fn = lambda b, t: mod_row(b)
    in_specs = [
        pl.BlockSpec((tm, D), lambda b, t: (new_idx(b, t), 0)),
        _mod_spec(1, 0, row_fn), _mod_spec(1, 1, row_fn),
        _full_spec((1, D)), _full_spec((D, nqkv)),
        _full_spec((1, 2 * LANES)), _full_spec((1, 2 * LANES)), _full_spec((2 * LANES, 2 * LANES)),
    ]
    args = [x2d, mods, mods, gn, w["wqkv"], w["gq"], w["gk"], w["bmat"]]
    if rope:
        in_specs += [pl.BlockSpec((tm, LANES), lambda b, t: (jnp.maximum(t - n_cache, 0), 0))] * 2
        args += list(rope_tabs)
    if n_cache:
        in_specs.append(pl.BlockSpec((None, 2, nk, tm), lambda b, t: (b, 0, 0, jnp.minimum(t, n_cache - 1))))
        args.append(cache)
    n_new = n_batch * t_new
    n_keys = n_batch * (t_cache + t_new)
    out_specs = [
        pl.BlockSpec((tm, nq), lambda b, t: (new_idx(b, t), 0)),
        pl.BlockSpec((tm, ndup), lambda b, t: (b * nt + t, 0)),
        pl.BlockSpec((tm, ndup), lambda b, t: (b * nt + t, 0)),
    ]
    out_shape = [
        jax.ShapeDtypeStruct((n_new, nq), BF16),
        jax.ShapeDtypeStruct((n_keys, ndup), BF16),
        jax.ShapeDtypeStruct((n_keys, ndup), BF16),
    ]
    if emit_kv:
        assert tm % t_state == 0
        out_specs.append(pl.BlockSpec((tm // t_state, 2, nk, t_state), lambda b, t: (new_idx(b, t), 0, 0, 0)))
        out_shape.append(jax.ShapeDtypeStruct((n_new // t_state, 2, nk, t_state), F32))
    return pl.pallas_call(
        functools.partial(_gqa_proj_kernel, rope=rope, n_cache=n_cache, emit_kv=emit_kv),
        grid=(n_batch, nt), in_specs=in_specs, out_specs=out_specs, out_shape=out_shape,
        compiler_params=_params(2), name="gqa_proj_s" if rope else "gqa_proj_p",
    )(*args)


def _attn_kernel(q_ref, k_ref, v_ref, o_ref, *, mla, n_pairs, ck):
    tq = q_ref.shape[0]
    nc = k_ref.shape[0] // ck
    lane = lax.broadcasted_iota(jnp.int32, (tq, LANES), 1)
    low = lane < HALF
    items = [(p, c, hh) for c in range(nc) for p in range(n_pairs) for hh in range(2)]
    heads = {}
    state = {}

    def head_operands(p, hh):
        if (p, hh) not in heads:
            if mla:
                hd = 2 * p + hh
                heads[(p, hh)] = (q_ref[:, hd * LANES:(hd + 1) * LANES], hd, p)
            else:
                g = p // 2 if n_pairs > 1 else 0
                qp = q_ref[:, p * LANES:(p + 1) * LANES]
                qh = jnp.where(low if hh == 0 else jnp.logical_not(low), qp, jnp.zeros_like(qp))
                heads[(p, hh)] = (qh, g, g)
        return heads[(p, hh)]

    def scores(item):
        p, c, hh = item
        qh, kc, _ = head_operands(p, hh)
        kh = k_ref[c * ck:(c + 1) * ck, kc * LANES:(kc + 1) * LANES]
        return lax.dot_general(qh, kh, (((1,), (1,)), ((), ())), preferred_element_type=F32)

    pending = [scores(item) for item in items[:ATTN_LOOKAHEAD]]
    for idx, (p, c, hh) in enumerate(items):
        if idx + ATTN_LOOKAHEAD < len(items):
            pending.append(scores(items[idx + ATTN_LOOKAHEAD]))
        s_cur = pending.pop(0)
        vc = head_operands(p, hh)[2]
        vb = v_ref[c * ck:(c + 1) * ck, vc * LANES:(vc + 1) * LANES]
        m_c = jnp.max(s_cur, axis=-1, keepdims=True)
        if c == 0:
            m = m_c
            e = jnp.exp2(s_cur - m)
            l = jnp.sum(e, axis=-1, keepdims=True)
            acc = jnp.dot(e.astype(BF16), vb, preferred_element_type=F32)
        else:
            m_old, l_old, acc_old = state[(p, hh)]
            m = jnp.maximum(m_old, m_c)
            alpha = jnp.exp2(m_old - m)
            e = jnp.exp2(s_cur - m)
            l = alpha * l_old + jnp.sum(e, axis=-1, keepdims=True)
            acc = alpha * acc_old + jnp.dot(e.astype(BF16), vb, preferred_element_type=F32)
        state[(p, hh)] = (m, l, acc)
        if c == nc - 1 and hh == 1:
            o0 = state[(p, 0)][2] / state[(p, 0)][1]
            o1 = state[(p, 1)][2] / state[(p, 1)][1]
            o_ref[:, p * LANES:(p + 1) * LANES] = jnp.where(low, o0, o1).astype(BF16)


def _attention(q, k, v, *, mla, n_batch, t_q, t_k, tq, n_pairs, name):
    total_pairs = 8
    nj = total_pairs // n_pairs
    nqt = t_q // tq
    if mla:
        qw, kw, vw = n_pairs * 2 * LANES, n_pairs * 2 * LANES, n_pairs * LANES
        kv_col = lambda j: j
    else:
        qw = n_pairs * LANES
        kw = vw = max(n_pairs // 2, 1) * LANES
        kv_col = (lambda j: j // 2) if n_pairs == 1 else (lambda j: j)
    return pl.pallas_call(
        functools.partial(_attn_kernel, mla=mla, n_pairs=n_pairs, ck=min(t_k, ATTN_KEY_CHUNK)),
        grid=(n_batch, nj, nqt),
        in_specs=[
            pl.BlockSpec((tq, qw), lambda b, j, i: (b * nqt + i, j)),
            pl.BlockSpec((t_k, kw), lambda b, j, i: (b, kv_col(j))),
            pl.BlockSpec((t_k, vw), lambda b, j, i: (b, kv_col(j))),
        ],
        out_specs=pl.BlockSpec((tq, n_pairs * LANES), lambda b, j, i: (b * nqt + i, j)),
        out_shape=jax.ShapeDtypeStruct((n_batch * t_q, total_pairs * LANES), BF16),
        compiler_params=_params(3), name=name,
    )(q, k, v)


def _route(logits):
    lane = lax.broadcasted_iota(jnp.int32, logits.shape, 1).astype(F32)
    big = jnp.float32(1e9)
    is_grp = (lane >= N_EXPERTS) & (lane < N_EXPERTS + N_GROUPS)
    gl = jnp.where(is_grp, logits, NEG)
    gm = jnp.max(gl, axis=-1, keepdims=True)
    g_w = 1.0 / jnp.sum(jnp.exp(gl - gm), axis=-1, keepdims=True)
    g_idx = jnp.min(jnp.where(gl == gm, lane, big), axis=-1, keepdims=True) - N_EXPERTS
    lo = g_idx * EXPERTS_PER_GROUP
    el = jnp.where((lane >= lo) & (lane < lo + EXPERTS_PER_GROUP), logits, NEG)
    m1 = jnp.max(el, axis=-1, keepdims=True)
    i1 = jnp.min(jnp.where(el == m1, lane, big), axis=-1, keepdims=True)
    el2 = jnp.where(lane == i1, NEG, el)
    m2 = jnp.max(el2, axis=-1, keepdims=True)
    i2 = jnp.min(jnp.where(el2 == m2, lane, big), axis=-1, keepdims=True)
    t = jnp.exp(m2 - m1)
    w1 = g_w / (1.0 + t)
    w2 = g_w * t / (1.0 + t)
    return jnp.where(lane == i1, w1, 0.0) + jnp.where(lane == i2, w2, 0.0), g_idx


MOE_TM = 512
MOE_CHUNK = 144
MOE_SLOTS = 1152
SLOT_RADIX = 32


def _moe_kernel(x_ref, o_ref, wo_ref, gta_ref, shm_ref, scm_ref, gtm_ref, gn_ref,
                wr_hi_ref, wr_lo_ref, br_ref, sel_ref, ltri_ref, wg_ref, wu_ref, wd_ref, fin_ref,
                y_ref, hs_ref, cs_ref, ys_ref, pt_ref, *, final):
    tm = x_ref.shape[0]
    ffg = EXPERTS_PER_GROUP * EXPERT_FF
    mix = jnp.dot(o_ref[...], wo_ref[...], preferred_element_type=F32)
    xm = x_ref[...] + gta_ref[...] * mix
    y_ref[...] = xm
    h = _rms(xm, gn_ref[...]) * (1.0 + scm_ref[...]) + shm_ref[...]
    h_hi = h.astype(BF16)
    h_lo = (h - h_hi.astype(F32)).astype(BF16)
    logits = (jnp.dot(h_hi, wr_hi_ref[...], preferred_element_type=F32)
              + jnp.dot(h_lo, wr_hi_ref[...], preferred_element_type=F32)
              + jnp.dot(h_hi, wr_lo_ref[...], preferred_element_type=F32)) + br_ref[...]
    comb, g_idx = _route(logits)

    lane_i = lax.broadcasted_iota(jnp.int32, (tm, LANES), 1)
    lane = lane_i.astype(F32)
    onehot = jnp.where(lane == g_idx, 1.0, 0.0)
    rank = jnp.dot(ltri_ref[...], onehot.astype(BF16), preferred_element_type=F32)
    cnt = jnp.sum(onehot, axis=0, keepdims=True)
    ends = []
    end = jnp.int32(0)
    start_vec = jnp.zeros_like(lane)
    for g in range(N_GROUPS):
        start_vec = jnp.where(lane_i == g, (end * MOE_CHUNK).astype(F32), start_vec)
        n_g = cnt[0, g].astype(jnp.int32)
        end = end + sum((n_g > k * MOE_CHUNK).astype(jnp.int32) for k in range(-(-tm // MOE_CHUNK)))
        ends.append(end)
    total = ends[-1]
    slot = jnp.sum(onehot * (start_vec + rank), axis=-1, keepdims=True)
    slot_i = slot.astype(jnp.int32)
    pt = jnp.where(lax.broadcasted_iota(jnp.int32, (tm, MOE_SLOTS), 1) == slot_i, 1.0, 0.0).astype(BF16)
    pt_ref[...] = pt
    a = jnp.floor(slot * (1.0 / SLOT_RADIX))
    b = slot - SLOT_RADIX * a
    digits = jnp.where(lane_i == 0, a, jnp.where(lane_i == 1, b, 0.0)).astype(BF16)
    rows = lax.dot_general(sel_ref[...], digits, (((1,), (1,)), ((), ())), preferred_element_type=F32)
    slot_row = (rows[0:1, :] * SLOT_RADIX + rows[1:2, :]).astype(jnp.int32)
    p = jnp.where(lax.broadcasted_iota(jnp.int32, (MOE_SLOTS, tm), 0) == slot_row, 1.0, 0.0).astype(BF16)

    hs_ref[...] = jnp.dot(p, h_hi, preferred_element_type=F32).astype(BF16)
    c_hi = comb.astype(BF16)
    r1 = comb - c_hi.astype(F32)
    c_mid = r1.astype(BF16)
    c_lo = (r1 - c_mid.astype(F32)).astype(BF16)
    packed = (c_hi.astype(F32) + pltpu.roll(c_mid.astype(F32), N_EXPERTS, 1)
              + pltpu.roll(c_lo.astype(F32), 2 * N_EXPERTS, 1)).astype(BF16)
    cs = jnp.dot(p, packed, preferred_element_type=F32)
    cs_ref[...] = cs + pltpu.roll(cs, LANES - N_EXPERTS, 1) + pltpu.roll(cs, LANES - 2 * N_EXPERTS, 1)

    def run_experts(c, n_chunks, g_c):
        rows_c = slice(c * MOE_CHUNK, (c + n_chunks) * MOE_CHUNK)
        lane_c = lax.broadcasted_iota(jnp.int32, (n_chunks * MOE_CHUNK, LANES), 1)
        xc = hs_ref[rows_c, :]
        cc = cs_ref[rows_c, :]
        hids = []
        for j in range(EXPERTS_PER_GROUP):
            e = g_c * EXPERTS_PER_GROUP + j
            gt = jnp.dot(xc, wg_ref[e], preferred_element_type=F32)
            up = jnp.dot(xc, wu_ref[e], preferred_element_type=F32)
            cj = jnp.sum(jnp.where(lane_c == e, cc, 0.0), axis=-1, keepdims=True)
            hids.append(((gt / (1.0 + jnp.exp(-gt))) * up * cj).astype(BF16))
        hid = jnp.concatenate(hids, axis=1)
        wd_g = wd_ref[pl.ds(pl.multiple_of(g_c * ffg, ffg), ffg), :]
        ys_ref[rows_c, :] = jnp.dot(hid, wd_g, preferred_element_type=F32).astype(BF16)

    n_slots = MOE_SLOTS // MOE_CHUNK
    for c in range(n_slots):
        g_c = sum((ends[g] <= c).astype(jnp.int32) for g in range(N_GROUPS - 1))
        start_c = jnp.int32(0)
        end_c = ends[0]
        for g in range(1, N_GROUPS):
            start_c = jnp.where(g_c == g, ends[g - 1], start_c)
            end_c = jnp.where(g_c == g, ends[g], end_c)
        live = c < total
        first_of_pair = ((c - start_c) & 1) == 0
        has_partner = c + 1 < end_c

        if c + 1 < n_slots:
            pl.when(live & first_of_pair & has_partner)(functools.partial(run_experts, c, 2, g_c))
        pl.when(live & first_of_pair & jnp.logical_not(has_partner))(functools.partial(run_experts, c, 1, g_c))

        @pl.when(jnp.logical_not(live))
        def _():
            ys_ref[c * MOE_CHUNK:(c + 1) * MOE_CHUNK, :] = jnp.zeros((MOE_CHUNK, D), BF16)

    y_tok = jnp.dot(pt_ref[...], ys_ref[...], preferred_element_type=F32)
    out = y_ref[...] + gtm_ref[...] * y_tok
    if final:
        out = _rms(out, fin_ref[...])
    y_ref[...] = out


def _moe(x2d, o2d, mods, layer, w, *, mod_row, final):
    n = x2d.shape[0]
    tm = MOE_TM
    row_fn = lambda i: mod_row(i)
    tile = pl.BlockSpec((tm, D), lambda i: (i, 0))
    once = lambda shape: pl.BlockSpec(shape, lambda i: (0,) * len(shape), pipeline_mode=pl.Buffered(1))
    per_layer = lambda shape: pl.BlockSpec((None,) + shape, lambda i: (layer,) + (0,) * len(shape),
                                           pipeline_mode=pl.Buffered(1))
    return pl.pallas_call(
        functools.partial(_moe_kernel, final=final),
        grid=(n // tm,),
        in_specs=[
            tile, tile, once((D, D)),
            _mod_spec(layer, 2, row_fn), _mod_spec(layer, 3, row_fn), _mod_spec(layer, 4, row_fn),
            _mod_spec(layer, 5, row_fn), _full_spec((1, D)),
            _full_spec((D, LANES)), _full_spec((D, LANES)), _full_spec((1, LANES)), _full_spec((8, LANES)),
            once((tm, tm)),
            per_layer((N_EXPERTS, D, EXPERT_FF)), per_layer((N_EXPERTS, D, EXPERT_FF)),
            per_layer((N_EXPERTS * EXPERT_FF, D)),
            _full_spec((1, D)),
        ],
        out_specs=tile,
        out_shape=jax.ShapeDtypeStruct((n, D), F32),
        scratch_shapes=[pltpu.VMEM((MOE_SLOTS, D), BF16), pltpu.VMEM((MOE_SLOTS, LANES), F32),
                        pltpu.VMEM((MOE_SLOTS, D), BF16), pltpu.VMEM((tm, MOE_SLOTS), BF16)],
        compiler_params=_params(1), name=f"moe_l{layer}",
    )(x2d, o2d, w["wo"], mods, mods, mods, mods, w["gn"], w["wr_hi"], w["wr_lo"], w["br"], w["sel2"],
      w["ltri"], w["wg_bf"], w["wu_bf"], w["wd_bf"], w["fin"])


def _mla_weights(w_dq, g_q, w_uq, w_dkv, g_kv, w_uk, w_uv):
    hd = MLA_NOPE + MLA_ROPE
    wuq = w_uq.reshape(Q_LORA, MLA_HEADS, hd)
    wuq = jnp.concatenate([wuq[..., MLA_NOPE:], wuq[..., :MLA_NOPE],
                           jnp.zeros((Q_LORA, MLA_HEADS, LANES - hd), F32)], axis=-1)
    wuk = jnp.concatenate([jnp.zeros((KV_LORA, MLA_HEADS, MLA_ROPE), F32), w_uk,
                           jnp.zeros((KV_LORA, MLA_HEADS, LANES - hd), F32)], axis=-1)
    return {
        "wdq": w_dq.astype(BF16), "gq": g_q.reshape(1, Q_LORA),
        "wuq": wuq.reshape(Q_LORA, MLA_HEADS * LANES).astype(BF16),
        "wdkv": jnp.pad(w_dkv, ((0, 0), (0, KV_PAD - KV_DIM))).astype(BF16), "gkv": g_kv.reshape(1, KV_LORA),
        "wkexp": wuk.reshape(KV_LORA, MLA_HEADS * LANES).astype(BF16),
        "wuv": w_uv.reshape(KV_LORA, MLA_HEADS * MLA_V).astype(BF16),
    }


def _gqa_weights(w_qkv, g_q, g_k):
    grp = jnp.arange(2 * LANES) // GQA_HEAD_DIM
    bmat = jnp.where(grp[:, None] == grp[None, :], 1.0 / GQA_HEAD_DIM, 0.0).astype(BF16)
    reps = 2 * LANES // GQA_HEAD_DIM
    return {"wqkv": w_qkv.astype(BF16), "gq": jnp.tile(g_q, reps).reshape(1, 2 * LANES),
            "gk": jnp.tile(g_k, reps).reshape(1, 2 * LANES), "bmat": bmat}


def _post_weights(l, w_o, norm_ffn, w_group, b_group, w_exp, b_exp, w_gate, w_up, w_down, final_norm):
    wr = jnp.concatenate([w_exp[l], w_group[l], jnp.zeros((D, LANES - N_EXPERTS - N_GROUPS), F32)], axis=1)
    br = jnp.concatenate([b_exp[l], b_group[l], jnp.zeros((LANES - N_EXPERTS - N_GROUPS,), F32)]).reshape(1, LANES)
    wr_hi = wr.astype(BF16)
    wr_lo = (wr - wr_hi.astype(F32)).astype(BF16)
    return {"wo": w_o.astype(BF16), "gn": norm_ffn[l].reshape(1, D), "wr_hi": wr_hi, "wr_lo": wr_lo, "br": br,
            "wg_bf": w_gate, "wu_bf": w_up, "wd_bf": w_down,
            "fin": final_norm.reshape(1, D),
            "sel2": jnp.zeros((8, LANES), BF16).at[0, 0].set(1.0).at[1, 1].set(1.0),
            "ltri": jnp.tri(MOE_TM, MOE_TM, -1, dtype=BF16)}


def kernel(x_prompt, x_sample, cache_mla, cache_gqa, c, c_ctx, ada_w, ada_b, norm_mix, norm_ffn,
           mla_w_dq, mla_q_norm, mla_w_uq, mla_w_dkv, mla_kv_norm, mla_w_uk, mla_w_uv, mla_w_o,
           gqa_w_qkv, gqa_q_norm, gqa_k_norm, gqa_w_o,
           moe_w_group, moe_b_group, moe_w_expert, moe_b_expert, moe_w_gate, moe_w_up, moe_w_down,
           final_norm):
    bp, tp, _ = x_prompt.shape
    bs, ts, _ = x_sample.shape
    t_past = cache_mla.shape[2]
    assert ada_w.shape[0] == 2 and 1 + bs <= N_MOD_ROWS
    n_p, n_s = bp * tp, bs * ts

    cc = jnp.concatenate([c_ctx[None], c, jnp.zeros((N_MOD_ROWS - 1 - bs, D), F32)], axis=0)
    mods = _ada_table(cc, ada_w, ada_b)

    xp = x_prompt.reshape(n_p, D)
    xs = x_sample.reshape(n_s, D)
    tm_proj = PROJ_TM
    tm_post = MOE_TM
    prompt_row = lambda *_: 0
    sample_row_proj = lambda b: 1 + b
    sample_row_post = lambda i: 1 + i // (ts // tm_post)
    depth = moe_w_gate.shape[0]
    moe_args = (norm_ffn, moe_w_group, moe_b_group, moe_w_expert, moe_b_expert,
                moe_w_gate.astype(BF16), moe_w_up.astype(BF16),
                moe_w_down.reshape(depth, N_EXPERTS * EXPERT_FF, D).astype(BF16), final_norm)

    wm = _mla_weights(mla_w_dq[0], mla_q_norm[0], mla_w_uq[0], mla_w_dkv[0], mla_kv_norm[0],
                      mla_w_uk[0], mla_w_uv[0])
    gn0 = norm_mix[0].reshape(1, D)
    q, k, v, keys_p = _mla_proj(xp, mods, gn0, wm, n_batch=1, t_new=n_p, tm=tm_proj, mod_row=prompt_row, t_state=tp)
    o = _attention(q, k, v, mla=True, n_batch=bp, t_q=tp, t_k=tp, tq=tp, n_pairs=8, name="attn_mla_p")
    wp0 = _post_weights(0, mla_w_o[0], *moe_args)
    xp = _moe(xp, o, mods, 0, wp0, mod_row=prompt_row, final=False)

    rope_mla = _rope_tables(ts, MLA_ROPE, 1)
    cache0 = jnp.transpose(cache_mla[:, 0], (0, 2, 1))
    q, k, v = _mla_proj(xs, mods, gn0, wm, n_batch=bs, t_new=ts, tm=tm_proj, mod_row=sample_row_proj,
                        rope_tabs=rope_mla, cache=cache0)
    o = _attention(q, k, v, mla=True, n_batch=bs, t_q=ts, t_k=t_past + ts, tq=ATTN_TQ, n_pairs=1, name="attn_mla_s")
    xs = _moe(xs, o, mods, 0, wp0, mod_row=sample_row_post, final=False)

    wg = _gqa_weights(gqa_w_qkv[0], gqa_q_norm[0], gqa_k_norm[0])
    gn1 = norm_mix[1].reshape(1, D)
    q, k, v, kv_p = _gqa_proj(xp, mods, gn1, wg, n_batch=1, t_new=n_p, tm=tm_proj, mod_row=prompt_row, t_state=tp)
    o = _attention(q, k, v, mla=False, n_batch=bp, t_q=tp, t_k=tp, tq=tp, n_pairs=8, name="attn_gqa_p")
    wp1 = _post_weights(1, gqa_w_o[0], *moe_args)
    y_prompt = _moe(xp, o, mods, 1, wp1, mod_row=prompt_row, final=True)

    rope_gqa = _rope_tables(ts, GQA_HEAD_DIM, LANES // GQA_HEAD_DIM)
    cache1 = jnp.transpose(cache_gqa[:, 0], (0, 2, 3, 4, 1)).reshape(bs, 2, GQA_KV_HEADS * GQA_HEAD_DIM, t_past)
    q, k, v = _gqa_proj(xs, mods, gn1, wg, n_batch=bs, t_new=ts, tm=tm_proj, mod_row=sample_row_proj,
                        rope_tabs=rope_gqa, cache=cache1)
    o = _attention(q, k, v, mla=False, n_batch=bs, t_q=ts, t_k=t_past + ts, tq=ATTN_TQ, n_pairs=1, name="attn_gqa_s")
    y_sample = _moe(xs, o, mods, 1, wp1, mod_row=sample_row_post, final=True)

    return (y_prompt.reshape(bp, tp, D), y_sample.reshape(bs, ts, D),
            jnp.transpose(keys_p, (0, 2, 1))[:, None],
            jnp.transpose(kv_p.reshape(bp, 2, GQA_KV_HEADS, GQA_HEAD_DIM, tp), (0, 4, 1, 2, 3))[:, None])
```

```python
import functools
import math

import jax
import jax.numpy as jnp
from jax import lax
from jax.experimental import pallas as pl
from jax.experimental.pallas import tpu as pltpu

F32 = jnp.float32
BF16 = jnp.bfloat16

D = 1024
EPS = 1e-6
GRID_W = 64
ROPE_THETA = 10000.0
LANES = 128
HALF = LANES // 2
MLA_HEADS = 16
MLA_NOPE = 64
MLA_ROPE = 32
MLA_V = 64
Q_LORA = 384
KV_LORA = 256
KV_DIM = KV_LORA + MLA_ROPE
KV_PAD = 384
MLA_SCALE = 1.0 / math.sqrt(MLA_NOPE + MLA_ROPE)
GQA_HEADS = 16
GQA_KV_HEADS = 4
GQA_HEAD_DIM = 64
GQA_SCALE = 1.0 / math.sqrt(GQA_HEAD_DIM)
N_GROUPS = 4
EXPERTS_PER_GROUP = 4
N_EXPERTS = 16
EXPERT_FF = 256
NEG = -3.0e38
LOG2E = 1.4426950408889634
ATTN_KEY_CHUNK = 512
ATTN_LOOKAHEAD = 1
ATTN_TQ = 512
PROJ_TM = 512

VMEM_LIMIT = 56 * 1024 * 1024
N_MOD_ROWS = 8


def _params(n_axes):
    return pltpu.CompilerParams(dimension_semantics=("arbitrary",) * n_axes,
                                vmem_limit_bytes=VMEM_LIMIT)


def _rms(x, g):
    ms = jnp.mean(x * x, axis=-1, keepdims=True)
    return x * lax.rsqrt(ms + EPS) * g


def _mod_index(layer, which, row):
    return (layer * 6 + which) * N_MOD_ROWS + row


def _mod_spec(layer, which, row_fn):
    return pl.BlockSpec((None, 1, D), lambda *g: (_mod_index(layer, which, row_fn(*g)), 0, 0))


def _full_spec(shape):
    n = len(shape)
    return pl.BlockSpec(shape, lambda *g: (0,) * n)


def _ada_kernel(c_ref, w_ref, b_ref, o_ref):
    c = c_ref[...]
    a = c / (1.0 + jnp.exp(-c))
    o_ref[...] = jnp.dot(a, w_ref[...], precision=lax.Precision.HIGHEST,
                         preferred_element_type=F32) + b_ref[...]


def _ada_table(cc, ada_w, ada_b):
    depth = ada_w.shape[0]
    out = pl.pallas_call(
        _ada_kernel,
        grid=(depth, 6),
        in_specs=[
            pl.BlockSpec((N_MOD_ROWS, D), lambda l, n: (0, 0)),
            pl.BlockSpec((None, D, D), lambda l, n: (l, 0, n)),
            pl.BlockSpec((None, 1, D), lambda l, n: (l, 0, n)),
        ],
        out_specs=pl.BlockSpec((None, None, N_MOD_ROWS, D), lambda l, n: (l, n, 0, 0)),
        out_shape=jax.ShapeDtypeStruct((depth, 6, N_MOD_ROWS, D), F32),
        compiler_params=_params(2),
        name="ada",
    )(cc, ada_w, ada_b.reshape(depth, 1, 6 * D))
    return out.reshape(depth * 6 * N_MOD_ROWS, 1, D)


def _rope(a, cos, sin_signed):
    lane = lax.broadcasted_iota(jnp.int32, a.shape, 1)
    nxt = pltpu.roll(a, LANES - 1, 1)
    prv = pltpu.roll(a, 1, 1)
    return a * cos + jnp.where((lane & 1) == 0, nxt, prv) * sin_signed


def _rope_tables(n_tokens, rot_dim, n_rep):
    t = jnp.arange(n_tokens)
    row = (t // GRID_W).astype(F32)
    col = (t % GRID_W).astype(F32)
    axis_dim = rot_dim // 2
    inv = jnp.power(ROPE_THETA, -jnp.arange(0, axis_dim, 2, dtype=F32) / axis_dim)
    ang = jnp.concatenate([row[:, None] * inv, col[:, None] * inv], axis=-1)
    cos = jnp.repeat(jnp.cos(ang), 2, axis=1)
    sin = jnp.repeat(jnp.sin(ang), 2, axis=1)
    sign = jnp.where(jnp.arange(rot_dim) % 2 == 0, -1.0, 1.0).astype(F32)
    sin = sin * sign
    rest = LANES - n_rep * rot_dim
    cos_t = jnp.concatenate([cos] * n_rep + [jnp.ones((n_tokens, rest), F32)], axis=1)
    sin_t = jnp.concatenate([sin] * n_rep + [jnp.zeros((n_tokens, rest), F32)], axis=1)
    return cos_t, sin_t


def _mla_proj_kernel(*refs, rope, n_cache, emit_keys):
    it = iter(refs)
    x_ref, sh_ref, sc_ref, gn_ref = next(it), next(it), next(it), next(it)
    wdq_ref, gq_ref, wuq_ref, wdkv_ref, gkv_ref, wkexp_ref, wuv_ref = (next(it) for _ in range(7))
    cos_ref = sin_ref = cache_ref = keys_ref = None
    if rope:
        cos_ref, sin_ref = next(it), next(it)
    if n_cache:
        cache_ref = next(it)
    q_ref, k_ref, v_ref = next(it), next(it), next(it)
    if emit_keys:
        keys_ref = next(it)

    def expand(ckv, pe):
        ckv = ckv.astype(BF16)
        kn = jnp.dot(ckv, wkexp_ref[...], preferred_element_type=F32)
        k_ref[...] = jnp.concatenate(
            [kn[:, h_ * LANES:(h_ + 1) * LANES] + pe for h_ in range(MLA_HEADS)], axis=1).astype(BF16)
        v_ref[...] = jnp.dot(ckv, wuv_ref[...], preferred_element_type=F32).astype(BF16)

    def new_tokens():
        x = x_ref[...]
        h = (_rms(x, gn_ref[...]) * (1.0 + sc_ref[...]) + sh_ref[...]).astype(BF16)
        ql = jnp.dot(h, wdq_ref[...], preferred_element_type=F32)
        qn = _rms(ql, gq_ref[...]).astype(BF16)
        q = jnp.dot(qn, wuq_ref[...], preferred_element_type=F32)
        kv = jnp.dot(h, wdkv_ref[...], preferred_element_type=F32)
        ckv = _rms(kv[:, :KV_LORA], gkv_ref[...])
        pe = kv[:, KV_LORA:]
        if rope:
            cos, sin = cos_ref[...], sin_ref[...]
            pe = _rope(pe, cos, sin)
            q = jnp.concatenate(
                [_rope(q[:, h_ * LANES:(h_ + 1) * LANES], cos, sin) for h_ in range(MLA_HEADS)], axis=1)
        q_ref[...] = (q * (MLA_SCALE * LOG2E)).astype(BF16)
        if emit_keys:
            t_req = keys_ref.shape[-1]
            for r in range(keys_ref.shape[0]):
                rows = slice(r * t_req, (r + 1) * t_req)
                keys_ref[r, 0:KV_LORA, :] = ckv[rows, :].T
                keys_ref[r, KV_LORA:KV_DIM, :] = pe[rows, :].T[0:MLA_ROPE, :]
        expand(ckv, pe)

    if n_cache:
        t = pl.program_id(1)
        pl.when(t >= n_cache)(new_tokens)

        @pl.when(t < n_cache)
        def _():
            pe_t = jnp.concatenate([cache_ref[KV_LORA:KV_DIM, :],
                                    jnp.zeros((LANES - MLA_ROPE, cache_ref.shape[-1]), F32)], axis=0)
            expand(cache_ref[0:KV_LORA, :].T, pe_t.T)
    else:
        new_tokens()


def _mla_proj(x2d, mods, gn, w, *, n_batch, t_new, tm, mod_row, rope_tabs=None, cache=None, t_state=None):
    rope = rope_tabs is not None
    t_cache = 0 if cache is None else cache.shape[-1]
    n_cache = t_cache // tm
    ntn = t_new // tm
    nt = n_cache + ntn
    emit_keys = cache is None

    def new_idx(b, t):
        return b * ntn + jnp.maximum(t - n_cache, 0)

    row_fn = lambda b, t: mod_row(b)
    in_specs = [
        pl.BlockSpec((tm, D), lambda b, t: (new_idx(b, t), 0)),
        _mod_spec(0, 0, row_fn), _mod_spec(0, 1, row_fn),
        _full_spec((1, D)),
        _full_spec((D, Q_LORA)), _full_spec((1, Q_LORA)), _full_spec((Q_LORA, MLA_HEADS * LANES)),
        _full_spec((D, KV_PAD)), _full_spec((1, KV_LORA)),
        _full_spec((KV_LORA, MLA_HEADS * LANES)), _full_spec((KV_LORA, MLA_HEADS * MLA_V)),
    ]
    args = [x2d, mods, mods, gn, w["wdq"], w["gq"], w["wuq"], w["wdkv"], w["gkv"], w["wkexp"], w["wuv"]]
    if rope:
        in_specs += [pl.BlockSpec((tm, LANES), lambda b, t: (jnp.maximum(t - n_cache, 0), 0))] * 2
        args += list(rope_tabs)
    if n_cache:
        in_specs.append(pl.BlockSpec((None, KV_DIM, tm), lambda b, t: (b, 0, jnp.minimum(t, n_cache - 1))))
        args.append(cache)
    n_new = n_batch * t_new
    n_keys = n_batch * (t_cache + t_new)
    out_specs = [
        pl.BlockSpec((tm, MLA_HEADS * LANES), lambda b, t: (new_idx(b, t), 0)),
        pl.BlockSpec((tm, MLA_HEADS * LANES), lambda b, t: (b * nt + t, 0)),
        pl.BlockSpec((tm, MLA_HEADS * MLA_V), lambda b, t: (b * nt + t, 0)),
    ]
    out_shape = [
        jax.ShapeDtypeStruct((n_new, MLA_HEADS * LANES), BF16),
        jax.ShapeDtypeStruct((n_keys, MLA_HEADS * LANES), BF16),
        jax.ShapeDtypeStruct((n_keys, MLA_HEADS * MLA_V), BF16),
    ]
    if emit_keys:
        assert tm % t_state == 0
        out_specs.append(pl.BlockSpec((tm // t_state, KV_DIM, t_state), lambda b, t: (new_idx(b, t), 0, 0)))
        out_shape.append(jax.ShapeDtypeStruct((n_new // t_state, KV_DIM, t_state), F32))
    return pl.pallas_call(
        functools.partial(_mla_proj_kernel, rope=rope, n_cache=n_cache, emit_keys=emit_keys),
        grid=(n_batch, nt), in_specs=in_specs, out_specs=out_specs, out_shape=out_shape,
        compiler_params=_params(2), name="mla_proj_s" if rope else "mla_proj_p",
    )(*args)


def _dup_halves(a):
    cols = []
    for c in range(a.shape[1] // LANES):
        blk = a[:, c * LANES:(c + 1) * LANES]
        rot = pltpu.roll(blk, HALF, 1)
        low = lax.broadcasted_iota(jnp.int32, blk.shape, 1) < HALF
        cols += [jnp.where(low, blk, rot), jnp.where(low, rot, blk)]
    return jnp.concatenate(cols, axis=1)


def _group_mean_sq(a, bmat):
    sq = a * a
    hi = sq.astype(BF16)
    lo = (sq - hi.astype(F32)).astype(BF16)
    return (jnp.dot(hi, bmat, preferred_element_type=F32) + jnp.dot(lo, bmat, preferred_element_type=F32))


def _gqa_proj_kernel(*refs, rope, n_cache, emit_kv):
    it = iter(refs)
    x_ref, sh_ref, sc_ref, gn_ref, w_ref, gq_ref, gk_ref, bmat_ref = (next(it) for _ in range(8))
    cos_ref = sin_ref = cache_ref = kv_ref = None
    if rope:
        cos_ref, sin_ref = next(it), next(it)
    if n_cache:
        cache_ref = next(it)
    q_ref, k_ref, v_ref = next(it), next(it), next(it)
    if emit_kv:
        kv_ref = next(it)
    nq = GQA_HEADS * GQA_HEAD_DIM
    nk = GQA_KV_HEADS * GQA_HEAD_DIM
    blk = 2 * LANES

    def new_tokens():
        x = x_ref[...]
        h = (_rms(x, gn_ref[...]) * (1.0 + sc_ref[...]) + sh_ref[...]).astype(BF16)
        qkv = jnp.dot(h, w_ref[...], preferred_element_type=F32)
        bmat = bmat_ref[...]

        def head_norm(a, g):
            return a * lax.rsqrt(_group_mean_sq(a, bmat) + EPS) * g

        def maybe_rope(a):
            if not rope:
                return a
            cos, sin = cos_ref[...], sin_ref[...]
            return jnp.concatenate(
                [_rope(a[:, c * LANES:(c + 1) * LANES], cos, sin) for c in range(a.shape[1] // LANES)], axis=1)

        for c in range(nq // blk):
            qb = head_norm(qkv[:, c * blk:(c + 1) * blk], gq_ref[...])
            q_ref[:, c * blk:(c + 1) * blk] = (maybe_rope(qb) * (GQA_SCALE * LOG2E)).astype(BF16)
        kn = head_norm(qkv[:, nq:nq + nk], gk_ref[...])
        vv = qkv[:, nq + nk:]
        if emit_kv:
            t_req = kv_ref.shape[-1]
            for r in range(kv_ref.shape[0]):
                kv_ref[r, 0] = kn[r * t_req:(r + 1) * t_req, :].T
                kv_ref[r, 1] = vv[r * t_req:(r + 1) * t_req, :].T
        k_ref[...] = _dup_halves(maybe_rope(kn)).astype(BF16)
        v_ref[...] = _dup_halves(vv).astype(BF16)

    if n_cache:
        t = pl.program_id(1)
        pl.when(t >= n_cache)(new_tokens)

        @pl.when(t < n_cache)
        def _():
            k_ref[...] = _dup_halves(cache_ref[0].T).astype(BF16)
            v_ref[...] = _dup_halves(cache_ref[1].T).astype(BF16)
    else:
        new_tokens()


def _gqa_proj(x2d, mods, gn, w, *, n_batch, t_new, tm, mod_row, rope_tabs=None, cache=None, t_state=None):
    rope = rope_tabs is not None
    t_cache = 0 if cache is None else cache.shape[-1]
    n_cache = t_cache // tm
    ntn = t_new // tm
    nt = n_cache + ntn
    emit_kv = cache is None
    nq = GQA_HEADS * GQA_HEAD_DIM
    nk = GQA_KV_HEADS * GQA_HEAD_DIM
    nqkv = nq + 2 * nk
    ndup = GQA_KV_HEADS * LANES

    def new_idx(b, t):
        return b * ntn + jnp.maximum(t - n_cache, 0)

    row_fn = lambda b, t: mod_row(b)
    in_specs = [
        pl.BlockSpec((tm, D), lambda b, t: (new_idx(b, t), 0)),
        _mod_spec(1, 0, row_fn), _mod_spec(1, 1, row_fn),
        _full_spec((1, D)), _full_spec((D, nqkv)),
        _full_spec((1, 2 * LANES)), _full_spec((1, 2 * LANES)), _full_spec((2 * LANES, 2 * LANES)),
    ]
    args = [x2d, mods, mods, gn, w["wqkv"], w["gq"], w["gk"], w["bmat"]]
    if rope:
        in_specs += [pl.BlockSpec((tm, LANES), lambda b, t: (jnp.maximum(t - n_cache, 0), 0))] * 2
        args += list(rope_tabs)
    if n_cache:
        in_specs.append(pl.BlockSpec((None, 2, nk, tm), lambda b, t: (b, 0, 0, jnp.minimum(t, n_cache - 1))))
        args.append(cache)
    n_new = n_batch * t_new
    n_keys = n_batch * (t_cache + t_new)
    out_specs = [
        pl.BlockSpec((tm, nq), lambda b, t: (new_idx(b, t), 0)),
        pl.BlockSpec((tm, ndup), lambda b, t: (b * nt + t, 0)),
        pl.BlockSpec((tm, ndup), lambda b, t: (b * nt + t, 0)),
    ]
    out_shape = [
        jax.ShapeDtypeStruct((n_new, nq), BF16),
        jax.ShapeDtypeStruct((n_keys, ndup), BF16),
        jax.ShapeDtypeStruct((n_keys, ndup), BF16),
    ]
    if emit_kv:
        assert tm % t_state == 0
        out_specs.append(pl.BlockSpec((tm // t_state, 2, nk, t_state), lambda b, t: (new_idx(b, t), 0, 0, 0)))
        out_shape.append(jax.ShapeDtypeStruct((n_new // t_state, 2, nk, t_state), F32))
    return pl.pallas_call(
        functools.partial(_gqa_proj_kernel, rope=rope, n_cache=n_cache, emit_kv=emit_kv),
        grid=(n_batch, nt), in_specs=in_specs, out_specs=out_specs, out_shape=out_shape,
        compiler_params=_params(2), name="gqa_proj_s" if rope else "gqa_proj_p",
    )(*args)


def _attn_kernel(q_ref, k_ref, v_ref, o_ref, *, mla, n_pairs, ck, keys_on_rows):
    tq = q_ref.shape[0]
    nc = k_ref.shape[0] // ck
    lane = lax.broadcasted_iota(jnp.int32, (tq, LANES), 1)
    low = lane < HALF
    items = [(p, c, hh) for c in range(nc) for p in range(n_pairs) for hh in range(2)]
    heads = {}
    state = {}

    def head_operands(p, hh):
        if (p, hh) not in heads:
            if mla:
                hd = 2 * p + hh
                heads[(p, hh)] = (q_ref[:, hd * LANES:(hd + 1) * LANES], hd, p)
            else:
                g = p // 2 if n_pairs > 1 else 0
                qp = q_ref[:, p * LANES:(p + 1) * LANES]
                qh = jnp.where(low if hh == 0 else jnp.logical_not(low), qp, jnp.zeros_like(qp))
                heads[(p, hh)] = (qh, g, g)
        return heads[(p, hh)]

    key_axis = 0 if keys_on_rows else -1
    first_half = (lax.broadcasted_iota(jnp.int32, (LANES, tq), 0) < HALF) if keys_on_rows else low

    def scores(item):
        p, c, hh = item
        qh, kc, _ = head_operands(p, hh)
        kh = k_ref[c * ck:(c + 1) * ck, kc * LANES:(kc + 1) * LANES]
        lhs, rhs = (kh, qh) if keys_on_rows else (qh, kh)
        return lax.dot_general(lhs, rhs, (((1,), (1,)), ((), ())), preferred_element_type=F32)

    def weighted_values(e, vb):
        if keys_on_rows:
            return lax.dot_general(vb, e.astype(BF16), (((0,), (0,)), ((), ())), preferred_element_type=F32)
        return jnp.dot(e.astype(BF16), vb, preferred_element_type=F32)

    pending = [scores(item) for item in items[:ATTN_LOOKAHEAD]]
    for idx, (p, c, hh) in enumerate(items):
        if idx + ATTN_LOOKAHEAD < len(items):
            pending.append(scores(items[idx + ATTN_LOOKAHEAD]))
        s_cur = pending.pop(0)
        vc = head_operands(p, hh)[2]
        vb = v_ref[c * ck:(c + 1) * ck, vc * LANES:(vc + 1) * LANES]
        m_c = jnp.max(s_cur, axis=key_axis, keepdims=True)
        if c == 0:
            m = m_c
            e = jnp.exp2(s_cur - m)
            l = jnp.sum(e, axis=key_axis, keepdims=True)
            acc = weighted_values(e, vb)
        else:
            m_old, l_old, acc_old = state[(p, hh)]
            m = jnp.maximum(m_old, m_c)
            alpha = jnp.exp2(m_old - m)
            e = jnp.exp2(s_cur - m)
            l = alpha * l_old + jnp.sum(e, axis=key_axis, keepdims=True)
            acc = alpha * acc_old + weighted_values(e, vb)
        state[(p, hh)] = (m, l, acc)
        if c == nc - 1 and hh == 1:
            o0 = state[(p, 0)][2] / state[(p, 0)][1]
            o1 = state[(p, 1)][2] / state[(p, 1)][1]
            o = jnp.where(first_half, o0, o1)
            o_ref[:, p * LANES:(p + 1) * LANES] = (o.T if keys_on_rows else o).astype(BF16)


def _attention(q, k, v, *, mla, n_batch, t_q, t_k, tq, n_pairs, keys_on_rows, name):
    total_pairs = 8
    nj = total_pairs // n_pairs
    nqt = t_q // tq
    if mla:
        qw, kw, vw = n_pairs * 2 * LANES, n_pairs * 2 * LANES, n_pairs * LANES
        kv_col = lambda j: j
    else:
        qw = n_pairs * LANES
        kw = vw = max(n_pairs // 2, 1) * LANES
        kv_col = (lambda j: j // 2) if n_pairs == 1 else (lambda j: j)
    return pl.pallas_call(
        functools.partial(_attn_kernel, mla=mla, n_pairs=n_pairs, ck=min(t_k, ATTN_KEY_CHUNK),
                          keys_on_rows=keys_on_rows),
        grid=(n_batch, nj, nqt),
        in_specs=[
            pl.BlockSpec((tq, qw), lambda b, j, i: (b * nqt + i, j)),
            pl.BlockSpec((t_k, kw), lambda b, j, i: (b, kv_col(j))),
            pl.BlockSpec((t_k, vw), lambda b, j, i: (b, kv_col(j))),
        ],
        out_specs=pl.BlockSpec((tq, n_pairs * LANES), lambda b, j, i: (b * nqt + i, j)),
        out_shape=jax.ShapeDtypeStruct((n_batch * t_q, total_pairs * LANES), BF16),
        compiler_params=_params(3), name=name,
    )(q, k, v)


def _route(logits):
    lane = lax.broadcasted_iota(jnp.int32, logits.shape, 1).astype(F32)
    big = jnp.float32(1e9)
    is_grp = (lane >= N_EXPERTS) & (lane < N_EXPERTS + N_GROUPS)
    gl = jnp.where(is_grp, logits, NEG)
    gm = jnp.max(gl, axis=-1, keepdims=True)
    g_w = 1.0 / jnp.sum(jnp.exp(gl - gm), axis=-1, keepdims=True)
    g_idx = jnp.min(jnp.where(gl == gm, lane, big), axis=-1, keepdims=True) - N_EXPERTS
    lo = g_idx * EXPERTS_PER_GROUP
    el = jnp.where((lane >= lo) & (lane < lo + EXPERTS_PER_GROUP), logits, NEG)
    m1 = jnp.max(el, axis=-1, keepdims=True)
    i1 = jnp.min(jnp.where(el == m1, lane, big), axis=-1, keepdims=True)
    el2 = jnp.where(lane == i1, NEG, el)
    m2 = jnp.max(el2, axis=-1, keepdims=True)
    i2 = jnp.min(jnp.where(el2 == m2, lane, big), axis=-1, keepdims=True)
    t = jnp.exp(m2 - m1)
    w1 = g_w / (1.0 + t)
    w2 = g_w * t / (1.0 + t)
    return jnp.where(lane == i1, w1, 0.0) + jnp.where(lane == i2, w2, 0.0), g_idx


MOE_TM = 512
MOE_CHUNK = 144
MOE_SLOTS = 1152
SLOT_RADIX = 32


def _moe_kernel(x_ref, o_ref, wo_ref, gta_ref, shm_ref, scm_ref, gtm_ref, gn_ref,
                wr_hi_ref, wr_lo_ref, br_ref, sel_ref, ltri_ref, wg_ref, wu_ref, wd_ref, fin_ref,
                y_ref, hs_ref, cs_ref, ys_ref, pt_ref, *, final):
    tm = x_ref.shape[0]
    ffg = EXPERTS_PER_GROUP * EXPERT_FF
    mix = jnp.dot(o_ref[...], wo_ref[...], preferred_element_type=F32)
    xm = x_ref[...] + gta_ref[...] * mix
    y_ref[...] = xm
    h = _rms(xm, gn_ref[...]) * (1.0 + scm_ref[...]) + shm_ref[...]
    h_hi = h.astype(BF16)
    h_lo = (h - h_hi.astype(F32)).astype(BF16)
    logits = (jnp.dot(h_hi, wr_hi_ref[...], preferred_element_type=F32)
              + jnp.dot(h_lo, wr_hi_ref[...], preferred_element_type=F32)
              + jnp.dot(h_hi, wr_lo_ref[...], preferred_element_type=F32)) + br_ref[...]
    comb, g_idx = _route(logits)

    lane_i = lax.broadcasted_iota(jnp.int32, (tm, LANES), 1)
    lane = lane_i.astype(F32)
    onehot = jnp.where(lane == g_idx, 1.0, 0.0)
    rank = jnp.dot(ltri_ref[...], onehot.astype(BF16), preferred_element_type=F32)
    cnt = jnp.sum(onehot, axis=0, keepdims=True)
    ends = []
    end = jnp.int32(0)
    start_vec = jnp.zeros_like(lane)
    for g in range(N_GROUPS):
        start_vec = jnp.where(lane_i == g, (end * MOE_CHUNK).astype(F32), start_vec)
        n_g = cnt[0, g].astype(jnp.int32)
        end = end + sum((n_g > k * MOE_CHUNK).astype(jnp.int32) for k in range(-(-tm // MOE_CHUNK)))
        ends.append(end)
    total = ends[-1]
    slot = jnp.sum(onehot * (start_vec + rank), axis=-1, keepdims=True)
    slot_i = slot.astype(jnp.int32)
    pt = jnp.where(lax.broadcasted_iota(jnp.int32, (tm, MOE_SLOTS), 1) == slot_i, 1.0, 0.0).astype(BF16)
    pt_ref[...] = pt
    a = jnp.floor(slot * (1.0 / SLOT_RADIX))
    b = slot - SLOT_RADIX * a
    digits = jnp.where(lane_i == 0, a, jnp.where(lane_i == 1, b, 0.0)).astype(BF16)
    rows = lax.dot_general(sel_ref[...], digits, (((1,), (1,)), ((), ())), preferred_element_type=F32)
    slot_row = (rows[0:1, :] * SLOT_RADIX + rows[1:2, :]).astype(jnp.int32)
    p = jnp.where(lax.broadcasted_iota(jnp.int32, (MOE_SLOTS, tm), 0) == slot_row, 1.0, 0.0).astype(BF16)

    hs_ref[...] = jnp.dot(p, h_hi, preferred_element_type=F32).astype(BF16)
    c_hi = comb.astype(BF16)
    r1 = comb - c_hi.astype(F32)
    c_mid = r1.astype(BF16)
    c_lo = (r1 - c_mid.astype(F32)).astype(BF16)
    packed = (c_hi.astype(F32) + pltpu.roll(c_mid.astype(F32), N_EXPERTS, 1)
              + pltpu.roll(c_lo.astype(F32), 2 * N_EXPERTS, 1)).astype(BF16)
    cs = jnp.dot(p, packed, preferred_element_type=F32)
    cs_ref[...] = cs + pltpu.roll(cs, LANES - N_EXPERTS, 1) + pltpu.roll(cs, LANES - 2 * N_EXPERTS, 1)

    def run_experts(c, n_chunks, g_c):
        rows_c = slice(c * MOE_CHUNK, (c + n_chunks) * MOE_CHUNK)
        lane_c = lax.broadcasted_iota(jnp.int32, (n_chunks * MOE_CHUNK, LANES), 1)
        xc = hs_ref[rows_c, :]
        cc = cs_ref[rows_c, :]
        hids = []
        for j in range(EXPERTS_PER_GROUP):
            e = g_c * EXPERTS_PER_GROUP + j
            gt = jnp.dot(xc, wg_ref[e], preferred_element_type=F32)
            up = jnp.dot(xc, wu_ref[e], preferred_element_type=F32)
            cj = jnp.sum(jnp.where(lane_c == e, cc, 0.0), axis=-1, keepdims=True)
            hids.append(((gt / (1.0 + jnp.exp(-gt))) * up * cj).astype(BF16))
        hid = jnp.concatenate(hids, axis=1)
        wd_g = wd_ref[pl.ds(pl.multiple_of(g_c * ffg, ffg), ffg), :]
        ys_ref[rows_c, :] = jnp.dot(hid, wd_g, preferred_element_type=F32).astype(BF16)

    n_slots = MOE_SLOTS // MOE_CHUNK
    for c in range(n_slots):
        g_c = sum((ends[g] <= c).astype(jnp.int32) for g in range(N_GROUPS - 1))
        start_c = jnp.int32(0)
        end_c = ends[0]
        for g in range(1, N_GROUPS):
            start_c = jnp.where(g_c == g, ends[g - 1], start_c)
            end_c = jnp.where(g_c == g, ends[g], end_c)
        live = c < total
        first_of_pair = ((c - start_c) & 1) == 0
        has_partner = c + 1 < end_c

        if c + 1 < n_slots:
            pl.when(live & first_of_pair & has_partner)(functools.partial(run_experts, c, 2, g_c))
        pl.when(live & first_of_pair & jnp.logical_not(has_partner))(functools.partial(run_experts, c, 1, g_c))

        @pl.when(jnp.logical_not(live))
        def _():
            ys_ref[c * MOE_CHUNK:(c + 1) * MOE_CHUNK, :] = jnp.zeros((MOE_CHUNK, D), BF16)

    y_tok = jnp.dot(pt_ref[...], ys_ref[...], preferred_element_type=F32)
    out = y_ref[...] + gtm_ref[...] * y_tok
    if final:
        out = _rms(out, fin_ref[...])
    y_ref[...] = out


def _moe(x2d, o2d, mods, layer, w, *, mod_row, final):
    n = x2d.shape[0]
    tm = MOE_TM
    row_fn = lambda i: mod_row(i)
    tile = pl.BlockSpec((tm, D), lambda i: (i, 0))
    once = lambda shape: pl.BlockSpec(shape, lambda i: (0,) * len(shape), pipeline_mode=pl.Buffered(1))
    per_layer = lambda shape: pl.BlockSpec((None,) + shape, lambda i: (layer,) + (0,) * len(shape),
                                           pipeline_mode=pl.Buffered(1))
    return pl.pallas_call(
        functools.partial(_moe_kernel, final=final),
        grid=(n // tm,),
        in_specs=[
            tile, tile, once((D, D)),
            _mod_spec(layer, 2, row_fn), _mod_spec(layer, 3, row_fn), _mod_spec(layer, 4, row_fn),
            _mod_spec(layer, 5, row_fn), _full_spec((1, D)),
            _full_spec((D, LANES)), _full_spec((D, LANES)), _full_spec((1, LANES)), _full_spec((8, LANES)),
            once((tm, tm)),
            per_layer((N_EXPERTS, D, EXPERT_FF)), per_layer((N_EXPERTS, D, EXPERT_FF)),
            per_layer((N_EXPERTS * EXPERT_FF, D)),
            _full_spec((1, D)),
        ],
        out_specs=tile,
        out_shape=jax.ShapeDtypeStruct((n, D), F32),
        scratch_shapes=[pltpu.VMEM((MOE_SLOTS, D), BF16), pltpu.VMEM((MOE_SLOTS, LANES), F32),
                        pltpu.VMEM((MOE_SLOTS, D), BF16), pltpu.VMEM((tm, MOE_SLOTS), BF16)],
        compiler_params=_params(1), name=f"moe_l{layer}",
    )(x2d, o2d, w["wo"], mods, mods, mods, mods, w["gn"], w["wr_hi"], w["wr_lo"], w["br"], w["sel2"],
      w["ltri"], w["wg_bf"], w["wu_bf"], w["wd_bf"], w["fin"])


def _mla_weights(w_dq, g_q, w_uq, w_dkv, g_kv, w_uk, w_uv):
    hd = MLA_NOPE + MLA_ROPE
    wuq = w_uq.reshape(Q_LORA, MLA_HEADS, hd)
    wuq = jnp.concatenate([wuq[..., MLA_NOPE:], wuq[..., :MLA_NOPE],
                           jnp.zeros((Q_LORA, MLA_HEADS, LANES - hd), F32)], axis=-1)
    wuk = jnp.concatenate([jnp.zeros((KV_LORA, MLA_HEADS, MLA_ROPE), F32), w_uk,
                           jnp.zeros((KV_LORA, MLA_HEADS, LANES - hd), F32)], axis=-1)
    return {
        "wdq": w_dq.astype(BF16), "gq": g_q.reshape(1, Q_LORA),
        "wuq": wuq.reshape(Q_LORA, MLA_HEADS * LANES).astype(BF16),
        "wdkv": jnp.pad(w_dkv, ((0, 0), (0, KV_PAD - KV_DIM))).astype(BF16), "gkv": g_kv.reshape(1, KV_LORA),
        "wkexp": wuk.reshape(KV_LORA, MLA_HEADS * LANES).astype(BF16),
        "wuv": w_uv.reshape(KV_LORA, MLA_HEADS * MLA_V).astype(BF16),
    }


def _gqa_weights(w_qkv, g_q, g_k):
    grp = jnp.arange(2 * LANES) // GQA_HEAD_DIM
    bmat = jnp.where(grp[:, None] == grp[None, :], 1.0 / GQA_HEAD_DIM, 0.0).astype(BF16)
    reps = 2 * LANES // GQA_HEAD_DIM
    return {"wqkv": w_qkv.astype(BF16), "gq": jnp.tile(g_q, reps).reshape(1, 2 * LANES),
            "gk": jnp.tile(g_k, reps).reshape(1, 2 * LANES), "bmat": bmat}


def _post_weights(l, w_o, norm_ffn, w_group, b_group, w_exp, b_exp, w_gate, w_up, w_down, final_norm):
    wr = jnp.concatenate([w_exp[l], w_group[l], jnp.zeros((D, LANES - N_EXPERTS - N_GROUPS), F32)], axis=1)
    br = jnp.concatenate([b_exp[l], b_group[l], jnp.zeros((LANES - N_EXPERTS - N_GROUPS,), F32)]).reshape(1, LANES)
    wr_hi = wr.astype(BF16)
    wr_lo = (wr - wr_hi.astype(F32)).astype(BF16)
    return {"wo": w_o.astype(BF16), "gn": norm_ffn[l].reshape(1, D), "wr_hi": wr_hi, "wr_lo": wr_lo, "br": br,
            "wg_bf": w_gate, "wu_bf": w_up, "wd_bf": w_down,
            "fin": final_norm.reshape(1, D),
            "sel2": jnp.zeros((8, LANES), BF16).at[0, 0].set(1.0).at[1, 1].set(1.0),
            "ltri": jnp.tri(MOE_TM, MOE_TM, -1, dtype=BF16)}


def kernel(x_prompt, x_sample, cache_mla, cache_gqa, c, c_ctx, ada_w, ada_b, norm_mix, norm_ffn,
           mla_w_dq, mla_q_norm, mla_w_uq, mla_w_dkv, mla_kv_norm, mla_w_uk, mla_w_uv, mla_w_o,
           gqa_w_qkv, gqa_q_norm, gqa_k_norm, gqa_w_o,
           moe_w_group, moe_b_group, moe_w_expert, moe_b_expert, moe_w_gate, moe_w_up, moe_w_down,
           final_norm):
    bp, tp, _ = x_prompt.shape
    bs, ts, _ = x_sample.shape
    t_past = cache_mla.shape[2]
    assert ada_w.shape[0] == 2 and 1 + bs <= N_MOD_ROWS
    n_p, n_s = bp * tp, bs * ts

    cc = jnp.concatenate([c_ctx[None], c, jnp.zeros((N_MOD_ROWS - 1 - bs, D), F32)], axis=0)
    mods = _ada_table(cc, ada_w, ada_b)

    xp = x_prompt.reshape(n_p, D)
    xs = x_sample.reshape(n_s, D)
    tm_proj = PROJ_TM
    tm_post = MOE_TM
    prompt_row = lambda *_: 0
    sample_row_proj = lambda b: 1 + b
    sample_row_post = lambda i: 1 + i // (ts // tm_post)
    depth = moe_w_gate.shape[0]
    moe_args = (norm_ffn, moe_w_group, moe_b_group, moe_w_expert, moe_b_expert,
                moe_w_gate.astype(BF16), moe_w_up.astype(BF16),
                moe_w_down.reshape(depth, N_EXPERTS * EXPERT_FF, D).astype(BF16), final_norm)

    wm = _mla_weights(mla_w_dq[0], mla_q_norm[0], mla_w_uq[0], mla_w_dkv[0], mla_kv_norm[0],
                      mla_w_uk[0], mla_w_uv[0])
    gn0 = norm_mix[0].reshape(1, D)
    q, k, v, keys_p = _mla_proj(xp, mods, gn0, wm, n_batch=1, t_new=n_p, tm=tm_proj, mod_row=prompt_row, t_state=tp)
    o = _attention(q, k, v, mla=True, n_batch=bp, t_q=tp, t_k=tp, tq=tp, n_pairs=8, keys_on_rows=False, name="attn_mla_p")
    wp0 = _post_weights(0, mla_w_o[0], *moe_args)
    xp = _moe(xp, o, mods, 0, wp0, mod_row=prompt_row, final=False)

    rope_mla = _rope_tables(ts, MLA_ROPE, 1)
    cache0 = jnp.transpose(cache_mla[:, 0], (0, 2, 1))
    q, k, v = _mla_proj(xs, mods, gn0, wm, n_batch=bs, t_new=ts, tm=tm_proj, mod_row=sample_row_proj,
                        rope_tabs=rope_mla, cache=cache0)
    o = _attention(q, k, v, mla=True, n_batch=bs, t_q=ts, t_k=t_past + ts, tq=ATTN_TQ, n_pairs=4, keys_on_rows=True, name="attn_mla_s")
    xs = _moe(xs, o, mods, 0, wp0, mod_row=sample_row_post, final=False)

    wg = _gqa_weights(gqa_w_qkv[0], gqa_q_norm[0], gqa_k_norm[0])
    gn1 = norm_mix[1].reshape(1, D)
    q, k, v, kv_p = _gqa_proj(xp, mods, gn1, wg, n_batch=1, t_new=n_p, tm=tm_proj, mod_row=prompt_row, t_state=tp)
    o = _attention(q, k, v, mla=False, n_batch=bp, t_q=tp, t_k=tp, tq=tp, n_pairs=8, keys_on_rows=False, name="attn_gqa_p")
    wp1 = _post_weights(1, gqa_w_o[0], *moe_args)
    y_prompt = _moe(xp, o, mods, 1, wp1, mod_row=prompt_row, final=True)

    rope_gqa = _rope_tables(ts, GQA_HEAD_DIM, LANES // GQA_HEAD_DIM)
    cache1 = jnp.transpose(cache_gqa[:, 0], (0, 2, 3, 4, 1)).reshape(bs, 2, GQA_KV_HEADS * GQA_HEAD_DIM, t_past)
    q, k, v = _gqa_proj(xs, mods, gn1, wg, n_batch=bs, t_new=ts, tm=tm_proj, mod_row=sample_row_proj,
                        rope_tabs=rope_gqa, cache=cache1)
    o = _attention(q, k, v, mla=False, n_batch=bs, t_q=ts, t_k=t_past + ts, tq=ATTN_TQ, n_pairs=4, keys_on_rows=True, name="attn_gqa_s")
    y_sample = _moe(xs, o, mods, 1, wp1, mod_row=sample_row_post, final=True)

    return (y_prompt.reshape(bp, tp, D), y_sample.reshape(bs, ts, D),
            jnp.transpose(keys_p, (0, 2, 1))[:, None],
            jnp.transpose(kv_p.reshape(bp, 2, GQA_KV_HEADS, GQA_HEAD_DIM, tp), (0, 4, 1, 2, 3))[:, None])
```

```python
import functools
import math

import jax
import jax.numpy as jnp
from jax import lax
from jax.experimental import pallas as pl
from jax.experimental.pallas import tpu as pltpu

F32 = jnp.float32
BF16 = jnp.bfloat16

D = 1024
EPS = 1e-6
GRID_W = 64
ROPE_THETA = 10000.0
LANES = 128
HALF = LANES // 2
MLA_HEADS = 16
MLA_NOPE = 64
MLA_ROPE = 32
MLA_V = 64
Q_LORA = 384
KV_LORA = 256
KV_DIM = KV_LORA + MLA_ROPE
KV_PAD = 384
MLA_SCALE = 1.0 / math.sqrt(MLA_NOPE + MLA_ROPE)
GQA_HEADS = 16
GQA_KV_HEADS = 4
GQA_HEAD_DIM = 64
GQA_SCALE = 1.0 / math.sqrt(GQA_HEAD_DIM)
N_GROUPS = 4
EXPERTS_PER_GROUP = 4
N_EXPERTS = 16
EXPERT_FF = 256
NEG = -3.0e38
LOG2E = 1.4426950408889634
ATTN_KEY_CHUNK = 512
ATTN_LOOKAHEAD = 1
ATTN_TQ = 512
PROJ_TM = 512

VMEM_LIMIT = 56 * 1024 * 1024
N_MOD_ROWS = 8


def _params(n_axes):
    return pltpu.CompilerParams(dimension_semantics=("arbitrary",) * n_axes,
                                vmem_limit_bytes=VMEM_LIMIT)


def _rms(x, g):
    ms = jnp.mean(x * x, axis=-1, keepdims=True)
    return x * lax.rsqrt(ms + EPS) * g


def _mod_index(layer, which, row):
    return (layer * 6 + which) * N_MOD_ROWS + row


def _mod_spec(layer, which, row_fn):
    return pl.BlockSpec((None, 1, D), lambda *g: (_mod_index(layer, which, row_fn(*g)), 0, 0))


def _full_spec(shape):
    n = len(shape)
    return pl.BlockSpec(shape, lambda *g: (0,) * n)


def _ada_kernel(c_ref, w_ref, b_ref, o_ref):
    c = c_ref[...]
    a = c / (1.0 + jnp.exp(-c))
    o_ref[...] = jnp.dot(a, w_ref[...], precision=lax.Precision.HIGHEST,
                         preferred_element_type=F32) + b_ref[...]


def _ada_table(cc, ada_w, ada_b):
    depth = ada_w.shape[0]
    out = pl.pallas_call(
        _ada_kernel,
        grid=(depth, 6),
        in_specs=[
            pl.BlockSpec((N_MOD_ROWS, D), lambda l, n: (0, 0)),
            pl.BlockSpec((None, D, D), lambda l, n: (l, 0, n)),
            pl.BlockSpec((None, 1, D), lambda l, n: (l, 0, n)),
        ],
        out_specs=pl.BlockSpec((None, None, N_MOD_ROWS, D), lambda l, n: (l, n, 0, 0)),
        out_shape=jax.ShapeDtypeStruct((depth, 6, N_MOD_ROWS, D), F32),
        compiler_params=_params(2),
        name="ada",
    )(cc, ada_w, ada_b.reshape(depth, 1, 6 * D))
    return out.reshape(depth * 6 * N_MOD_ROWS, 1, D)


def _rope(a, cos, sin_signed):
    lane = lax.broadcasted_iota(jnp.int32, a.shape, 1)
    nxt = pltpu.roll(a, LANES - 1, 1)
    prv = pltpu.roll(a, 1, 1)
    return a * cos + jnp.where((lane & 1) == 0, nxt, prv) * sin_signed


def _rope_tables(n_tokens, rot_dim, n_rep):
    t = jnp.arange(n_tokens)
    row = (t // GRID_W).astype(F32)
    col = (t % GRID_W).astype(F32)
    axis_dim = rot_dim // 2
    inv = jnp.power(ROPE_THETA, -jnp.arange(0, axis_dim, 2, dtype=F32) / axis_dim)
    ang = jnp.concatenate([row[:, None] * inv, col[:, None] * inv], axis=-1)
    cos = jnp.repeat(jnp.cos(ang), 2, axis=1)
    sin = jnp.repeat(jnp.sin(ang), 2, axis=1)
    sign = jnp.where(jnp.arange(rot_dim) % 2 == 0, -1.0, 1.0).astype(F32)
    sin = sin * sign
    rest = LANES - n_rep * rot_dim
    cos_t = jnp.concatenate([cos] * n_rep + [jnp.ones((n_tokens, rest), F32)], axis=1)
    sin_t = jnp.concatenate([sin] * n_rep + [jnp.zeros((n_tokens, rest), F32)], axis=1)
    return cos_t, sin_t


def _mla_proj_kernel(*refs, rope, n_cache, emit_keys):
    it = iter(refs)
    x_ref, sh_ref, sc_ref, gn_ref = next(it), next(it), next(it), next(it)
    wdq_ref, gq_ref, wuq_ref, wdkv_ref, gkv_ref, wkexp_ref, wuv_ref = (next(it) for _ in range(7))
    cos_ref = sin_ref = cache_ref = keys_ref = None
    if rope:
        cos_ref, sin_ref = next(it), next(it)
    if n_cache:
        cache_ref = next(it)
    q_ref, k_ref, v_ref = next(it), next(it), next(it)
    if emit_keys:
        keys_ref = next(it)

    def expand(ckv, pe):
        ckv = ckv.astype(BF16)
        kn = jnp.dot(ckv, wkexp_ref[...], preferred_element_type=F32)
        k_ref[...] = jnp.concatenate(
            [kn[:, h_ * LANES:(h_ + 1) * LANES] + pe for h_ in range(MLA_HEADS)], axis=1).astype(BF16)
        v_ref[...] = jnp.dot(ckv, wuv_ref[...], preferred_element_type=F32).astype(BF16)

    def new_tokens():
        x = x_ref[...]
        h = (_rms(x, gn_ref[...]) * (1.0 + sc_ref[...]) + sh_ref[...]).astype(BF16)
        ql = jnp.dot(h, wdq_ref[...], preferred_element_type=F32)
        qn = _rms(ql, gq_ref[...]).astype(BF16)
        q = jnp.dot(qn, wuq_ref[...], preferred_element_type=F32)
        kv = jnp.dot(h, wdkv_ref[...], preferred_element_type=F32)
        ckv = _rms(kv[:, :KV_LORA], gkv_ref[...])
        pe = kv[:, KV_LORA:]
        if rope:
            cos, sin = cos_ref[...], sin_ref[...]
            pe = _rope(pe, cos, sin)
            q = jnp.concatenate(
                [_rope(q[:, h_ * LANES:(h_ + 1) * LANES], cos, sin) for h_ in range(MLA_HEADS)], axis=1)
        q_ref[...] = (q * (MLA_SCALE * LOG2E)).astype(BF16)
        if emit_keys:
            t_req = keys_ref.shape[-1]
            for r in range(keys_ref.shape[0]):
                rows = slice(r * t_req, (r + 1) * t_req)
                keys_ref[r, 0:KV_LORA, :] = ckv[rows, :].T
                keys_ref[r, KV_LORA:KV_DIM, :] = pe[rows, :].T[0:MLA_ROPE, :]
        expand(ckv, pe)

    if n_cache:
        t = pl.program_id(1)
        pl.when(t >= n_cache)(new_tokens)

        @pl.when(t < n_cache)
        def _():
            pe_t = jnp.concatenate([cache_ref[KV_LORA:KV_DIM, :],
                                    jnp.zeros((LANES - MLA_ROPE, cache_ref.shape[-1]), F32)], axis=0)
            expand(cache_ref[0:KV_LORA, :].T, pe_t.T)
    else:
        new_tokens()


def _mla_proj(x2d, mods, gn, w, *, n_batch, t_new, tm, mod_row, rope_tabs=None, cache=None, t_state=None):
    rope = rope_tabs is not None
    t_cache = 0 if cache is None else cache.shape[-1]
    n_cache = t_cache // tm
    ntn = t_new // tm
    nt = n_cache + ntn
    emit_keys = cache is None

    def new_idx(b, t):
        return b * ntn + jnp.maximum(t - n_cache, 0)

    row_fn = lambda b, t: mod_row(b)
    in_specs = [
        pl.BlockSpec((tm, D), lambda b, t: (new_idx(b, t), 0)),
        _mod_spec(0, 0, row_fn), _mod_spec(0, 1, row_fn),
        _full_spec((1, D)),
        _full_spec((D, Q_LORA)), _full_spec((1, Q_LORA)), _full_spec((Q_LORA, MLA_HEADS * LANES)),
        _full_spec((D, KV_PAD)), _full_spec((1, KV_LORA)),
        _full_spec((KV_LORA, MLA_HEADS * LANES)), _full_spec((KV_LORA, MLA_HEADS * MLA_V)),
    ]
    args = [x2d, mods, mods, gn, w["wdq"], w["gq"], w["wuq"], w["wdkv"], w["gkv"], w["wkexp"], w["wuv"]]
    if rope:
        in_specs += [pl.BlockSpec((tm, LANES), lambda b, t: (jnp.maximum(t - n_cache, 0), 0))] * 2
        args += list(rope_tabs)
    if n_cache:
        in_specs.append(pl.BlockSpec((None, KV_DIM, tm), lambda b, t: (b, 0, jnp.minimum(t, n_cache - 1))))
        args.append(cache)
    n_new = n_batch * t_new
    n_keys = n_batch * (t_cache + t_new)
    out_specs = [
        pl.BlockSpec((tm, MLA_HEADS * LANES), lambda b, t: (new_idx(b, t), 0)),
        pl.BlockSpec((tm, MLA_HEADS * LANES), lambda b, t: (b * nt + t, 0)),
        pl.BlockSpec((tm, MLA_HEADS * MLA_V), lambda b, t: (b * nt + t, 0)),
    ]
    out_shape = [
        jax.ShapeDtypeStruct((n_new, MLA_HEADS * LANES), BF16),
        jax.ShapeDtypeStruct((n_keys, MLA_HEADS * LANES), BF16),
        jax.ShapeDtypeStruct((n_keys, MLA_HEADS * MLA_V), BF16),
    ]
    if emit_keys:
        assert tm % t_state == 0
        out_specs.append(pl.BlockSpec((tm // t_state, KV_DIM, t_state), lambda b, t: (new_idx(b, t), 0, 0)))
        out_shape.append(jax.ShapeDtypeStruct((n_new // t_state, KV_DIM, t_state), F32))
    return pl.pallas_call(
        functools.partial(_mla_proj_kernel, rope=rope, n_cache=n_cache, emit_keys=emit_keys),
        grid=(n_batch, nt), in_specs=in_specs, out_specs=out_specs, out_shape=out_shape,
        compiler_params=_params(2), name="mla_proj_s" if rope else "mla_proj_p",
    )(*args)


def _dup_halves(a):
    cols = []
    for c in range(a.shape[1] // LANES):
        blk = a[:, c * LANES:(c + 1) * LANES]
        rot = pltpu.roll(blk, HALF, 1)
        low = lax.broadcasted_iota(jnp.int32, blk.shape, 1) < HALF
        cols += [jnp.where(low, blk, rot), jnp.where(low, rot, blk)]
    return jnp.concatenate(cols, axis=1)


def _group_mean_sq(a, bmat):
    sq = a * a
    hi = sq.astype(BF16)
    lo = (sq - hi.astype(F32)).astype(BF16)
    return (jnp.dot(hi, bmat, preferred_element_type=F32) + jnp.dot(lo, bmat, preferred_element_type=F32))


def _gqa_proj_kernel(*refs, rope, n_cache, emit_kv):
    it = iter(refs)
    x_ref, sh_ref, sc_ref, gn_ref, w_ref, gq_ref, gk_ref, bmat_ref = (next(it) for _ in range(8))
    cos_ref = sin_ref = cache_ref = kv_ref = None
    if rope:
        cos_ref, sin_ref = next(it), next(it)
    if n_cache:
        cache_ref = next(it)
    q_ref, k_ref, v_ref = next(it), next(it), next(it)
    if emit_kv:
        kv_ref = next(it)
    nq = GQA_HEADS * GQA_HEAD_DIM
    nk = GQA_KV_HEADS * GQA_HEAD_DIM
    blk = 2 * LANES

    def new_tokens():
        x = x_ref[...]
        h = (_rms(x, gn_ref[...]) * (1.0 + sc_ref[...]) + sh_ref[...]).astype(BF16)
        qkv = jnp.dot(h, w_ref[...], preferred_element_type=F32)
        bmat = bmat_ref[...]

        def head_norm(a, g):
            return a * lax.rsqrt(_group_mean_sq(a, bmat) + EPS) * g

        def maybe_rope(a):
            if not rope:
                return a
            cos, sin = cos_ref[...], sin_ref[...]
            return jnp.concatenate(
                [_rope(a[:, c * LANES:(c + 1) * LANES], cos, sin) for c in range(a.shape[1] // LANES)], axis=1)

        for c in range(nq // blk):
            qb = head_norm(qkv[:, c * blk:(c + 1) * blk], gq_ref[...])
            q_ref[:, c * blk:(c + 1) * blk] = (maybe_rope(qb) * (GQA_SCALE * LOG2E)).astype(BF16)
        kn = head_norm(qkv[:, nq:nq + nk], gk_ref[...])
        vv = qkv[:, nq + nk:]
        if emit_kv:
            t_req = kv_ref.shape[-1]
            for r in range(kv_ref.shape[0]):
                kv_ref[r, 0] = kn[r * t_req:(r + 1) * t_req, :].T
                kv_ref[r, 1] = vv[r * t_req:(r + 1) * t_req, :].T
        k_ref[...] = _dup_halves(maybe_rope(kn)).astype(BF16)
        v_ref[...] = _dup_halves(vv).astype(BF16)

    if n_cache:
        t = pl.program_id(1)
        pl.when(t >= n_cache)(new_tokens)

        @pl.when(t < n_cache)
        def _():
            k_ref[...] = _dup_halves(cache_ref[0].T).astype(BF16)
            v_ref[...] = _dup_halves(cache_ref[1].T).astype(BF16)
    else:
        new_tokens()


def _gqa_proj(x2d, mods, gn, w, *, n_batch, t_new, tm, mod_row, rope_tabs=None, cache=None, t_state=None):
    rope = rope_tabs is not None
    t_cache = 0 if cache is None else cache.shape[-1]
    n_cache = t_cache // tm
    ntn = t_new // tm
    nt = n_cache + ntn
    emit_kv = cache is None
    nq = GQA_HEADS * GQA_HEAD_DIM
    nk = GQA_KV_HEADS * GQA_HEAD_DIM
    nqkv = nq + 2 * nk
    ndup = GQA_KV_HEADS * LANES

    def new_idx(b, t):
        return b * ntn + jnp.maximum(t - n_cache, 0)

    row_fn = lambda b, t: mod_row(b)
    in_specs = [
        pl.BlockSpec((tm, D), lambda b, t: (new_idx(b, t), 0)),
        _mod_spec(1, 0, row_fn), _mod_spec(1, 1, row_fn),
        _full_spec((1, D)), _full_spec((D, nqkv)),
        _full_spec((1, 2 * LANES)), _full_spec((1, 2 * LANES)), _full_spec((2 * LANES, 2 * LANES)),
    ]
    args = [x2d, mods, mods, gn, w["wqkv"], w["gq"], w["gk"], w["bmat"]]
    if rope:
        in_specs += [pl.BlockSpec((tm, LANES), lambda b, t: (jnp.maximum(t - n_cache, 0), 0))] * 2
        args += list(rope_tabs)
    if n_cache:
        in_specs.append(pl.BlockSpec((None, 2, nk, tm), lambda b, t: (b, 0, 0, jnp.minimum(t, n_cache - 1))))
        args.append(cache)
    n_new = n_batch * t_new
    n_keys = n_batch * (t_cache + t_new)
    out_specs = [
        pl.BlockSpec((tm, nq), lambda b, t: (new_idx(b, t), 0)),
        pl.BlockSpec((tm, ndup), lambda b, t: (b * nt + t, 0)),
        pl.BlockSpec((tm, ndup), lambda b, t: (b * nt + t, 0)),
    ]
    out_shape = [
        jax.ShapeDtypeStruct((n_new, nq), BF16),
        jax.ShapeDtypeStruct((n_keys, ndup), BF16),
        jax.ShapeDtypeStruct((n_keys, ndup), BF16),
    ]
    if emit_kv:
        assert tm % t_state == 0
        out_specs.append(pl.BlockSpec((tm // t_state, 2, nk, t_state), lambda b, t: (new_idx(b, t), 0, 0, 0)))
        out_shape.append(jax.ShapeDtypeStruct((n_new // t_state, 2, nk, t_state), F32))
    return pl.pallas_call(
        functools.partial(_gqa_proj_kernel, rope=rope, n_cache=n_cache, emit_kv=emit_kv),
        grid=(n_batch, nt), in_specs=in_specs, out_specs=out_specs, out_shape=out_shape,
        compiler_params=_params(2), name="gqa_proj_s" if rope else "gqa_proj_p",
    )(*args)


def _attn_kernel(q_ref, k_ref, v_ref, o_ref, *, mla, n_pairs, ck, keys_on_rows):
    tq = q_ref.shape[0]
    nc = k_ref.shape[0] // ck
    lane = lax.broadcasted_iota(jnp.int32, (tq, LANES), 1)
    low = lane < HALF
    items = [(p, c, hh) for c in range(nc) for p in range(n_pairs) for hh in range(2)]
    heads = {}
    state = {}

    def head_operands(p, hh):
        if (p, hh) not in heads:
            if mla:
                hd = 2 * p + hh
                heads[(p, hh)] = (q_ref[:, hd * LANES:(hd + 1) * LANES], hd, p)
            else:
                g = p // 2 if n_pairs > 1 else 0
                qp = q_ref[:, p * LANES:(p + 1) * LANES]
                qh = jnp.where(low if hh == 0 else jnp.logical_not(low), qp, jnp.zeros_like(qp))
                heads[(p, hh)] = (qh, g, g)
        return heads[(p, hh)]

    key_axis = 0 if keys_on_rows else -1
    first_half = (lax.broadcasted_iota(jnp.int32, (LANES, tq), 0) < HALF) if keys_on_rows else low

    def scores(item):
        p, c, hh = item
        qh, kc, _ = head_operands(p, hh)
        kh = k_ref[c * ck:(c + 1) * ck, kc * LANES:(kc + 1) * LANES]
        lhs, rhs = (kh, qh) if keys_on_rows else (qh, kh)
        return lax.dot_general(lhs, rhs, (((1,), (1,)), ((), ())), preferred_element_type=F32)

    def weighted_values(e, vb):
        if keys_on_rows:
            return lax.dot_general(vb, e.astype(BF16), (((0,), (0,)), ((), ())), preferred_element_type=F32)
        return jnp.dot(e.astype(BF16), vb, preferred_element_type=F32)

    pending = [scores(item) for item in items[:ATTN_LOOKAHEAD]]
    for idx, (p, c, hh) in enumerate(items):
        if idx + ATTN_LOOKAHEAD < len(items):
            pending.append(scores(items[idx + ATTN_LOOKAHEAD]))
        s_cur = pending.pop(0)
        vc = head_operands(p, hh)[2]
        vb = v_ref[c * ck:(c + 1) * ck, vc * LANES:(vc + 1) * LANES]
        m_c = jnp.max(s_cur, axis=key_axis, keepdims=True)
        if c == 0:
            m = m_c
            e = jnp.exp2(s_cur - m)
            l = jnp.sum(e, axis=key_axis, keepdims=True)
            acc = weighted_values(e, vb)
        else:
            m_old, l_old, acc_old = state[(p, hh)]
            m = jnp.maximum(m_old, m_c)
            alpha = jnp.exp2(m_old - m)
            e = jnp.exp2(s_cur - m)
            l = alpha * l_old + jnp.sum(e, axis=key_axis, keepdims=True)
            acc = alpha * acc_old + weighted_values(e, vb)
        state[(p, hh)] = (m, l, acc)
        if c == nc - 1 and hh == 1:
            o0 = state[(p, 0)][2] / state[(p, 0)][1]
            o1 = state[(p, 1)][2] / state[(p, 1)][1]
            o = jnp.where(first_half, o0, o1)
            o_ref[:, p * LANES:(p + 1) * LANES] = (o.T if keys_on_rows else o).astype(BF16)


def _attention(q, k, v, *, mla, n_batch, t_q, t_k, tq, n_pairs, keys_on_rows, name):
    total_pairs = 8
    nj = total_pairs // n_pairs
    nqt = t_q // tq
    if mla:
        qw, kw, vw = n_pairs * 2 * LANES, n_pairs * 2 * LANES, n_pairs * LANES
        kv_col = lambda j: j
    else:
        qw = n_pairs * LANES
        kw = vw = max(n_pairs // 2, 1) * LANES
        kv_col = (lambda j: j // 2) if n_pairs == 1 else (lambda j: j)
    return pl.pallas_call(
        functools.partial(_attn_kernel, mla=mla, n_pairs=n_pairs, ck=min(t_k, ATTN_KEY_CHUNK),
                          keys_on_rows=keys_on_rows),
        grid=(n_batch, nj, nqt),
        in_specs=[
            pl.BlockSpec((tq, qw), lambda b, j, i: (b * nqt + i, j)),
            pl.BlockSpec((t_k, kw), lambda b, j, i: (b, kv_col(j))),
            pl.BlockSpec((t_k, vw), lambda b, j, i: (b, kv_col(j))),
        ],
        out_specs=pl.BlockSpec((tq, n_pairs * LANES), lambda b, j, i: (b * nqt + i, j)),
        out_shape=jax.ShapeDtypeStruct((n_batch * t_q, total_pairs * LANES), BF16),
        compiler_params=_params(3), name=name,
    )(q, k, v)


def _route(logits):
    lane = lax.broadcasted_iota(jnp.int32, logits.shape, 1).astype(F32)
    big = jnp.float32(1e9)
    is_grp = (lane >= N_EXPERTS) & (lane < N_EXPERTS + N_GROUPS)
    gl = jnp.where(is_grp, logits, NEG)
    gm = jnp.max(gl, axis=-1, keepdims=True)
    g_w = 1.0 / jnp.sum(jnp.exp(gl - gm), axis=-1, keepdims=True)
    g_idx = jnp.min(jnp.where(gl == gm, lane, big), axis=-1, keepdims=True) - N_EXPERTS
    lo = g_idx * EXPERTS_PER_GROUP
    el = jnp.where((lane >= lo) & (lane < lo + EXPERTS_PER_GROUP), logits, NEG)
    m1 = jnp.max(el, axis=-1, keepdims=True)
    i1 = jnp.min(jnp.where(el == m1, lane, big), axis=-1, keepdims=True)
    el2 = jnp.where(lane == i1, NEG, el)
    m2 = jnp.max(el2, axis=-1, keepdims=True)
    i2 = jnp.min(jnp.where(el2 == m2, lane, big), axis=-1, keepdims=True)
    t = jnp.exp(m2 - m1)
    w1 = g_w / (1.0 + t)
    w2 = g_w * t / (1.0 + t)
    return jnp.where(lane == i1, w1, 0.0) + jnp.where(lane == i2, w2, 0.0), g_idx


MOE_TM = 512
MOE_CHUNK = 144
MOE_SLOTS = 1152
SLOT_RADIX = 32


def _moe_kernel(x_ref, o_ref, wo_ref, gta_ref, shm_ref, scm_ref, gtm_ref, gn_ref,
                wr_hi_ref, wr_lo_ref, br_ref, sel_ref, ltri_ref, wg_ref, wu_ref, wd_ref, fin_ref,
                y_ref, hs_ref, cs_ref, ys_ref, pt_ref, *, final):
    tm = x_ref.shape[0]
    ffg = EXPERTS_PER_GROUP * EXPERT_FF
    mix = jnp.dot(o_ref[...], wo_ref[...], preferred_element_type=F32)
    xm = x_ref[...] + gta_ref[...] * mix
    y_ref[...] = xm
    h = _rms(xm, gn_ref[...]) * (1.0 + scm_ref[...]) + shm_ref[...]
    h_hi = h.astype(BF16)
    h_lo = (h - h_hi.astype(F32)).astype(BF16)
    logits = (jnp.dot(h_hi, wr_hi_ref[...], preferred_element_type=F32)
              + jnp.dot(h_lo, wr_hi_ref[...], preferred_element_type=F32)
              + jnp.dot(h_hi, wr_lo_ref[...], preferred_element_type=F32)) + br_ref[...]
    comb, g_idx = _route(logits)

    lane_i = lax.broadcasted_iota(jnp.int32, (tm, LANES), 1)
    lane = lane_i.astype(F32)
    onehot = jnp.where(lane == g_idx, 1.0, 0.0)
    rank = jnp.dot(ltri_ref[...], onehot.astype(BF16), preferred_element_type=F32)
    cnt = jnp.sum(onehot, axis=0, keepdims=True)
    ends = []
    end = jnp.int32(0)
    start_vec = jnp.zeros_like(lane)
    for g in range(N_GROUPS):
        start_vec = jnp.where(lane_i == g, (end * MOE_CHUNK).astype(F32), start_vec)
        n_g = cnt[0, g].astype(jnp.int32)
        end = end + sum((n_g > k * MOE_CHUNK).astype(jnp.int32) for k in range(-(-tm // MOE_CHUNK)))
        ends.append(end)
    total = ends[-1]
    slot = jnp.sum(onehot * (start_vec + rank), axis=-1, keepdims=True)
    slot_i = slot.astype(jnp.int32)
    pt = jnp.where(lax.broadcasted_iota(jnp.int32, (tm, MOE_SLOTS), 1) == slot_i, 1.0, 0.0).astype(BF16)
    pt_ref[...] = pt
    a = jnp.floor(slot * (1.0 / SLOT_RADIX))
    b = slot - SLOT_RADIX * a
    digits = jnp.where(lane_i == 0, a, jnp.where(lane_i == 1, b, 0.0)).astype(BF16)
    rows = lax.dot_general(sel_ref[...], digits, (((1,), (1,)), ((), ())), preferred_element_type=F32)
    slot_row = (rows[0:1, :] * SLOT_RADIX + rows[1:2, :]).astype(jnp.int32)
    p = jnp.where(lax.broadcasted_iota(jnp.int32, (MOE_SLOTS, tm), 0) == slot_row, 1.0, 0.0).astype(BF16)

    hs_ref[...] = jnp.dot(p, h_hi, preferred_element_type=F32).astype(BF16)
    c_hi = comb.astype(BF16)
    r1 = comb - c_hi.astype(F32)
    c_mid = r1.astype(BF16)
    c_lo = (r1 - c_mid.astype(F32)).astype(BF16)
    packed = (c_hi.astype(F32) + pltpu.roll(c_mid.astype(F32), N_EXPERTS, 1)
              + pltpu.roll(c_lo.astype(F32), 2 * N_EXPERTS, 1)).astype(BF16)
    cs = jnp.dot(p, packed, preferred_element_type=F32)
    cs_ref[...] = cs + pltpu.roll(cs, LANES - N_EXPERTS, 1) + pltpu.roll(cs, LANES - 2 * N_EXPERTS, 1)

    def run_experts(c, n_chunks, g_c):
        rows_c = slice(c * MOE_CHUNK, (c + n_chunks) * MOE_CHUNK)
        lane_c = lax.broadcasted_iota(jnp.int32, (n_chunks * MOE_CHUNK, LANES), 1)
        xc = hs_ref[rows_c, :]
        cc = cs_ref[rows_c, :]
        hids = []
        for j in range(EXPERTS_PER_GROUP):
            e = g_c * EXPERTS_PER_GROUP + j
            gt = jnp.dot(xc, wg_ref[e], preferred_element_type=F32)
            up = jnp.dot(xc, wu_ref[e], preferred_element_type=F32)
            cj = jnp.sum(jnp.where(lane_c == e, cc, 0.0), axis=-1, keepdims=True)
            hids.append(((gt / (1.0 + jnp.exp(-gt))) * up * cj).astype(BF16))
        hid = jnp.concatenate(hids, axis=1)
        wd_g = wd_ref[pl.ds(pl.multiple_of(g_c * ffg, ffg), ffg), :]
        ys_ref[rows_c, :] = jnp.dot(hid, wd_g, preferred_element_type=F32).astype(BF16)

    n_slots = MOE_SLOTS // MOE_CHUNK
    for c in range(n_slots):
        g_c = sum((ends[g] <= c).astype(jnp.int32) for g in range(N_GROUPS - 1))
        start_c = jnp.int32(0)
        end_c = ends[0]
        for g in range(1, N_GROUPS):
            start_c = jnp.where(g_c == g, ends[g - 1], start_c)
            end_c = jnp.where(g_c == g, ends[g], end_c)
        live = c < total
        first_of_pair = ((c - start_c) & 1) == 0
        has_partner = c + 1 < end_c

        if c + 1 < n_slots:
            pl.when(live & first_of_pair & has_partner)(functools.partial(run_experts, c, 2, g_c))
        pl.when(live & first_of_pair & jnp.logical_not(has_partner))(functools.partial(run_experts, c, 1, g_c))

        @pl.when(jnp.logical_not(live))
        def _():
            ys_ref[c * MOE_CHUNK:(c + 1) * MOE_CHUNK, :] = jnp.zeros((MOE_CHUNK, D), BF16)

    y_tok = jnp.dot(pt_ref[...], ys_ref[...], preferred_element_type=F32)
    out = y_ref[...] + gtm_ref[...] * y_tok
    if final:
        out = _rms(out, fin_ref[...])
    y_ref[...] = out


def _moe(x2d, o2d, mods, layer, w, *, mod_row, final):
    n = x2d.shape[0]
    tm = MOE_TM
    row_fn = lambda i: mod_row(i)
    tile = pl.BlockSpec((tm, D), lambda i: (i, 0))
    once = lambda shape: pl.BlockSpec(shape, lambda i: (0,) * len(shape), pipeline_mode=pl.Buffered(1))
    per_layer = lambda shape: pl.BlockSpec((None,) + shape, lambda i: (layer,) + (0,) * len(shape),
                                           pipeline_mode=pl.Buffered(1))
    return pl.pallas_call(
        functools.partial(_moe_kernel, final=final),
        grid=(n // tm,),
        in_specs=[
            tile, tile, once((D, D)),
            _mod_spec(layer, 2, row_fn), _mod_spec(layer, 3, row_fn), _mod_spec(layer, 4, row_fn),
            _mod_spec(layer, 5, row_fn), _full_spec((1, D)),
            _full_spec((D, LANES)), _full_spec((D, LANES)), _full_spec((1, LANES)), _full_spec((8, LANES)),
            once((tm, tm)),
            per_layer((N_EXPERTS, D, EXPERT_FF)), per_layer((N_EXPERTS, D, EXPERT_FF)),
            per_layer((N_EXPERTS * EXPERT_FF, D)),
            _full_spec((1, D)),
        ],
        out_specs=tile,
        out_shape=jax.ShapeDtypeStruct((n, D), F32),
        scratch_shapes=[pltpu.VMEM((MOE_SLOTS, D), BF16), pltpu.VMEM((MOE_SLOTS, LANES), F32),
                        pltpu.VMEM((MOE_SLOTS, D), BF16), pltpu.VMEM((tm, MOE_SLOTS), BF16)],
        compiler_params=_params(1), name=f"moe_l{layer}",
    )(x2d, o2d, w["wo"], mods, mods, mods, mods, w["gn"], w["wr_hi"], w["wr_lo"], w["br"], w["sel2"],
      w["ltri"], w["wg_bf"], w["wu_bf"], w["wd_bf"], w["fin"])


def _mla_weights(w_dq, g_q, w_uq, w_dkv, g_kv, w_uk, w_uv):
    hd = MLA_NOPE + MLA_ROPE
    wuq = w_uq.reshape(Q_LORA, MLA_HEADS, hd)
    wuq = jnp.concatenate([wuq[..., MLA_NOPE:], wuq[..., :MLA_NOPE],
                           jnp.zeros((Q_LORA, MLA_HEADS, LANES - hd), F32)], axis=-1)
    wuk = jnp.concatenate([jnp.zeros((KV_LORA, MLA_HEADS, MLA_ROPE), F32), w_uk,
                           jnp.zeros((KV_LORA, MLA_HEADS, LANES - hd), F32)], axis=-1)
    return {
        "wdq": w_dq.astype(BF16), "gq": g_q.reshape(1, Q_LORA),
        "wuq": wuq.reshape(Q_LORA, MLA_HEADS * LANES).astype(BF16),
        "wdkv": jnp.pad(w_dkv, ((0, 0), (0, KV_PAD - KV_DIM))).astype(BF16), "gkv": g_kv.reshape(1, KV_LORA),
        "wkexp": wuk.reshape(KV_LORA, MLA_HEADS * LANES).astype(BF16),
        "wuv": w_uv.reshape(KV_LORA, MLA_HEADS * MLA_V).astype(BF16),
    }


def _gqa_weights(w_qkv, g_q, g_k):
    grp = jnp.arange(2 * LANES) // GQA_HEAD_DIM
    bmat = jnp.where(grp[:, None] == grp[None, :], 1.0 / GQA_HEAD_DIM, 0.0).astype(BF16)
    reps = 2 * LANES // GQA_HEAD_DIM
    return {"wqkv": w_qkv.astype(BF16), "gq": jnp.tile(g_q, reps).reshape(1, 2 * LANES),
            "gk": jnp.tile(g_k, reps).reshape(1, 2 * LANES), "bmat": bmat}


def _post_weights(l, w_o, norm_ffn, w_group, b_group, w_exp, b_exp, w_gate, w_up, w_down, final_norm):
    wr = jnp.concatenate([w_exp[l], w_group[l], jnp.zeros((D, LANES - N_EXPERTS - N_GROUPS), F32)], axis=1)
    br = jnp.concatenate([b_exp[l], b_group[l], jnp.zeros((LANES - N_EXPERTS - N_GROUPS,), F32)]).reshape(1, LANES)
    wr_hi = wr.astype(BF16)
    wr_lo = (wr - wr_hi.astype(F32)).astype(BF16)
    return {"wo": w_o.astype(BF16), "gn": norm_ffn[l].reshape(1, D), "wr_hi": wr_hi, "wr_lo": wr_lo, "br": br,
            "wg_bf": w_gate, "wu_bf": w_up, "wd_bf": w_down,
            "fin": final_norm.reshape(1, D),
            "sel2": jnp.zeros((8, LANES), BF16).at[0, 0].set(1.0).at[1, 1].set(1.0),
            "ltri": jnp.tri(MOE_TM, MOE_TM, -1, dtype=BF16)}


def kernel(x_prompt, x_sample, cache_mla, cache_gqa, c, c_ctx, ada_w, ada_b, norm_mix, norm_ffn,
           mla_w_dq, mla_q_norm, mla_w_uq, mla_w_dkv, mla_kv_norm, mla_w_uk, mla_w_uv, mla_w_o,
           gqa_w_qkv, gqa_q_norm, gqa_k_norm, gqa_w_o,
           moe_w_group, moe_b_group, moe_w_expert, moe_b_expert, moe_w_gate, moe_w_up, moe_w_down,
           final_norm):
    bp, tp, _ = x_prompt.shape
    bs, ts, _ = x_sample.shape
    t_past = cache_mla.shape[2]
    assert ada_w.shape[0] == 2 and 1 + bs <= N_MOD_ROWS
    n_p, n_s = bp * tp, bs * ts

    cc = jnp.concatenate([c_ctx[None], c, jnp.zeros((N_MOD_ROWS - 1 - bs, D), F32)], axis=0)
    mods = _ada_table(cc, ada_w, ada_b)

    xp = x_prompt.reshape(n_p, D)
    xs = x_sample.reshape(n_s, D)
    tm_proj = PROJ_TM
    tm_post = MOE_TM
    prompt_row = lambda *_: 0
    sample_row_proj = lambda b: 1 + b
    sample_row_post = lambda i: 1 + i // (ts // tm_post)
    depth = moe_w_gate.shape[0]
    moe_args = (norm_ffn, moe_w_group, moe_b_group, moe_w_expert, moe_b_expert,
                moe_w_gate.astype(BF16), moe_w_up.astype(BF16),
                moe_w_down.reshape(depth, N_EXPERTS * EXPERT_FF, D).astype(BF16), final_norm)

    wm = _mla_weights(mla_w_dq[0], mla_q_norm[0], mla_w_uq[0], mla_w_dkv[0], mla_kv_norm[0],
                      mla_w_uk[0], mla_w_uv[0])
    gn0 = norm_mix[0].reshape(1, D)
    q, k, v, keys_p = _mla_proj(xp, mods, gn0, wm, n_batch=1, t_new=n_p, tm=tm_proj, mod_row=prompt_row, t_state=tp)
    o = _attention(q, k, v, mla=True, n_batch=bp, t_q=tp, t_k=tp, tq=tp, n_pairs=8, keys_on_rows=False, name="attn_mla_p")
    wp0 = _post_weights(0, mla_w_o[0], *moe_args)
    xp = _moe(xp, o, mods, 0, wp0, mod_row=prompt_row, final=False)

    rope_mla = _rope_tables(ts, MLA_ROPE, 1)
    cache0 = jnp.transpose(cache_mla[:, 0], (0, 2, 1))
    q, k, v = _mla_proj(xs, mods, gn0, wm, n_batch=bs, t_new=ts, tm=tm_proj, mod_row=sample_row_proj,
                        rope_tabs=rope_mla, cache=cache0)
    o = _attention(q, k, v, mla=True, n_batch=bs, t_q=ts, t_k=t_past + ts, tq=ATTN_TQ, n_pairs=1,
                   keys_on_rows=False, name="attn_mla_s")
    xs = _moe(xs, o, mods, 0, wp0, mod_row=sample_row_post, final=False)

    wg = _gqa_weights(gqa_w_qkv[0], gqa_q_norm[0], gqa_k_norm[0])
    gn1 = norm_mix[1].reshape(1, D)
    q, k, v, kv_p = _gqa_proj(xp, mods, gn1, wg, n_batch=1, t_new=n_p, tm=tm_proj, mod_row=prompt_row, t_state=tp)
    o = _attention(q, k, v, mla=False, n_batch=bp, t_q=tp, t_k=tp, tq=tp, n_pairs=8, keys_on_rows=False, name="attn_gqa_p")
    wp1 = _post_weights(1, gqa_w_o[0], *moe_args)
    y_prompt = _moe(xp, o, mods, 1, wp1, mod_row=prompt_row, final=True)

    rope_gqa = _rope_tables(ts, GQA_HEAD_DIM, LANES // GQA_HEAD_DIM)
    cache1 = jnp.transpose(cache_gqa[:, 0], (0, 2, 3, 4, 1)).reshape(bs, 2, GQA_KV_HEADS * GQA_HEAD_DIM, t_past)
    q, k, v = _gqa_proj(xs, mods, gn1, wg, n_batch=bs, t_new=ts, tm=tm_proj, mod_row=sample_row_proj,
                        rope_tabs=rope_gqa, cache=cache1)
    o = _attention(q, k, v, mla=False, n_batch=bs, t_q=ts, t_k=t_past + ts, tq=ATTN_TQ, n_pairs=4,
                   keys_on_rows=True, name="attn_gqa_s")
    y_sample = _moe(xs, o, mods, 1, wp1, mod_row=sample_row_post, final=True)

    return (y_prompt.reshape(bp, tp, D), y_sample.reshape(bs, ts, D),
            jnp.transpose(keys_p, (0, 2, 1))[:, None],
            jnp.transpose(kv_p.reshape(bp, 2, GQA_KV_HEADS, GQA_HEAD_DIM, tp), (0, 4, 1, 2, 3))[:, None])
```

```python
import functools
import math

import jax
import jax.numpy as jnp
from jax import lax
from jax.experimental import pallas as pl
from jax.experimental.pallas import tpu as pltpu

F32 = jnp.float32
BF16 = jnp.bfloat16

D = 1024
EPS = 1e-6
GRID_W = 64
ROPE_THETA = 10000.0
LANES = 128
HALF = LANES // 2
MLA_HEADS = 16
MLA_NOPE = 64
MLA_ROPE = 32
MLA_V = 64
Q_LORA = 384
KV_LORA = 256
KV_DIM = KV_LORA + MLA_ROPE
KV_PAD = 384
MLA_SCALE = 1.0 / math.sqrt(MLA_NOPE + MLA_ROPE)
GQA_HEADS = 16
GQA_KV_HEADS = 4
GQA_HEAD_DIM = 64
GQA_SCALE = 1.0 / math.sqrt(GQA_HEAD_DIM)
N_GROUPS = 4
EXPERTS_PER_GROUP = 4
N_EXPERTS = 16
EXPERT_FF = 256
NEG = -3.0e38
LOG2E = 1.4426950408889634
ATTN_KEY_CHUNK = 512
ATTN_LOOKAHEAD = 1
ATTN_TQ = 512
PROJ_TM = 512

VMEM_LIMIT = 56 * 1024 * 1024
N_MOD_ROWS = 8


def _params(n_axes):
    return pltpu.CompilerParams(dimension_semantics=("arbitrary",) * n_axes,
                                vmem_limit_bytes=VMEM_LIMIT)


def _rms(x, g):
    ms = jnp.mean(x * x, axis=-1, keepdims=True)
    return x * lax.rsqrt(ms + EPS) * g


def _mod_index(layer, which, row):
    return (layer * 6 + which) * N_MOD_ROWS + row


def _mod_spec(layer, which, row_fn):
    return pl.BlockSpec((None, 1, D), lambda *g: (_mod_index(layer, which, row_fn(*g)), 0, 0))


def _full_spec(shape):
    n = len(shape)
    return pl.BlockSpec(shape, lambda *g: (0,) * n)


def _ada_kernel(c_ref, w_ref, b_ref, o_ref):
    c = c_ref[...]
    a = c / (1.0 + jnp.exp(-c))
    o_ref[...] = jnp.dot(a, w_ref[...], precision=lax.Precision.HIGHEST,
                         preferred_element_type=F32) + b_ref[...]


def _ada_table(cc, ada_w, ada_b):
    depth = ada_w.shape[0]
    out = pl.pallas_call(
        _ada_kernel,
        grid=(depth, 6),
        in_specs=[
            pl.BlockSpec((N_MOD_ROWS, D), lambda l, n: (0, 0)),
            pl.BlockSpec((None, D, D), lambda l, n: (l, 0, n)),
            pl.BlockSpec((None, 1, D), lambda l, n: (l, 0, n)),
        ],
        out_specs=pl.BlockSpec((None, None, N_MOD_ROWS, D), lambda l, n: (l, n, 0, 0)),
        out_shape=jax.ShapeDtypeStruct((depth, 6, N_MOD_ROWS, D), F32),
        compiler_params=_params(2),
        name="ada",
    )(cc, ada_w, ada_b.reshape(depth, 1, 6 * D))
    return out.reshape(depth * 6 * N_MOD_ROWS, 1, D)


def _rope(a, cos, sin_signed):
    lane = lax.broadcasted_iota(jnp.int32, a.shape, 1)
    nxt = pltpu.roll(a, LANES - 1, 1)
    prv = pltpu.roll(a, 1, 1)
    return a * cos + jnp.where((lane & 1) == 0, nxt, prv) * sin_signed


def _rope_tables(n_tokens, rot_dim, n_rep):
    t = jnp.arange(n_tokens)
    row = (t // GRID_W).astype(F32)
    col = (t % GRID_W).astype(F32)
    axis_dim = rot_dim // 2
    inv = jnp.power(ROPE_THETA, -jnp.arange(0, axis_dim, 2, dtype=F32) / axis_dim)
    ang = jnp.concatenate([row[:, None] * inv, col[:, None] * inv], axis=-1)
    cos = jnp.repeat(jnp.cos(ang), 2, axis=1)
    sin = jnp.repeat(jnp.sin(ang), 2, axis=1)
    sign = jnp.where(jnp.arange(rot_dim) % 2 == 0, -1.0, 1.0).astype(F32)
    sin = sin * sign
    rest = LANES - n_rep * rot_dim
    cos_t = jnp.concatenate([cos] * n_rep + [jnp.ones((n_tokens, rest), F32)], axis=1)
    sin_t = jnp.concatenate([sin] * n_rep + [jnp.zeros((n_tokens, rest), F32)], axis=1)
    return cos_t, sin_t


def _mla_proj_kernel(*refs, rope, n_cache, emit_keys):
    it = iter(refs)
    x_ref, sh_ref, sc_ref, gn_ref = next(it), next(it), next(it), next(it)
    wdq_ref, gq_ref, wuq_ref, wdkv_ref, gkv_ref, wkexp_ref, wuv_ref = (next(it) for _ in range(7))
    cos_ref = sin_ref = cache_ref = keys_ref = None
    if rope:
        cos_ref, sin_ref = next(it), next(it)
    if n_cache:
        cache_ref = next(it)
    q_ref, k_ref, v_ref = next(it), next(it), next(it)
    if emit_keys:
        keys_ref = next(it)

    def expand(ckv, pe):
        ckv = ckv.astype(BF16)
        kn = jnp.dot(ckv, wkexp_ref[...], preferred_element_type=F32)
        k_ref[...] = jnp.concatenate(
            [kn[:, h_ * LANES:(h_ + 1) * LANES] + pe for h_ in range(MLA_HEADS)], axis=1).astype(BF16)
        v_ref[...] = jnp.dot(ckv, wuv_ref[...], preferred_element_type=F32).astype(BF16)

    def new_tokens():
        x = x_ref[...]
        h = (_rms(x, gn_ref[...]) * (1.0 + sc_ref[...]) + sh_ref[...]).astype(BF16)
        ql = jnp.dot(h, wdq_ref[...], preferred_element_type=F32)
        qn = _rms(ql, gq_ref[...]).astype(BF16)
        q = jnp.dot(qn, wuq_ref[...], preferred_element_type=F32)
        kv = jnp.dot(h, wdkv_ref[...], preferred_element_type=F32)
        ckv = _rms(kv[:, :KV_LORA], gkv_ref[...])
        pe = kv[:, KV_LORA:]
        if rope:
            cos, sin = cos_ref[...], sin_ref[...]
            pe = _rope(pe, cos, sin)
            q = jnp.concatenate(
                [_rope(q[:, h_ * LANES:(h_ + 1) * LANES], cos, sin) for h_ in range(MLA_HEADS)], axis=1)
        q_ref[...] = (q * (MLA_SCALE * LOG2E)).astype(BF16)
        if emit_keys:
            t_req = keys_ref.shape[-1]
            for r in range(keys_ref.shape[0]):
                rows = slice(r * t_req, (r + 1) * t_req)
                keys_ref[r, 0:KV_LORA, :] = ckv[rows, :].T
                keys_ref[r, KV_LORA:KV_DIM, :] = pe[rows, :].T[0:MLA_ROPE, :]
        expand(ckv, pe)

    if n_cache:
        t = pl.program_id(1)
        pl.when(t >= n_cache)(new_tokens)

        @pl.when(t < n_cache)
        def _():
            pe_t = jnp.concatenate([cache_ref[KV_LORA:KV_DIM, :],
                                    jnp.zeros((LANES - MLA_ROPE, cache_ref.shape[-1]), F32)], axis=0)
            expand(cache_ref[0:KV_LORA, :].T, pe_t.T)
    else:
        new_tokens()


def _mla_proj(x2d, mods, gn, w, *, n_batch, t_new, tm, mod_row, rope_tabs=None, cache=None, t_state=None):
    rope = rope_tabs is not None
    t_cache = 0 if cache is None else cache.shape[-1]
    n_cache = t_cache // tm
    ntn = t_new // tm
    nt = n_cache + ntn
    emit_keys = cache is None

    def new_idx(b, t):
        return b * ntn + jnp.maximum(t - n_cache, 0)

    row_fn = lambda b, t: mod_row(b)
    in_specs = [
        pl.BlockSpec((tm, D), lambda b, t: (new_idx(b, t), 0)),
        _mod_spec(0, 0, row_fn), _mod_spec(0, 1, row_fn),
        _full_spec((1, D)),
        _full_spec((D, Q_LORA)), _full_spec((1, Q_LORA)), _full_spec((Q_LORA, MLA_HEADS * LANES)),
        _full_spec((D, KV_PAD)), _full_spec((1, KV_LORA)),
        _full_spec((KV_LORA, MLA_HEADS * LANES)), _full_spec((KV_LORA, MLA_HEADS * MLA_V)),
    ]
    args = [x2d, mods, mods, gn, w["wdq"], w["gq"], w["wuq"], w["wdkv"], w["gkv"], w["wkexp"], w["wuv"]]
    if rope:
        in_specs += [pl.BlockSpec((tm, LANES), lambda b, t: (jnp.maximum(t - n_cache, 0), 0))] * 2
        args += list(rope_tabs)
    if n_cache:
        in_specs.append(pl.BlockSpec((None, KV_DIM, tm), lambda b, t: (b, 0, jnp.minimum(t, n_cache - 1))))
        args.append(cache)
    n_new = n_batch * t_new
    n_keys = n_batch * (t_cache + t_new)
    out_specs = [
        pl.BlockSpec((tm, MLA_HEADS * LANES), lambda b, t: (new_idx(b, t), 0)),
        pl.BlockSpec((tm, MLA_HEADS * LANES), lambda b, t: (b * nt + t, 0)),
        pl.BlockSpec((tm, MLA_HEADS * MLA_V), lambda b, t: (b * nt + t, 0)),
    ]
    out_shape = [
        jax.ShapeDtypeStruct((n_new, MLA_HEADS * LANES), BF16),
        jax.ShapeDtypeStruct((n_keys, MLA_HEADS * LANES), BF16),
        jax.ShapeDtypeStruct((n_keys, MLA_HEADS * MLA_V), BF16),
    ]
    if emit_keys:
        assert tm % t_state == 0
        out_specs.append(pl.BlockSpec((tm // t_state, KV_DIM, t_state), lambda b, t: (new_idx(b, t), 0, 0)))
        out_shape.append(jax.ShapeDtypeStruct((n_new // t_state, KV_DIM, t_state), F32))
    return pl.pallas_call(
        functools.partial(_mla_proj_kernel, rope=rope, n_cache=n_cache, emit_keys=emit_keys),
        grid=(n_batch, nt), in_specs=in_specs, out_specs=out_specs, out_shape=out_shape,
        compiler_params=_params(2), name="mla_proj_s" if rope else "mla_proj_p",
    )(*args)


def _dup_halves(a):
    cols = []
    for c in range(a.shape[1] // LANES):
        blk = a[:, c * LANES:(c + 1) * LANES]
        rot = pltpu.roll(blk, HALF, 1)
        low = lax.broadcasted_iota(jnp.int32, blk.shape, 1) < HALF
        cols += [jnp.where(low, blk, rot), jnp.where(low, rot, blk)]
    return jnp.concatenate(cols, axis=1)


def _group_mean_sq(a, bmat):
    sq = a * a
    hi = sq.astype(BF16)
    lo = (sq - hi.astype(F32)).astype(BF16)
    return (jnp.dot(hi, bmat, preferred_element_type=F32) + jnp.dot(lo, bmat, preferred_element_type=F32))


def _gqa_proj_kernel(*refs, rope, n_cache, emit_kv):
    it = iter(refs)
    x_ref, sh_ref, sc_ref, gn_ref, w_ref, gq_ref, gk_ref, bmat_ref = (next(it) for _ in range(8))
    cos_ref = sin_ref = cache_ref = kv_ref = None
    if rope:
        cos_ref, sin_ref = next(it), next(it)
    if n_cache:
        cache_ref = next(it)
    q_ref, k_ref, v_ref = next(it), next(it), next(it)
    if emit_kv:
        kv_ref = next(it)
    nq = GQA_HEADS * GQA_HEAD_DIM
    nk = GQA_KV_HEADS * GQA_HEAD_DIM
    blk = 2 * LANES

    def new_tokens():
        x = x_ref[...]
        h = (_rms(x, gn_ref[...]) * (1.0 + sc_ref[...]) + sh_ref[...]).astype(BF16)
        qkv = jnp.dot(h, w_ref[...], preferred_element_type=F32)
        bmat = bmat_ref[...]

        def head_norm(a, g):
            return a * lax.rsqrt(_group_mean_sq(a, bmat) + EPS) * g

        def maybe_rope(a):
            if not rope:
                return a
            cos, sin = cos_ref[...], sin_ref[...]
            return jnp.concatenate(
                [_rope(a[:, c * LANES:(c + 1) * LANES], cos, sin) for c in range(a.shape[1] // LANES)], axis=1)

        for c in range(nq // blk):
            qb = head_norm(qkv[:, c * blk:(c + 1) * blk], gq_ref[...])
            q_ref[:, c * blk:(c + 1) * blk] = (maybe_rope(qb) * (GQA_SCALE * LOG2E)).astype(BF16)
        kn = head_norm(qkv[:, nq:nq + nk], gk_ref[...])
        vv = qkv[:, nq + nk:]
        if emit_kv:
            t_req = kv_ref.shape[-1]
            for r in range(kv_ref.shape[0]):
                kv_ref[r, 0] = kn[r * t_req:(r + 1) * t_req, :].T
                kv_ref[r, 1] = vv[r * t_req:(r + 1) * t_req, :].T
        k_ref[...] = _dup_halves(maybe_rope(kn)).astype(BF16)
        v_ref[...] = _dup_halves(vv).astype(BF16)

    if n_cache:
        t = pl.program_id(1)
        pl.when(t >= n_cache)(new_tokens)

        @pl.when(t < n_cache)
        def _():
            k_ref[...] = _dup_halves(cache_ref[0].T).astype(BF16)
            v_ref[...] = _dup_halves(cache_ref[1].T).astype(BF16)
    else:
        new_tokens()


def _gqa_proj(x2d, mods, gn, w, *, n_batch, t_new, tm, mod_row, rope_tabs=None, cache=None, t_state=None):
    rope = rope_tabs is not None
    t_cache = 0 if cache is None else cache.shape[-1]
    n_cache = t_cache // tm
    ntn = t_new // tm
    nt = n_cache + ntn
    emit_kv = cache is None
    nq = GQA_HEADS * GQA_HEAD_DIM
    nk = GQA_KV_HEADS * GQA_HEAD_DIM
    nqkv = nq + 2 * nk
    ndup = GQA_KV_HEADS * LANES

    def new_idx(b, t):
        return b * ntn + jnp.maximum(t - n_cache, 0)

    row_fn = lambda b, t: mod_row(b)
    in_specs = [
        pl.BlockSpec((tm, D), lambda b, t: (new_idx(b, t), 0)),
        _mod_spec(1, 0, row_fn), _mod_spec(1, 1, row_fn),
        _full_spec((1, D)), _full_spec((D, nqkv)),
        _full_spec((1, 2 * LANES)), _full_spec((1, 2 * LANES)), _full_spec((2 * LANES, 2 * LANES)),
    ]
    args = [x2d, mods, mods, gn, w["wqkv"], w["gq"], w["gk"], w["bmat"]]
    if rope:
        in_specs += [pl.BlockSpec((tm, LANES), lambda b, t: (jnp.maximum(t - n_cache, 0), 0))] * 2
        args += list(rope_tabs)
    if n_cache:
        in_specs.append(pl.BlockSpec((None, 2, nk, tm), lambda b, t: (b, 0, 0, jnp.minimum(t, n_cache - 1))))
        args.append(cache)
    n_new = n_batch * t_new
    n_keys = n_batch * (t_cache + t_new)
    out_specs = [
        pl.BlockSpec((tm, nq), lambda b, t: (new_idx(b, t), 0)),
        pl.BlockSpec((tm, ndup), lambda b, t: (b * nt + t, 0)),
        pl.BlockSpec((tm, ndup), lambda b, t: (b * nt + t, 0)),
    ]
    out_shape = [
        jax.ShapeDtypeStruct((n_new, nq), BF16),
        jax.ShapeDtypeStruct((n_keys, ndup), BF16),
        jax.ShapeDtypeStruct((n_keys, ndup), BF16),
    ]
    if emit_kv:
        assert tm % t_state == 0
        out_specs.append(pl.BlockSpec((tm // t_state, 2, nk, t_state), lambda b, t: (new_idx(b, t), 0, 0, 0)))
        out_shape.append(jax.ShapeDtypeStruct((n_new // t_state, 2, nk, t_state), F32))
    return pl.pallas_call(
        functools.partial(_gqa_proj_kernel, rope=rope, n_cache=n_cache, emit_kv=emit_kv),
        grid=(n_batch, nt), in_specs=in_specs, out_specs=out_specs, out_shape=out_shape,
        compiler_params=_params(2), name="gqa_proj_s" if rope else "gqa_proj_p",
    )(*args)


def _attn_kernel(q_ref, k_ref, v_ref, o_ref, *, mla, n_pairs, ck, keys_on_rows):
    tq = q_ref.shape[0]
    nc = k_ref.shape[0] // ck
    lane = lax.broadcasted_iota(jnp.int32, (tq, LANES), 1)
    low = lane < HALF
    items = [(p, c, hh) for c in range(nc) for p in range(n_pairs) for hh in range(2)]
    heads = {}
    state = {}

    def head_operands(p, hh):
        if (p, hh) not in heads:
            if mla:
                hd = 2 * p + hh
                heads[(p, hh)] = (q_ref[:, hd * LANES:(hd + 1) * LANES], hd, p)
            else:
                g = p // 2 if n_pairs > 1 else 0
                qp = q_ref[:, p * LANES:(p + 1) * LANES]
                qh = jnp.where(low if hh == 0 else jnp.logical_not(low), qp, jnp.zeros_like(qp))
                heads[(p, hh)] = (qh, g, g)
        return heads[(p, hh)]

    key_axis = 0 if keys_on_rows else -1
    first_half = (lax.broadcasted_iota(jnp.int32, (LANES, tq), 0) < HALF) if keys_on_rows else low

    def scores(item):
        p, c, hh = item
        qh, kc, _ = head_operands(p, hh)
        kh = k_ref[c * ck:(c + 1) * ck, kc * LANES:(kc + 1) * LANES]
        lhs, rhs = (kh, qh) if keys_on_rows else (qh, kh)
        return lax.dot_general(lhs, rhs, (((1,), (1,)), ((), ())), preferred_element_type=F32)

    def weighted_values(e, vb):
        if keys_on_rows:
            return lax.dot_general(vb, e.astype(BF16), (((0,), (0,)), ((), ())), preferred_element_type=F32)
        return jnp.dot(e.astype(BF16), vb, preferred_element_type=F32)

    pending = [scores(item) for item in items[:ATTN_LOOKAHEAD]]
    for idx, (p, c, hh) in enumerate(items):
        if idx + ATTN_LOOKAHEAD < len(items):
            pending.append(scores(items[idx + ATTN_LOOKAHEAD]))
        s_cur = pending.pop(0)
        vc = head_operands(p, hh)[2]
        vb = v_ref[c * ck:(c + 1) * ck, vc * LANES:(vc + 1) * LANES]
        m_c = jnp.max(s_cur, axis=key_axis, keepdims=True)
        if c == 0:
            m = m_c
            e = jnp.exp2(s_cur - m)
            l = jnp.sum(e, axis=key_axis, keepdims=True)
            acc = weighted_values(e, vb)
        else:
            m_old, l_old, acc_old = state[(p, hh)]
            m = jnp.maximum(m_old, m_c)
            alpha = jnp.exp2(m_old - m)
            e = jnp.exp2(s_cur - m)
            l = alpha * l_old + jnp.sum(e, axis=key_axis, keepdims=True)
            acc = alpha * acc_old + weighted_values(e, vb)
        state[(p, hh)] = (m, l, acc)
        if c == nc - 1 and hh == 1:
            o0 = state[(p, 0)][2] / state[(p, 0)][1]
            o1 = state[(p, 1)][2] / state[(p, 1)][1]
            o = jnp.where(first_half, o0, o1)
            o_ref[:, p * LANES:(p + 1) * LANES] = (o.T if keys_on_rows else o).astype(BF16)


def _attention(q, k, v, *, mla, n_batch, t_q, t_k, tq, n_pairs, keys_on_rows, name):
    total_pairs = 8
    nj = total_pairs // n_pairs
    nqt = t_q // tq
    if mla:
        qw, kw, vw = n_pairs * 2 * LANES, n_pairs * 2 * LANES, n_pairs * LANES
        kv_col = lambda j: j
    else:
        qw = n_pairs * LANES
        kw = vw = max(n_pairs // 2, 1) * LANES
        kv_col = (lambda j: j // 2) if n_pairs == 1 else (lambda j: j)
    return pl.pallas_call(
        functools.partial(_attn_kernel, mla=mla, n_pairs=n_pairs, ck=min(t_k, ATTN_KEY_CHUNK),
                          keys_on_rows=keys_on_rows),
        grid=(n_batch, nj, nqt),
        in_specs=[
            pl.BlockSpec((tq, qw), lambda b, j, i: (b * nqt + i, j)),
            pl.BlockSpec((t_k, kw), lambda b, j, i: (b, kv_col(j))),
            pl.BlockSpec((t_k, vw), lambda b, j, i: (b, kv_col(j))),
        ],
        out_specs=pl.BlockSpec((tq, n_pairs * LANES), lambda b, j, i: (b * nqt + i, j)),
        out_shape=jax.ShapeDtypeStruct((n_batch * t_q, total_pairs * LANES), BF16),
        compiler_params=_params(3), name=name,
    )(q, k, v)


def _route(logits):
    lane = lax.broadcasted_iota(jnp.int32, logits.shape, 1).astype(F32)
    big = jnp.float32(1e9)
    is_grp = (lane >= N_EXPERTS) & (lane < N_EXPERTS + N_GROUPS)
    gl = jnp.where(is_grp, logits, NEG)
    gm = jnp.max(gl, axis=-1, keepdims=True)
    g_w = 1.0 / jnp.sum(jnp.exp(gl - gm), axis=-1, keepdims=True)
    g_idx = jnp.min(jnp.where(gl == gm, lane, big), axis=-1, keepdims=True) - N_EXPERTS
    lo = g_idx * EXPERTS_PER_GROUP
    el = jnp.where((lane >= lo) & (lane < lo + EXPERTS_PER_GROUP), logits, NEG)
    m1 = jnp.max(el, axis=-1, keepdims=True)
    i1 = jnp.min(jnp.where(el == m1, lane, big), axis=-1, keepdims=True)
    el2 = jnp.where(lane == i1, NEG, el)
    m2 = jnp.max(el2, axis=-1, keepdims=True)
    i2 = jnp.min(jnp.where(el2 == m2, lane, big), axis=-1, keepdims=True)
    t = jnp.exp(m2 - m1)
    w1 = g_w / (1.0 + t)
    w2 = g_w * t / (1.0 + t)
    return jnp.where(lane == i1, w1, 0.0) + jnp.where(lane == i2, w2, 0.0), g_idx


MOE_TM = 512
MOE_CHUNK = 144
MOE_SLOTS = 1152
MOE_MAIN_SLOTS = 896
SLOT_RADIX = 32


def _moe_kernel(x_ref, o_ref, wo_ref, gta_ref, shm_ref, scm_ref, gtm_ref, gn_ref,
                wr_hi_ref, wr_lo_ref, br_ref, sel_ref, ltri_ref, wg_ref, wu_ref, wd_ref, fin_ref,
                y_ref, hs_ref, cs_ref, ys_ref, pt_ref, *, final):
    tm = x_ref.shape[0]
    ffg = EXPERTS_PER_GROUP * EXPERT_FF
    mix = jnp.dot(o_ref[...], wo_ref[...], preferred_element_type=F32)
    xm = x_ref[...] + gta_ref[...] * mix
    y_ref[...] = xm
    h = _rms(xm, gn_ref[...]) * (1.0 + scm_ref[...]) + shm_ref[...]
    h_hi = h.astype(BF16)
    h_lo = (h - h_hi.astype(F32)).astype(BF16)
    logits = (jnp.dot(h_hi, wr_hi_ref[...], preferred_element_type=F32)
              + jnp.dot(h_lo, wr_hi_ref[...], preferred_element_type=F32)
              + jnp.dot(h_hi, wr_lo_ref[...], preferred_element_type=F32)) + br_ref[...]
    comb, g_idx = _route(logits)

    lane_i = lax.broadcasted_iota(jnp.int32, (tm, LANES), 1)
    lane = lane_i.astype(F32)
    onehot = jnp.where(lane == g_idx, 1.0, 0.0)
    rank = jnp.dot(ltri_ref[...], onehot.astype(BF16), preferred_element_type=F32)
    cnt = jnp.sum(onehot, axis=0, keepdims=True)
    ends = []
    end = jnp.int32(0)
    start_vec = jnp.zeros_like(lane)
    for g in range(N_GROUPS):
        start_vec = jnp.where(lane_i == g, (end * MOE_CHUNK).astype(F32), start_vec)
        n_g = cnt[0, g].astype(jnp.int32)
        end = end + sum((n_g > k * MOE_CHUNK).astype(jnp.int32) for k in range(-(-tm // MOE_CHUNK)))
        ends.append(end)
    total = ends[-1]
    slot = jnp.sum(onehot * (start_vec + rank), axis=-1, keepdims=True)
    slot_i = slot.astype(jnp.int32)
    pt = jnp.where(lax.broadcasted_iota(jnp.int32, (tm, MOE_SLOTS), 1) == slot_i, 1.0, 0.0).astype(BF16)
    pt_ref[...] = pt
    a = jnp.floor(slot * (1.0 / SLOT_RADIX))
    b = slot - SLOT_RADIX * a
    digits = jnp.where(lane_i == 0, a, jnp.where(lane_i == 1, b, 0.0)).astype(BF16)
    rows = lax.dot_general(sel_ref[...], digits, (((1,), (1,)), ((), ())), preferred_element_type=F32)
    slot_row = (rows[0:1, :] * SLOT_RADIX + rows[1:2, :]).astype(jnp.int32)

    c_hi = comb.astype(BF16)
    r1 = comb - c_hi.astype(F32)
    c_mid = r1.astype(BF16)
    c_lo = (r1 - c_mid.astype(F32)).astype(BF16)
    packed = (c_hi.astype(F32) + pltpu.roll(c_mid.astype(F32), N_EXPERTS, 1)
              + pltpu.roll(c_lo.astype(F32), 2 * N_EXPERTS, 1)).astype(BF16)

    def sort_rows(row0, n_rows):
        p = jnp.where(lax.broadcasted_iota(jnp.int32, (n_rows, tm), 0) + row0 == slot_row, 1.0, 0.0).astype(BF16)
        hs_ref[row0:row0 + n_rows, :] = jnp.dot(p, h_hi, preferred_element_type=F32).astype(BF16)
        cs = jnp.dot(p, packed, preferred_element_type=F32)
        cs_ref[row0:row0 + n_rows, :] = (cs + pltpu.roll(cs, LANES - N_EXPERTS, 1)
                                         + pltpu.roll(cs, LANES - 2 * N_EXPERTS, 1))

    tail_live = total * MOE_CHUNK > MOE_MAIN_SLOTS
    sort_rows(0, MOE_MAIN_SLOTS)
    pl.when(tail_live)(functools.partial(sort_rows, MOE_MAIN_SLOTS, MOE_SLOTS - MOE_MAIN_SLOTS))

    def run_experts(c, n_chunks, g_c):
        rows_c = slice(c * MOE_CHUNK, (c + n_chunks) * MOE_CHUNK)
        lane_c = lax.broadcasted_iota(jnp.int32, (n_chunks * MOE_CHUNK, LANES), 1)
        xc = hs_ref[rows_c, :]
        cc = cs_ref[rows_c, :]
        hids = []
        for j in range(EXPERTS_PER_GROUP):
            e = g_c * EXPERTS_PER_GROUP + j
            gt = jnp.dot(xc, wg_ref[e], preferred_element_type=F32)
            up = jnp.dot(xc, wu_ref[e], preferred_element_type=F32)
            cj = jnp.sum(jnp.where(lane_c == e, cc, 0.0), axis=-1, keepdims=True)
            hids.append(((gt / (1.0 + jnp.exp(-gt))) * up * cj).astype(BF16))
        hid = jnp.concatenate(hids, axis=1)
        wd_g = wd_ref[pl.ds(pl.multiple_of(g_c * ffg, ffg), ffg), :]
        ys_ref[rows_c, :] = jnp.dot(hid, wd_g, preferred_element_type=F32).astype(BF16)

    n_slots = MOE_SLOTS // MOE_CHUNK
    for c in range(n_slots):
        g_c = sum((ends[g] <= c).astype(jnp.int32) for g in range(N_GROUPS - 1))
        start_c = jnp.int32(0)
        end_c = ends[0]
        for g in range(1, N_GROUPS):
            start_c = jnp.where(g_c == g, ends[g - 1], start_c)
            end_c = jnp.where(g_c == g, ends[g], end_c)
        live = c < total
        first_of_pair = ((c - start_c) & 1) == 0
        has_partner = c + 1 < end_c

        if c + 1 < n_slots:
            pl.when(live & first_of_pair & has_partner)(functools.partial(run_experts, c, 2, g_c))
        pl.when(live & first_of_pair & jnp.logical_not(has_partner))(functools.partial(run_experts, c, 1, g_c))

        @pl.when(jnp.logical_not(live))
        def _():
            ys_ref[c * MOE_CHUNK:(c + 1) * MOE_CHUNK, :] = jnp.zeros((MOE_CHUNK, D), BF16)

    y_tok = jnp.dot(pt_ref[:, 0:MOE_MAIN_SLOTS], ys_ref[0:MOE_MAIN_SLOTS, :], preferred_element_type=F32)
    y_ref[...] = y_ref[...] + gtm_ref[...] * y_tok

    @pl.when(tail_live)
    def _():
        y_tail = jnp.dot(pt_ref[:, MOE_MAIN_SLOTS:MOE_SLOTS], ys_ref[MOE_MAIN_SLOTS:MOE_SLOTS, :],
                         preferred_element_type=F32)
        y_ref[...] = y_ref[...] + gtm_ref[...] * y_tail

    if final:
        y_ref[...] = _rms(y_ref[...], fin_ref[...])


def _moe(x2d, o2d, mods, layer, w, *, mod_row, final):
    n = x2d.shape[0]
    tm = MOE_TM
    row_fn = lambda i: mod_row(i)
    tile = pl.BlockSpec((tm, D), lambda i: (i, 0))
    once = lambda shape: pl.BlockSpec(shape, lambda i: (0,) * len(shape), pipeline_mode=pl.Buffered(1))
    per_layer = lambda shape: pl.BlockSpec((None,) + shape, lambda i: (layer,) + (0,) * len(shape),
                                           pipeline_mode=pl.Buffered(1))
    return pl.pallas_call(
        functools.partial(_moe_kernel, final=final),
        grid=(n // tm,),
        in_specs=[
            tile, tile, once((D, D)),
            _mod_spec(layer, 2, row_fn), _mod_spec(layer, 3, row_fn), _mod_spec(layer, 4, row_fn),
            _mod_spec(layer, 5, row_fn), _full_spec((1, D)),
            _full_spec((D, LANES)), _full_spec((D, LANES)), _full_spec((1, LANES)), _full_spec((8, LANES)),
            once((tm, tm)),
            per_layer((N_EXPERTS, D, EXPERT_FF)), per_layer((N_EXPERTS, D, EXPERT_FF)),
            per_layer((N_EXPERTS * EXPERT_FF, D)),
            _full_spec((1, D)),
        ],
        out_specs=tile,
        out_shape=jax.ShapeDtypeStruct((n, D), F32),
        scratch_shapes=[pltpu.VMEM((MOE_SLOTS, D), BF16), pltpu.VMEM((MOE_SLOTS, LANES), F32),
                        pltpu.VMEM((MOE_SLOTS, D), BF16), pltpu.VMEM((tm, MOE_SLOTS), BF16)],
        compiler_params=_params(1), name=f"moe_l{layer}",
    )(x2d, o2d, w["wo"], mods, mods, mods, mods, w["gn"], w["wr_hi"], w["wr_lo"], w["br"], w["sel2"],
      w["ltri"], w["wg_bf"], w["wu_bf"], w["wd_bf"], w["fin"])


def _mla_weights(w_dq, g_q, w_uq, w_dkv, g_kv, w_uk, w_uv):
    hd = MLA_NOPE + MLA_ROPE
    wuq = w_uq.reshape(Q_LORA, MLA_HEADS, hd)
    wuq = jnp.concatenate([wuq[..., MLA_NOPE:], wuq[..., :MLA_NOPE],
                           jnp.zeros((Q_LORA, MLA_HEADS, LANES - hd), F32)], axis=-1)
    wuk = jnp.concatenate([jnp.zeros((KV_LORA, MLA_HEADS, MLA_ROPE), F32), w_uk,
                           jnp.zeros((KV_LORA, MLA_HEADS, LANES - hd), F32)], axis=-1)
    return {
        "wdq": w_dq.astype(BF16), "gq": g_q.reshape(1, Q_LORA),
        "wuq": wuq.reshape(Q_LORA, MLA_HEADS * LANES).astype(BF16),
        "wdkv": jnp.pad(w_dkv, ((0, 0), (0, KV_PAD - KV_DIM))).astype(BF16), "gkv": g_kv.reshape(1, KV_LORA),
        "wkexp": wuk.reshape(KV_LORA, MLA_HEADS * LANES).astype(BF16),
        "wuv": w_uv.reshape(KV_LORA, MLA_HEADS * MLA_V).astype(BF16),
    }


def _gqa_weights(w_qkv, g_q, g_k):
    grp = jnp.arange(2 * LANES) // GQA_HEAD_DIM
    bmat = jnp.where(grp[:, None] == grp[None, :], 1.0 / GQA_HEAD_DIM, 0.0).astype(BF16)
    reps = 2 * LANES // GQA_HEAD_DIM
    return {"wqkv": w_qkv.astype(BF16), "gq": jnp.tile(g_q, reps).reshape(1, 2 * LANES),
            "gk": jnp.tile(g_k, reps).reshape(1, 2 * LANES), "bmat": bmat}


def _post_weights(l, w_o, norm_ffn, w_group, b_group, w_exp, b_exp, w_gate, w_up, w_down, final_norm):
    wr = jnp.concatenate([w_exp[l], w_group[l], jnp.zeros((D, LANES - N_EXPERTS - N_GROUPS), F32)], axis=1)
    br = jnp.concatenate([b_exp[l], b_group[l], jnp.zeros((LANES - N_EXPERTS - N_GROUPS,), F32)]).reshape(1, LANES)
    wr_hi = wr.astype(BF16)
    wr_lo = (wr - wr_hi.astype(F32)).astype(BF16)
    return {"wo": w_o.astype(BF16), "gn": norm_ffn[l].reshape(1, D), "wr_hi": wr_hi, "wr_lo": wr_lo, "br": br,
            "wg_bf": w_gate, "wu_bf": w_up, "wd_bf": w_down,
            "fin": final_norm.reshape(1, D),
            "sel2": jnp.zeros((8, LANES), BF16).at[0, 0].set(1.0).at[1, 1].set(1.0),
            "ltri": jnp.tri(MOE_TM, MOE_TM, -1, dtype=BF16)}


def kernel(x_prompt, x_sample, cache_mla, cache_gqa, c, c_ctx, ada_w, ada_b, norm_mix, norm_ffn,
           mla_w_dq, mla_q_norm, mla_w_uq, mla_w_dkv, mla_kv_norm, mla_w_uk, mla_w_uv, mla_w_o,
           gqa_w_qkv, gqa_q_norm, gqa_k_norm, gqa_w_o,
           moe_w_group, moe_b_group, moe_w_expert, moe_b_expert, moe_w_gate, moe_w_up, moe_w_down,
           final_norm):
    bp, tp, _ = x_prompt.shape
    bs, ts, _ = x_sample.shape
    t_past = cache_mla.shape[2]
    assert ada_w.shape[0] == 2 and 1 + bs <= N_MOD_ROWS
    n_p, n_s = bp * tp, bs * ts

    cc = jnp.concatenate([c_ctx[None], c, jnp.zeros((N_MOD_ROWS - 1 - bs, D), F32)], axis=0)
    mods = _ada_table(cc, ada_w, ada_b)

    xp = x_prompt.reshape(n_p, D)
    xs = x_sample.reshape(n_s, D)
    tm_proj = PROJ_TM
    tm_post = MOE_TM
    prompt_row = lambda *_: 0
    sample_row_proj = lambda b: 1 + b
    sample_row_post = lambda i: 1 + i // (ts // tm_post)
    depth = moe_w_gate.shape[0]
    moe_args = (norm_ffn, moe_w_group, moe_b_group, moe_w_expert, moe_b_expert,
                moe_w_gate.astype(BF16), moe_w_up.astype(BF16),
                moe_w_down.reshape(depth, N_EXPERTS * EXPERT_FF, D).astype(BF16), final_norm)

    wm = _mla_weights(mla_w_dq[0], mla_q_norm[0], mla_w_uq[0], mla_w_dkv[0], mla_kv_norm[0],
                      mla_w_uk[0], mla_w_uv[0])
    gn0 = norm_mix[0].reshape(1, D)
    q, k, v, keys_p = _mla_proj(xp, mods, gn0, wm, n_batch=1, t_new=n_p, tm=tm_proj, mod_row=prompt_row, t_state=tp)
    o = _attention(q, k, v, mla=True, n_batch=bp, t_q=tp, t_k=tp, tq=tp, n_pairs=8, keys_on_rows=False, name="attn_mla_p")
    wp0 = _post_weights(0, mla_w_o[0], *moe_args)
    xp = _moe(xp, o, mods, 0, wp0, mod_row=prompt_row, final=False)

    rope_mla = _rope_tables(ts, MLA_ROPE, 1)
    cache0 = jnp.transpose(cache_mla[:, 0], (0, 2, 1))
    q, k, v = _mla_proj(xs, mods, gn0, wm, n_batch=bs, t_new=ts, tm=tm_proj, mod_row=sample_row_proj,
                        rope_tabs=rope_mla, cache=cache0)
    o = _attention(q, k, v, mla=True, n_batch=bs, t_q=ts, t_k=t_past + ts, tq=ATTN_TQ, n_pairs=1,
                   keys_on_rows=False, name="attn_mla_s")
    xs = _moe(xs, o, mods, 0, wp0, mod_row=sample_row_post, final=False)

    wg = _gqa_weights(gqa_w_qkv[0], gqa_q_norm[0], gqa_k_norm[0])
    gn1 = norm_mix[1].reshape(1, D)
    q, k, v, kv_p = _gqa_proj(xp, mods, gn1, wg, n_batch=1, t_new=n_p, tm=tm_proj, mod_row=prompt_row, t_state=tp)
    o = _attention(q, k, v, mla=False, n_batch=bp, t_q=tp, t_k=tp, tq=tp, n_pairs=8, keys_on_rows=False, name="attn_gqa_p")
    wp1 = _post_weights(1, gqa_w_o[0], *moe_args)
    y_prompt = _moe(xp, o, mods, 1, wp1, mod_row=prompt_row, final=True)

    rope_gqa = _rope_tables(ts, GQA_HEAD_DIM, LANES // GQA_HEAD_DIM)
    cache1 = jnp.transpose(cache_gqa[:, 0], (0, 2, 3, 4, 1)).reshape(bs, 2, GQA_KV_HEADS * GQA_HEAD_DIM, t_past)
    q, k, v = _gqa_proj(xs, mods, gn1, wg, n_batch=bs, t_new=ts, tm=tm_proj, mod_row=sample_row_proj,
                        rope_tabs=rope_gqa, cache=cache1)
    o = _attention(q, k, v, mla=False, n_batch=bs, t_q=ts, t_k=t_past + ts, tq=ATTN_TQ, n_pairs=4,
                   keys_on_rows=True, name="attn_gqa_s")
    y_sample = _moe(xs, o, mods, 1, wp1, mod_row=sample_row_post, final=True)

    return (y_prompt.reshape(bp, tp, D), y_sample.reshape(bs, ts, D),
            jnp.transpose(keys_p, (0, 2, 1))[:, None],
            jnp.transpose(kv_p.reshape(bp, 2, GQA_KV_HEADS, GQA_HEAD_DIM, tp), (0, 4, 1, 2, 3))[:, None])
```

```python
import functools
import math

import jax
import jax.numpy as jnp
from jax import lax
from jax.experimental import pallas as pl
from jax.experimental.pallas import tpu as pltpu

F32 = jnp.float32
BF16 = jnp.bfloat16

D = 1024
EPS = 1e-6
GRID_W = 64
ROPE_THETA = 10000.0
LANES = 128
HALF = LANES // 2
MLA_HEADS = 16
MLA_NOPE = 64
MLA_ROPE = 32
MLA_V = 64
Q_LORA = 384
KV_LORA = 256
KV_DIM = KV_LORA + MLA_ROPE
KV_PAD = 384
MLA_SCALE = 1.0 / math.sqrt(MLA_NOPE + MLA_ROPE)
GQA_HEADS = 16
GQA_KV_HEADS = 4
GQA_HEAD_DIM = 64
GQA_SCALE = 1.0 / math.sqrt(GQA_HEAD_DIM)
N_GROUPS = 4
EXPERTS_PER_GROUP = 4
N_EXPERTS = 16
EXPERT_FF = 256
NEG = -3.0e38
LOG2E = 1.4426950408889634
ATTN_KEY_CHUNK = 512
ATTN_LOOKAHEAD = 1
ATTN_TQ = 512
PROMPT_REQUESTS_PER_STEP = 4
PROJ_TM = 512

VMEM_LIMIT = 56 * 1024 * 1024
N_MOD_ROWS = 8


def _params(n_axes):
    return pltpu.CompilerParams(dimension_semantics=("arbitrary",) * n_axes,
                                vmem_limit_bytes=VMEM_LIMIT)


def _rms(x, g):
    ms = jnp.mean(x * x, axis=-1, keepdims=True)
    return x * lax.rsqrt(ms + EPS) * g


def _mod_index(layer, which, row):
    return (layer * 6 + which) * N_MOD_ROWS + row


def _mod_spec(layer, which, row_fn):
    return pl.BlockSpec((None, 1, D), lambda *g: (_mod_index(layer, which, row_fn(*g)), 0, 0))


def _full_spec(shape):
    n = len(shape)
    return pl.BlockSpec(shape, lambda *g: (0,) * n)


def _ada_kernel(c_ref, w_ref, b_ref, o_ref):
    c = c_ref[...]
    a = c / (1.0 + jnp.exp(-c))
    o_ref[...] = jnp.dot(a, w_ref[...], precision=lax.Precision.HIGHEST,
                         preferred_element_type=F32) + b_ref[...]


def _ada_table(cc, ada_w, ada_b):
    depth = ada_w.shape[0]
    out = pl.pallas_call(
        _ada_kernel,
        grid=(depth, 6),
        in_specs=[
            pl.BlockSpec((N_MOD_ROWS, D), lambda l, n: (0, 0)),
            pl.BlockSpec((None, D, D), lambda l, n: (l, 0, n)),
            pl.BlockSpec((None, 1, D), lambda l, n: (l, 0, n)),
        ],
        out_specs=pl.BlockSpec((None, None, N_MOD_ROWS, D), lambda l, n: (l, n, 0, 0)),
        out_shape=jax.ShapeDtypeStruct((depth, 6, N_MOD_ROWS, D), F32),
        compiler_params=_params(2),
        name="ada",
    )(cc, ada_w, ada_b.reshape(depth, 1, 6 * D))
    return out.reshape(depth * 6 * N_MOD_ROWS, 1, D)


def _rope(a, cos, sin_signed):
    lane = lax.broadcasted_iota(jnp.int32, a.shape, 1)
    nxt = pltpu.roll(a, LANES - 1, 1)
    prv = pltpu.roll(a, 1, 1)
    return a * cos + jnp.where((lane & 1) == 0, nxt, prv) * sin_signed


def _rope_tables(n_tokens, rot_dim, n_rep):
    t = jnp.arange(n_tokens)
    row = (t // GRID_W).astype(F32)
    col = (t % GRID_W).astype(F32)
    axis_dim = rot_dim // 2
    inv = jnp.power(ROPE_THETA, -jnp.arange(0, axis_dim, 2, dtype=F32) / axis_dim)
    ang = jnp.concatenate([row[:, None] * inv, col[:, None] * inv], axis=-1)
    cos = jnp.repeat(jnp.cos(ang), 2, axis=1)
    sin = jnp.repeat(jnp.sin(ang), 2, axis=1)
    sign = jnp.where(jnp.arange(rot_dim) % 2 == 0, -1.0, 1.0).astype(F32)
    sin = sin * sign
    rest = LANES - n_rep * rot_dim
    cos_t = jnp.concatenate([cos] * n_rep + [jnp.ones((n_tokens, rest), F32)], axis=1)
    sin_t = jnp.concatenate([sin] * n_rep + [jnp.zeros((n_tokens, rest), F32)], axis=1)
    return cos_t, sin_t


def _mla_proj_kernel(*refs, rope, n_cache, emit_keys):
    it = iter(refs)
    x_ref, sh_ref, sc_ref, gn_ref = next(it), next(it), next(it), next(it)
    wdq_ref, gq_ref, wuq_ref, wdkv_ref, gkv_ref, wkexp_ref, wuv_ref = (next(it) for _ in range(7))
    cos_ref = sin_ref = cache_ref = keys_ref = None
    if rope:
        cos_ref, sin_ref = next(it), next(it)
    if n_cache:
        cache_ref = next(it)
    q_ref, k_ref, v_ref = next(it), next(it), next(it)
    if emit_keys:
        keys_ref = next(it)

    def expand(ckv, pe):
        ckv = ckv.astype(BF16)
        kn = jnp.dot(ckv, wkexp_ref[...], preferred_element_type=F32)
        k_ref[...] = jnp.concatenate(
            [kn[:, h_ * LANES:(h_ + 1) * LANES] + pe for h_ in range(MLA_HEADS)], axis=1).astype(BF16)
        v_ref[...] = jnp.dot(ckv, wuv_ref[...], preferred_element_type=F32).astype(BF16)

    def new_tokens():
        x = x_ref[...]
        h = (_rms(x, gn_ref[...]) * (1.0 + sc_ref[...]) + sh_ref[...]).astype(BF16)
        ql = jnp.dot(h, wdq_ref[...], preferred_element_type=F32)
        qn = _rms(ql, gq_ref[...]).astype(BF16)
        q = jnp.dot(qn, wuq_ref[...], preferred_element_type=F32)
        kv = jnp.dot(h, wdkv_ref[...], preferred_element_type=F32)
        ckv = _rms(kv[:, :KV_LORA], gkv_ref[...])
        pe = kv[:, KV_LORA:]
        if rope:
            cos, sin = cos_ref[...], sin_ref[...]
            pe = _rope(pe, cos, sin)
            q = jnp.concatenate(
                [_rope(q[:, h_ * LANES:(h_ + 1) * LANES], cos, sin) for h_ in range(MLA_HEADS)], axis=1)
        q_ref[...] = (q * (MLA_SCALE * LOG2E)).astype(BF16)
        if emit_keys:
            t_req = keys_ref.shape[-1]
            for r in range(keys_ref.shape[0]):
                rows = slice(r * t_req, (r + 1) * t_req)
                keys_ref[r, 0:KV_LORA, :] = ckv[rows, :].T
                keys_ref[r, KV_LORA:KV_DIM, :] = pe[rows, :].T[0:MLA_ROPE, :]
        expand(ckv, pe)

    if n_cache:
        t = pl.program_id(1)
        pl.when(t >= n_cache)(new_tokens)

        @pl.when(t < n_cache)
        def _():
            pe_t = jnp.concatenate([cache_ref[KV_LORA:KV_DIM, :],
                                    jnp.zeros((LANES - MLA_ROPE, cache_ref.shape[-1]), F32)], axis=0)
            expand(cache_ref[0:KV_LORA, :].T, pe_t.T)
    else:
        new_tokens()


def _mla_proj(x2d, mods, gn, w, *, n_batch, t_new, tm, mod_row, rope_tabs=None, cache=None, t_state=None):
    rope = rope_tabs is not None
    t_cache = 0 if cache is None else cache.shape[-1]
    n_cache = t_cache // tm
    ntn = t_new // tm
    nt = n_cache + ntn
    emit_keys = cache is None

    def new_idx(b, t):
        return b * ntn + jnp.maximum(t - n_cache, 0)

    row_fn = lambda b, t: mod_row(b)
    in_specs = [
        pl.BlockSpec((tm, D), lambda b, t: (new_idx(b, t), 0)),
        _mod_spec(0, 0, row_fn), _mod_spec(0, 1, row_fn),
        _full_spec((1, D)),
        _full_spec((D, Q_LORA)), _full_spec((1, Q_LORA)), _full_spec((Q_LORA, MLA_HEADS * LANES)),
        _full_spec((D, KV_PAD)), _full_spec((1, KV_LORA)),
        _full_spec((KV_LORA, MLA_HEADS * LANES)), _full_spec((KV_LORA, MLA_HEADS * MLA_V)),
    ]
    args = [x2d, mods, mods, gn, w["wdq"], w["gq"], w["wuq"], w["wdkv"], w["gkv"], w["wkexp"], w["wuv"]]
    if rope:
        in_specs += [pl.BlockSpec((tm, LANES), lambda b, t: (jnp.maximum(t - n_cache, 0), 0))] * 2
        args += list(rope_tabs)
    if n_cache:
        in_specs.append(pl.BlockSpec((None, KV_DIM, tm), lambda b, t: (b, 0, jnp.minimum(t, n_cache - 1))))
        args.append(cache)
    n_new = n_batch * t_new
    n_keys = n_batch * (t_cache + t_new)
    out_specs = [
        pl.BlockSpec((tm, MLA_HEADS * LANES), lambda b, t: (new_idx(b, t), 0)),
        pl.BlockSpec((tm, MLA_HEADS * LANES), lambda b, t: (b * nt + t, 0)),
        pl.BlockSpec((tm, MLA_HEADS * MLA_V), lambda b, t: (b * nt + t, 0)),
    ]
    out_shape = [
        jax.ShapeDtypeStruct((n_new, MLA_HEADS * LANES), BF16),
        jax.ShapeDtypeStruct((n_keys, MLA_HEADS * LANES), BF16),
        jax.ShapeDtypeStruct((n_keys, MLA_HEADS * MLA_V), BF16),
    ]
    if emit_keys:
        assert tm % t_state == 0
        out_specs.append(pl.BlockSpec((tm // t_state, KV_DIM, t_state), lambda b, t: (new_idx(b, t), 0, 0)))
        out_shape.append(jax.ShapeDtypeStruct((n_new // t_state, KV_DIM, t_state), F32))
    return pl.pallas_call(
        functools.partial(_mla_proj_kernel, rope=rope, n_cache=n_cache, emit_keys=emit_keys),
        grid=(n_batch, nt), in_specs=in_specs, out_specs=out_specs, out_shape=out_shape,
        compiler_params=_params(2), name="mla_proj_s" if rope else "mla_proj_p",
    )(*args)


def _dup_halves(a):
    cols = []
    for c in range(a.shape[1] // LANES):
        blk = a[:, c * LANES:(c + 1) * LANES]
        rot = pltpu.roll(blk, HALF, 1)
        low = lax.broadcasted_iota(jnp.int32, blk.shape, 1) < HALF
        cols += [jnp.where(low, blk, rot), jnp.where(low, rot, blk)]
    return jnp.concatenate(cols, axis=1)


def _group_mean_sq(a, bmat):
    sq = a * a
    hi = sq.astype(BF16)
    lo = (sq - hi.astype(F32)).astype(BF16)
    return (jnp.dot(hi, bmat, preferred_element_type=F32) + jnp.dot(lo, bmat, preferred_element_type=F32))


def _gqa_proj_kernel(*refs, rope, n_cache, emit_kv):
    it = iter(refs)
    x_ref, sh_ref, sc_ref, gn_ref, w_ref, gq_ref, gk_ref, bmat_ref = (next(it) for _ in range(8))
    cos_ref = sin_ref = cache_ref = kv_ref = None
    if rope:
        cos_ref, sin_ref = next(it), next(it)
    if n_cache:
        cache_ref = next(it)
    q_ref, k_ref, v_ref = next(it), next(it), next(it)
    if emit_kv:
        kv_ref = next(it)
    nq = GQA_HEADS * GQA_HEAD_DIM
    nk = GQA_KV_HEADS * GQA_HEAD_DIM
    blk = 2 * LANES

    def new_tokens():
        x = x_ref[...]
        h = (_rms(x, gn_ref[...]) * (1.0 + sc_ref[...]) + sh_ref[...]).astype(BF16)
        qkv = jnp.dot(h, w_ref[...], preferred_element_type=F32)
        bmat = bmat_ref[...]

        def head_norm(a, g):
            return a * lax.rsqrt(_group_mean_sq(a, bmat) + EPS) * g

        def maybe_rope(a):
            if not rope:
                return a
            cos, sin = cos_ref[...], sin_ref[...]
            return jnp.concatenate(
                [_rope(a[:, c * LANES:(c + 1) * LANES], cos, sin) for c in range(a.shape[1] // LANES)], axis=1)

        for c in range(nq // blk):
            qb = head_norm(qkv[:, c * blk:(c + 1) * blk], gq_ref[...])
            q_ref[:, c * blk:(c + 1) * blk] = (maybe_rope(qb) * (GQA_SCALE * LOG2E)).astype(BF16)
        kn = head_norm(qkv[:, nq:nq + nk], gk_ref[...])
        vv = qkv[:, nq + nk:]
        if emit_kv:
            t_req = kv_ref.shape[-1]
            for r in range(kv_ref.shape[0]):
                kv_ref[r, 0] = kn[r * t_req:(r + 1) * t_req, :].T
                kv_ref[r, 1] = vv[r * t_req:(r + 1) * t_req, :].T
        k_ref[...] = _dup_halves(maybe_rope(kn)).astype(BF16)
        v_ref[...] = _dup_halves(vv).astype(BF16)

    if n_cache:
        t = pl.program_id(1)
        pl.when(t >= n_cache)(new_tokens)

        @pl.when(t < n_cache)
        def _():
            k_ref[...] = _dup_halves(cache_ref[0].T).astype(BF16)
            v_ref[...] = _dup_halves(cache_ref[1].T).astype(BF16)
    else:
        new_tokens()


def _gqa_proj(x2d, mods, gn, w, *, n_batch, t_new, tm, mod_row, rope_tabs=None, cache=None, t_state=None):
    rope = rope_tabs is not None
    t_cache = 0 if cache is None else cache.shape[-1]
    n_cache = t_cache // tm
    ntn = t_new // tm
    nt = n_cache + ntn
    emit_kv = cache is None
    nq = GQA_HEADS * GQA_HEAD_DIM
    nk = GQA_KV_HEADS * GQA_HEAD_DIM
    nqkv = nq + 2 * nk
    ndup = GQA_KV_HEADS * LANES

    def new_idx(b, t):
        return b * ntn + jnp.maximum(t - n_cache, 0)

    row_fn = lambda b, t: mod_row(b)
    in_specs = [
        pl.BlockSpec((tm, D), lambda b, t: (new_idx(b, t), 0)),
        _mod_spec(1, 0, row_fn), _mod_spec(1, 1, row_fn),
        _full_spec((1, D)), _full_spec((D, nqkv)),
        _full_spec((1, 2 * LANES)), _full_spec((1, 2 * LANES)), _full_spec((2 * LANES, 2 * LANES)),
    ]
    args = [x2d, mods, mods, gn, w["wqkv"], w["gq"], w["gk"], w["bmat"]]
    if rope:
        in_specs += [pl.BlockSpec((tm, LANES), lambda b, t: (jnp.maximum(t - n_cache, 0), 0))] * 2
        args += list(rope_tabs)
    if n_cache:
        in_specs.append(pl.BlockSpec((None, 2, nk, tm), lambda b, t: (b, 0, 0, jnp.minimum(t, n_cache - 1))))
        args.append(cache)
    n_new = n_batch * t_new
    n_keys = n_batch * (t_cache + t_new)
    out_specs = [
        pl.BlockSpec((tm, nq), lambda b, t: (new_idx(b, t), 0)),
        pl.BlockSpec((tm, ndup), lambda b, t: (b * nt + t, 0)),
        pl.BlockSpec((tm, ndup), lambda b, t: (b * nt + t, 0)),
    ]
    out_shape = [
        jax.ShapeDtypeStruct((n_new, nq), BF16),
        jax.ShapeDtypeStruct((n_keys, ndup), BF16),
        jax.ShapeDtypeStruct((n_keys, ndup), BF16),
    ]
    if emit_kv:
        assert tm % t_state == 0
        out_specs.append(pl.BlockSpec((tm // t_state, 2, nk, t_state), lambda b, t: (new_idx(b, t), 0, 0, 0)))
        out_shape.append(jax.ShapeDtypeStruct((n_new // t_state, 2, nk, t_state), F32))
    return pl.pallas_call(
        functools.partial(_gqa_proj_kernel, rope=rope, n_cache=n_cache, emit_kv=emit_kv),
        grid=(n_batch, nt), in_specs=in_specs, out_specs=out_specs, out_shape=out_shape,
        compiler_params=_params(2), name="gqa_proj_s" if rope else "gqa_proj_p",
    )(*args)


def _attn_kernel(q_ref, k_ref, v_ref, o_ref, *, n_sub, **kw):
    tq, tk = q_ref.shape[0] // n_sub, k_ref.shape[0] // n_sub
    for r in range(n_sub):
        _attend(q_ref.at[pl.ds(r * tq, tq)], k_ref.at[pl.ds(r * tk, tk)], v_ref.at[pl.ds(r * tk, tk)],
                o_ref.at[pl.ds(r * tq, tq)], **kw)


def _attend(q_ref, k_ref, v_ref, o_ref, *, mla, n_pairs, ck, keys_on_rows):
    tq = q_ref.shape[0]
    nc = k_ref.shape[0] // ck
    lane = lax.broadcasted_iota(jnp.int32, (tq, LANES), 1)
    low = lane < HALF
    items = [(p, c, hh) for c in range(nc) for p in range(n_pairs) for hh in range(2)]
    heads = {}
    state = {}

    def head_operands(p, hh):
        if (p, hh) not in heads:
            if mla:
                hd = 2 * p + hh
                heads[(p, hh)] = (q_ref[:, hd * LANES:(hd + 1) * LANES], hd, p)
            else:
                g = p // 2 if n_pairs > 1 else 0
                qp = q_ref[:, p * LANES:(p + 1) * LANES]
                qh = jnp.where(low if hh == 0 else jnp.logical_not(low), qp, jnp.zeros_like(qp))
                heads[(p, hh)] = (qh, g, g)
        return heads[(p, hh)]

    key_axis = 0 if keys_on_rows else -1
    first_half = (lax.broadcasted_iota(jnp.int32, (LANES, tq), 0) < HALF) if keys_on_rows else low

    def scores(item):
        p, c, hh = item
        qh, kc, _ = head_operands(p, hh)
        kh = k_ref[c * ck:(c + 1) * ck, kc * LANES:(kc + 1) * LANES]
        lhs, rhs = (kh, qh) if keys_on_rows else (qh, kh)
        return lax.dot_general(lhs, rhs, (((1,), (1,)), ((), ())), preferred_element_type=F32)

    def weighted_values(e, vb):
        if keys_on_rows:
            return lax.dot_general(vb, e.astype(BF16), (((0,), (0,)), ((), ())), preferred_element_type=F32)
        return jnp.dot(e.astype(BF16), vb, preferred_element_type=F32)

    pending = [scores(item) for item in items[:ATTN_LOOKAHEAD]]
    for idx, (p, c, hh) in enumerate(items):
        if idx + ATTN_LOOKAHEAD < len(items):
            pending.append(scores(items[idx + ATTN_LOOKAHEAD]))
        s_cur = pending.pop(0)
        vc = head_operands(p, hh)[2]
        vb = v_ref[c * ck:(c + 1) * ck, vc * LANES:(vc + 1) * LANES]
        m_c = jnp.max(s_cur, axis=key_axis, keepdims=True)
        if c == 0:
            m = m_c
            e = jnp.exp2(s_cur - m)
            l = jnp.sum(e, axis=key_axis, keepdims=True)
            acc = weighted_values(e, vb)
        else:
            m_old, l_old, acc_old = state[(p, hh)]
            m = jnp.maximum(m_old, m_c)
            alpha = jnp.exp2(m_old - m)
            e = jnp.exp2(s_cur - m)
            l = alpha * l_old + jnp.sum(e, axis=key_axis, keepdims=True)
            acc = alpha * acc_old + weighted_values(e, vb)
        state[(p, hh)] = (m, l, acc)
        if c == nc - 1 and hh == 1:
            o0 = state[(p, 0)][2] / state[(p, 0)][1]
            o1 = state[(p, 1)][2] / state[(p, 1)][1]
            o = jnp.where(first_half, o0, o1)
            o_ref[:, p * LANES:(p + 1) * LANES] = (o.T if keys_on_rows else o).astype(BF16)


def _attention(q, k, v, *, mla, n_batch, t_q, t_k, tq, n_pairs, keys_on_rows, name, n_sub=1):
    total_pairs = 8
    nj = total_pairs // n_pairs
    nqt = t_q // tq
    if mla:
        qw, kw, vw = n_pairs * 2 * LANES, n_pairs * 2 * LANES, n_pairs * LANES
        kv_col = lambda j: j
    else:
        qw = n_pairs * LANES
        kw = vw = max(n_pairs // 2, 1) * LANES
        kv_col = (lambda j: j // 2) if n_pairs == 1 else (lambda j: j)
    assert n_sub == 1 or tq == t_q
    return pl.pallas_call(
        functools.partial(_attn_kernel, n_sub=n_sub, mla=mla, n_pairs=n_pairs, ck=min(t_k, ATTN_KEY_CHUNK),
                          keys_on_rows=keys_on_rows),
        grid=(n_batch // n_sub, nj, nqt),
        in_specs=[
            pl.BlockSpec((n_sub * tq, qw), lambda b, j, i: (b * nqt + i, j)),
            pl.BlockSpec((n_sub * t_k, kw), lambda b, j, i: (b, kv_col(j))),
            pl.BlockSpec((n_sub * t_k, vw), lambda b, j, i: (b, kv_col(j))),
        ],
        out_specs=pl.BlockSpec((n_sub * tq, n_pairs * LANES), lambda b, j, i: (b * nqt + i, j)),
        out_shape=jax.ShapeDtypeStruct((n_batch * t_q, total_pairs * LANES), BF16),
        compiler_params=_params(3), name=name,
    )(q, k, v)


def _route(logits):
    lane = lax.broadcasted_iota(jnp.int32, logits.shape, 1).astype(F32)
    big = jnp.float32(1e9)
    is_grp = (lane >= N_EXPERTS) & (lane < N_EXPERTS + N_GROUPS)
    gl = jnp.where(is_grp, logits, NEG)
    gm = jnp.max(gl, axis=-1, keepdims=True)
    g_w = 1.0 / jnp.sum(jnp.exp(gl - gm), axis=-1, keepdims=True)
    g_idx = jnp.min(jnp.where(gl == gm, lane, big), axis=-1, keepdims=True) - N_EXPERTS
    lo = g_idx * EXPERTS_PER_GROUP
    el = jnp.where((lane >= lo) & (lane < lo + EXPERTS_PER_GROUP), logits, NEG)
    m1 = jnp.max(el, axis=-1, keepdims=True)
    i1 = jnp.min(jnp.where(el == m1, lane, big), axis=-1, keepdims=True)
    el2 = jnp.where(lane == i1, NEG, el)
    m2 = jnp.max(el2, axis=-1, keepdims=True)
    i2 = jnp.min(jnp.where(el2 == m2, lane, big), axis=-1, keepdims=True)
    t = jnp.exp(m2 - m1)
    w1 = g_w / (1.0 + t)
    w2 = g_w * t / (1.0 + t)
    return jnp.where(lane == i1, w1, 0.0) + jnp.where(lane == i2, w2, 0.0), g_idx


MOE_TM = 512
MOE_CHUNK = 144
MOE_SLOTS = 1152
MOE_MAIN_SLOTS = 896
SLOT_RADIX = 32


def _moe_kernel(x_ref, o_ref, wo_ref, gta_ref, shm_ref, scm_ref, gtm_ref, gn_ref,
                wr_hi_ref, wr_lo_ref, br_ref, sel_ref, ltri_ref, wg_ref, wu_ref, wd_ref, fin_ref,
                y_ref, hs_ref, cs_ref, ys_ref, pt_ref, *, final):
    tm = x_ref.shape[0]
    ffg = EXPERTS_PER_GROUP * EXPERT_FF
    mix = jnp.dot(o_ref[...], wo_ref[...], preferred_element_type=F32)
    xm = x_ref[...] + gta_ref[...] * mix
    y_ref[...] = xm
    h = _rms(xm, gn_ref[...]) * (1.0 + scm_ref[...]) + shm_ref[...]
    h_hi = h.astype(BF16)
    h_lo = (h - h_hi.astype(F32)).astype(BF16)
    logits = (jnp.dot(h_hi, wr_hi_ref[...], preferred_element_type=F32)
              + jnp.dot(h_lo, wr_hi_ref[...], preferred_element_type=F32)
              + jnp.dot(h_hi, wr_lo_ref[...], preferred_element_type=F32)) + br_ref[...]
    comb, g_idx = _route(logits)

    lane_i = lax.broadcasted_iota(jnp.int32, (tm, LANES), 1)
    lane = lane_i.astype(F32)
    onehot = jnp.where(lane == g_idx, 1.0, 0.0)
    rank = jnp.dot(ltri_ref[...], onehot.astype(BF16), preferred_element_type=F32)
    cnt = jnp.sum(onehot, axis=0, keepdims=True)
    ends = []
    end = jnp.int32(0)
    start_vec = jnp.zeros_like(lane)
    for g in range(N_GROUPS):
        start_vec = jnp.where(lane_i == g, (end * MOE_CHUNK).astype(F32), start_vec)
        n_g = cnt[0, g].astype(jnp.int32)
        end = end + sum((n_g > k * MOE_CHUNK).astype(jnp.int32) for k in range(-(-tm // MOE_CHUNK)))
        ends.append(end)
    total = ends[-1]
    slot = jnp.sum(onehot * (start_vec + rank), axis=-1, keepdims=True)
    slot_i = slot.astype(jnp.int32)
    pt = jnp.where(lax.broadcasted_iota(jnp.int32, (tm, MOE_SLOTS), 1) == slot_i, 1.0, 0.0).astype(BF16)
    pt_ref[...] = pt
    a = jnp.floor(slot * (1.0 / SLOT_RADIX))
    b = slot - SLOT_RADIX * a
    digits = jnp.where(lane_i == 0, a, jnp.where(lane_i == 1, b, 0.0)).astype(BF16)
    rows = lax.dot_general(sel_ref[...], digits, (((1,), (1,)), ((), ())), preferred_element_type=F32)
    slot_row = (rows[0:1, :] * SLOT_RADIX + rows[1:2, :]).astype(jnp.int32)

    c_hi = comb.astype(BF16)
    r1 = comb - c_hi.astype(F32)
    c_mid = r1.astype(BF16)
    c_lo = (r1 - c_mid.astype(F32)).astype(BF16)
    packed = (c_hi.astype(F32) + pltpu.roll(c_mid.astype(F32), N_EXPERTS, 1)
              + pltpu.roll(c_lo.astype(F32), 2 * N_EXPERTS, 1)).astype(BF16)

    def sort_rows(row0, n_rows):
        p = jnp.where(lax.broadcasted_iota(jnp.int32, (n_rows, tm), 0) + row0 == slot_row, 1.0, 0.0).astype(BF16)
        hs_ref[row0:row0 + n_rows, :] = jnp.dot(p, h_hi, preferred_element_type=F32).astype(BF16)
        cs = jnp.dot(p, packed, preferred_element_type=F32)
        cs_ref[row0:row0 + n_rows, :] = (cs + pltpu.roll(cs, LANES - N_EXPERTS, 1)
                                         + pltpu.roll(cs, LANES - 2 * N_EXPERTS, 1))

    tail_live = total * MOE_CHUNK > MOE_MAIN_SLOTS
    sort_rows(0, MOE_MAIN_SLOTS)
    pl.when(tail_live)(functools.partial(sort_rows, MOE_MAIN_SLOTS, MOE_SLOTS - MOE_MAIN_SLOTS))

    def run_experts(c, n_chunks, g_c):
        rows_c = slice(c * MOE_CHUNK, (c + n_chunks) * MOE_CHUNK)
        lane_c = lax.broadcasted_iota(jnp.int32, (n_chunks * MOE_CHUNK, LANES), 1)
        xc = hs_ref[rows_c, :]
        cc = cs_ref[rows_c, :]
        hids = []
        for j in range(EXPERTS_PER_GROUP):
            e = g_c * EXPERTS_PER_GROUP + j
            gt = jnp.dot(xc, wg_ref[e], preferred_element_type=F32)
            up = jnp.dot(xc, wu_ref[e], preferred_element_type=F32)
            cj = jnp.sum(jnp.where(lane_c == e, cc, 0.0), axis=-1, keepdims=True)
            hids.append(((gt / (1.0 + jnp.exp(-gt))) * up * cj).astype(BF16))
        hid = jnp.concatenate(hids, axis=1)
        wd_g = wd_ref[pl.ds(pl.multiple_of(g_c * ffg, ffg), ffg), :]
        ys_ref[rows_c, :] = jnp.dot(hid, wd_g, preferred_element_type=F32).astype(BF16)

    n_slots = MOE_SLOTS // MOE_CHUNK
    for c in range(n_slots):
        g_c = sum((ends[g] <= c).astype(jnp.int32) for g in range(N_GROUPS - 1))
        start_c = jnp.int32(0)
        end_c = ends[0]
        for g in range(1, N_GROUPS):
            start_c = jnp.where(g_c == g, ends[g - 1], start_c)
            end_c = jnp.where(g_c == g, ends[g], end_c)
        live = c < total
        first_of_pair = ((c - start_c) & 1) == 0
        has_partner = c + 1 < end_c

        if c + 1 < n_slots:
            pl.when(live & first_of_pair & has_partner)(functools.partial(run_experts, c, 2, g_c))
        pl.when(live & first_of_pair & jnp.logical_not(has_partner))(functools.partial(run_experts, c, 1, g_c))

        @pl.when(jnp.logical_not(live))
        def _():
            ys_ref[c * MOE_CHUNK:(c + 1) * MOE_CHUNK, :] = jnp.zeros((MOE_CHUNK, D), BF16)

    y_tok = jnp.dot(pt_ref[:, 0:MOE_MAIN_SLOTS], ys_ref[0:MOE_MAIN_SLOTS, :], preferred_element_type=F32)
    y_ref[...] = y_ref[...] + gtm_ref[...] * y_tok

    @pl.when(tail_live)
    def _():
        y_tail = jnp.dot(pt_ref[:, MOE_MAIN_SLOTS:MOE_SLOTS], ys_ref[MOE_MAIN_SLOTS:MOE_SLOTS, :],
                         preferred_element_type=F32)
        y_ref[...] = y_ref[...] + gtm_ref[...] * y_tail

    if final:
        y_ref[...] = _rms(y_ref[...], fin_ref[...])


def _moe(x2d, o2d, mods, layer, w, *, mod_row, final):
    n = x2d.shape[0]
    tm = MOE_TM
    row_fn = lambda i: mod_row(i)
    tile = pl.BlockSpec((tm, D), lambda i: (i, 0))
    once = lambda shape: pl.BlockSpec(shape, lambda i: (0,) * len(shape), pipeline_mode=pl.Buffered(1))
    per_layer = lambda shape: pl.BlockSpec((None,) + shape, lambda i: (layer,) + (0,) * len(shape),
                                           pipeline_mode=pl.Buffered(1))
    return pl.pallas_call(
        functools.partial(_moe_kernel, final=final),
        grid=(n // tm,),
        in_specs=[
            tile, tile, once((D, D)),
            _mod_spec(layer, 2, row_fn), _mod_spec(layer, 3, row_fn), _mod_spec(layer, 4, row_fn),
            _mod_spec(layer, 5, row_fn), _full_spec((1, D)),
            _full_spec((D, LANES)), _full_spec((D, LANES)), _full_spec((1, LANES)), _full_spec((8, LANES)),
            once((tm, tm)),
            per_layer((N_EXPERTS, D, EXPERT_FF)), per_layer((N_EXPERTS, D, EXPERT_FF)),
            per_layer((N_EXPERTS * EXPERT_FF, D)),
            _full_spec((1, D)),
        ],
        out_specs=tile,
        out_shape=jax.ShapeDtypeStruct((n, D), F32),
        scratch_shapes=[pltpu.VMEM((MOE_SLOTS, D), BF16), pltpu.VMEM((MOE_SLOTS, LANES), F32),
                        pltpu.VMEM((MOE_SLOTS, D), BF16), pltpu.VMEM((tm, MOE_SLOTS), BF16)],
        compiler_params=_params(1), name=f"moe_l{layer}",
    )(x2d, o2d, w["wo"], mods, mods, mods, mods, w["gn"], w["wr_hi"], w["wr_lo"], w["br"], w["sel2"],
      w["ltri"], w["wg_bf"], w["wu_bf"], w["wd_bf"], w["fin"])


def _mla_weights(w_dq, g_q, w_uq, w_dkv, g_kv, w_uk, w_uv):
    hd = MLA_NOPE + MLA_ROPE
    wuq = w_uq.reshape(Q_LORA, MLA_HEADS, hd)
    wuq = jnp.concatenate([wuq[..., MLA_NOPE:], wuq[..., :MLA_NOPE],
                           jnp.zeros((Q_LORA, MLA_HEADS, LANES - hd), F32)], axis=-1)
    wuk = jnp.concatenate([jnp.zeros((KV_LORA, MLA_HEADS, MLA_ROPE), F32), w_uk,
                           jnp.zeros((KV_LORA, MLA_HEADS, LANES - hd), F32)], axis=-1)
    return {
        "wdq": w_dq.astype(BF16), "gq": g_q.reshape(1, Q_LORA),
        "wuq": wuq.reshape(Q_LORA, MLA_HEADS * LANES).astype(BF16),
        "wdkv": jnp.pad(w_dkv, ((0, 0), (0, KV_PAD - KV_DIM))).astype(BF16), "gkv": g_kv.reshape(1, KV_LORA),
        "wkexp": wuk.reshape(KV_LORA, MLA_HEADS * LANES).astype(BF16),
        "wuv": w_uv.reshape(KV_LORA, MLA_HEADS * MLA_V).astype(BF16),
    }


def _gqa_weights(w_qkv, g_q, g_k):
    grp = jnp.arange(2 * LANES) // GQA_HEAD_DIM
    bmat = jnp.where(grp[:, None] == grp[None, :], 1.0 / GQA_HEAD_DIM, 0.0).astype(BF16)
    reps = 2 * LANES // GQA_HEAD_DIM
    return {"wqkv": w_qkv.astype(BF16), "gq": jnp.tile(g_q, reps).reshape(1, 2 * LANES),
            "gk": jnp.tile(g_k, reps).reshape(1, 2 * LANES), "bmat": bmat}


def _post_weights(l, w_o, norm_ffn, w_group, b_group, w_exp, b_exp, w_gate, w_up, w_down, final_norm):
    wr = jnp.concatenate([w_exp[l], w_group[l], jnp.zeros((D, LANES - N_EXPERTS - N_GROUPS), F32)], axis=1)
    br = jnp.concatenate([b_exp[l], b_group[l], jnp.zeros((LANES - N_EXPERTS - N_GROUPS,), F32)]).reshape(1, LANES)
    wr_hi = wr.astype(BF16)
    wr_lo = (wr - wr_hi.astype(F32)).astype(BF16)
    return {"wo": w_o.astype(BF16), "gn": norm_ffn[l].reshape(1, D), "wr_hi": wr_hi, "wr_lo": wr_lo, "br": br,
            "wg_bf": w_gate, "wu_bf": w_up, "wd_bf": w_down,
            "fin": final_norm.reshape(1, D),
            "sel2": jnp.zeros((8, LANES), BF16).at[0, 0].set(1.0).at[1, 1].set(1.0),
            "ltri": jnp.tri(MOE_TM, MOE_TM, -1, dtype=BF16)}


def kernel(x_prompt, x_sample, cache_mla, cache_gqa, c, c_ctx, ada_w, ada_b, norm_mix, norm_ffn,
           mla_w_dq, mla_q_norm, mla_w_uq, mla_w_dkv, mla_kv_norm, mla_w_uk, mla_w_uv, mla_w_o,
           gqa_w_qkv, gqa_q_norm, gqa_k_norm, gqa_w_o,
           moe_w_group, moe_b_group, moe_w_expert, moe_b_expert, moe_w_gate, moe_w_up, moe_w_down,
           final_norm):
    bp, tp, _ = x_prompt.shape
    bs, ts, _ = x_sample.shape
    t_past = cache_mla.shape[2]
    assert ada_w.shape[0] == 2 and 1 + bs <= N_MOD_ROWS
    n_p, n_s = bp * tp, bs * ts

    cc = jnp.concatenate([c_ctx[None], c, jnp.zeros((N_MOD_ROWS - 1 - bs, D), F32)], axis=0)
    mods = _ada_table(cc, ada_w, ada_b)

    xp = x_prompt.reshape(n_p, D)
    xs = x_sample.reshape(n_s, D)
    tm_proj = PROJ_TM
    tm_post = MOE_TM
    prompt_row = lambda *_: 0
    sample_row_proj = lambda b: 1 + b
    sample_row_post = lambda i: 1 + i // (ts // tm_post)
    depth = moe_w_gate.shape[0]
    moe_args = (norm_ffn, moe_w_group, moe_b_group, moe_w_expert, moe_b_expert,
                moe_w_gate.astype(BF16), moe_w_up.astype(BF16),
                moe_w_down.reshape(depth, N_EXPERTS * EXPERT_FF, D).astype(BF16), final_norm)

    wm = _mla_weights(mla_w_dq[0], mla_q_norm[0], mla_w_uq[0], mla_w_dkv[0], mla_kv_norm[0],
                      mla_w_uk[0], mla_w_uv[0])
    gn0 = norm_mix[0].reshape(1, D)
    q, k, v, keys_p = _mla_proj(xp, mods, gn0, wm, n_batch=1, t_new=n_p, tm=tm_proj, mod_row=prompt_row, t_state=tp)
    o = _attention(q, k, v, mla=True, n_batch=bp, t_q=tp, t_k=tp, tq=tp, n_pairs=8, keys_on_rows=False, n_sub=PROMPT_REQUESTS_PER_STEP, name="attn_mla_p")
    wp0 = _post_weights(0, mla_w_o[0], *moe_args)
    xp = _moe(xp, o, mods, 0, wp0, mod_row=prompt_row, final=False)

    rope_mla = _rope_tables(ts, MLA_ROPE, 1)
    cache0 = jnp.transpose(cache_mla[:, 0], (0, 2, 1))
    q, k, v = _mla_proj(xs, mods, gn0, wm, n_batch=bs, t_new=ts, tm=tm_proj, mod_row=sample_row_proj,
                        rope_tabs=rope_mla, cache=cache0)
    o = _attention(q, k, v, mla=True, n_batch=bs, t_q=ts, t_k=t_past + ts, tq=ATTN_TQ, n_pairs=1,
                   keys_on_rows=False, name="attn_mla_s")
    xs = _moe(xs, o, mods, 0, wp0, mod_row=sample_row_post, final=False)

    wg = _gqa_weights(gqa_w_qkv[0], gqa_q_norm[0], gqa_k_norm[0])
    gn1 = norm_mix[1].reshape(1, D)
    q, k, v, kv_p = _gqa_proj(xp, mods, gn1, wg, n_batch=1, t_new=n_p, tm=tm_proj, mod_row=prompt_row, t_state=tp)
    o = _attention(q, k, v, mla=False, n_batch=bp, t_q=tp, t_k=tp, tq=tp, n_pairs=8, keys_on_rows=False, n_sub=PROMPT_REQUESTS_PER_STEP, name="attn_gqa_p")
    wp1 = _post_weights(1, gqa_w_o[0], *moe_args)
    y_prompt = _moe(xp, o, mods, 1, wp1, mod_row=prompt_row, final=True)

    rope_gqa = _rope_tables(ts, GQA_HEAD_DIM, LANES // GQA_HEAD_DIM)
    cache1 = jnp.transpose(cache_gqa[:, 0], (0, 2, 3, 4, 1)).reshape(bs, 2, GQA_KV_HEADS * GQA_HEAD_DIM, t_past)
    q, k, v = _gqa_proj(xs, mods, gn1, wg, n_batch=bs, t_new=ts, tm=tm_proj, mod_row=sample_row_proj,
                        rope_tabs=rope_gqa, cache=cache1)
    o = _attention(q, k, v, mla=False, n_batch=bs, t_q=ts, t_k=t_past + ts, tq=ATTN_TQ, n_pairs=4,
                   keys_on_rows=True, name="attn_gqa_s")
    y_sample = _moe(xs, o, mods, 1, wp1, mod_row=sample_row_post, final=True)

    return (y_prompt.reshape(bp, tp, D), y_sample.reshape(bs, ts, D),
            jnp.transpose(keys_p, (0, 2, 1))[:, None],
            jnp.transpose(kv_p.reshape(bp, 2, GQA_KV_HEADS, GQA_HEAD_DIM, tp), (0, 4, 1, 2, 3))[:, None])
```

```python
import functools
import math

import jax
import jax.numpy as jnp
from jax import lax
from jax.experimental import pallas as pl
from jax.experimental.pallas import tpu as pltpu

F32 = jnp.float32
BF16 = jnp.bfloat16

D = 1024
EPS = 1e-6
GRID_W = 64
ROPE_THETA = 10000.0
LANES = 128
HALF = LANES // 2
MLA_HEADS = 16
MLA_NOPE = 64
MLA_ROPE = 32
MLA_V = 64
Q_LORA = 384
KV_LORA = 256
KV_DIM = KV_LORA + MLA_ROPE
KV_PAD = 384
MLA_SCALE = 1.0 / math.sqrt(MLA_NOPE + MLA_ROPE)
GQA_HEADS = 16
GQA_KV_HEADS = 4
GQA_HEAD_DIM = 64
GQA_SCALE = 1.0 / math.sqrt(GQA_HEAD_DIM)
N_GROUPS = 4
EXPERTS_PER_GROUP = 4
N_EXPERTS = 16
EXPERT_FF = 256
NEG = -3.0e38
LOG2E = 1.4426950408889634
ATTN_KEY_CHUNK = 512
ATTN_LOOKAHEAD = 1
ATTN_TQ = 512
PROMPT_REQUESTS_PER_STEP = 4
PROJ_TM = 512

VMEM_LIMIT = 56 * 1024 * 1024
N_MOD_ROWS = 8


def _params(n_axes):
    return pltpu.CompilerParams(dimension_semantics=("arbitrary",) * n_axes,
                                vmem_limit_bytes=VMEM_LIMIT)


def _rms(x, g):
    ms = jnp.mean(x * x, axis=-1, keepdims=True)
    return x * lax.rsqrt(ms + EPS) * g


def _mod_index(layer, which, row):
    return (layer * 6 + which) * N_MOD_ROWS + row


def _mod_spec(layer, which, row_fn):
    return pl.BlockSpec((None, 1, D), lambda *g: (_mod_index(layer, which, row_fn(*g)), 0, 0))


def _full_spec(shape):
    n = len(shape)
    return pl.BlockSpec(shape, lambda *g: (0,) * n)


def _ada_kernel(c_ref, w_ref, b_ref, o_ref):
    c = c_ref[...]
    a = c / (1.0 + jnp.exp(-c))
    o_ref[...] = jnp.dot(a, w_ref[...], precision=lax.Precision.HIGHEST,
                         preferred_element_type=F32) + b_ref[...]


def _ada_table(cc, ada_w, ada_b):
    depth = ada_w.shape[0]
    out = pl.pallas_call(
        _ada_kernel,
        grid=(depth, 6),
        in_specs=[
            pl.BlockSpec((N_MOD_ROWS, D), lambda l, n: (0, 0)),
            pl.BlockSpec((None, D, D), lambda l, n: (l, 0, n)),
            pl.BlockSpec((None, 1, D), lambda l, n: (l, 0, n)),
        ],
        out_specs=pl.BlockSpec((None, None, N_MOD_ROWS, D), lambda l, n: (l, n, 0, 0)),
        out_shape=jax.ShapeDtypeStruct((depth, 6, N_MOD_ROWS, D), F32),
        compiler_params=_params(2),
        name="ada",
    )(cc, ada_w, ada_b.reshape(depth, 1, 6 * D))
    return out.reshape(depth * 6 * N_MOD_ROWS, 1, D)


def _rope(a, cos, sin_signed):
    lane = lax.broadcasted_iota(jnp.int32, a.shape, 1)
    nxt = pltpu.roll(a, LANES - 1, 1)
    prv = pltpu.roll(a, 1, 1)
    return a * cos + jnp.where((lane & 1) == 0, nxt, prv) * sin_signed


def _rope_tables(n_tokens, rot_dim, n_rep):
    t = jnp.arange(n_tokens)
    row = (t // GRID_W).astype(F32)
    col = (t % GRID_W).astype(F32)
    axis_dim = rot_dim // 2
    inv = jnp.power(ROPE_THETA, -jnp.arange(0, axis_dim, 2, dtype=F32) / axis_dim)
    ang = jnp.concatenate([row[:, None] * inv, col[:, None] * inv], axis=-1)
    cos = jnp.repeat(jnp.cos(ang), 2, axis=1)
    sin = jnp.repeat(jnp.sin(ang), 2, axis=1)
    sign = jnp.where(jnp.arange(rot_dim) % 2 == 0, -1.0, 1.0).astype(F32)
    sin = sin * sign
    rest = LANES - n_rep * rot_dim
    cos_t = jnp.concatenate([cos] * n_rep + [jnp.ones((n_tokens, rest), F32)], axis=1)
    sin_t = jnp.concatenate([sin] * n_rep + [jnp.zeros((n_tokens, rest), F32)], axis=1)
    return cos_t, sin_t


def _mla_proj_kernel(*refs, rope, n_cache, emit_keys):
    it = iter(refs)
    x_ref, sh_ref, sc_ref, gn_ref = next(it), next(it), next(it), next(it)
    wdq_ref, gq_ref, wuq_ref, wdkv_ref, gkv_ref, wkexp_ref, wuv_ref = (next(it) for _ in range(7))
    cos_ref = sin_ref = cache_ref = keys_ref = None
    if rope:
        cos_ref, sin_ref = next(it), next(it)
    if n_cache:
        cache_ref = next(it)
    q_ref, k_ref, v_ref = next(it), next(it), next(it)
    if emit_keys:
        keys_ref = next(it)

    def expand(ckv, pe):
        ckv = ckv.astype(BF16)
        kn = jnp.dot(ckv, wkexp_ref[...], preferred_element_type=F32)
        k_ref[...] = jnp.concatenate(
            [kn[:, h_ * LANES:(h_ + 1) * LANES] + pe for h_ in range(MLA_HEADS)], axis=1).astype(BF16)
        v_ref[...] = jnp.dot(ckv, wuv_ref[...], preferred_element_type=F32).astype(BF16)

    def new_tokens():
        x = x_ref[...]
        h = (_rms(x, gn_ref[...]) * (1.0 + sc_ref[...]) + sh_ref[...]).astype(BF16)
        ql = jnp.dot(h, wdq_ref[...], preferred_element_type=F32)
        qn = _rms(ql, gq_ref[...]).astype(BF16)
        q = jnp.dot(qn, wuq_ref[...], preferred_element_type=F32)
        kv = jnp.dot(h, wdkv_ref[...], preferred_element_type=F32)
        ckv = _rms(kv[:, :KV_LORA], gkv_ref[...])
        pe = kv[:, KV_LORA:]
        if rope:
            cos, sin = cos_ref[...], sin_ref[...]
            pe = _rope(pe, cos, sin)
            q = jnp.concatenate(
                [_rope(q[:, h_ * LANES:(h_ + 1) * LANES], cos, sin) for h_ in range(MLA_HEADS)], axis=1)
        q_ref[...] = (q * (MLA_SCALE * LOG2E)).astype(BF16)
        if emit_keys:
            t_req = keys_ref.shape[-1]
            for r in range(keys_ref.shape[0]):
                rows = slice(r * t_req, (r + 1) * t_req)
                keys_ref[r, 0:KV_LORA, :] = ckv[rows, :].T
                keys_ref[r, KV_LORA:KV_DIM, :] = pe[rows, :].T[0:MLA_ROPE, :]
        expand(ckv, pe)

    if n_cache:
        t = pl.program_id(1)
        pl.when(t >= n_cache)(new_tokens)

        @pl.when(t < n_cache)
        def _():
            pe_t = jnp.concatenate([cache_ref[KV_LORA:KV_DIM, :],
                                    jnp.zeros((LANES - MLA_ROPE, cache_ref.shape[-1]), F32)], axis=0)
            expand(cache_ref[0:KV_LORA, :].T, pe_t.T)
    else:
        new_tokens()


def _mla_proj(x2d, mods, gn, w, *, n_batch, t_new, tm, mod_row, rope_tabs=None, cache=None, t_state=None):
    rope = rope_tabs is not None
    t_cache = 0 if cache is None else cache.shape[-1]
    n_cache = t_cache // tm
    ntn = t_new // tm
    nt = n_cache + ntn
    emit_keys = cache is None

    def new_idx(b, t):
        return b * ntn + jnp.maximum(t - n_cache, 0)

    row_fn = lambda b, t: mod_row(b)
    in_specs = [
        pl.BlockSpec((tm, D), lambda b, t: (new_idx(b, t), 0)),
        _mod_spec(0, 0, row_fn), _mod_spec(0, 1, row_fn),
        _full_spec((1, D)),
        _full_spec((D, Q_LORA)), _full_spec((1, Q_LORA)), _full_spec((Q_LORA, MLA_HEADS * LANES)),
        _full_spec((D, KV_PAD)), _full_spec((1, KV_LORA)),
        _full_spec((KV_LORA, MLA_HEADS * LANES)), _full_spec((KV_LORA, MLA_HEADS * MLA_V)),
    ]
    args = [x2d, mods, mods, gn, w["wdq"], w["gq"], w["wuq"], w["wdkv"], w["gkv"], w["wkexp"], w["wuv"]]
    if rope:
        in_specs += [pl.BlockSpec((tm, LANES), lambda b, t: (jnp.maximum(t - n_cache, 0), 0))] * 2
        args += list(rope_tabs)
    if n_cache:
        in_specs.append(pl.BlockSpec((None, KV_DIM, tm), lambda b, t: (b, 0, jnp.minimum(t, n_cache - 1))))
        args.append(cache)
    n_new = n_batch * t_new
    n_keys = n_batch * (t_cache + t_new)
    out_specs = [
        pl.BlockSpec((tm, MLA_HEADS * LANES), lambda b, t: (new_idx(b, t), 0)),
        pl.BlockSpec((tm, MLA_HEADS * LANES), lambda b, t: (b * nt + t, 0)),
        pl.BlockSpec((tm, MLA_HEADS * MLA_V), lambda b, t: (b * nt + t, 0)),
    ]
    out_shape = [
        jax.ShapeDtypeStruct((n_new, MLA_HEADS * LANES), BF16),
        jax.ShapeDtypeStruct((n_keys, MLA_HEADS * LANES), BF16),
        jax.ShapeDtypeStruct((n_keys, MLA_HEADS * MLA_V), BF16),
    ]
    if emit_keys:
        assert tm % t_state == 0
        out_specs.append(pl.BlockSpec((tm // t_state, KV_DIM, t_state), lambda b, t: (new_idx(b, t), 0, 0)))
        out_shape.append(jax.ShapeDtypeStruct((n_new // t_state, KV_DIM, t_state), F32))
    return pl.pallas_call(
        functools.partial(_mla_proj_kernel, rope=rope, n_cache=n_cache, emit_keys=emit_keys),
        grid=(n_batch, nt), in_specs=in_specs, out_specs=out_specs, out_shape=out_shape,
        compiler_params=_params(2), name="mla_proj_s" if rope else "mla_proj_p",
    )(*args)


def _dup_halves(a):
    cols = []
    for c in range(a.shape[1] // LANES):
        blk = a[:, c * LANES:(c + 1) * LANES]
        rot = pltpu.roll(blk, HALF, 1)
        low = lax.broadcasted_iota(jnp.int32, blk.shape, 1) < HALF
        cols += [jnp.where(low, blk, rot), jnp.where(low, rot, blk)]
    return jnp.concatenate(cols, axis=1)


def _group_mean_sq(a, bmat):
    sq = a * a
    hi = sq.astype(BF16)
    lo = (sq - hi.astype(F32)).astype(BF16)
    return (jnp.dot(hi, bmat, preferred_element_type=F32) + jnp.dot(lo, bmat, preferred_element_type=F32))


def _gqa_proj_kernel(*refs, rope, n_cache, emit_kv):
    it = iter(refs)
    x_ref, sh_ref, sc_ref, gn_ref, w_ref, gq_ref, gk_ref, bmat_ref = (next(it) for _ in range(8))
    cos_ref = sin_ref = cache_ref = kv_ref = None
    if rope:
        cos_ref, sin_ref = next(it), next(it)
    if n_cache:
        cache_ref = next(it)
    q_ref, k_ref, v_ref = next(it), next(it), next(it)
    if emit_kv:
        kv_ref = next(it)
    nq = GQA_HEADS * GQA_HEAD_DIM
    nk = GQA_KV_HEADS * GQA_HEAD_DIM
    blk = 2 * LANES

    def new_tokens():
        x = x_ref[...]
        h = (_rms(x, gn_ref[...]) * (1.0 + sc_ref[...]) + sh_ref[...]).astype(BF16)
        qkv = jnp.dot(h, w_ref[...], preferred_element_type=F32)
        bmat = bmat_ref[...]

        def head_norm(a, g):
            return a * lax.rsqrt(_group_mean_sq(a, bmat) + EPS) * g

        def maybe_rope(a):
            if not rope:
                return a
            cos, sin = cos_ref[...], sin_ref[...]
            return jnp.concatenate(
                [_rope(a[:, c * LANES:(c + 1) * LANES], cos, sin) for c in range(a.shape[1] // LANES)], axis=1)

        for c in range(nq // blk):
            qb = head_norm(qkv[:, c * blk:(c + 1) * blk], gq_ref[...])
            q_ref[:, c * blk:(c + 1) * blk] = (maybe_rope(qb) * (GQA_SCALE * LOG2E)).astype(BF16)
        kn = head_norm(qkv[:, nq:nq + nk], gk_ref[...])
        vv = qkv[:, nq + nk:]
        if emit_kv:
            t_req = kv_ref.shape[-1]
            for r in range(kv_ref.shape[0]):
                kv_ref[r, 0] = kn[r * t_req:(r + 1) * t_req, :].T
                kv_ref[r, 1] = vv[r * t_req:(r + 1) * t_req, :].T
        k_ref[...] = _dup_halves(maybe_rope(kn)).astype(BF16)
        v_ref[...] = _dup_halves(vv).astype(BF16)

    if n_cache:
        t = pl.program_id(1)
        pl.when(t >= n_cache)(new_tokens)

        @pl.when(t < n_cache)
        def _():
            k_ref[...] = _dup_halves(cache_ref[0].T).astype(BF16)
            v_ref[...] = _dup_halves(cache_ref[1].T).astype(BF16)
    else:
        new_tokens()


def _gqa_proj(x2d, mods, gn, w, *, n_batch, t_new, tm, mod_row, rope_tabs=None, cache=None, t_state=None):
    rope = rope_tabs is not None
    t_cache = 0 if cache is None else cache.shape[-1]
    n_cache = t_cache // tm
    ntn = t_new // tm
    nt = n_cache + ntn
    emit_kv = cache is None
    nq = GQA_HEADS * GQA_HEAD_DIM
    nk = GQA_KV_HEADS * GQA_HEAD_DIM
    nqkv = nq + 2 * nk
    ndup = GQA_KV_HEADS * LANES

    def new_idx(b, t):
        return b * ntn + jnp.maximum(t - n_cache, 0)

    row_fn = lambda b, t: mod_row(b)
    in_specs = [
        pl.BlockSpec((tm, D), lambda b, t: (new_idx(b, t), 0)),
        _mod_spec(1, 0, row_fn), _mod_spec(1, 1, row_fn),
        _full_spec((1, D)), _full_spec((D, nqkv)),
        _full_spec((1, 2 * LANES)), _full_spec((1, 2 * LANES)), _full_spec((2 * LANES, 2 * LANES)),
    ]
    args = [x2d, mods, mods, gn, w["wqkv"], w["gq"], w["gk"], w["bmat"]]
    if rope:
        in_specs += [pl.BlockSpec((tm, LANES), lambda b, t: (jnp.maximum(t - n_cache, 0), 0))] * 2
        args += list(rope_tabs)
    if n_cache:
        in_specs.append(pl.BlockSpec((None, 2, nk, tm), lambda b, t: (b, 0, 0, jnp.minimum(t, n_cache - 1))))
        args.append(cache)
    n_new = n_batch * t_new
    n_keys = n_batch * (t_cache + t_new)
    out_specs = [
        pl.BlockSpec((tm, nq), lambda b, t: (new_idx(b, t), 0)),
        pl.BlockSpec((tm, ndup), lambda b, t: (b * nt + t, 0)),
        pl.BlockSpec((tm, ndup), lambda b, t: (b * nt + t, 0)),
    ]
    out_shape = [
        jax.ShapeDtypeStruct((n_new, nq), BF16),
        jax.ShapeDtypeStruct((n_keys, ndup), BF16),
        jax.ShapeDtypeStruct((n_keys, ndup), BF16),
    ]
    if emit_kv:
        assert tm % t_state == 0
        out_specs.append(pl.BlockSpec((tm // t_state, 2, nk, t_state), lambda b, t: (new_idx(b, t), 0, 0, 0)))
        out_shape.append(jax.ShapeDtypeStruct((n_new // t_state, 2, nk, t_state), F32))
    return pl.pallas_call(
        functools.partial(_gqa_proj_kernel, rope=rope, n_cache=n_cache, emit_kv=emit_kv),
        grid=(n_batch, nt), in_specs=in_specs, out_specs=out_specs, out_shape=out_shape,
        compiler_params=_params(2), name="gqa_proj_s" if rope else "gqa_proj_p",
    )(*args)


def _attn_kernel(q_ref, k_ref, v_ref, o_ref, *, n_sub, **kw):
    tq, tk = q_ref.shape[0] // n_sub, k_ref.shape[0] // n_sub
    for r in range(n_sub):
        _attend(q_ref.at[pl.ds(r * tq, tq)], k_ref.at[pl.ds(r * tk, tk)], v_ref.at[pl.ds(r * tk, tk)],
                o_ref.at[pl.ds(r * tq, tq)], **kw)


def _attend(q_ref, k_ref, v_ref, o_ref, *, mla, n_pairs, ck, keys_on_rows):
    tq = q_ref.shape[0]
    nc = k_ref.shape[0] // ck
    lane = lax.broadcasted_iota(jnp.int32, (tq, LANES), 1)
    low = lane < HALF
    items = [(p, c, hh) for c in range(nc) for p in range(n_pairs) for hh in range(2)]
    heads = {}
    state = {}

    def head_operands(p, hh):
        if (p, hh) not in heads:
            if mla:
                hd = 2 * p + hh
                heads[(p, hh)] = (q_ref[:, hd * LANES:(hd + 1) * LANES], hd, p)
            else:
                g = p // 2 if n_pairs > 1 else 0
                qp = q_ref[:, p * LANES:(p + 1) * LANES]
                qh = jnp.where(low if hh == 0 else jnp.logical_not(low), qp, jnp.zeros_like(qp))
                heads[(p, hh)] = (qh, g, g)
        return heads[(p, hh)]

    key_axis = 0 if keys_on_rows else -1
    first_half = (lax.broadcasted_iota(jnp.int32, (LANES, tq), 0) < HALF) if keys_on_rows else low

    def scores(item):
        p, c, hh = item
        qh, kc, _ = head_operands(p, hh)
        kh = k_ref[c * ck:(c + 1) * ck, kc * LANES:(kc + 1) * LANES]
        lhs, rhs = (kh, qh) if keys_on_rows else (qh, kh)
        return lax.dot_general(lhs, rhs, (((1,), (1,)), ((), ())), preferred_element_type=F32)

    def weighted_values(e, vb):
        if keys_on_rows:
            return lax.dot_general(vb, e.astype(BF16), (((0,), (0,)), ((), ())), preferred_element_type=F32)
        return jnp.dot(e.astype(BF16), vb, preferred_element_type=F32)

    pending = [scores(item) for item in items[:ATTN_LOOKAHEAD]]
    for idx, (p, c, hh) in enumerate(items):
        if idx + ATTN_LOOKAHEAD < len(items):
            pending.append(scores(items[idx + ATTN_LOOKAHEAD]))
        s_cur = pending.pop(0)
        vc = head_operands(p, hh)[2]
        vb = v_ref[c * ck:(c + 1) * ck, vc * LANES:(vc + 1) * LANES]
        m_c = jnp.max(s_cur, axis=key_axis, keepdims=True)
        if c == 0:
            m = m_c
            e = jnp.exp2(s_cur - m)
            l = jnp.sum(e, axis=key_axis, keepdims=True)
            acc = weighted_values(e, vb)
        else:
            m_old, l_old, acc_old = state[(p, hh)]
            m = jnp.maximum(m_old, m_c)
            alpha = jnp.exp2(m_old - m)
            e = jnp.exp2(s_cur - m)
            l = alpha * l_old + jnp.sum(e, axis=key_axis, keepdims=True)
            acc = alpha * acc_old + weighted_values(e, vb)
        state[(p, hh)] = (m, l, acc)
        if c == nc - 1 and hh == 1:
            o0 = state[(p, 0)][2] / state[(p, 0)][1]
            o1 = state[(p, 1)][2] / state[(p, 1)][1]
            o = jnp.where(first_half, o0, o1)
            o_ref[:, p * LANES:(p + 1) * LANES] = (o.T if keys_on_rows else o).astype(BF16)


def _attention(q, k, v, *, mla, n_batch, t_q, t_k, tq, n_pairs, keys_on_rows, name, n_sub=1):
    total_pairs = 8
    nj = total_pairs // n_pairs
    nqt = t_q // tq
    if mla:
        qw, kw, vw = n_pairs * 2 * LANES, n_pairs * 2 * LANES, n_pairs * LANES
        kv_col = lambda j: j
    else:
        qw = n_pairs * LANES
        kw = vw = max(n_pairs // 2, 1) * LANES
        kv_col = (lambda j: j // 2) if n_pairs == 1 else (lambda j: j)
    assert n_sub == 1 or tq == t_q
    return pl.pallas_call(
        functools.partial(_attn_kernel, n_sub=n_sub, mla=mla, n_pairs=n_pairs, ck=min(t_k, ATTN_KEY_CHUNK),
                          keys_on_rows=keys_on_rows),
        grid=(n_batch // n_sub, nj, nqt),
        in_specs=[
            pl.BlockSpec((n_sub * tq, qw), lambda b, j, i: (b * nqt + i, j)),
            pl.BlockSpec((n_sub * t_k, kw), lambda b, j, i: (b, kv_col(j))),
            pl.BlockSpec((n_sub * t_k, vw), lambda b, j, i: (b, kv_col(j))),
        ],
        out_specs=pl.BlockSpec((n_sub * tq, n_pairs * LANES), lambda b, j, i: (b * nqt + i, j)),
        out_shape=jax.ShapeDtypeStruct((n_batch * t_q, total_pairs * LANES), BF16),
        compiler_params=_params(3), name=name,
    )(q, k, v)


def _route(logits):
    lane = lax.broadcasted_iota(jnp.int32, logits.shape, 1).astype(F32)
    big = jnp.float32(1e9)
    is_grp = (lane >= N_EXPERTS) & (lane < N_EXPERTS + N_GROUPS)
    gl = jnp.where(is_grp, logits, NEG)
    gm = jnp.max(gl, axis=-1, keepdims=True)
    g_w = 1.0 / jnp.sum(jnp.exp(gl - gm), axis=-1, keepdims=True)
    g_idx = jnp.min(jnp.where(gl == gm, lane, big), axis=-1, keepdims=True) - N_EXPERTS
    lo = g_idx * EXPERTS_PER_GROUP
    el = jnp.where((lane >= lo) & (lane < lo + EXPERTS_PER_GROUP), logits, NEG)
    m1 = jnp.max(el, axis=-1, keepdims=True)
    i1 = jnp.min(jnp.where(el == m1, lane, big), axis=-1, keepdims=True)
    el2 = jnp.where(lane == i1, NEG, el)
    m2 = jnp.max(el2, axis=-1, keepdims=True)
    i2 = jnp.min(jnp.where(el2 == m2, lane, big), axis=-1, keepdims=True)
    t = jnp.exp(m2 - m1)
    w1 = g_w / (1.0 + t)
    w2 = g_w * t / (1.0 + t)
    return jnp.where(lane == i1, w1, 0.0) + jnp.where(lane == i2, w2, 0.0), g_idx


MOE_TM = 512
MOE_CHUNK = 144
MOE_SLOTS = 1152
MOE_MAIN_SLOTS = 896
SLOT_RADIX = 32


def _moe_kernel(x_ref, o_ref, wo_ref, gta_ref, shm_ref, scm_ref, gtm_ref, gn_ref,
                wr_hi_ref, wr_lo_ref, br_ref, sel_ref, ltri_ref, wg_ref, wu_ref, wd_ref, fin_ref,
                y_ref, hs_ref, cs_ref, ys_ref, pt_ref, *, final):
    tm = x_ref.shape[0]
    ffg = EXPERTS_PER_GROUP * EXPERT_FF
    mix = jnp.dot(o_ref[...], wo_ref[...], preferred_element_type=F32)
    xm = x_ref[...] + gta_ref[...] * mix
    y_ref[...] = xm
    h = _rms(xm, gn_ref[...]) * (1.0 + scm_ref[...]) + shm_ref[...]
    h_hi = h.astype(BF16)
    h_lo = (h - h_hi.astype(F32)).astype(BF16)
    logits = (jnp.dot(h_hi, wr_hi_ref[...], preferred_element_type=F32)
              + jnp.dot(h_lo, wr_hi_ref[...], preferred_element_type=F32)
              + jnp.dot(h_hi, wr_lo_ref[...], preferred_element_type=F32)) + br_ref[...]
    comb, g_idx = _route(logits)

    lane_i = lax.broadcasted_iota(jnp.int32, (tm, LANES), 1)
    lane = lane_i.astype(F32)
    onehot = jnp.where(lane == g_idx, 1.0, 0.0)
    rank = jnp.dot(ltri_ref[...], onehot.astype(BF16), preferred_element_type=F32)
    cnt = jnp.sum(onehot, axis=0, keepdims=True)
    ends = []
    end = jnp.int32(0)
    start_vec = jnp.zeros_like(lane)
    for g in range(N_GROUPS):
        start_vec = jnp.where(lane_i == g, (end * MOE_CHUNK).astype(F32), start_vec)
        n_g = cnt[0, g].astype(jnp.int32)
        end = end + sum((n_g > k * MOE_CHUNK).astype(jnp.int32) for k in range(-(-tm // MOE_CHUNK)))
        ends.append(end)
    total = ends[-1]
    slot = jnp.sum(onehot * (start_vec + rank), axis=-1, keepdims=True)
    slot_i = slot.astype(jnp.int32)
    pt = jnp.where(lax.broadcasted_iota(jnp.int32, (tm, MOE_SLOTS), 1) == slot_i, 1.0, 0.0).astype(BF16)
    pt_ref[...] = pt
    a = jnp.floor(slot * (1.0 / SLOT_RADIX))
    b = slot - SLOT_RADIX * a
    digits = jnp.where(lane_i == 0, a, jnp.where(lane_i == 1, b, 0.0)).astype(BF16)
    rows = lax.dot_general(sel_ref[...], digits, (((1,), (1,)), ((), ())), preferred_element_type=F32)
    slot_row = (rows[0:1, :] * SLOT_RADIX + rows[1:2, :]).astype(jnp.int32)

    c_hi = comb.astype(BF16)
    r1 = comb - c_hi.astype(F32)
    c_mid = r1.astype(BF16)
    c_lo = (r1 - c_mid.astype(F32)).astype(BF16)
    packed = (c_hi.astype(F32) + pltpu.roll(c_mid.astype(F32), N_EXPERTS, 1)
              + pltpu.roll(c_lo.astype(F32), 2 * N_EXPERTS, 1)).astype(BF16)

    def sort_rows(row0, n_rows):
        p = jnp.where(lax.broadcasted_iota(jnp.int32, (n_rows, tm), 0) + row0 == slot_row, 1.0, 0.0).astype(BF16)
        hs_ref[row0:row0 + n_rows, :] = jnp.dot(p, h_hi, preferred_element_type=F32).astype(BF16)
        cs = jnp.dot(p, packed, preferred_element_type=F32)
        cs_ref[row0:row0 + n_rows, :] = (cs + pltpu.roll(cs, LANES - N_EXPERTS, 1)
                                         + pltpu.roll(cs, LANES - 2 * N_EXPERTS, 1))

    tail_live = total * MOE_CHUNK > MOE_MAIN_SLOTS
    sort_rows(0, MOE_MAIN_SLOTS)
    pl.when(tail_live)(functools.partial(sort_rows, MOE_MAIN_SLOTS, MOE_SLOTS - MOE_MAIN_SLOTS))

    def run_experts(c, n_chunks, g_c):
        rows_c = slice(c * MOE_CHUNK, (c + n_chunks) * MOE_CHUNK)
        lane_c = lax.broadcasted_iota(jnp.int32, (n_chunks * MOE_CHUNK, LANES), 1)
        xc = hs_ref[rows_c, :]
        cc = cs_ref[rows_c, :]
        hids = []
        for j in range(EXPERTS_PER_GROUP):
            e = g_c * EXPERTS_PER_GROUP + j
            gt = jnp.dot(xc, wg_ref[e], preferred_element_type=F32)
            up = jnp.dot(xc, wu_ref[e], preferred_element_type=F32)
            cj = jnp.sum(jnp.where(lane_c == e, cc, 0.0), axis=-1, keepdims=True)
            hids.append(((gt / (1.0 + jnp.exp(-gt))) * up * cj).astype(BF16))
        hid = jnp.concatenate(hids, axis=1)
        wd_g = wd_ref[pl.ds(pl.multiple_of(g_c * ffg, ffg), ffg), :]
        ys_ref[rows_c, :] = jnp.dot(hid, wd_g, preferred_element_type=F32).astype(BF16)

    n_slots = MOE_SLOTS // MOE_CHUNK
    for c in range(n_slots):
        g_c = sum((ends[g] <= c).astype(jnp.int32) for g in range(N_GROUPS - 1))
        start_c = jnp.int32(0)
        end_c = ends[0]
        for g in range(1, N_GROUPS):
            start_c = jnp.where(g_c == g, ends[g - 1], start_c)
            end_c = jnp.where(g_c == g, ends[g], end_c)
        live = c < total
        first_of_pair = ((c - start_c) & 1) == 0
        has_partner = c + 1 < end_c

        if c + 1 < n_slots:
            pl.when(live & first_of_pair & has_partner)(functools.partial(run_experts, c, 2, g_c))
        pl.when(live & first_of_pair & jnp.logical_not(has_partner))(functools.partial(run_experts, c, 1, g_c))

        @pl.when(jnp.logical_not(live))
        def _():
            ys_ref[c * MOE_CHUNK:(c + 1) * MOE_CHUNK, :] = jnp.zeros((MOE_CHUNK, D), BF16)

    y_tok = jnp.dot(pt_ref[:, 0:MOE_MAIN_SLOTS], ys_ref[0:MOE_MAIN_SLOTS, :], preferred_element_type=F32)
    y_ref[...] = y_ref[...] + gtm_ref[...] * y_tok

    @pl.when(tail_live)
    def _():
        y_tail = jnp.dot(pt_ref[:, MOE_MAIN_SLOTS:MOE_SLOTS], ys_ref[MOE_MAIN_SLOTS:MOE_SLOTS, :],
                         preferred_element_type=F32)
        y_ref[...] = y_ref[...] + gtm_ref[...] * y_tail

    if final:
        y_ref[...] = _rms(y_ref[...], fin_ref[...])


def _moe(x2d, o2d, mods, layer, w, *, mod_row, final):
    n = x2d.shape[0]
    tm = MOE_TM
    row_fn = lambda i: mod_row(i)
    tile = pl.BlockSpec((tm, D), lambda i: (i, 0))
    once = lambda shape: pl.BlockSpec(shape, lambda i: (0,) * len(shape), pipeline_mode=pl.Buffered(1))
    per_layer = lambda shape: pl.BlockSpec((None,) + shape, lambda i: (layer,) + (0,) * len(shape),
                                           pipeline_mode=pl.Buffered(1))
    return pl.pallas_call(
        functools.partial(_moe_kernel, final=final),
        grid=(n // tm,),
        in_specs=[
            tile, tile, once((D, D)),
            _mod_spec(layer, 2, row_fn), _mod_spec(layer, 3, row_fn), _mod_spec(layer, 4, row_fn),
            _mod_spec(layer, 5, row_fn), _full_spec((1, D)),
            _full_spec((D, LANES)), _full_spec((D, LANES)), _full_spec((1, LANES)), _full_spec((8, LANES)),
            once((tm, tm)),
            per_layer((N_EXPERTS, D, EXPERT_FF)), per_layer((N_EXPERTS, D, EXPERT_FF)),
            per_layer((N_EXPERTS * EXPERT_FF, D)),
            _full_spec((1, D)),
        ],
        out_specs=tile,
        out_shape=jax.ShapeDtypeStruct((n, D), F32),
        scratch_shapes=[pltpu.VMEM((MOE_SLOTS, D), BF16), pltpu.VMEM((MOE_SLOTS, LANES), F32),
                        pltpu.VMEM((MOE_SLOTS, D), BF16), pltpu.VMEM((tm, MOE_SLOTS), BF16)],
        compiler_params=_params(1), name=f"moe_l{layer}",
    )(x2d, o2d, w["wo"], mods, mods, mods, mods, w["gn"], w["wr_hi"], w["wr_lo"], w["br"], w["sel2"],
      w["ltri"], w["wg_bf"], w["wu_bf"], w["wd_bf"], w["fin"])


def _mla_weights(w_dq, g_q, w_uq, w_dkv, g_kv, w_uk, w_uv):
    hd = MLA_NOPE + MLA_ROPE
    wuq = w_uq.reshape(Q_LORA, MLA_HEADS, hd)
    wuq = jnp.concatenate([wuq[..., MLA_NOPE:], wuq[..., :MLA_NOPE],
                           jnp.zeros((Q_LORA, MLA_HEADS, LANES - hd), F32)], axis=-1)
    wuk = jnp.concatenate([jnp.zeros((KV_LORA, MLA_HEADS, MLA_ROPE), F32), w_uk,
                           jnp.zeros((KV_LORA, MLA_HEADS, LANES - hd), F32)], axis=-1)
    return {
        "wdq": w_dq.astype(BF16), "gq": g_q.reshape(1, Q_LORA),
        "wuq": wuq.reshape(Q_LORA, MLA_HEADS * LANES).astype(BF16),
        "wdkv": jnp.pad(w_dkv, ((0, 0), (0, KV_PAD - KV_DIM))).astype(BF16), "gkv": g_kv.reshape(1, KV_LORA),
        "wkexp": wuk.reshape(KV_LORA, MLA_HEADS * LANES).astype(BF16),
        "wuv": w_uv.reshape(KV_LORA, MLA_HEADS * MLA_V).astype(BF16),
    }


def _gqa_weights(w_qkv, g_q, g_k):
    grp = jnp.arange(2 * LANES) // GQA_HEAD_DIM
    bmat = jnp.where(grp[:, None] == grp[None, :], 1.0 / GQA_HEAD_DIM, 0.0).astype(BF16)
    reps = 2 * LANES // GQA_HEAD_DIM
    return {"wqkv": w_qkv.astype(BF16), "gq": jnp.tile(g_q, reps).reshape(1, 2 * LANES),
            "gk": jnp.tile(g_k, reps).reshape(1, 2 * LANES), "bmat": bmat}


def _post_weights(l, w_o, norm_ffn, w_group, b_group, w_exp, b_exp, w_gate, w_up, w_down, final_norm):
    wr = jnp.concatenate([w_exp[l], w_group[l], jnp.zeros((D, LANES - N_EXPERTS - N_GROUPS), F32)], axis=1)
    br = jnp.concatenate([b_exp[l], b_group[l], jnp.zeros((LANES - N_EXPERTS - N_GROUPS,), F32)]).reshape(1, LANES)
    wr_hi = wr.astype(BF16)
    wr_lo = (wr - wr_hi.astype(F32)).astype(BF16)
    return {"wo": w_o.astype(BF16), "gn": norm_ffn[l].reshape(1, D), "wr_hi": wr_hi, "wr_lo": wr_lo, "br": br,
            "wg_bf": w_gate, "wu_bf": w_up, "wd_bf": w_down,
            "fin": final_norm.reshape(1, D),
            "sel2": jnp.zeros((8, LANES), BF16).at[0, 0].set(1.0).at[1, 1].set(1.0),
            "ltri": jnp.tri(MOE_TM, MOE_TM, -1, dtype=BF16)}


def kernel(x_prompt, x_sample, cache_mla, cache_gqa, c, c_ctx, ada_w, ada_b, norm_mix, norm_ffn,
           mla_w_dq, mla_q_norm, mla_w_uq, mla_w_dkv, mla_kv_norm, mla_w_uk, mla_w_uv, mla_w_o,
           gqa_w_qkv, gqa_q_norm, gqa_k_norm, gqa_w_o,
           moe_w_group, moe_b_group, moe_w_expert, moe_b_expert, moe_w_gate, moe_w_up, moe_w_down,
           final_norm):
    bp, tp, _ = x_prompt.shape
    bs, ts, _ = x_sample.shape
    t_past = cache_mla.shape[2]
    assert ada_w.shape[0] == 2 and 1 + bs <= N_MOD_ROWS
    n_p, n_s = bp * tp, bs * ts

    cc = jnp.concatenate([c_ctx[None], c, jnp.zeros((N_MOD_ROWS - 1 - bs, D), F32)], axis=0)
    mods = _ada_table(cc, ada_w, ada_b)

    xp = x_prompt.reshape(n_p, D)
    xs = x_sample.reshape(n_s, D)
    tm_proj = PROJ_TM
    tm_post = MOE_TM
    prompt_row = lambda *_: 0
    sample_row_proj = lambda b: 1 + b
    sample_row_post = lambda i: 1 + i // (ts // tm_post)
    depth = moe_w_gate.shape[0]
    moe_args = (norm_ffn, moe_w_group, moe_b_group, moe_w_expert, moe_b_expert,
                moe_w_gate.astype(BF16), moe_w_up.astype(BF16),
                moe_w_down.reshape(depth, N_EXPERTS * EXPERT_FF, D).astype(BF16), final_norm)

    wm = _mla_weights(mla_w_dq[0], mla_q_norm[0], mla_w_uq[0], mla_w_dkv[0], mla_kv_norm[0],
                      mla_w_uk[0], mla_w_uv[0])
    gn0 = norm_mix[0].reshape(1, D)
    q, k, v, keys_p = _mla_proj(xp, mods, gn0, wm, n_batch=1, t_new=n_p, tm=tm_proj, mod_row=prompt_row, t_state=tp)
    o = _attention(q, k, v, mla=True, n_batch=bp, t_q=tp, t_k=tp, tq=tp, n_pairs=8, keys_on_rows=False, n_sub=PROMPT_REQUESTS_PER_STEP, name="attn_mla_p")
    wp0 = _post_weights(0, mla_w_o[0], *moe_args)
    xp = _moe(xp, o, mods, 0, wp0, mod_row=prompt_row, final=False)

    rope_mla = _rope_tables(ts, MLA_ROPE, 1)
    cache0 = jnp.transpose(cache_mla[:, 0], (0, 2, 1))
    q, k, v = _mla_proj(xs, mods, gn0, wm, n_batch=bs, t_new=ts, tm=tm_proj, mod_row=sample_row_proj,
                        rope_tabs=rope_mla, cache=cache0)
    o = _attention(q, k, v, mla=True, n_batch=bs, t_q=ts, t_k=t_past + ts, tq=ATTN_TQ, n_pairs=1,
                   keys_on_rows=False, name="attn_mla_s")
    xs = _moe(xs, o, mods, 0, wp0, mod_row=sample_row_post, final=False)

    wg = _gqa_weights(gqa_w_qkv[0], gqa_q_norm[0], gqa_k_norm[0])
    gn1 = norm_mix[1].reshape(1, D)
    q, k, v, kv_p = _gqa_proj(xp, mods, gn1, wg, n_batch=1, t_new=n_p, tm=tm_proj, mod_row=prompt_row, t_state=tp)
    o = _attention(q, k, v, mla=False, n_batch=bp, t_q=tp, t_k=tp, tq=tp, n_pairs=8, keys_on_rows=False, n_sub=PROMPT_REQUESTS_PER_STEP, name="attn_gqa_p")
    wp1 = _post_weights(1, gqa_w_o[0], *moe_args)
    y_prompt = _moe(xp, o, mods, 1, wp1, mod_row=prompt_row, final=True)

    rope_gqa = _rope_tables(ts, GQA_HEAD_DIM, LANES // GQA_HEAD_DIM)
    cache1 = jnp.transpose(cache_gqa[:, 0], (0, 2, 3, 4, 1)).reshape(bs, 2, GQA_KV_HEADS * GQA_HEAD_DIM, t_past)
    q, k, v = _gqa_proj(xs, mods, gn1, wg, n_batch=bs, t_new=ts, tm=tm_proj, mod_row=sample_row_proj,
                        rope_tabs=rope_gqa, cache=cache1)
    o = _attention(q, k, v, mla=False, n_batch=bs, t_q=ts, t_k=t_past + ts, tq=2 * ATTN_TQ, n_pairs=4,
                   keys_on_rows=True, name="attn_gqa_s")
    y_sample = _moe(xs, o, mods, 1, wp1, mod_row=sample_row_post, final=True)

    return (y_prompt.reshape(bp, tp, D), y_sample.reshape(bs, ts, D),
            jnp.transpose(keys_p, (0, 2, 1))[:, None],
            jnp.transpose(kv_p.reshape(bp, 2, GQA_KV_HEADS, GQA_HEAD_DIM, tp), (0, 4, 1, 2, 3))[:, None])
```

```python
import functools
import math

import jax
import jax.numpy as jnp
from jax import lax
from jax.experimental import pallas as pl
from jax.experimental.pallas import tpu as pltpu

F32 = jnp.float32
BF16 = jnp.bfloat16

D = 1024
EPS = 1e-6
GRID_W = 64
ROPE_THETA = 10000.0
LANES = 128
HALF = LANES // 2
MLA_HEADS = 16
MLA_NOPE = 64
MLA_ROPE = 32
MLA_V = 64
Q_LORA = 384
KV_LORA = 256
KV_DIM = KV_LORA + MLA_ROPE
KV_PAD = 384
MLA_SCALE = 1.0 / math.sqrt(MLA_NOPE + MLA_ROPE)
GQA_HEADS = 16
GQA_KV_HEADS = 4
GQA_HEAD_DIM = 64
GQA_SCALE = 1.0 / math.sqrt(GQA_HEAD_DIM)
N_GROUPS = 4
EXPERTS_PER_GROUP = 4
N_EXPERTS = 16
EXPERT_FF = 256
NEG = -3.0e38
LOG2E = 1.4426950408889634
ATTN_KEY_CHUNK = 512
ATTN_LOOKAHEAD = 1
ATTN_TQ = 512
PROMPT_REQUESTS_PER_STEP = 4
PROJ_TM = 512

VMEM_LIMIT = 56 * 1024 * 1024
N_MOD_ROWS = 8


def _params(n_axes):
    return pltpu.CompilerParams(dimension_semantics=("arbitrary",) * n_axes,
                                vmem_limit_bytes=VMEM_LIMIT)


def _rms(x, g):
    ms = jnp.mean(x * x, axis=-1, keepdims=True)
    return x * lax.rsqrt(ms + EPS) * g


def _mod_index(layer, which, row):
    return (layer * 6 + which) * N_MOD_ROWS + row


def _mod_spec(layer, which, row_fn):
    return pl.BlockSpec((None, 1, D), lambda *g: (_mod_index(layer, which, row_fn(*g)), 0, 0))


def _full_spec(shape):
    n = len(shape)
    return pl.BlockSpec(shape, lambda *g: (0,) * n)


def _ada_kernel(c_ref, w_ref, b_ref, o_ref):
    c = c_ref[...]
    a = c / (1.0 + jnp.exp(-c))
    o_ref[...] = jnp.dot(a, w_ref[...], precision=lax.Precision.HIGHEST,
                         preferred_element_type=F32) + b_ref[...]


def _ada_table(cc, ada_w, ada_b):
    depth = ada_w.shape[0]
    out = pl.pallas_call(
        _ada_kernel,
        grid=(depth, 6),
        in_specs=[
            pl.BlockSpec((N_MOD_ROWS, D), lambda l, n: (0, 0)),
            pl.BlockSpec((None, D, D), lambda l, n: (l, 0, n)),
            pl.BlockSpec((None, 1, D), lambda l, n: (l, 0, n)),
        ],
        out_specs=pl.BlockSpec((None, None, N_MOD_ROWS, D), lambda l, n: (l, n, 0, 0)),
        out_shape=jax.ShapeDtypeStruct((depth, 6, N_MOD_ROWS, D), F32),
        compiler_params=_params(2),
        name="ada",
    )(cc, ada_w, ada_b.reshape(depth, 1, 6 * D))
    return out.reshape(depth * 6 * N_MOD_ROWS, 1, D)


def _rope(a, cos, sin_signed):
    lane = lax.broadcasted_iota(jnp.int32, a.shape, 1)
    nxt = pltpu.roll(a, LANES - 1, 1)
    prv = pltpu.roll(a, 1, 1)
    return a * cos + jnp.where((lane & 1) == 0, nxt, prv) * sin_signed


def _rope_tables(n_tokens, rot_dim, n_rep):
    t = jnp.arange(n_tokens)
    row = (t // GRID_W).astype(F32)
    col = (t % GRID_W).astype(F32)
    axis_dim = rot_dim // 2
    inv = jnp.power(ROPE_THETA, -jnp.arange(0, axis_dim, 2, dtype=F32) / axis_dim)
    ang = jnp.concatenate([row[:, None] * inv, col[:, None] * inv], axis=-1)
    cos = jnp.repeat(jnp.cos(ang), 2, axis=1)
    sin = jnp.repeat(jnp.sin(ang), 2, axis=1)
    sign = jnp.where(jnp.arange(rot_dim) % 2 == 0, -1.0, 1.0).astype(F32)
    sin = sin * sign
    rest = LANES - n_rep * rot_dim
    cos_t = jnp.concatenate([cos] * n_rep + [jnp.ones((n_tokens, rest), F32)], axis=1)
    sin_t = jnp.concatenate([sin] * n_rep + [jnp.zeros((n_tokens, rest), F32)], axis=1)
    return cos_t, sin_t


def _mla_proj_kernel(*refs, rope, n_cache, emit_keys):
    it = iter(refs)
    x_ref, sh_ref, sc_ref, gn_ref = next(it), next(it), next(it), next(it)
    wdq_ref, gq_ref, wuq_ref, wdkv_ref, gkv_ref, wkexp_ref, wuv_ref = (next(it) for _ in range(7))
    cos_ref = sin_ref = cache_ref = keys_ref = None
    if rope:
        cos_ref, sin_ref = next(it), next(it)
    if n_cache:
        cache_ref = next(it)
    q_ref, k_ref, v_ref = next(it), next(it), next(it)
    if emit_keys:
        keys_ref = next(it)

    def expand(ckv, pe):
        ckv = ckv.astype(BF16)
        kn = jnp.dot(ckv, wkexp_ref[...], preferred_element_type=F32)
        k_ref[...] = jnp.concatenate(
            [kn[:, h_ * LANES:(h_ + 1) * LANES] + pe for h_ in range(MLA_HEADS)], axis=1).astype(BF16)
        v_ref[...] = jnp.dot(ckv, wuv_ref[...], preferred_element_type=F32).astype(BF16)

    def new_tokens():
        x = x_ref[...]
        h = (_rms(x, gn_ref[...]) * (1.0 + sc_ref[...]) + sh_ref[...]).astype(BF16)
        ql = jnp.dot(h, wdq_ref[...], preferred_element_type=F32)
        qn = _rms(ql, gq_ref[...]).astype(BF16)
        q = jnp.dot(qn, wuq_ref[...], preferred_element_type=F32)
        kv = jnp.dot(h, wdkv_ref[...], preferred_element_type=F32)
        ckv = _rms(kv[:, :KV_LORA], gkv_ref[...])
        pe = kv[:, KV_LORA:]
        if rope:
            cos, sin = cos_ref[...], sin_ref[...]
            pe = _rope(pe, cos, sin)
            q = jnp.concatenate(
                [_rope(q[:, h_ * LANES:(h_ + 1) * LANES], cos, sin) for h_ in range(MLA_HEADS)], axis=1)
        q_ref[...] = (q * (MLA_SCALE * LOG2E)).astype(BF16)
        if emit_keys:
            t_req = keys_ref.shape[-1]
            for r in range(keys_ref.shape[0]):
                rows = slice(r * t_req, (r + 1) * t_req)
                keys_ref[r, 0:KV_LORA, :] = ckv[rows, :].T
                keys_ref[r, KV_LORA:KV_DIM, :] = pe[rows, :].T[0:MLA_ROPE, :]
        expand(ckv, pe)

    if n_cache:
        t = pl.program_id(1)
        pl.when(t >= n_cache)(new_tokens)

        @pl.when(t < n_cache)
        def _():
            pe_t = jnp.concatenate([cache_ref[KV_LORA:KV_DIM, :],
                                    jnp.zeros((LANES - MLA_ROPE, cache_ref.shape[-1]), F32)], axis=0)
            expand(cache_ref[0:KV_LORA, :].T, pe_t.T)
    else:
        new_tokens()


def _mla_proj(x2d, mods, gn, w, *, n_batch, t_new, tm, mod_row, rope_tabs=None, cache=None, t_state=None):
    rope = rope_tabs is not None
    t_cache = 0 if cache is None else cache.shape[-1]
    n_cache = t_cache // tm
    ntn = t_new // tm
    nt = n_cache + ntn
    emit_keys = cache is None

    def new_idx(b, t):
        return b * ntn + jnp.maximum(t - n_cache, 0)

    row_fn = lambda b, t: mod_row(b)
    in_specs = [
        pl.BlockSpec((tm, D), lambda b, t: (new_idx(b, t), 0)),
        _mod_spec(0, 0, row_fn), _mod_spec(0, 1, row_fn),
        _full_spec((1, D)),
        _full_spec((D, Q_LORA)), _full_spec((1, Q_LORA)), _full_spec((Q_LORA, MLA_HEADS * LANES)),
        _full_spec((D, KV_PAD)), _full_spec((1, KV_LORA)),
        _full_spec((KV_LORA, MLA_HEADS * LANES)), _full_spec((KV_LORA, MLA_HEADS * MLA_V)),
    ]
    args = [x2d, mods, mods, gn, w["wdq"], w["gq"], w["wuq"], w["wdkv"], w["gkv"], w["wkexp"], w["wuv"]]
    if rope:
        in_specs += [pl.BlockSpec((tm, LANES), lambda b, t: (jnp.maximum(t - n_cache, 0), 0))] * 2
        args += list(rope_tabs)
    if n_cache:
        in_specs.append(pl.BlockSpec((None, KV_DIM, tm), lambda b, t: (b, 0, jnp.minimum(t, n_cache - 1))))
        args.append(cache)
    n_new = n_batch * t_new
    n_keys = n_batch * (t_cache + t_new)
    out_specs = [
        pl.BlockSpec((tm, MLA_HEADS * LANES), lambda b, t: (new_idx(b, t), 0)),
        pl.BlockSpec((tm, MLA_HEADS * LANES), lambda b, t: (b * nt + t, 0)),
        pl.BlockSpec((tm, MLA_HEADS * MLA_V), lambda b, t: (b * nt + t, 0)),
    ]
    out_shape = [
        jax.ShapeDtypeStruct((n_new, MLA_HEADS * LANES), BF16),
        jax.ShapeDtypeStruct((n_keys, MLA_HEADS * LANES), BF16),
        jax.ShapeDtypeStruct((n_keys, MLA_HEADS * MLA_V), BF16),
    ]
    if emit_keys:
        assert tm % t_state == 0
        out_specs.append(pl.BlockSpec((tm // t_state, KV_DIM, t_state), lambda b, t: (new_idx(b, t), 0, 0)))
        out_shape.append(jax.ShapeDtypeStruct((n_new // t_state, KV_DIM, t_state), F32))
    return pl.pallas_call(
        functools.partial(_mla_proj_kernel, rope=rope, n_cache=n_cache, emit_keys=emit_keys),
        grid=(n_batch, nt), in_specs=in_specs, out_specs=out_specs, out_shape=out_shape,
        compiler_params=_params(2), name="mla_proj_s" if rope else "mla_proj_p",
    )(*args)


def _dup_halves(a):
    cols = []
    for c in range(a.shape[1] // LANES):
        blk = a[:, c * LANES:(c + 1) * LANES]
        rot = pltpu.roll(blk, HALF, 1)
        low = lax.broadcasted_iota(jnp.int32, blk.shape, 1) < HALF
        cols += [jnp.where(low, blk, rot), jnp.where(low, rot, blk)]
    return jnp.concatenate(cols, axis=1)


def _group_mean_sq(a, bmat):
    sq = a * a
    hi = sq.astype(BF16)
    lo = (sq - hi.astype(F32)).astype(BF16)
    return (jnp.dot(hi, bmat, preferred_element_type=F32) + jnp.dot(lo, bmat, preferred_element_type=F32))


def _gqa_proj_kernel(*refs, rope, n_cache, emit_kv):
    it = iter(refs)
    x_ref, sh_ref, sc_ref, gn_ref, w_ref, gq_ref, gk_ref, bmat_ref = (next(it) for _ in range(8))
    cos_ref = sin_ref = cache_ref = kv_ref = None
    if rope:
        cos_ref, sin_ref = next(it), next(it)
    if n_cache:
        cache_ref = next(it)
    q_ref, k_ref, v_ref = next(it), next(it), next(it)
    if emit_kv:
        kv_ref = next(it)
    nq = GQA_HEADS * GQA_HEAD_DIM
    nk = GQA_KV_HEADS * GQA_HEAD_DIM
    blk = 2 * LANES

    def new_tokens():
        x = x_ref[...]
        h = (_rms(x, gn_ref[...]) * (1.0 + sc_ref[...]) + sh_ref[...]).astype(BF16)
        qkv = jnp.dot(h, w_ref[...], preferred_element_type=F32)
        bmat = bmat_ref[...]

        def head_norm(a, g):
            return a * lax.rsqrt(_group_mean_sq(a, bmat) + EPS) * g

        def maybe_rope(a):
            if not rope:
                return a
            cos, sin = cos_ref[...], sin_ref[...]
            return jnp.concatenate(
                [_rope(a[:, c * LANES:(c + 1) * LANES], cos, sin) for c in range(a.shape[1] // LANES)], axis=1)

        for c in range(nq // blk):
            qb = head_norm(qkv[:, c * blk:(c + 1) * blk], gq_ref[...])
            q_ref[:, c * blk:(c + 1) * blk] = (maybe_rope(qb) * (GQA_SCALE * LOG2E)).astype(BF16)
        kn = head_norm(qkv[:, nq:nq + nk], gk_ref[...])
        vv = qkv[:, nq + nk:]
        if emit_kv:
            t_req = kv_ref.shape[-1]
            for r in range(kv_ref.shape[0]):
                kv_ref[r, 0] = kn[r * t_req:(r + 1) * t_req, :].T
                kv_ref[r, 1] = vv[r * t_req:(r + 1) * t_req, :].T
        k_ref[...] = _dup_halves(maybe_rope(kn)).astype(BF16)
        v_ref[...] = _dup_halves(vv).astype(BF16)

    if n_cache:
        t = pl.program_id(1)
        pl.when(t >= n_cache)(new_tokens)

        @pl.when(t < n_cache)
        def _():
            k_ref[...] = _dup_halves(cache_ref[0].T).astype(BF16)
            v_ref[...] = _dup_halves(cache_ref[1].T).astype(BF16)
    else:
        new_tokens()


def _gqa_proj(x2d, mods, gn, w, *, n_batch, t_new, tm, mod_row, rope_tabs=None, cache=None, t_state=None):
    rope = rope_tabs is not None
    t_cache = 0 if cache is None else cache.shape[-1]
    n_cache = t_cache // tm
    ntn = t_new // tm
    nt = n_cache + ntn
    emit_kv = cache is None
    nq = GQA_HEADS * GQA_HEAD_DIM
    nk = GQA_KV_HEADS * GQA_HEAD_DIM
    nqkv = nq + 2 * nk
    ndup = GQA_KV_HEADS * LANES

    def new_idx(b, t):
        return b * ntn + jnp.maximum(t - n_cache, 0)

    row_fn = lambda b, t: mod_row(b)
    in_specs = [
        pl.BlockSpec((tm, D), lambda b, t: (new_idx(b, t), 0)),
        _mod_spec(1, 0, row_fn), _mod_spec(1, 1, row_fn),
        _full_spec((1, D)), _full_spec((D, nqkv)),
        _full_spec((1, 2 * LANES)), _full_spec((1, 2 * LANES)), _full_spec((2 * LANES, 2 * LANES)),
    ]
    args = [x2d, mods, mods, gn, w["wqkv"], w["gq"], w["gk"], w["bmat"]]
    if rope:
        in_specs += [pl.BlockSpec((tm, LANES), lambda b, t: (jnp.maximum(t - n_cache, 0), 0))] * 2
        args += list(rope_tabs)
    if n_cache:
        in_specs.append(pl.BlockSpec((None, 2, nk, tm), lambda b, t: (b, 0, 0, jnp.minimum(t, n_cache - 1))))
        args.append(cache)
    n_new = n_batch * t_new
    n_keys = n_batch * (t_cache + t_new)
    out_specs = [
        pl.BlockSpec((tm, nq), lambda b, t: (new_idx(b, t), 0)),
        pl.BlockSpec((tm, ndup), lambda b, t: (b * nt + t, 0)),
        pl.BlockSpec((tm, ndup), lambda b, t: (b * nt + t, 0)),
    ]
    out_shape = [
        jax.ShapeDtypeStruct((n_new, nq), BF16),
        jax.ShapeDtypeStruct((n_keys, ndup), BF16),
        jax.ShapeDtypeStruct((n_keys, ndup), BF16),
    ]
    if emit_kv:
        assert tm % t_state == 0
        out_specs.append(pl.BlockSpec((tm // t_state, 2, nk, t_state), lambda b, t: (new_idx(b, t), 0, 0, 0)))
        out_shape.append(jax.ShapeDtypeStruct((n_new // t_state, 2, nk, t_state), F32))
    return pl.pallas_call(
        functools.partial(_gqa_proj_kernel, rope=rope, n_cache=n_cache, emit_kv=emit_kv),
        grid=(n_batch, nt), in_specs=in_specs, out_specs=out_specs, out_shape=out_shape,
        compiler_params=_params(2), name="gqa_proj_s" if rope else "gqa_proj_p",
    )(*args)


def _attn_kernel(q_ref, k_ref, v_ref, o_ref, *, n_sub, **kw):
    tq, tk = q_ref.shape[0] // n_sub, k_ref.shape[0] // n_sub
    for r in range(n_sub):
        _attend(q_ref.at[pl.ds(r * tq, tq)], k_ref.at[pl.ds(r * tk, tk)], v_ref.at[pl.ds(r * tk, tk)],
                o_ref.at[pl.ds(r * tq, tq)], **kw)


def _attend(q_ref, k_ref, v_ref, o_ref, *, mla, n_pairs, ck, keys_on_rows):
    tq = q_ref.shape[0]
    nc = k_ref.shape[0] // ck
    lane = lax.broadcasted_iota(jnp.int32, (tq, LANES), 1)
    low = lane < HALF
    items = [(p, c, hh) for c in range(nc) for p in range(n_pairs) for hh in range(2)]
    heads = {}
    state = {}

    def head_operands(p, hh):
        if (p, hh) not in heads:
            if mla:
                hd = 2 * p + hh
                heads[(p, hh)] = (q_ref[:, hd * LANES:(hd + 1) * LANES], hd, p)
            else:
                g = p // 2 if n_pairs > 1 else 0
                qp = q_ref[:, p * LANES:(p + 1) * LANES]
                qh = jnp.where(low if hh == 0 else jnp.logical_not(low), qp, jnp.zeros_like(qp))
                heads[(p, hh)] = (qh, g, g)
        return heads[(p, hh)]

    key_axis = 0 if keys_on_rows else -1
    first_half = (lax.broadcasted_iota(jnp.int32, (LANES, tq), 0) < HALF) if keys_on_rows else low

    def scores(item):
        p, c, hh = item
        qh, kc, _ = head_operands(p, hh)
        kh = k_ref[c * ck:(c + 1) * ck, kc * LANES:(kc + 1) * LANES]
        lhs, rhs = (kh, qh) if keys_on_rows else (qh, kh)
        return lax.dot_general(lhs, rhs, (((1,), (1,)), ((), ())), preferred_element_type=F32)

    def weighted_values(e, vb):
        if keys_on_rows:
            return lax.dot_general(vb, e.astype(BF16), (((0,), (0,)), ((), ())), preferred_element_type=F32)
        return jnp.dot(e.astype(BF16), vb, preferred_element_type=F32)

    pending = [scores(item) for item in items[:ATTN_LOOKAHEAD]]
    for idx, (p, c, hh) in enumerate(items):
        if idx + ATTN_LOOKAHEAD < len(items):
            pending.append(scores(items[idx + ATTN_LOOKAHEAD]))
        s_cur = pending.pop(0)
        vc = head_operands(p, hh)[2]
        vb = v_ref[c * ck:(c + 1) * ck, vc * LANES:(vc + 1) * LANES]
        m_c = jnp.max(s_cur, axis=key_axis, keepdims=True)
        if c == 0:
            m = m_c
            e = jnp.exp2(s_cur - m)
            l = jnp.sum(e, axis=key_axis, keepdims=True)
            acc = weighted_values(e, vb)
        else:
            m_old, l_old, acc_old = state[(p, hh)]
            m = jnp.maximum(m_old, m_c)
            alpha = jnp.exp2(m_old - m)
            e = jnp.exp2(s_cur - m)
            l = alpha * l_old + jnp.sum(e, axis=key_axis, keepdims=True)
            acc = alpha * acc_old + weighted_values(e, vb)
        state[(p, hh)] = (m, l, acc)
        if c == nc - 1 and hh == 1:
            o0 = state[(p, 0)][2] / state[(p, 0)][1]
            o1 = state[(p, 1)][2] / state[(p, 1)][1]
            o = jnp.where(first_half, o0, o1)
            o_ref[:, p * LANES:(p + 1) * LANES] = (o.T if keys_on_rows else o).astype(BF16)


def _attention(q, k, v, *, mla, n_batch, t_q, t_k, tq, n_pairs, keys_on_rows, name, n_sub=1):
    total_pairs = 8
    nj = total_pairs // n_pairs
    nqt = t_q // tq
    if mla:
        qw, kw, vw = n_pairs * 2 * LANES, n_pairs * 2 * LANES, n_pairs * LANES
        kv_col = lambda j: j
    else:
        qw = n_pairs * LANES
        kw = vw = max(n_pairs // 2, 1) * LANES
        kv_col = (lambda j: j // 2) if n_pairs == 1 else (lambda j: j)
    assert n_sub == 1 or tq == t_q
    return pl.pallas_call(
        functools.partial(_attn_kernel, n_sub=n_sub, mla=mla, n_pairs=n_pairs, ck=min(t_k, ATTN_KEY_CHUNK),
                          keys_on_rows=keys_on_rows),
        grid=(n_batch // n_sub, nj, nqt),
        in_specs=[
            pl.BlockSpec((n_sub * tq, qw), lambda b, j, i: (b * nqt + i, j)),
            pl.BlockSpec((n_sub * t_k, kw), lambda b, j, i: (b, kv_col(j))),
            pl.BlockSpec((n_sub * t_k, vw), lambda b, j, i: (b, kv_col(j))),
        ],
        out_specs=pl.BlockSpec((n_sub * tq, n_pairs * LANES), lambda b, j, i: (b * nqt + i, j)),
        out_shape=jax.ShapeDtypeStruct((n_batch * t_q, total_pairs * LANES), BF16),
        compiler_params=_params(3), name=name,
    )(q, k, v)


def _route(logits):
    lane = lax.broadcasted_iota(jnp.int32, logits.shape, 1).astype(F32)
    big = jnp.float32(1e9)
    is_grp = (lane >= N_EXPERTS) & (lane < N_EXPERTS + N_GROUPS)
    gl = jnp.where(is_grp, logits, NEG)
    gm = jnp.max(gl, axis=-1, keepdims=True)
    g_w = 1.0 / jnp.sum(jnp.exp(gl - gm), axis=-1, keepdims=True)
    g_idx = jnp.min(jnp.where(gl == gm, lane, big), axis=-1, keepdims=True) - N_EXPERTS
    lo = g_idx * EXPERTS_PER_GROUP
    el = jnp.where((lane >= lo) & (lane < lo + EXPERTS_PER_GROUP), logits, NEG)
    m1 = jnp.max(el, axis=-1, keepdims=True)
    i1 = jnp.min(jnp.where(el == m1, lane, big), axis=-1, keepdims=True)
    el2 = jnp.where(lane == i1, NEG, el)
    m2 = jnp.max(el2, axis=-1, keepdims=True)
    i2 = jnp.min(jnp.where(el2 == m2, lane, big), axis=-1, keepdims=True)
    t = jnp.exp(m2 - m1)
    w1 = g_w / (1.0 + t)
    w2 = g_w * t / (1.0 + t)
    return jnp.where(lane == i1, w1, 0.0) + jnp.where(lane == i2, w2, 0.0), g_idx


MOE_TM = 512
MOE_CHUNK = 144
MOE_SLOTS = 1152
MOE_MAIN_SLOTS = 896
SLOT_RADIX = 32


def _moe_kernel(x_ref, o_ref, wo_ref, gta_ref, shm_ref, scm_ref, gtm_ref, gn_ref,
                wr_ref, br_ref, sel_ref, ltri_ref, wg_ref, wu_ref, wd_ref, fin_ref,
                y_ref, hs_ref, cs_ref, ys_ref, pt_ref, *, final):
    tm = x_ref.shape[0]
    ffg = EXPERTS_PER_GROUP * EXPERT_FF
    mix = jnp.dot(o_ref[...], wo_ref[...], preferred_element_type=F32)
    xm = x_ref[...] + gta_ref[...] * mix
    y_ref[...] = xm
    h = _rms(xm, gn_ref[...]) * (1.0 + scm_ref[...]) + shm_ref[...]
    h_hi = h.astype(BF16)
    h_lo = (h - h_hi.astype(F32)).astype(BF16)
    both = jnp.dot(h_hi, wr_ref[...], preferred_element_type=F32)
    logits = (both[:, 0:LANES] + jnp.dot(h_lo, wr_ref[:, 0:LANES], preferred_element_type=F32)
              + both[:, LANES:2 * LANES]) + br_ref[...]
    comb, g_idx = _route(logits)

    lane_i = lax.broadcasted_iota(jnp.int32, (tm, LANES), 1)
    lane = lane_i.astype(F32)
    onehot = jnp.where(lane == g_idx, 1.0, 0.0)
    rank = jnp.dot(ltri_ref[...], onehot.astype(BF16), preferred_element_type=F32)
    cnt = jnp.sum(onehot, axis=0, keepdims=True)
    ends = []
    end = jnp.int32(0)
    start_vec = jnp.zeros_like(lane)
    for g in range(N_GROUPS):
        start_vec = jnp.where(lane_i == g, (end * MOE_CHUNK).astype(F32), start_vec)
        n_g = cnt[0, g].astype(jnp.int32)
        end = end + sum((n_g > k * MOE_CHUNK).astype(jnp.int32) for k in range(-(-tm // MOE_CHUNK)))
        ends.append(end)
    total = ends[-1]
    slot = jnp.sum(onehot * (start_vec + rank), axis=-1, keepdims=True)
    slot_i = slot.astype(jnp.int32)
    pt = jnp.where(lax.broadcasted_iota(jnp.int32, (tm, MOE_SLOTS), 1) == slot_i, 1.0, 0.0).astype(BF16)
    pt_ref[...] = pt
    a = jnp.floor(slot * (1.0 / SLOT_RADIX))
    b = slot - SLOT_RADIX * a
    digits = jnp.where(lane_i == 0, a, jnp.where(lane_i == 1, b, 0.0)).astype(BF16)
    rows = lax.dot_general(sel_ref[...], digits, (((1,), (1,)), ((), ())), preferred_element_type=F32)
    slot_row = (rows[0:1, :] * SLOT_RADIX + rows[1:2, :]).astype(jnp.int32)

    c_hi = comb.astype(BF16)
    r1 = comb - c_hi.astype(F32)
    c_mid = r1.astype(BF16)
    c_lo = (r1 - c_mid.astype(F32)).astype(BF16)
    packed = (c_hi.astype(F32) + pltpu.roll(c_mid.astype(F32), N_EXPERTS, 1)
              + pltpu.roll(c_lo.astype(F32), 2 * N_EXPERTS, 1)).astype(BF16)

    def sort_rows(row0, n_rows):
        p = jnp.where(lax.broadcasted_iota(jnp.int32, (n_rows, tm), 0) + row0 == slot_row, 1.0, 0.0).astype(BF16)
        hs_ref[row0:row0 + n_rows, :] = jnp.dot(p, h_hi, preferred_element_type=F32).astype(BF16)
        cs = jnp.dot(p, packed, preferred_element_type=F32)
        cs_ref[row0:row0 + n_rows, :] = (cs + pltpu.roll(cs, LANES - N_EXPERTS, 1)
                                         + pltpu.roll(cs, LANES - 2 * N_EXPERTS, 1))

    tail_live = total * MOE_CHUNK > MOE_MAIN_SLOTS
    sort_rows(0, MOE_MAIN_SLOTS)
    pl.when(tail_live)(functools.partial(sort_rows, MOE_MAIN_SLOTS, MOE_SLOTS - MOE_MAIN_SLOTS))

    def run_experts(c, n_chunks, g_c):
        rows_c = slice(c * MOE_CHUNK, (c + n_chunks) * MOE_CHUNK)
        lane_c = lax.broadcasted_iota(jnp.int32, (n_chunks * MOE_CHUNK, LANES), 1)
        xc = hs_ref[rows_c, :]
        cc = cs_ref[rows_c, :]
        hids = []
        for j in range(EXPERTS_PER_GROUP):
            e = g_c * EXPERTS_PER_GROUP + j
            gt = jnp.dot(xc, wg_ref[e], preferred_element_type=F32)
            up = jnp.dot(xc, wu_ref[e], preferred_element_type=F32)
            cj = jnp.sum(jnp.where(lane_c == e, cc, 0.0), axis=-1, keepdims=True)
            hids.append(((gt / (1.0 + jnp.exp(-gt))) * up * cj).astype(BF16))
        hid = jnp.concatenate(hids, axis=1)
        wd_g = wd_ref[pl.ds(pl.multiple_of(g_c * ffg, ffg), ffg), :]
        ys_ref[rows_c, :] = jnp.dot(hid, wd_g, preferred_element_type=F32).astype(BF16)

    n_slots = MOE_SLOTS // MOE_CHUNK
    for c in range(n_slots):
        g_c = sum((ends[g] <= c).astype(jnp.int32) for g in range(N_GROUPS - 1))
        start_c = jnp.int32(0)
        end_c = ends[0]
        for g in range(1, N_GROUPS):
            start_c = jnp.where(g_c == g, ends[g - 1], start_c)
            end_c = jnp.where(g_c == g, ends[g], end_c)
        live = c < total
        first_of_pair = ((c - start_c) & 1) == 0
        has_partner = c + 1 < end_c

        if c + 1 < n_slots:
            pl.when(live & first_of_pair & has_partner)(functools.partial(run_experts, c, 2, g_c))
        pl.when(live & first_of_pair & jnp.logical_not(has_partner))(functools.partial(run_experts, c, 1, g_c))

        @pl.when(jnp.logical_not(live))
        def _():
            ys_ref[c * MOE_CHUNK:(c + 1) * MOE_CHUNK, :] = jnp.zeros((MOE_CHUNK, D), BF16)

    y_tok = jnp.dot(pt_ref[:, 0:MOE_MAIN_SLOTS], ys_ref[0:MOE_MAIN_SLOTS, :], preferred_element_type=F32)
    y_ref[...] = y_ref[...] + gtm_ref[...] * y_tok

    @pl.when(tail_live)
    def _():
        y_tail = jnp.dot(pt_ref[:, MOE_MAIN_SLOTS:MOE_SLOTS], ys_ref[MOE_MAIN_SLOTS:MOE_SLOTS, :],
                         preferred_element_type=F32)
        y_ref[...] = y_ref[...] + gtm_ref[...] * y_tail

    if final:
        y_ref[...] = _rms(y_ref[...], fin_ref[...])


def _moe(x2d, o2d, mods, layer, w, *, mod_row, final):
    n = x2d.shape[0]
    tm = MOE_TM
    row_fn = lambda i: mod_row(i)
    tile = pl.BlockSpec((tm, D), lambda i: (i, 0))
    once = lambda shape: pl.BlockSpec(shape, lambda i: (0,) * len(shape), pipeline_mode=pl.Buffered(1))
    per_layer = lambda shape: pl.BlockSpec((None,) + shape, lambda i: (layer,) + (0,) * len(shape),
                                           pipeline_mode=pl.Buffered(1))
    return pl.pallas_call(
        functools.partial(_moe_kernel, final=final),
        grid=(n // tm,),
        in_specs=[
            tile, tile, once((D, D)),
            _mod_spec(layer, 2, row_fn), _mod_spec(layer, 3, row_fn), _mod_spec(layer, 4, row_fn),
            _mod_spec(layer, 5, row_fn), _full_spec((1, D)),
            _full_spec((D, 2 * LANES)), _full_spec((1, LANES)), _full_spec((8, LANES)),
            once((tm, tm)),
            per_layer((N_EXPERTS, D, EXPERT_FF)), per_layer((N_EXPERTS, D, EXPERT_FF)),
            per_layer((N_EXPERTS * EXPERT_FF, D)),
            _full_spec((1, D)),
        ],
        out_specs=tile,
        out_shape=jax.ShapeDtypeStruct((n, D), F32),
        scratch_shapes=[pltpu.VMEM((MOE_SLOTS, D), BF16), pltpu.VMEM((MOE_SLOTS, LANES), F32),
                        pltpu.VMEM((MOE_SLOTS, D), BF16), pltpu.VMEM((tm, MOE_SLOTS), BF16)],
        compiler_params=_params(1), name=f"moe_l{layer}",
    )(x2d, o2d, w["wo"], mods, mods, mods, mods, w["gn"], w["wr"], w["br"], w["sel2"],
      w["ltri"], w["wg_bf"], w["wu_bf"], w["wd_bf"], w["fin"])


def _mla_weights(w_dq, g_q, w_uq, w_dkv, g_kv, w_uk, w_uv):
    hd = MLA_NOPE + MLA_ROPE
    wuq = w_uq.reshape(Q_LORA, MLA_HEADS, hd)
    wuq = jnp.concatenate([wuq[..., MLA_NOPE:], wuq[..., :MLA_NOPE],
                           jnp.zeros((Q_LORA, MLA_HEADS, LANES - hd), F32)], axis=-1)
    wuk = jnp.concatenate([jnp.zeros((KV_LORA, MLA_HEADS, MLA_ROPE), F32), w_uk,
                           jnp.zeros((KV_LORA, MLA_HEADS, LANES - hd), F32)], axis=-1)
    return {
        "wdq": w_dq.astype(BF16), "gq": g_q.reshape(1, Q_LORA),
        "wuq": wuq.reshape(Q_LORA, MLA_HEADS * LANES).astype(BF16),
        "wdkv": jnp.pad(w_dkv, ((0, 0), (0, KV_PAD - KV_DIM))).astype(BF16), "gkv": g_kv.reshape(1, KV_LORA),
        "wkexp": wuk.reshape(KV_LORA, MLA_HEADS * LANES).astype(BF16),
        "wuv": w_uv.reshape(KV_LORA, MLA_HEADS * MLA_V).astype(BF16),
    }


def _gqa_weights(w_qkv, g_q, g_k):
    grp = jnp.arange(2 * LANES) // GQA_HEAD_DIM
    bmat = jnp.where(grp[:, None] == grp[None, :], 1.0 / GQA_HEAD_DIM, 0.0).astype(BF16)
    reps = 2 * LANES // GQA_HEAD_DIM
    return {"wqkv": w_qkv.astype(BF16), "gq": jnp.tile(g_q, reps).reshape(1, 2 * LANES),
            "gk": jnp.tile(g_k, reps).reshape(1, 2 * LANES), "bmat": bmat}


def _post_weights(l, w_o, norm_ffn, w_group, b_group, w_exp, b_exp, w_gate, w_up, w_down, final_norm):
    wr = jnp.concatenate([w_exp[l], w_group[l], jnp.zeros((D, LANES - N_EXPERTS - N_GROUPS), F32)], axis=1)
    br = jnp.concatenate([b_exp[l], b_group[l], jnp.zeros((LANES - N_EXPERTS - N_GROUPS,), F32)]).reshape(1, LANES)
    wr_hi = wr.astype(BF16)
    wr_lo = (wr - wr_hi.astype(F32)).astype(BF16)
    return {"wo": w_o.astype(BF16), "gn": norm_ffn[l].reshape(1, D), "wr": jnp.concatenate([wr_hi, wr_lo], axis=1), "br": br,
            "wg_bf": w_gate, "wu_bf": w_up, "wd_bf": w_down,
            "fin": final_norm.reshape(1, D),
            "sel2": jnp.zeros((8, LANES), BF16).at[0, 0].set(1.0).at[1, 1].set(1.0),
            "ltri": jnp.tri(MOE_TM, MOE_TM, -1, dtype=BF16)}


def kernel(x_prompt, x_sample, cache_mla, cache_gqa, c, c_ctx, ada_w, ada_b, norm_mix, norm_ffn,
           mla_w_dq, mla_q_norm, mla_w_uq, mla_w_dkv, mla_kv_norm, mla_w_uk, mla_w_uv, mla_w_o,
           gqa_w_qkv, gqa_q_norm, gqa_k_norm, gqa_w_o,
           moe_w_group, moe_b_group, moe_w_expert, moe_b_expert, moe_w_gate, moe_w_up, moe_w_down,
           final_norm):
    bp, tp, _ = x_prompt.shape
    bs, ts, _ = x_sample.shape
    t_past = cache_mla.shape[2]
    assert ada_w.shape[0] == 2 and 1 + bs <= N_MOD_ROWS
    n_p, n_s = bp * tp, bs * ts

    cc = jnp.concatenate([c_ctx[None], c, jnp.zeros((N_MOD_ROWS - 1 - bs, D), F32)], axis=0)
    mods = _ada_table(cc, ada_w, ada_b)

    xp = x_prompt.reshape(n_p, D)
    xs = x_sample.reshape(n_s, D)
    tm_proj = PROJ_TM
    tm_post = MOE_TM
    prompt_row = lambda *_: 0
    sample_row_proj = lambda b: 1 + b
    sample_row_post = lambda i: 1 + i // (ts // tm_post)
    depth = moe_w_gate.shape[0]
    moe_args = (norm_ffn, moe_w_group, moe_b_group, moe_w_expert, moe_b_expert,
                moe_w_gate.astype(BF16), moe_w_up.astype(BF16),
                moe_w_down.reshape(depth, N_EXPERTS * EXPERT_FF, D).astype(BF16), final_norm)

    wm = _mla_weights(mla_w_dq[0], mla_q_norm[0], mla_w_uq[0], mla_w_dkv[0], mla_kv_norm[0],
                      mla_w_uk[0], mla_w_uv[0])
    gn0 = norm_mix[0].reshape(1, D)
    q, k, v, keys_p = _mla_proj(xp, mods, gn0, wm, n_batch=1, t_new=n_p, tm=tm_proj, mod_row=prompt_row, t_state=tp)
    o = _attention(q, k, v, mla=True, n_batch=bp, t_q=tp, t_k=tp, tq=tp, n_pairs=8, keys_on_rows=False, n_sub=PROMPT_REQUESTS_PER_STEP, name="attn_mla_p")
    wp0 = _post_weights(0, mla_w_o[0], *moe_args)
    xp = _moe(xp, o, mods, 0, wp0, mod_row=prompt_row, final=False)

    rope_mla = _rope_tables(ts, MLA_ROPE, 1)
    cache0 = jnp.transpose(cache_mla[:, 0], (0, 2, 1))
    q, k, v = _mla_proj(xs, mods, gn0, wm, n_batch=bs, t_new=ts, tm=tm_proj, mod_row=sample_row_proj,
                        rope_tabs=rope_mla, cache=cache0)
    o = _attention(q, k, v, mla=True, n_batch=bs, t_q=ts, t_k=t_past + ts, tq=ATTN_TQ, n_pairs=1,
                   keys_on_rows=False, name="attn_mla_s")
    xs = _moe(xs, o, mods, 0, wp0, mod_row=sample_row_post, final=False)

    wg = _gqa_weights(gqa_w_qkv[0], gqa_q_norm[0], gqa_k_norm[0])
    gn1 = norm_mix[1].reshape(1, D)
    q, k, v, kv_p = _gqa_proj(xp, mods, gn1, wg, n_batch=1, t_new=n_p, tm=tm_proj, mod_row=prompt_row, t_state=tp)
    o = _attention(q, k, v, mla=False, n_batch=bp, t_q=tp, t_k=tp, tq=tp, n_pairs=8, keys_on_rows=False, n_sub=PROMPT_REQUESTS_PER_STEP, name="attn_gqa_p")
    wp1 = _post_weights(1, gqa_w_o[0], *moe_args)
    y_prompt = _moe(xp, o, mods, 1, wp1, mod_row=prompt_row, final=True)

    rope_gqa = _rope_tables(ts, GQA_HEAD_DIM, LANES // GQA_HEAD_DIM)
    cache1 = jnp.transpose(cache_gqa[:, 0], (0, 2, 3, 4, 1)).reshape(bs, 2, GQA_KV_HEADS * GQA_HEAD_DIM, t_past)
    q, k, v = _gqa_proj(xs, mods, gn1, wg, n_batch=bs, t_new=ts, tm=tm_proj, mod_row=sample_row_proj,
                        rope_tabs=rope_gqa, cache=cache1)
    o = _attention(q, k, v, mla=False, n_batch=bs, t_q=ts, t_k=t_past + ts, tq=2 * ATTN_TQ, n_pairs=4,
                   keys_on_rows=True, name="attn_gqa_s")
    y_sample = _moe(xs, o, mods, 1, wp1, mod_row=sample_row_post, final=True)

    return (y_prompt.reshape(bp, tp, D), y_sample.reshape(bs, ts, D),
            jnp.transpose(keys_p, (0, 2, 1))[:, None],
            jnp.transpose(kv_p.reshape(bp, 2, GQA_KV_HEADS, GQA_HEAD_DIM, tp), (0, 4, 1, 2, 3))[:, None])
```
